```python
import jax, jax.numpy as jnp
from jax import lax
import numpy as np

D_MODEL = 1024
BATCH = 16
SEQ = 4096
DEPTH = 1

CONV_WIDTH = 512
CONV_K = 3
N_HEADS = 8
HEAD_DIM = 64
ATTN_WIDTH = N_HEADS * HEAD_DIM
Q_BLOCK = 128
N_GROUPS = 4
EXPERTS_PER_GROUP = 8
N_EXPERTS = N_GROUPS * EXPERTS_PER_GROUP
TOP_K_IN_GROUP = 2
D_EXPERT = 256
EPS = 1e-6
IN_COLS = 3 * CONV_WIDTH + 3 * ATTN_WIDTH + N_HEADS + 2 * D_MODEL

kernel_name = "hybrid_shortconv_fox_hmoe_block"


def rmsnorm(x, w):
    xf = x.astype(jnp.float32)
    y = xf * lax.rsqrt(jnp.mean(xf * xf, axis=-1, keepdims=True) + EPS)
    return (y * w.astype(jnp.float32)).astype(x.dtype)


def modulate(x, norm_w, shift, scale):
    return rmsnorm(x, norm_w) * (1 + scale[:, None, :]) + shift[:, None, :]


def causal_depthwise_conv(u, w):
    S = u.shape[1]
    up = jnp.pad(u, ((0, 0), (CONV_K - 1, 0), (0, 0)))
    out = w[0] * up[:, 0:S]
    for j in range(1, CONV_K):
        out = out + w[j] * up[:, j:j + S]
    return out


def forgetting_attention(q, k, v, log_f):
    B, H, S, Dh = q.shape
    nblk = S // Q_BLOCK
    F = jnp.cumsum(log_f, axis=-1)
    scale = HEAD_DIM ** -0.5
    k_pos = jnp.arange(S)
    q_blocks = q.reshape(B, H, nblk, Q_BLOCK, Dh).transpose(2, 0, 1, 3, 4)
    F_blocks = F.reshape(B, H, nblk, Q_BLOCK).transpose(2, 0, 1, 3)

    def one_block(args):
        i, q_i, F_i = args
        s = jnp.einsum('bhqd,bhkd->bhqk', q_i, k,
                       preferred_element_type=jnp.float32) * scale
        s = s + (F_i[..., :, None] - F[..., None, :])
        q_pos = i * Q_BLOCK + jnp.arange(Q_BLOCK)
        s = jnp.where(k_pos[None, :] <= q_pos[:, None], s, -jnp.inf)
        p = jax.nn.softmax(s, axis=-1)
        return jnp.einsum('bhqk,bhkd->bhqd', p.astype(v.dtype), v)

    o = lax.map(one_block, (jnp.arange(nblk), q_blocks, F_blocks))
    return o.transpose(1, 0, 3, 2, 4).reshape(B, S, H * Dh)


def hierarchical_moe(h, w_rg, b_rg, w_re, b_re, w_gate, w_up, w_down):
    B, S, _ = h.shape
    lg = (h @ w_rg).astype(jnp.float32) + b_rg.astype(jnp.float32)
    p_g = jax.nn.softmax(lg, axis=-1)
    g_idx = jnp.argmax(lg, axis=-1)
    p_sel = jnp.take_along_axis(p_g, g_idx[..., None], axis=-1)
    le = ((h @ w_re).astype(jnp.float32) + b_re.astype(jnp.float32)
          ).reshape(B, S, N_GROUPS, EXPERTS_PER_GROUP)
    le_g = jnp.take_along_axis(le, g_idx[..., None, None], axis=2)[:, :, 0]
    top_v, top_i = lax.top_k(le_g, TOP_K_IN_GROUP)
    w_k = jax.nn.softmax(top_v, axis=-1) * p_sel
    e_idx = g_idx[..., None] * EXPERTS_PER_GROUP + top_i
    comb = jnp.sum(jax.nn.one_hot(e_idx, N_EXPERTS, dtype=jnp.float32)
                   * w_k[..., None], axis=-2).astype(h.dtype)
    out = jnp.zeros_like(h)
    for e in range(N_EXPERTS):
        a = jax.nn.silu(h @ w_gate[e]) * (h @ w_up[e])
        out = out + comb[..., e:e + 1] * (a @ w_down[e])
    return out


def hybrid_layer(x, c, w_ada, b_ada, norm1_w, w_in, b_forget, conv_w,
                 q_norm_w, k_norm_w, w_out_conv, w_out_attn, w_o, norm2_w,
                 w_router_group, b_router_group, w_router_expert, b_router_expert,
                 w_gate, w_up, w_down):
    B, S, _ = x.shape
    mod = jax.nn.silu(c) @ w_ada + b_ada
    shift1, scale1, gate1, shift2, scale2, gate2 = jnp.split(mod, 6, axis=-1)

    h = modulate(x, norm1_w, shift1, scale1)
    proj = h @ w_in
    sizes = [CONV_WIDTH, CONV_WIDTH, CONV_WIDTH, ATTN_WIDTH, ATTN_WIDTH,
             ATTN_WIDTH, N_HEADS, D_MODEL]
    cuts = [int(v) for v in np.cumsum(sizes)]
    x_in, conv_b, conv_c, q, k, v, f_logit, gate_conv_l, gate_attn_l = jnp.split(
        proj, cuts, axis=-1)

    y_a = conv_b * causal_depthwise_conv(conv_c * x_in, conv_w)
    p_a = y_a @ w_out_conv

    q = rmsnorm(q.reshape(B, S, N_HEADS, HEAD_DIM), q_norm_w).transpose(0, 2, 1, 3)
    k = rmsnorm(k.reshape(B, S, N_HEADS, HEAD_DIM), k_norm_w).transpose(0, 2, 1, 3)
    v = v.reshape(B, S, N_HEADS, HEAD_DIM).transpose(0, 2, 1, 3)
    log_f = jax.nn.log_sigmoid((f_logit + b_forget).astype(jnp.float32)).transpose(0, 2, 1)
    y_b = forgetting_attention(q, k, v, log_f)
    p_b = y_b @ w_out_attn

    merged = jax.nn.sigmoid(gate_conv_l) * p_a + jax.nn.sigmoid(gate_attn_l) * p_b
    x = x + gate1[:, None, :] * (merged @ w_o)

    h2 = modulate(x, norm2_w, shift2, scale2)
    moe = hierarchical_moe(h2, w_router_group, b_router_group, w_router_expert,
                           b_router_expert, w_gate, w_up, w_down)
    return x + gate2[:, None, :] * moe


def setup_inputs(seed: int = 0) -> dict:
    key = jax.random.key(seed)
    ks = jax.random.split(key, 22)
    f32 = jnp.float32
    L = DEPTH

    def nrm(k, shape, scale):
        return jax.random.normal(k, shape, f32) * scale

    return {
        "x": nrm(ks[0], (BATCH, SEQ, D_MODEL), 1.0),
        "c": nrm(ks[1], (BATCH, D_MODEL), 1.0),
        "w_ada": nrm(ks[2], (L, D_MODEL, 6 * D_MODEL), 0.5 * D_MODEL ** -0.5),
        "b_ada": nrm(ks[3], (L, 6 * D_MODEL), 0.02),
        "norm1_w": 1.0 + nrm(ks[4], (L, D_MODEL), 0.02),
        "w_in": nrm(ks[5], (L, D_MODEL, IN_COLS), D_MODEL ** -0.5),
        "b_forget": 4.0 + nrm(ks[6], (L, N_HEADS), 0.5),
        "conv_w": nrm(ks[7], (L, CONV_K, CONV_WIDTH), CONV_K ** -0.5),
        "q_norm_w": 1.0 + nrm(ks[8], (L, HEAD_DIM), 0.02),
        "k_norm_w": 1.0 + nrm(ks[9], (L, HEAD_DIM), 0.02),
        "w_out_conv": nrm(ks[10], (L, CONV_WIDTH, D_MODEL), CONV_WIDTH ** -0.5),
        "w_out_attn": nrm(ks[11], (L, ATTN_WIDTH, D_MODEL), ATTN_WIDTH ** -0.5),
        "w_o": nrm(ks[12], (L, D_MODEL, D_MODEL), D_MODEL ** -0.5),
        "norm2_w": 1.0 + nrm(ks[13], (L, D_MODEL), 0.02),
        "w_router_group": nrm(ks[14], (L, D_MODEL, N_GROUPS), D_MODEL ** -0.5),
        "b_router_group": nrm(ks[15], (L, N_GROUPS), 0.01),
        "w_router_expert": nrm(ks[16], (L, D_MODEL, N_EXPERTS), D_MODEL ** -0.5),
        "b_router_expert": nrm(ks[17], (L, N_EXPERTS), 0.01),
        "w_gate": nrm(ks[18], (L, N_EXPERTS, D_MODEL, D_EXPERT), D_MODEL ** -0.5),
        "w_up": nrm(ks[19], (L, N_EXPERTS, D_MODEL, D_EXPERT), D_MODEL ** -0.5),
        "w_down": nrm(ks[20], (L, N_EXPERTS, D_EXPERT, D_MODEL), D_EXPERT ** -0.5),
    }


def reference(x, c, w_ada, b_ada, norm1_w, w_in, b_forget, conv_w, q_norm_w,
              k_norm_w, w_out_conv, w_out_attn, w_o, norm2_w, w_router_group,
              b_router_group, w_router_expert, b_router_expert, w_gate, w_up,
              w_down):
    for l in range(DEPTH):
        x = hybrid_layer(x, c, w_ada[l], b_ada[l], norm1_w[l], w_in[l], b_forget[l],
                         conv_w[l], q_norm_w[l], k_norm_w[l], w_out_conv[l],
                         w_out_attn[l], w_o[l], norm2_w[l], w_router_group[l],
                         b_router_group[l], w_router_expert[l], b_router_expert[l],
                         w_gate[l], w_up[l], w_down[l])
    return x
```

```python
import functools

import jax
import jax.numpy as jnp
import numpy as np
from jax import lax
from jax.experimental import pallas as pl
from jax.experimental.pallas import tpu as pltpu

D_MODEL = 1024
CONV_WIDTH = 512
CONV_K = 3
N_HEADS = 8
HEAD_DIM = 64
ATTN_WIDTH = N_HEADS * HEAD_DIM
N_PAIRS = N_HEADS // 2
N_GROUPS = 4
EXPERTS_PER_GROUP = 8
N_EXPERTS = N_GROUPS * EXPERTS_PER_GROUP
D_EXPERT = 256
EPS = 1e-6
LANES = 128
AUG = 2 * LANES
NEG = -1e30

F32 = jnp.float32
BF16 = jnp.bfloat16
VMEM_LIMIT = 56 * 1024 * 1024


def _sigmoid(z):
    return 1.0 / (1.0 + jnp.exp(-z))


def _split3(z):
    hi = z.astype(BF16)
    r = z - hi.astype(F32)
    mid = r.astype(BF16)
    lo = (r - mid.astype(F32)).astype(BF16)
    return hi, mid, lo


def _dot(a, b):
    return jnp.dot(a, b, preferred_element_type=F32)


def _modulated_norm(x, nw, shift, scale):
    ms = jnp.mean(x * x, axis=-1, keepdims=True)
    return (x * lax.rsqrt(ms + EPS) * nw) * (1.0 + scale) + shift


def _ada_kernel(c_ref, w_ref, b_ref, o_ref):
    c = c_ref[...]
    a = c * _sigmoid(c)
    o_ref[...] = jnp.dot(a, w_ref[...], precision=lax.Precision.HIGHEST,
                         preferred_element_type=F32) + b_ref[...]


def _ada(c, w_ada, b_ada):
    B = c.shape[0]
    n = w_ada.shape[1] // D_MODEL
    return pl.pallas_call(
        _ada_kernel,
        out_shape=jax.ShapeDtypeStruct((B, n * D_MODEL), F32),
        grid=(n,),
        in_specs=[pl.BlockSpec((B, D_MODEL), lambda j: (0, 0)),
                  pl.BlockSpec((D_MODEL, D_MODEL), lambda j: (0, j)),
                  pl.BlockSpec((1, D_MODEL), lambda j: (0, j))],
        out_specs=pl.BlockSpec((B, D_MODEL), lambda j: (0, j)),
        compiler_params=pltpu.CompilerParams(dimension_semantics=("arbitrary",)),
        name="ada",
    )(c, w_ada, b_ada)


def _qkv_kernel(x_ref, nw_ref, sh_ref, sc_ref, wqkv_ref, wf_ref, bf_ref, qnw_ref, knw_ref,
                gsum_ref, pq_ref, pk_ref, cq_ref, ck_ref,
                q_ref, k_ref, v_ref, carry_ref):
    tm = x_ref.shape[1]
    h = _modulated_norm(x_ref[0], nw_ref[...], sh_ref[0], sc_ref[0])
    hb = h.astype(BF16)
    qkv = _dot(hb, wqkv_ref[...])
    gsum = gsum_ref[...]

    def head_norm(z, w):
        zz = z * z
        hi = zz.astype(BF16)
        lo = (zz - hi.astype(F32)).astype(BF16)
        ss = _dot(hi, gsum) + _dot(lo, gsum)
        return z * lax.rsqrt(ss * (1.0 / HEAD_DIM) + EPS) * w

    qn = head_norm(qkv[:, :ATTN_WIDTH], qnw_ref[...]) * (HEAD_DIM ** -0.5)
    kn = head_norm(qkv[:, ATTN_WIDTH:2 * ATTN_WIDTH], knw_ref[...])
    v_ref[0] = qkv[:, 2 * ATTN_WIDTH:].astype(BF16)

    fl = _dot(hb, wf_ref[...]) + bf_ref[...]
    lf = jnp.minimum(fl, 0.0) - jnp.log(1.0 + jnp.exp(-jnp.abs(fl)))

    @pl.when(pl.program_id(1) == 0)
    def _():
        carry_ref[...] = jnp.zeros_like(carry_ref)

    row = lax.broadcasted_iota(jnp.int32, (tm, LANES), 0)
    cum = lf
    s = 1
    while s < tm:
        cum = cum + jnp.where(row >= s, pltpu.roll(cum, s, 0), 0.0)
        s *= 2
    cum = cum + carry_ref[7:8, :]
    carry_ref[...] = cum[tm - 8:, :]

    hi, mid, lo = _split3(cum)
    eq = _dot(hi, pq_ref[0]) + _dot(mid, pq_ref[1]) + _dot(lo, pq_ref[2]) + cq_ref[...]
    ek = _dot(hi, pk_ref[0]) + _dot(mid, pk_ref[1]) + _dot(lo, pk_ref[2]) + ck_ref[...]
    for j in range(N_PAIRS):
        q_ref[0, :, j * AUG:j * AUG + LANES] = qn[:, j * LANES:(j + 1) * LANES].astype(BF16)
        q_ref[0, :, j * AUG + LANES:(j + 1) * AUG] = eq[:, j * LANES:(j + 1) * LANES].astype(BF16)
        k_ref[0, :, j * AUG:j * AUG + LANES] = kn[:, j * LANES:(j + 1) * LANES].astype(BF16)
        k_ref[0, :, j * AUG + LANES:(j + 1) * AUG] = ek[:, j * LANES:(j + 1) * LANES].astype(BF16)


def _bias_placement():
    pq = np.zeros((3, LANES, N_PAIRS * LANES), np.float32)
    pk = np.zeros((3, LANES, N_PAIRS * LANES), np.float32)
    cq = np.zeros((1, N_PAIRS * LANES), np.float32)
    ck = np.zeros((1, N_PAIRS * LANES), np.float32)
    for hd in range(N_HEADS):
        base = (hd // 2) * LANES + 6 * (hd % 2)
        for p in range(3):
            pq[p, hd, base + p] = 1.0
            pk[p, hd, base + 3 + p] = -1.0
            cq[0, base + 3 + p] = 1.0
            ck[0, base + p] = 1.0
    return (jnp.asarray(pq, BF16), jnp.asarray(pk, BF16), jnp.asarray(cq), jnp.asarray(ck))


def _qkv(x, nw, shift, scale, wqkv, wf, bf, qnw, knw, tm):
    B, S, _ = x.shape
    gsum = jnp.asarray(np.kron(np.eye(N_HEADS), np.ones((HEAD_DIM, HEAD_DIM))), BF16)
    pq, pk, cq, ck = _bias_placement()
    const = lambda *shape: pl.BlockSpec(shape, lambda b, i: (0,) * len(shape))
    return pl.pallas_call(
        _qkv_kernel,
        out_shape=(jax.ShapeDtypeStruct((B, S, N_PAIRS * AUG), BF16),
                   jax.ShapeDtypeStruct((B, S, N_PAIRS * AUG), BF16),
                   jax.ShapeDtypeStruct((B, S, ATTN_WIDTH), BF16)),
        grid=(B, S // tm),
        in_specs=[pl.BlockSpec((1, tm, D_MODEL), lambda b, i: (b, i, 0)),
                  const(1, D_MODEL),
                  pl.BlockSpec((1, 1, D_MODEL), lambda b, i: (b, 0, 0)),
                  pl.BlockSpec((1, 1, D_MODEL), lambda b, i: (b, 0, 0)),
                  const(D_MODEL, 3 * ATTN_WIDTH),
                  const(D_MODEL, LANES),
                  const(1, LANES),
                  const(1, ATTN_WIDTH),
                  const(1, ATTN_WIDTH),
                  const(ATTN_WIDTH, ATTN_WIDTH),
                  const(3, LANES, N_PAIRS * LANES),
                  const(3, LANES, N_PAIRS * LANES),
                  const(1, N_PAIRS * LANES),
                  const(1, N_PAIRS * LANES)],
        out_specs=(pl.BlockSpec((1, tm, N_PAIRS * AUG), lambda b, i: (b, i, 0)),
                   pl.BlockSpec((1, tm, N_PAIRS * AUG), lambda b, i: (b, i, 0)),
                   pl.BlockSpec((1, tm, ATTN_WIDTH), lambda b, i: (b, i, 0))),
        scratch_shapes=[pltpu.VMEM((8, LANES), F32)],
        compiler_params=pltpu.CompilerParams(
            dimension_semantics=("arbitrary", "arbitrary"), vmem_limit_bytes=VMEM_LIMIT),
        name="qkv",
    )(x, nw, shift, scale, wqkv, wf, bf, qnw, knw, gsum, pq, pk, cq, ck)


def _attn_kernel(q_ref, k_ref, v_ref, o_ref, *, tq, tk):
    qi = pl.program_id(2)
    q = q_ref[0]
    lane = lax.broadcasted_iota(jnp.int32, (tq, AUG), 1)
    row = lax.broadcasted_iota(jnp.int32, (tq, tk), 0)
    col = lax.broadcasted_iota(jnp.int32, (tq, tk), 1)
    n_sub = tq // tk
    outs = []
    for t in range(2):
        keep = ((lane >= t * HEAD_DIM) & (lane < (t + 1) * HEAD_DIM)) | \
               ((lane >= LANES + 6 * t) & (lane < LANES + 6 * (t + 1)))
        qh = jnp.where(keep, q, jnp.zeros_like(q))

        def step(j, carry, masked_sub=None):
            m, l, acc = carry
            start = pl.multiple_of(j * tk, tk)
            kj = k_ref[0, pl.ds(start, tk), :]
            vj = v_ref[0, pl.ds(start, tk), :]
            s = lax.dot_general(qh, kj, (((1,), (1,)), ((), ())), preferred_element_type=F32)
            if masked_sub is not None:
                s = jnp.where(col + masked_sub * tk <= row, s, NEG)
            m_new = jnp.maximum(m, jnp.max(s, axis=-1, keepdims=True))
            p = jnp.exp(s - m_new)
            alpha = jnp.exp(m - m_new)
            l = alpha * l + jnp.sum(p, axis=-1, keepdims=True)
            acc = alpha * acc + _dot(p.astype(BF16), vj)
            return m_new, l, acc

        carry = (jnp.full((tq, 1), NEG, F32), jnp.zeros((tq, 1), F32), jnp.zeros((tq, LANES), F32))
        carry = lax.fori_loop(0, qi * n_sub, step, carry)
        for u in range(n_sub):
            carry = step(qi * n_sub + u, carry, masked_sub=u)
        m, l, acc = carry
        outs.append(acc / l)
    olane = lax.broadcasted_iota(jnp.int32, (tq, LANES), 1)
    o_ref[0] = jnp.where(olane < HEAD_DIM, outs[0], outs[1]).astype(BF16)


def _attention(q_aug, k_aug, v, tq, tk):
    B, S, _ = v.shape
    return pl.pallas_call(
        functools.partial(_attn_kernel, tq=tq, tk=tk),
        out_shape=jax.ShapeDtypeStruct((B, S, ATTN_WIDTH), BF16),
        grid=(B, N_PAIRS, S // tq),
        in_specs=[pl.BlockSpec((1, tq, AUG), lambda b, j, i: (b, i, j)),
                  pl.BlockSpec((1, S, AUG), lambda b, j, i: (b, 0, j)),
                  pl.BlockSpec((1, S, LANES), lambda b, j, i: (b, 0, j))],
        out_specs=pl.BlockSpec((1, tq, LANES), lambda b, j, i: (b, i, j)),
        compiler_params=pltpu.CompilerParams(
            dimension_semantics=("arbitrary", "arbitrary", "arbitrary"),
            vmem_limit_bytes=VMEM_LIMIT),
        name="attn",
    )(q_aug, k_aug, v)


def _post_kernel(x_ref, yb_ref, n1_ref, sh1_ref, sc1_ref, g1_ref, n2_ref, sh2_ref, sc2_ref,
                 wc_ref, cw_ref, woc_ref, woa_ref, wo_ref, wr_ref, br_ref,
                 x1_ref, h2_ref, comb_ref, carry_ref):
    tm = x_ref.shape[1]
    x = x_ref[0]
    hb = _modulated_norm(x, n1_ref[...], sh1_ref[0], sc1_ref[0]).astype(BF16)
    pc = _dot(hb, wc_ref[...])
    x_in = pc[:, :CONV_WIDTH]
    conv_b = pc[:, CONV_WIDTH:2 * CONV_WIDTH]
    conv_c = pc[:, 2 * CONV_WIDTH:3 * CONV_WIDTH]
    gate_c = pc[:, 3 * CONV_WIDTH:3 * CONV_WIDTH + D_MODEL]
    gate_a = pc[:, 3 * CONV_WIDTH + D_MODEL:]

    @pl.when(pl.program_id(1) == 0)
    def _():
        carry_ref[...] = jnp.zeros_like(carry_ref)

    u = conv_c * x_in
    prev = carry_ref[...]
    carry_ref[...] = u[tm - 8:, :]
    row8 = lax.broadcasted_iota(jnp.int32, (8, CONV_WIDTH), 0)

    def shifted(k):
        r = pltpu.roll(u, k, 0)
        top = jnp.where(row8 < k, pltpu.roll(prev, k, 0), r[:8])
        return jnp.concatenate([top, r[8:]], axis=0)

    cw = cw_ref[...]
    conv = cw[0:1] * shifted(2) + cw[1:2] * shifted(1) + cw[2:3] * u
    y_a = (conv_b * conv).astype(BF16)
    p_a = _dot(y_a, woc_ref[...])
    p_b = _dot(yb_ref[0], woa_ref[...])
    merged = (_sigmoid(gate_c) * p_a + _sigmoid(gate_a) * p_b).astype(BF16)
    x1 = x + g1_ref[0] * _dot(merged, wo_ref[...])
    x1_ref[0] = x1

    h2 = _modulated_norm(x1, n2_ref[...], sh2_ref[0], sc2_ref[0])
    h2_ref[0] = h2.astype(BF16)

    h_hi = h2.astype(BF16)
    h_lo = (h2 - h_hi.astype(F32)).astype(BF16)
    lg = (_dot(h_hi, wr_ref[0]) + _dot(h_lo, wr_ref[0]) + _dot(h_hi, wr_ref[1])) + br_ref[...]
    lane = lax.broadcasted_iota(jnp.int32, (tm, LANES), 1)
    big = jnp.int32(1 << 20)

    def first_argmax(vals):
        mx = jnp.max(vals, axis=-1, keepdims=True)
        idx = jnp.min(jnp.where(vals == mx, lane, big), axis=-1, keepdims=True)
        return mx, idx

    is_g = (lane >= N_EXPERTS) & (lane < N_EXPERTS + N_GROUPS)
    g_mx, g_lane = first_argmax(jnp.where(is_g, lg, NEG))
    p_sel = 1.0 / jnp.sum(jnp.where(is_g, jnp.exp(lg - g_mx), 0.0), axis=-1, keepdims=True)
    g_idx = g_lane - N_EXPERTS
    in_g = (lane >= g_idx * EXPERTS_PER_GROUP) & (lane < (g_idx + 1) * EXPERTS_PER_GROUP)
    le = jnp.where(in_g, lg, NEG)
    v1, i1 = first_argmax(le)
    v2, i2 = first_argmax(jnp.where(lane == i1, NEG, le))
    e2 = jnp.exp(v2 - v1)
    w1 = p_sel / (1.0 + e2)
    w2 = w1 * e2
    comb_ref[0] = jnp.where(lane == i1, w1, 0.0) + jnp.where(lane == i2, w2, 0.0)


def _post(x, yb, n1, sh1, sc1, g1, n2, sh2, sc2, wc, cw, woc, woa, wo, wr, br, tm):
    B, S, _ = x.shape
    const = lambda *shape: pl.BlockSpec(shape, lambda b, i: (0,) * len(shape))
    perb = pl.BlockSpec((1, 1, D_MODEL), lambda b, i: (b, 0, 0))
    tok = lambda w: pl.BlockSpec((1, tm, w), lambda b, i: (b, i, 0))
    return pl.pallas_call(
        _post_kernel,
        out_shape=(jax.ShapeDtypeStruct((B, S, D_MODEL), F32),
                   jax.ShapeDtypeStruct((B, S, D_MODEL), BF16),
                   jax.ShapeDtypeStruct((B, S, LANES), F32)),
        grid=(B, S // tm),
        in_specs=[tok(D_MODEL), tok(ATTN_WIDTH), const(1, D_MODEL), perb, perb, perb,
                  const(1, D_MODEL), perb, perb,
                  const(D_MODEL, 3 * CONV_WIDTH + 2 * D_MODEL),
                  const(8, CONV_WIDTH),
                  const(CONV_WIDTH, D_MODEL), const(ATTN_WIDTH, D_MODEL),
                  const(D_MODEL, D_MODEL),
                  const(2, D_MODEL, LANES), const(1, LANES)],
        out_specs=(tok(D_MODEL), tok(D_MODEL), tok(LANES)),
        scratch_shapes=[pltpu.VMEM((8, CONV_WIDTH), F32)],
        compiler_params=pltpu.CompilerParams(
            dimension_semantics=("arbitrary", "arbitrary"), vmem_limit_bytes=VMEM_LIMIT),
        name="post",
    )(x, yb, n1, sh1, sc1, g1, n2, sh2, sc2, wc, cw, woc, woa, wo, wr, br)


def _moe_kernel(h_ref, comb_ref, x1_ref, g2_ref, wg_ref, wu_ref, wd_ref, o_ref, acc_ref):
    e = pl.program_id(1)

    @pl.when(e == 0)
    def _():
        acc_ref[...] = jnp.zeros_like(acc_ref)

    h = h_ref[...]
    g = _dot(h, wg_ref[0])
    u = _dot(h, wu_ref[0])
    lane = lax.broadcasted_iota(jnp.int32, comb_ref.shape, 1)
    ce = jnp.sum(jnp.where(lane == e, comb_ref[...], 0.0), axis=-1, keepdims=True)
    a = (g * _sigmoid(g) * u).astype(BF16)
    acc_ref[...] += ce * _dot(a, wd_ref[0])

    @pl.when(e == pl.num_programs(1) - 1)
    def _():
        o_ref[...] = x1_ref[...] + g2_ref[0] * acc_ref[...]


def _moe(h2, comb, x1, g2, wg, wu, wd, tm, S):
    T = h2.shape[0]
    tiles_per_seq = S // tm
    return pl.pallas_call(
        _moe_kernel,
        out_shape=jax.ShapeDtypeStruct((T, D_MODEL), F32),
        grid=(T // tm, N_EXPERTS),
        in_specs=[pl.BlockSpec((tm, D_MODEL), lambda i, e: (i, 0)),
                  pl.BlockSpec((tm, LANES), lambda i, e: (i, 0)),
                  pl.BlockSpec((tm, D_MODEL), lambda i, e: (i, 0)),
                  pl.BlockSpec((1, 1, D_MODEL), lambda i, e: (i // tiles_per_seq, 0, 0)),
                  pl.BlockSpec((1, D_MODEL, D_EXPERT), lambda i, e: (e, 0, 0)),
                  pl.BlockSpec((1, D_MODEL, D_EXPERT), lambda i, e: (e, 0, 0)),
                  pl.BlockSpec((1, D_EXPERT, D_MODEL), lambda i, e: (e, 0, 0))],
        out_specs=pl.BlockSpec((tm, D_MODEL), lambda i, e: (i, 0)),
        scratch_shapes=[pltpu.VMEM((tm, D_MODEL), F32)],
        compiler_params=pltpu.CompilerParams(
            dimension_semantics=("arbitrary", "arbitrary"), vmem_limit_bytes=VMEM_LIMIT),
        name="moe",
    )(h2, comb, x1, g2, wg, wu, wd)


def _pick(n, pref):
    t = min(n, pref)
    assert n % t == 0, (n, t)
    return t


def _layer(x, c, w_ada, b_ada, norm1_w, w_in, b_forget, conv_w, q_norm_w, k_norm_w,
           w_out_conv, w_out_attn, w_o, norm2_w, w_rg, b_rg, w_re, b_re, w_gate, w_up, w_down):
    B, S, _ = x.shape
    mod = _ada(c, w_ada, b_ada.reshape(1, -1)).reshape(B, 6, 1, D_MODEL)
    shift1, scale1, gate1, shift2, scale2, gate2 = (mod[:, t] for t in range(6))

    cuts = np.cumsum([0, CONV_WIDTH, CONV_WIDTH, CONV_WIDTH, ATTN_WIDTH, ATTN_WIDTH, ATTN_WIDTH,
                      N_HEADS, D_MODEL, D_MODEL])
    w_conv3 = w_in[:, cuts[0]:cuts[3]]
    w_qkv = w_in[:, cuts[3]:cuts[6]].astype(BF16)
    w_f = jnp.pad(w_in[:, cuts[6]:cuts[7]], ((0, 0), (0, LANES - N_HEADS))).astype(BF16)
    b_f = jnp.pad(b_forget, (0, LANES - N_HEADS)).reshape(1, LANES)
    w_cgg = jnp.concatenate([w_conv3, w_in[:, cuts[7]:cuts[9]]], axis=1).astype(BF16)

    tm_qkv = _pick(S, 512)
    q_aug, k_aug, v = _qkv(x, norm1_w.reshape(1, -1), shift1, scale1, w_qkv, w_f, b_f,
                           jnp.tile(q_norm_w, N_HEADS).reshape(1, -1),
                           jnp.tile(k_norm_w, N_HEADS).reshape(1, -1), tm_qkv)

    tq = _pick(S, 512)
    y_b = _attention(q_aug, k_aug, v, tq, _pick(tq, 512))

    w_r = jnp.pad(jnp.concatenate([w_re, w_rg], axis=1),
                  ((0, 0), (0, LANES - N_EXPERTS - N_GROUPS)))
    w_r_hi = w_r.astype(BF16)
    w_r_lo = (w_r - w_r_hi.astype(F32)).astype(BF16)
    b_r = jnp.pad(jnp.concatenate([b_re, b_rg]), (0, LANES - N_EXPERTS - N_GROUPS)).reshape(1, LANES)
    cw = jnp.pad(conv_w, ((0, 8 - CONV_K), (0, 0)))
    tm_post = _pick(S, 256)
    x1, h2, comb = _post(x, y_b, norm1_w.reshape(1, -1), shift1, scale1, gate1,
                         norm2_w.reshape(1, -1), shift2, scale2,
                         w_cgg, cw, w_out_conv.astype(BF16), w_out_attn.astype(BF16),
                         w_o.astype(BF16), jnp.stack([w_r_hi, w_r_lo]), b_r, tm_post)

    tm_moe = _pick(S, 1024)
    out = _moe(h2.reshape(B * S, D_MODEL), comb.reshape(B * S, LANES), x1.reshape(B * S, D_MODEL),
               gate2, w_gate.astype(BF16), w_up.astype(BF16), w_down.astype(BF16), tm_moe, S)
    return out.reshape(B, S, D_MODEL)


def kernel(x, c, w_ada, b_ada, norm1_w, w_in, b_forget, conv_w, q_norm_w, k_norm_w, w_out_conv,
           w_out_attn, w_o, norm2_w, w_router_group, b_router_group, w_router_expert,
           b_router_expert, w_gate, w_up, w_down):
    for l in range(w_ada.shape[0]):
        x = _layer(x, c, w_ada[l], b_ada[l], norm1_w[l], w_in[l], b_forget[l], conv_w[l],
                   q_norm_w[l], k_norm_w[l], w_out_conv[l], w_out_attn[l], w_o[l], norm2_w[l],
                   w_router_group[l], b_router_group[l], w_router_expert[l], b_router_expert[l],
                   w_gate[l], w_up[l], w_down[l])
    return x
```

```python
import functools

import jax
import jax.numpy as jnp
import numpy as np
from jax import lax
from jax.experimental import pallas as pl
from jax.experimental.pallas import tpu as pltpu
from jax.experimental.pallas import tpu_sc as plsc

D_MODEL = 1024
CONV_WIDTH = 512
CONV_K = 3
N_HEADS = 8
HEAD_DIM = 64
ATTN_WIDTH = N_HEADS * HEAD_DIM
N_PAIRS = N_HEADS // 2
N_GROUPS = 4
EXPERTS_PER_GROUP = 8
N_EXPERTS = N_GROUPS * EXPERTS_PER_GROUP
D_EXPERT = 256
EPS = 1e-6
LANES = 128
AUG = 2 * LANES
NEG = -1e30

F32 = jnp.float32
BF16 = jnp.bfloat16
VMEM_LIMIT = 56 * 1024 * 1024


def _sigmoid(z):
    return 1.0 / (1.0 + jnp.exp(-z))


def _split3(z):
    hi = z.astype(BF16)
    r = z - hi.astype(F32)
    mid = r.astype(BF16)
    lo = (r - mid.astype(F32)).astype(BF16)
    return hi, mid, lo


def _dot(a, b):
    return jnp.dot(a, b, preferred_element_type=F32)


def _modulated_norm(x, nw, shift, scale):
    ms = jnp.mean(x * x, axis=-1, keepdims=True)
    return (x * lax.rsqrt(ms + EPS) * nw) * (1.0 + scale) + shift


def _ada_kernel(c_ref, w_ref, b_ref, o_ref):
    c = c_ref[...]
    a = c * _sigmoid(c)
    o_ref[...] = jnp.dot(a, w_ref[...], precision=lax.Precision.HIGHEST,
                         preferred_element_type=F32) + b_ref[...]


def _ada(c, w_ada, b_ada):
    B = c.shape[0]
    n = w_ada.shape[1] // D_MODEL
    return pl.pallas_call(
        _ada_kernel,
        out_shape=jax.ShapeDtypeStruct((B, n * D_MODEL), F32),
        grid=(n,),
        in_specs=[pl.BlockSpec((B, D_MODEL), lambda j: (0, 0)),
                  pl.BlockSpec((D_MODEL, D_MODEL), lambda j: (0, j)),
                  pl.BlockSpec((1, D_MODEL), lambda j: (0, j))],
        out_specs=pl.BlockSpec((B, D_MODEL), lambda j: (0, j)),
        compiler_params=pltpu.CompilerParams(dimension_semantics=("arbitrary",)),
        name="ada",
    )(c, w_ada, b_ada)


def _qkv_kernel(x_ref, nw_ref, sh_ref, sc_ref, wqkv_ref, wf_ref, bf_ref, qnw_ref, knw_ref,
                gsum_ref, pq_ref, pk_ref, cq_ref, ck_ref,
                q_ref, k_ref, v_ref, carry_ref):
    tm = x_ref.shape[1]
    h = _modulated_norm(x_ref[0], nw_ref[...], sh_ref[0], sc_ref[0])
    hb = h.astype(BF16)
    qkv = _dot(hb, wqkv_ref[...])
    gsum = gsum_ref[...]

    def head_norm(z, w):
        zz = z * z
        hi = zz.astype(BF16)
        lo = (zz - hi.astype(F32)).astype(BF16)
        ss = _dot(hi, gsum) + _dot(lo, gsum)
        return z * lax.rsqrt(ss * (1.0 / HEAD_DIM) + EPS) * w

    qn = head_norm(qkv[:, :ATTN_WIDTH], qnw_ref[...]) * (HEAD_DIM ** -0.5)
    kn = head_norm(qkv[:, ATTN_WIDTH:2 * ATTN_WIDTH], knw_ref[...])
    v_ref[0] = qkv[:, 2 * ATTN_WIDTH:].astype(BF16)

    fl = _dot(hb, wf_ref[...]) + bf_ref[...]
    lf = jnp.minimum(fl, 0.0) - jnp.log(1.0 + jnp.exp(-jnp.abs(fl)))

    @pl.when(pl.program_id(1) == 0)
    def _():
        carry_ref[...] = jnp.zeros_like(carry_ref)

    row = lax.broadcasted_iota(jnp.int32, (tm, LANES), 0)
    cum = lf
    s = 1
    while s < tm:
        cum = cum + jnp.where(row >= s, pltpu.roll(cum, s, 0), 0.0)
        s *= 2
    cum = cum + carry_ref[7:8, :]
    carry_ref[...] = cum[tm - 8:, :]

    hi, mid, lo = _split3(cum)
    eq = _dot(hi, pq_ref[0]) + _dot(mid, pq_ref[1]) + _dot(lo, pq_ref[2]) + cq_ref[...]
    ek = _dot(hi, pk_ref[0]) + _dot(mid, pk_ref[1]) + _dot(lo, pk_ref[2]) + ck_ref[...]
    for j in range(N_PAIRS):
        q_ref[0, :, j * AUG:j * AUG + LANES] = qn[:, j * LANES:(j + 1) * LANES].astype(BF16)
        q_ref[0, :, j * AUG + LANES:(j + 1) * AUG] = eq[:, j * LANES:(j + 1) * LANES].astype(BF16)
        k_ref[0, :, j * AUG:j * AUG + LANES] = kn[:, j * LANES:(j + 1) * LANES].astype(BF16)
        k_ref[0, :, j * AUG + LANES:(j + 1) * AUG] = ek[:, j * LANES:(j + 1) * LANES].astype(BF16)


def _bias_placement():
    pq = np.zeros((3, LANES, N_PAIRS * LANES), np.float32)
    pk = np.zeros((3, LANES, N_PAIRS * LANES), np.float32)
    cq = np.zeros((1, N_PAIRS * LANES), np.float32)
    ck = np.zeros((1, N_PAIRS * LANES), np.float32)
    for hd in range(N_HEADS):
        base = (hd // 2) * LANES + 6 * (hd % 2)
        for p in range(3):
            pq[p, hd, base + p] = 1.0
            pk[p, hd, base + 3 + p] = -1.0
            cq[0, base + 3 + p] = 1.0
            ck[0, base + p] = 1.0
    return (jnp.asarray(pq, BF16), jnp.asarray(pk, BF16), jnp.asarray(cq), jnp.asarray(ck))


def _qkv(x, nw, shift, scale, wqkv, wf, bf, qnw, knw, tm):
    B, S, _ = x.shape
    gsum = jnp.asarray(np.kron(np.eye(N_HEADS), np.ones((HEAD_DIM, HEAD_DIM))), BF16)
    pq, pk, cq, ck = _bias_placement()
    const = lambda *shape: pl.BlockSpec(shape, lambda b, i: (0,) * len(shape))
    return pl.pallas_call(
        _qkv_kernel,
        out_shape=(jax.ShapeDtypeStruct((B, S, N_PAIRS * AUG), BF16),
                   jax.ShapeDtypeStruct((B, S, N_PAIRS * AUG), BF16),
                   jax.ShapeDtypeStruct((B, S, ATTN_WIDTH), BF16)),
        grid=(B, S // tm),
        in_specs=[pl.BlockSpec((1, tm, D_MODEL), lambda b, i: (b, i, 0)),
                  const(1, D_MODEL),
                  pl.BlockSpec((1, 1, D_MODEL), lambda b, i: (b, 0, 0)),
                  pl.BlockSpec((1, 1, D_MODEL), lambda b, i: (b, 0, 0)),
                  const(D_MODEL, 3 * ATTN_WIDTH),
                  const(D_MODEL, LANES),
                  const(1, LANES),
                  const(1, ATTN_WIDTH),
                  const(1, ATTN_WIDTH),
                  const(ATTN_WIDTH, ATTN_WIDTH),
                  const(3, LANES, N_PAIRS * LANES),
                  const(3, LANES, N_PAIRS * LANES),
                  const(1, N_PAIRS * LANES),
                  const(1, N_PAIRS * LANES)],
        out_specs=(pl.BlockSpec((1, tm, N_PAIRS * AUG), lambda b, i: (b, i, 0)),
                   pl.BlockSpec((1, tm, N_PAIRS * AUG), lambda b, i: (b, i, 0)),
                   pl.BlockSpec((1, tm, ATTN_WIDTH), lambda b, i: (b, i, 0))),
        scratch_shapes=[pltpu.VMEM((8, LANES), F32)],
        compiler_params=pltpu.CompilerParams(
            dimension_semantics=("arbitrary", "arbitrary"), vmem_limit_bytes=VMEM_LIMIT),
        name="qkv",
    )(x, nw, shift, scale, wqkv, wf, bf, qnw, knw, gsum, pq, pk, cq, ck)


def _attn_kernel(q_ref, k_ref, v_ref, o_ref, *, tq, tk):
    qi = pl.program_id(2)
    q = q_ref[0]
    lane = lax.broadcasted_iota(jnp.int32, (tq, AUG), 1)
    row = lax.broadcasted_iota(jnp.int32, (tq, tk), 0)
    col = lax.broadcasted_iota(jnp.int32, (tq, tk), 1)
    n_sub = tq // tk
    outs = []
    for t in range(2):
        keep = ((lane >= t * HEAD_DIM) & (lane < (t + 1) * HEAD_DIM)) | \
               ((lane >= LANES + 6 * t) & (lane < LANES + 6 * (t + 1)))
        qh = jnp.where(keep, q, jnp.zeros_like(q))

        def step(j, carry, masked_sub=None):
            m, l, acc = carry
            start = pl.multiple_of(j * tk, tk)
            kj = k_ref[0, pl.ds(start, tk), :]
            vj = v_ref[0, pl.ds(start, tk), :]
            s = lax.dot_general(qh, kj, (((1,), (1,)), ((), ())), preferred_element_type=F32)
            if masked_sub is not None:
                s = jnp.where(col + masked_sub * tk <= row, s, NEG)
            m_new = jnp.maximum(m, jnp.max(s, axis=-1, keepdims=True))
            p = jnp.exp(s - m_new)
            alpha = jnp.exp(m - m_new)
            l = alpha * l + jnp.sum(p, axis=-1, keepdims=True)
            acc = alpha * acc + _dot(p.astype(BF16), vj)
            return m_new, l, acc

        carry = (jnp.full((tq, 1), NEG, F32), jnp.zeros((tq, 1), F32), jnp.zeros((tq, LANES), F32))
        carry = lax.fori_loop(0, qi * n_sub, step, carry)
        for u in range(n_sub):
            carry = step(qi * n_sub + u, carry, masked_sub=u)
        m, l, acc = carry
        outs.append(acc / l)
    olane = lax.broadcasted_iota(jnp.int32, (tq, LANES), 1)
    o_ref[0] = jnp.where(olane < HEAD_DIM, outs[0], outs[1]).astype(BF16)


def _attention(q_aug, k_aug, v, tq, tk):
    B, S, _ = v.shape
    return pl.pallas_call(
        functools.partial(_attn_kernel, tq=tq, tk=tk),
        out_shape=jax.ShapeDtypeStruct((B, S, ATTN_WIDTH), BF16),
        grid=(B, N_PAIRS, S // tq),
        in_specs=[pl.BlockSpec((1, tq, AUG), lambda b, j, i: (b, i, j)),
                  pl.BlockSpec((1, S, AUG), lambda b, j, i: (b, 0, j)),
                  pl.BlockSpec((1, S, LANES), lambda b, j, i: (b, 0, j))],
        out_specs=pl.BlockSpec((1, tq, LANES), lambda b, j, i: (b, i, j)),
        compiler_params=pltpu.CompilerParams(
            dimension_semantics=("arbitrary", "arbitrary", "arbitrary"),
            vmem_limit_bytes=VMEM_LIMIT),
        name="attn",
    )(q_aug, k_aug, v)


def _pack_bf16_pairs(z):
    w = z.shape[1] // 2
    bits = pltpu.bitcast(z.astype(BF16).astype(F32), jnp.uint32)
    return bits[:, :w] | (bits[:, w:] >> 16)


def _unpack_bf16_pairs(p):
    return (pltpu.bitcast(p & jnp.uint32(0xFFFF0000), F32), pltpu.bitcast(p << 16, F32))


def _post_kernel(x_ref, yb_ref, n1_ref, sh1_ref, sc1_ref, g1_ref, n2_ref, sh2_ref, sc2_ref,
                 wc_ref, cw_ref, woc_ref, woa_ref, wo_ref, wr_ref, br_ref, tri_ref,
                 x1_ref, h2_ref, ridx_ref, rw_ref, cnt_ref, carry_ref):
    tm = x_ref.shape[1]
    x = x_ref[0]
    hb = _modulated_norm(x, n1_ref[...], sh1_ref[0], sc1_ref[0]).astype(BF16)
    pc = _dot(hb, wc_ref[...])
    x_in = pc[:, :CONV_WIDTH]
    conv_b = pc[:, CONV_WIDTH:2 * CONV_WIDTH]
    conv_c = pc[:, 2 * CONV_WIDTH:3 * CONV_WIDTH]
    gate_c = pc[:, 3 * CONV_WIDTH:3 * CONV_WIDTH + D_MODEL]
    gate_a = pc[:, 3 * CONV_WIDTH + D_MODEL:]

    @pl.when(pl.program_id(1) == 0)
    def _():
        carry_ref[...] = jnp.zeros_like(carry_ref)

    u = conv_c * x_in
    prev = carry_ref[...]
    carry_ref[...] = u[tm - 8:, :]
    row8 = lax.broadcasted_iota(jnp.int32, (8, CONV_WIDTH), 0)

    def shifted(k):
        r = pltpu.roll(u, k, 0)
        top = jnp.where(row8 < k, pltpu.roll(prev, k, 0), r[:8])
        return jnp.concatenate([top, r[8:]], axis=0)

    cw = cw_ref[...]
    conv = cw[0:1] * shifted(2) + cw[1:2] * shifted(1) + cw[2:3] * u
    y_a = (conv_b * conv).astype(BF16)
    p_a = _dot(y_a, woc_ref[...])
    p_b = _dot(yb_ref[0], woa_ref[...])
    merged = (_sigmoid(gate_c) * p_a + _sigmoid(gate_a) * p_b).astype(BF16)
    x1 = x + g1_ref[0] * _dot(merged, wo_ref[...])
    x1_ref[0] = x1

    h2 = _modulated_norm(x1, n2_ref[...], sh2_ref[0], sc2_ref[0])
    h2_ref[0] = _pack_bf16_pairs(h2)

    h_hi = h2.astype(BF16)
    h_lo = (h2 - h_hi.astype(F32)).astype(BF16)
    lg = (_dot(h_hi, wr_ref[0]) + _dot(h_lo, wr_ref[0]) + _dot(h_hi, wr_ref[1])) + br_ref[...]
    lane = lax.broadcasted_iota(jnp.int32, (tm, LANES), 1)
    big = jnp.int32(1 << 20)

    def first_argmax(vals):
        mx = jnp.max(vals, axis=-1, keepdims=True)
        idx = jnp.min(jnp.where(vals == mx, lane, big), axis=-1, keepdims=True)
        return mx, idx

    is_g = (lane >= N_EXPERTS) & (lane < N_EXPERTS + N_GROUPS)
    g_mx, g_lane = first_argmax(jnp.where(is_g, lg, NEG))
    p_sel = 1.0 / jnp.sum(jnp.where(is_g, jnp.exp(lg - g_mx), 0.0), axis=-1, keepdims=True)
    g_idx = g_lane - N_EXPERTS
    in_g = (lane >= g_idx * EXPERTS_PER_GROUP) & (lane < (g_idx + 1) * EXPERTS_PER_GROUP)
    le = jnp.where(in_g, lg, NEG)
    v1, i1 = first_argmax(le)
    v2, i2 = first_argmax(jnp.where(lane == i1, NEG, le))
    e2 = jnp.exp(v2 - v1)
    w1 = p_sel / (1.0 + e2)
    w2 = w1 * e2
    rw_ref[0] = jnp.where(lane == 0, w1, 0.0) + jnp.where(lane == 1, w2, 0.0)

    @pl.when((pl.program_id(0) == 0) & (pl.program_id(1) == 0))
    def _():
        cnt_ref[...] = jnp.zeros_like(cnt_ref)

    onehot = jnp.where((lane == i1) | (lane == i2), 1.0, 0.0)
    before = _dot(tri_ref[...], onehot.astype(BF16)) + cnt_ref[0:1, :]
    r1 = jnp.sum(jnp.where(lane == i1, before, 0.0), axis=-1, keepdims=True)
    r2 = jnp.sum(jnp.where(lane == i2, before, 0.0), axis=-1, keepdims=True)
    cnt_ref[...] = cnt_ref[...] + jnp.sum(onehot, axis=0, keepdims=True)
    ridx = (jnp.where(lane == 0, i1, 0) + jnp.where(lane == 1, i2, 0)
            + jnp.where(lane == 2, r1.astype(jnp.int32), 0)
            + jnp.where(lane == 3, r2.astype(jnp.int32), 0))
    ridx_ref[0] = ridx


def _post(x, yb, n1, sh1, sc1, g1, n2, sh2, sc2, wc, cw, woc, woa, wo, wr, br, tm):
    B, S, _ = x.shape
    tri = jnp.asarray(np.tril(np.ones((tm, tm), np.float32), -1), BF16)
    const = lambda *shape: pl.BlockSpec(shape, lambda b, i: (0,) * len(shape))
    perb = pl.BlockSpec((1, 1, D_MODEL), lambda b, i: (b, 0, 0))
    tok = lambda w: pl.BlockSpec((1, tm, w), lambda b, i: (b, i, 0))
    return pl.pallas_call(
        _post_kernel,
        out_shape=(jax.ShapeDtypeStruct((B, S, D_MODEL), F32),
                   jax.ShapeDtypeStruct((B, S, D_MODEL // 2), jnp.uint32),
                   jax.ShapeDtypeStruct((B, S, LANES), jnp.int32),
                   jax.ShapeDtypeStruct((B, S, LANES), F32),
                   jax.ShapeDtypeStruct((8, LANES), F32)),
        grid=(B, S // tm),
        in_specs=[tok(D_MODEL), tok(ATTN_WIDTH), const(1, D_MODEL), perb, perb, perb,
                  const(1, D_MODEL), perb, perb,
                  const(D_MODEL, 3 * CONV_WIDTH + 2 * D_MODEL),
                  const(8, CONV_WIDTH),
                  const(CONV_WIDTH, D_MODEL), const(ATTN_WIDTH, D_MODEL),
                  const(D_MODEL, D_MODEL),
                  const(2, D_MODEL, LANES), const(1, LANES), const(tm, tm)],
        out_specs=(tok(D_MODEL), tok(D_MODEL // 2), tok(LANES), tok(LANES), const(8, LANES)),
        scratch_shapes=[pltpu.VMEM((8, CONV_WIDTH), F32)],
        compiler_params=pltpu.CompilerParams(
            dimension_semantics=("arbitrary", "arbitrary"), vmem_limit_bytes=VMEM_LIMIT),
        name="post",
    )(x, yb, n1, sh1, sc1, g1, n2, sh2, sc2, wc, cw, woc, woa, wo, wr, br, tri)


SC_CORES = 2
SC_SUBCORES = 16
SC_WORKERS = SC_CORES * SC_SUBCORES
SC_CHUNK = 64
ROW_WORDS = D_MODEL // 2


def _sc_mesh():
    return plsc.VectorSubcoreMesh(core_axis_name="c", subcore_axis_name="s",
                                  num_cores=SC_CORES, num_subcores=SC_SUBCORES)


def _dispatch_body(rows_hbm, idx_hbm, xs_hbm, idx_v, rows_v, *, n_chunks):
    wid = lax.axis_index("s") * SC_CORES + lax.axis_index("c")
    pltpu.sync_copy(idx_hbm.at[wid], idx_v)
    base = wid * (n_chunks * SC_CHUNK)

    @pl.loop(0, n_chunks)
    def _(j):
        pltpu.sync_copy(rows_hbm.at[pl.ds(base + j * SC_CHUNK, SC_CHUNK)], rows_v)
        pltpu.sync_copy(rows_v, xs_hbm.at[idx_v.at[2 * j]])
        pltpu.sync_copy(rows_v, xs_hbm.at[idx_v.at[2 * j + 1]])


def _dispatch(rows, idx, n_slots):
    T = rows.shape[0]
    n_chunks = T // (SC_WORKERS * SC_CHUNK)
    return pl.kernel(
        functools.partial(_dispatch_body, n_chunks=n_chunks),
        out_type=jax.ShapeDtypeStruct((n_slots, ROW_WORDS), jnp.uint32),
        mesh=_sc_mesh(),
        scratch_types=[pltpu.VMEM((2 * n_chunks, SC_CHUNK), jnp.int32),
                       pltpu.VMEM((SC_CHUNK, ROW_WORDS), jnp.uint32)],
        name="dispatch",
    )(rows, idx)


def _collect_body(ys_hbm, idx_hbm, g1_hbm, g2_hbm, idx_v, rows_v, *, n_chunks):
    wid = lax.axis_index("s") * SC_CORES + lax.axis_index("c")
    pltpu.sync_copy(idx_hbm.at[wid], idx_v)
    base = wid * (n_chunks * SC_CHUNK)

    @pl.loop(0, n_chunks)
    def _(j):
        dst = pl.ds(base + j * SC_CHUNK, SC_CHUNK)
        pltpu.sync_copy(ys_hbm.at[idx_v.at[2 * j]], rows_v)
        pltpu.sync_copy(rows_v, g1_hbm.at[dst])
        pltpu.sync_copy(ys_hbm.at[idx_v.at[2 * j + 1]], rows_v)
        pltpu.sync_copy(rows_v, g2_hbm.at[dst])


def _collect(ys, idx, T):
    n_chunks = T // (SC_WORKERS * SC_CHUNK)
    out = jax.ShapeDtypeStruct((T, ROW_WORDS), jnp.uint32)
    return pl.kernel(
        functools.partial(_collect_body, n_chunks=n_chunks),
        out_type=(out, out),
        mesh=_sc_mesh(),
        scratch_types=[pltpu.VMEM((2 * n_chunks, SC_CHUNK), jnp.int32),
                       pltpu.VMEM((SC_CHUNK, ROW_WORDS), jnp.uint32)],
        name="collect",
    )(ys, idx)


def _moe_kernel(te_ref, nt_ref, xs_ref, wg_ref, wu_ref, wd_ref, ys_ref):
    @pl.when(pl.program_id(0) < nt_ref[0])
    def _():
        left, right = _unpack_bf16_pairs(xs_ref[...])
        xb = jnp.concatenate([left.astype(BF16), right.astype(BF16)], axis=1)
        g = _dot(xb, wg_ref[0])
        u = _dot(xb, wu_ref[0])
        a = (g * _sigmoid(g) * u).astype(BF16)
        ys_ref[...] = _pack_bf16_pairs(_dot(a, wd_ref[0]))


def _moe(xs, tile_expert, n_tiles, wg, wu, wd, tm):
    n_slots = xs.shape[0]
    row_blk = lambda i, te, nt: (jnp.minimum(i, nt[0] - 1), 0)
    w_blk = lambda i, te, nt: (te[i], 0, 0)
    return pl.pallas_call(
        _moe_kernel,
        out_shape=jax.ShapeDtypeStruct((n_slots, ROW_WORDS), jnp.uint32),
        grid_spec=pltpu.PrefetchScalarGridSpec(
            num_scalar_prefetch=2,
            grid=(n_slots // tm,),
            in_specs=[pl.BlockSpec((tm, ROW_WORDS), row_blk),
                      pl.BlockSpec((1, D_MODEL, D_EXPERT), w_blk),
                      pl.BlockSpec((1, D_MODEL, D_EXPERT), w_blk),
                      pl.BlockSpec((1, D_EXPERT, D_MODEL), w_blk)],
            out_specs=pl.BlockSpec((tm, ROW_WORDS), row_blk)),
        compiler_params=pltpu.CompilerParams(
            dimension_semantics=("arbitrary",), vmem_limit_bytes=VMEM_LIMIT),
        name="moe",
    )(tile_expert, n_tiles, xs, wg, wu, wd)


def _final_kernel(x1_ref, g1_ref, g2_ref, rw_ref, gate_ref, o_ref):
    rw = rw_ref[0]
    w1 = rw[:, 0:1]
    w2 = rw[:, 1:2]
    a_l, a_r = _unpack_bf16_pairs(g1_ref[0])
    b_l, b_r = _unpack_bf16_pairs(g2_ref[0])
    moe = jnp.concatenate([w1 * a_l + w2 * b_l, w1 * a_r + w2 * b_r], axis=1)
    o_ref[0] = x1_ref[0] + gate_ref[0] * moe


def _final(x1, g1, g2, rw, gate2, tm):
    B, S, _ = x1.shape
    tok = lambda w: pl.BlockSpec((1, tm, w), lambda b, i: (b, i, 0))
    return pl.pallas_call(
        _final_kernel,
        out_shape=jax.ShapeDtypeStruct((B, S, D_MODEL), F32),
        grid=(B, S // tm),
        in_specs=[tok(D_MODEL), tok(ROW_WORDS), tok(ROW_WORDS), tok(LANES),
                  pl.BlockSpec((1, 1, D_MODEL), lambda b, i: (b, 0, 0))],
        out_specs=tok(D_MODEL),
        compiler_params=pltpu.CompilerParams(
            dimension_semantics=("arbitrary", "arbitrary"), vmem_limit_bytes=VMEM_LIMIT),
        name="final",
    )(x1, g1, g2, rw, gate2)


def _pick(n, pref):
    t = min(n, pref)
    assert n % t == 0, (n, t)
    return t


def _route_plan(ridx, counts, tm_e, T):
    counts = counts.astype(jnp.int32)
    tiles = (counts + tm_e - 1) // tm_e
    tile_end = jnp.cumsum(tiles)
    offs = (tile_end - tiles) * tm_e
    slot1 = offs[ridx[:, 0]] + ridx[:, 2]
    slot2 = offs[ridx[:, 1]] + ridx[:, 3]
    n_chunks = T // (SC_WORKERS * SC_CHUNK)
    idx = jnp.stack([slot1.reshape(SC_WORKERS, n_chunks, SC_CHUNK),
                     slot2.reshape(SC_WORKERS, n_chunks, SC_CHUNK)], axis=2)
    idx = idx.reshape(SC_WORKERS, 2 * n_chunks, SC_CHUNK)
    n_tiles_max = 2 * T // tm_e + N_EXPERTS
    tile_expert = jnp.searchsorted(tile_end, jnp.arange(n_tiles_max, dtype=jnp.int32), side="right")
    tile_expert = jnp.minimum(tile_expert, N_EXPERTS - 1).astype(jnp.int32)
    return idx, tile_expert, tile_end[-1:].astype(jnp.int32), n_tiles_max * tm_e


def _layer(x, c, w_ada, b_ada, norm1_w, w_in, b_forget, conv_w, q_norm_w, k_norm_w,
           w_out_conv, w_out_attn, w_o, norm2_w, w_rg, b_rg, w_re, b_re, w_gate, w_up, w_down):
    B, S, _ = x.shape
    T = B * S
    assert T % (SC_WORKERS * SC_CHUNK) == 0, T
    mod = _ada(c, w_ada, b_ada.reshape(1, -1)).reshape(B, 6, 1, D_MODEL)
    shift1, scale1, gate1, shift2, scale2, gate2 = (mod[:, t] for t in range(6))

    cuts = np.cumsum([0, CONV_WIDTH, CONV_WIDTH, CONV_WIDTH, ATTN_WIDTH, ATTN_WIDTH, ATTN_WIDTH,
                      N_HEADS, D_MODEL, D_MODEL])
    w_conv3 = w_in[:, cuts[0]:cuts[3]]
    w_qkv = w_in[:, cuts[3]:cuts[6]].astype(BF16)
    w_f = jnp.pad(w_in[:, cuts[6]:cuts[7]], ((0, 0), (0, LANES - N_HEADS))).astype(BF16)
    b_f = jnp.pad(b_forget, (0, LANES - N_HEADS)).reshape(1, LANES)
    w_cgg = jnp.concatenate([w_conv3, w_in[:, cuts[7]:cuts[9]]], axis=1).astype(BF16)

    tm_qkv = _pick(S, 512)
    q_aug, k_aug, v = _qkv(x, norm1_w.reshape(1, -1), shift1, scale1, w_qkv, w_f, b_f,
                           jnp.tile(q_norm_w, N_HEADS).reshape(1, -1),
                           jnp.tile(k_norm_w, N_HEADS).reshape(1, -1), tm_qkv)

    tq = _pick(S, 512)
    y_b = _attention(q_aug, k_aug, v, tq, _pick(tq, 512))

    w_r = jnp.pad(jnp.concatenate([w_re, w_rg], axis=1),
                  ((0, 0), (0, LANES - N_EXPERTS - N_GROUPS)))
    w_r_hi = w_r.astype(BF16)
    w_r_lo = (w_r - w_r_hi.astype(F32)).astype(BF16)
    b_r = jnp.pad(jnp.concatenate([b_re, b_rg]), (0, LANES - N_EXPERTS - N_GROUPS)).reshape(1, LANES)
    cw = jnp.pad(conv_w, ((0, 8 - CONV_K), (0, 0)))
    tm_post = _pick(S, 256)
    x1, h2p, ridx, rw, counts = _post(x, y_b, norm1_w.reshape(1, -1), shift1, scale1, gate1,
                                      norm2_w.reshape(1, -1), shift2, scale2,
                                      w_cgg, cw, w_out_conv.astype(BF16), w_out_attn.astype(BF16),
                                      w_o.astype(BF16), jnp.stack([w_r_hi, w_r_lo]), b_r, tm_post)

    tm_e = 256
    idx, tile_expert, n_tiles, n_slots = _route_plan(
        ridx.reshape(T, LANES)[:, :4], counts[0, :N_EXPERTS], tm_e, T)
    xs = _dispatch(h2p.reshape(T, ROW_WORDS), idx, n_slots)
    ys = _moe(xs, tile_expert, n_tiles, w_gate.astype(BF16), w_up.astype(BF16),
              w_down.astype(BF16), tm_e)
    g1, g2 = _collect(ys, idx, T)
    return _final(x1, g1.reshape(B, S, ROW_WORDS), g2.reshape(B, S, ROW_WORDS), rw, gate2,
                  _pick(S, 512))


def kernel(x, c, w_ada, b_ada, norm1_w, w_in, b_forget, conv_w, q_norm_w, k_norm_w, w_out_conv,
           w_out_attn, w_o, norm2_w, w_router_group, b_router_group, w_router_expert,
           b_router_expert, w_gate, w_up, w_down):
    for l in range(w_ada.shape[0]):
        x = _layer(x, c, w_ada[l], b_ada[l], norm1_w[l], w_in[l], b_forget[l], conv_w[l],
                   q_norm_w[l], k_norm_w[l], w_out_conv[l], w_out_attn[l], w_o[l], norm2_w[l],
                   w_router_group[l], b_router_group[l], w_router_expert[l], b_router_expert[l],
                   w_gate[l], w_up[l], w_down[l])
    return x
```

```python
import functools

import jax
import jax.numpy as jnp
import numpy as np
from jax import lax
from jax.experimental import pallas as pl
from jax.experimental.pallas import tpu as pltpu
from jax.experimental.pallas import tpu_sc as plsc

D_MODEL = 1024
CONV_WIDTH = 512
CONV_K = 3
N_HEADS = 8
HEAD_DIM = 64
ATTN_WIDTH = N_HEADS * HEAD_DIM
N_PAIRS = N_HEADS // 2
N_GROUPS = 4
EXPERTS_PER_GROUP = 8
N_EXPERTS = N_GROUPS * EXPERTS_PER_GROUP
D_EXPERT = 256
EPS = 1e-6
LANES = 128
AUG = 2 * LANES
VROWS = HEAD_DIM + 16
NEG = -1e30
LOG2E = 1.4426950408889634

F32 = jnp.float32
BF16 = jnp.bfloat16
VMEM_LIMIT = 56 * 1024 * 1024


def _sigmoid(z):
    return 1.0 / (1.0 + jnp.exp(-z))


def _split3(z):
    hi = z.astype(BF16)
    r = z - hi.astype(F32)
    mid = r.astype(BF16)
    lo = (r - mid.astype(F32)).astype(BF16)
    return hi, mid, lo


def _dot(a, b):
    return jnp.dot(a, b, preferred_element_type=F32)


def _modulated_norm(x, nw, shift, scale):
    ms = jnp.mean(x * x, axis=-1, keepdims=True)
    return (x * lax.rsqrt(ms + EPS) * nw) * (1.0 + scale) + shift


def _ada_kernel(c_ref, w_ref, b_ref, o_ref):
    c = c_ref[...]
    a = c * _sigmoid(c)
    o_ref[...] = jnp.dot(a, w_ref[...], precision=lax.Precision.HIGHEST,
                         preferred_element_type=F32) + b_ref[...]


def _ada(c, w_ada, b_ada):
    B = c.shape[0]
    n = w_ada.shape[1] // D_MODEL
    return pl.pallas_call(
        _ada_kernel,
        out_shape=jax.ShapeDtypeStruct((B, n * D_MODEL), F32),
        grid=(n,),
        in_specs=[pl.BlockSpec((B, D_MODEL), lambda j: (0, 0)),
                  pl.BlockSpec((D_MODEL, D_MODEL), lambda j: (0, j)),
                  pl.BlockSpec((1, D_MODEL), lambda j: (0, j))],
        out_specs=pl.BlockSpec((B, D_MODEL), lambda j: (0, j)),
        compiler_params=pltpu.CompilerParams(dimension_semantics=("arbitrary",)),
        name="ada",
    )(c, w_ada, b_ada)


_NT = (((1,), (1,)), ((), ()))


def _lane_tile(a, width):
    return jnp.concatenate([a] * (width // a.shape[1]), axis=1)


def _qkv_kernel(x_ref, nw_ref, sh_ref, sc_ref, wqv_ref, wk_ref, wf_ref, bf_ref, qnw_ref, knw_ref,
                gsum_ref, pq_ref, pk_ref, cq_ref, ck_ref,
                qT_ref, k_ref, vT_ref, carry_ref):
    tm = x_ref.shape[1]
    h = _modulated_norm(x_ref[0], nw_ref[...], sh_ref[0], sc_ref[0])
    hb = h.astype(BF16)
    qvT = lax.dot_general(wqv_ref[...], hb, _NT, preferred_element_type=F32)
    k = _dot(hb, wk_ref[...])

    heads = []
    for hd in range(N_HEADS):
        z = qvT[hd * HEAD_DIM:(hd + 1) * HEAD_DIM]
        heads.append(z * lax.rsqrt(jnp.mean(z * z, axis=0, keepdims=True) + EPS))
    qnT = jnp.concatenate(heads, axis=0) * _lane_tile(qnw_ref[...], tm)

    kk = k * k
    kk_hi = kk.astype(BF16)
    kk_lo = (kk - kk_hi.astype(F32)).astype(BF16)
    ss = _dot(kk_hi, gsum_ref[...]) + _dot(kk_lo, gsum_ref[...])
    kn = k * lax.rsqrt(ss * (1.0 / HEAD_DIM) + EPS) * knw_ref[...]

    fl = _dot(hb, wf_ref[...]) + bf_ref[...]
    lf = jnp.minimum(fl, 0.0) - jnp.log(1.0 + jnp.exp(-jnp.abs(fl)))

    @pl.when(pl.program_id(1) == 0)
    def _():
        carry_ref[...] = jnp.zeros_like(carry_ref)

    row = lax.broadcasted_iota(jnp.int32, (tm, LANES), 0)
    cum = lf
    s = 1
    while s < tm:
        cum = cum + jnp.where(row >= s, pltpu.roll(cum, s, 0), 0.0)
        s *= 2
    cum = cum + carry_ref[7:8, :]
    carry_ref[...] = cum[tm - 8:, :]

    hi, mid, lo = _split3(cum * LOG2E)
    eqT = (lax.dot_general(pq_ref[0], hi, _NT, preferred_element_type=F32)
           + lax.dot_general(pq_ref[1], mid, _NT, preferred_element_type=F32)
           + lax.dot_general(pq_ref[2], lo, _NT, preferred_element_type=F32)
           + _lane_tile(cq_ref[...], tm))
    ek = _dot(hi, pk_ref[0]) + _dot(mid, pk_ref[1]) + _dot(lo, pk_ref[2]) + ck_ref[...]
    for j in range(N_PAIRS):
        qT_ref[0, j, :LANES, :] = qnT[j * LANES:(j + 1) * LANES].astype(BF16)
        qT_ref[0, j, LANES:, :] = eqT[j * LANES:(j + 1) * LANES].astype(BF16)
        for t in range(2):
            r0 = ATTN_WIDTH + (2 * j + t) * HEAD_DIM
            vT_ref[0, j, t * VROWS:t * VROWS + HEAD_DIM, :] = qvT[r0:r0 + HEAD_DIM].astype(BF16)
            vT_ref[0, j, t * VROWS + HEAD_DIM:(t + 1) * VROWS, :] = jnp.ones((VROWS - HEAD_DIM, tm), BF16)
        k_ref[0, :, j * AUG:j * AUG + LANES] = kn[:, j * LANES:(j + 1) * LANES].astype(BF16)
        k_ref[0, :, j * AUG + LANES:(j + 1) * AUG] = ek[:, j * LANES:(j + 1) * LANES].astype(BF16)


def _bias_placement():
    pq = np.zeros((3, N_PAIRS * LANES, LANES), np.float32)
    pk = np.zeros((3, LANES, N_PAIRS * LANES), np.float32)
    cq = np.zeros((N_PAIRS * LANES, LANES), np.float32)
    ck = np.zeros((1, N_PAIRS * LANES), np.float32)
    for hd in range(N_HEADS):
        base = (hd // 2) * LANES + 6 * (hd % 2)
        for p in range(3):
            pq[p, base + p, hd] = 1.0
            pk[p, hd, base + 3 + p] = -1.0
            cq[base + 3 + p, :] = 1.0
            ck[0, base + p] = 1.0
    return (jnp.asarray(pq, BF16), jnp.asarray(pk, BF16), jnp.asarray(cq), jnp.asarray(ck))


def _qkv(x, nw, shift, scale, wqvT, wk, wf, bf, qnwT, knw, tm):
    B, S, _ = x.shape
    gsum = jnp.asarray(np.kron(np.eye(N_HEADS), np.ones((HEAD_DIM, HEAD_DIM))), BF16)
    pq, pk, cq, ck = _bias_placement()
    const = lambda *shape: pl.BlockSpec(shape, lambda b, i: (0,) * len(shape))
    return pl.pallas_call(
        _qkv_kernel,
        out_shape=(jax.ShapeDtypeStruct((B, N_PAIRS, AUG, S), BF16),
                   jax.ShapeDtypeStruct((B, S, N_PAIRS * AUG), BF16),
                   jax.ShapeDtypeStruct((B, N_PAIRS, 2 * VROWS, S), BF16)),
        grid=(B, S // tm),
        in_specs=[pl.BlockSpec((1, tm, D_MODEL), lambda b, i: (b, i, 0)),
                  const(1, D_MODEL),
                  pl.BlockSpec((1, 1, D_MODEL), lambda b, i: (b, 0, 0)),
                  pl.BlockSpec((1, 1, D_MODEL), lambda b, i: (b, 0, 0)),
                  const(2 * ATTN_WIDTH, D_MODEL),
                  const(D_MODEL, ATTN_WIDTH),
                  const(D_MODEL, LANES),
                  const(1, LANES),
                  const(ATTN_WIDTH, LANES),
                  const(1, ATTN_WIDTH),
                  const(ATTN_WIDTH, ATTN_WIDTH),
                  const(3, N_PAIRS * LANES, LANES),
                  const(3, LANES, N_PAIRS * LANES),
                  const(N_PAIRS * LANES, LANES),
                  const(1, N_PAIRS * LANES)],
        out_specs=(pl.BlockSpec((1, N_PAIRS, AUG, tm), lambda b, i: (b, 0, 0, i)),
                   pl.BlockSpec((1, tm, N_PAIRS * AUG), lambda b, i: (b, i, 0)),
                   pl.BlockSpec((1, N_PAIRS, 2 * VROWS, tm), lambda b, i: (b, 0, 0, i))),
        scratch_shapes=[pltpu.VMEM((8, LANES), F32)],
        compiler_params=pltpu.CompilerParams(
            dimension_semantics=("arbitrary", "arbitrary"), vmem_limit_bytes=VMEM_LIMIT),
        name="qkv",
    )(x, nw, shift, scale, wqvT, wk, wf, bf, qnwT, knw, gsum, pq, pk, cq, ck)


def _attn_kernel(qT_ref, k_ref, vT_ref, o_ref, qq_ref, s_ref, smax_ref, m_ref, acc_ref, *, tq, tk, cw):
    qi = pl.program_id(2)
    n = tq // cw
    chains = [(t, c) for t in range(2) for c in range(n)]
    qT = qT_ref[0, 0]
    feat = lax.broadcasted_iota(jnp.int32, (AUG, tq), 0)
    for t in range(2):
        keep = ((feat >= t * HEAD_DIM) & (feat < (t + 1) * HEAD_DIM)) | \
               ((feat >= LANES + 6 * t) & (feat < LANES + 6 * (t + 1)))
        qh = jnp.where(keep, qT, jnp.zeros_like(qT))
        for c in range(n):
            qq_ref[t * n + c] = qh[:, c * cw:(c + 1) * cw]
    kpos = lax.broadcasted_iota(jnp.int32, (tk, cw), 0)
    qpos = lax.broadcasted_iota(jnp.int32, (tk, cw), 1)

    def scores(j, slot):
        start = pl.multiple_of(j * tk, tk)
        for ci in range(len(chains)):
            s = _dot(k_ref[0, pl.ds(start, tk), :], qq_ref[ci])
            s_ref[slot, ci] = s
            smax_ref[slot, ci] = jnp.broadcast_to(jnp.max(s, axis=0, keepdims=True), (8, cw))

    def absorb(j, slot, diagonal=False):
        start = pl.multiple_of(j * tk, tk)
        for ci, (t, c) in enumerate(chains):
            vj = vT_ref[0, 0, t * VROWS:(t + 1) * VROWS, pl.ds(start, tk)]
            s = s_ref[slot, ci]
            if diagonal:
                s = jnp.where(kpos <= qpos + c * cw, s, NEG)
                smax = jnp.max(s, axis=0, keepdims=True)
            else:
                smax = smax_ref[slot, ci, 0:1]
            m = m_ref[ci, 0:1]
            m_new = jnp.maximum(m, smax)
            p = jnp.exp2(s - m_new).astype(BF16)
            acc_ref[ci] = jnp.exp2(m - m_new) * acc_ref[ci] + _dot(vj, p)
            m_ref[ci] = jnp.broadcast_to(m_new, (8, cw))

    m_ref[...] = jnp.full(m_ref.shape, NEG, F32)
    acc_ref[...] = jnp.zeros(acc_ref.shape, F32)
    scores(0, 0)

    def two_blocks(jj, _):
        j = 2 * jj
        scores(j + 1, 1)
        absorb(j, 0)
        scores(j + 2, 0)
        absorb(j + 1, 1)
        return 0

    lax.fori_loop(0, qi // 2, two_blocks, 0)
    odd = qi % 2 == 1

    @pl.when(odd)
    def _():
        scores(qi, 1)
        absorb(qi - 1, 0)
        absorb(qi, 1, diagonal=True)

    @pl.when(jnp.logical_not(odd))
    def _():
        absorb(qi, 0, diagonal=True)

    outs = [acc_ref[ci, :HEAD_DIM] / acc_ref[ci, HEAD_DIM:HEAD_DIM + 1] for ci in range(len(chains))]
    oT = jnp.concatenate([jnp.concatenate(outs[:n], axis=1), jnp.concatenate(outs[n:], axis=1)], axis=0)
    o_ref[0] = oT.T.astype(BF16)


def _attention(qT_aug, k_aug, vT, tq, cw):
    B, S, _ = k_aug.shape
    n_chains = 2 * tq // cw
    return pl.pallas_call(
        functools.partial(_attn_kernel, tq=tq, tk=tq, cw=cw),
        scratch_shapes=[pltpu.VMEM((n_chains, AUG, cw), BF16),
                        pltpu.VMEM((2, n_chains, tq, cw), F32),
                        pltpu.VMEM((2, n_chains, 8, cw), F32),
                        pltpu.VMEM((n_chains, 8, cw), F32),
                        pltpu.VMEM((n_chains, VROWS, cw), F32)],
        out_shape=jax.ShapeDtypeStruct((B, S, ATTN_WIDTH), BF16),
        grid=(B, N_PAIRS, S // tq),
        in_specs=[pl.BlockSpec((1, 1, AUG, tq), lambda b, j, i: (b, j, 0, i)),
                  pl.BlockSpec((1, S, AUG), lambda b, j, i: (b, 0, j)),
                  pl.BlockSpec((1, 1, 2 * VROWS, S), lambda b, j, i: (b, j, 0, 0))],
        out_specs=pl.BlockSpec((1, tq, LANES), lambda b, j, i: (b, i, j)),
        compiler_params=pltpu.CompilerParams(
            dimension_semantics=("arbitrary", "arbitrary", "arbitrary"),
            vmem_limit_bytes=VMEM_LIMIT),
        name="attn",
    )(qT_aug, k_aug, vT)


def _pack_bf16_pairs(z):
    w = z.shape[1] // 2
    bits = pltpu.bitcast(z.astype(BF16).astype(F32), jnp.uint32)
    return bits[:, :w] | (bits[:, w:] >> 16)


def _unpack_bf16_pairs(p):
    return (pltpu.bitcast(p & jnp.uint32(0xFFFF0000), F32), pltpu.bitcast(p << 16, F32))


def _post_kernel(x_ref, yb_ref, n1_ref, sh1_ref, sc1_ref, g1_ref, n2_ref, sh2_ref, sc2_ref,
                 wc_ref, cw_ref, woc_ref, woa_ref, wo_ref, wr_ref, br_ref, tri_ref,
                 x1_ref, h2_ref, ridx_ref, rw_ref, cnt_ref, carry_ref):
    tm = x_ref.shape[1]
    x = x_ref[0]
    hb = _modulated_norm(x, n1_ref[...], sh1_ref[0], sc1_ref[0]).astype(BF16)
    pc = _dot(hb, wc_ref[...])
    x_in = pc[:, :CONV_WIDTH]
    conv_b = pc[:, CONV_WIDTH:2 * CONV_WIDTH]
    conv_c = pc[:, 2 * CONV_WIDTH:3 * CONV_WIDTH]
    gate_c = pc[:, 3 * CONV_WIDTH:3 * CONV_WIDTH + D_MODEL]
    gate_a = pc[:, 3 * CONV_WIDTH + D_MODEL:]

    @pl.when(pl.program_id(1) == 0)
    def _():
        carry_ref[...] = jnp.zeros_like(carry_ref)

    u = conv_c * x_in
    prev = carry_ref[...]
    carry_ref[...] = u[tm - 8:, :]
    row8 = lax.broadcasted_iota(jnp.int32, (8, CONV_WIDTH), 0)

    def shifted(k):
        r = pltpu.roll(u, k, 0)
        top = jnp.where(row8 < k, pltpu.roll(prev, k, 0), r[:8])
        return jnp.concatenate([top, r[8:]], axis=0)

    cw = cw_ref[...]
    conv = cw[0:1] * shifted(2) + cw[1:2] * shifted(1) + cw[2:3] * u
    y_a = (conv_b * conv).astype(BF16)
    p_a = _dot(y_a, woc_ref[...])
    p_b = _dot(yb_ref[0], woa_ref[...])
    merged = (_sigmoid(gate_c) * p_a + _sigmoid(gate_a) * p_b).astype(BF16)
    x1 = x + g1_ref[0] * _dot(merged, wo_ref[...])
    x1_ref[0] = x1

    h2 = _modulated_norm(x1, n2_ref[...], sh2_ref[0], sc2_ref[0])
    h2_ref[0] = _pack_bf16_pairs(h2)

    h_hi = h2.astype(BF16)
    h_lo = (h2 - h_hi.astype(F32)).astype(BF16)
    lg = (_dot(h_hi, wr_ref[0]) + _dot(h_lo, wr_ref[0]) + _dot(h_hi, wr_ref[1])) + br_ref[...]
    lane = lax.broadcasted_iota(jnp.int32, (tm, LANES), 1)
    big = jnp.int32(1 << 20)

    def first_argmax(vals):
        mx = jnp.max(vals, axis=-1, keepdims=True)
        idx = jnp.min(jnp.where(vals == mx, lane, big), axis=-1, keepdims=True)
        return mx, idx

    is_g = (lane >= N_EXPERTS) & (lane < N_EXPERTS + N_GROUPS)
    g_mx, g_lane = first_argmax(jnp.where(is_g, lg, NEG))
    p_sel = 1.0 / jnp.sum(jnp.where(is_g, jnp.exp(lg - g_mx), 0.0), axis=-1, keepdims=True)
    g_idx = g_lane - N_EXPERTS
    in_g = (lane >= g_idx * EXPERTS_PER_GROUP) & (lane < (g_idx + 1) * EXPERTS_PER_GROUP)
    le = jnp.where(in_g, lg, NEG)
    v1, i1 = first_argmax(le)
    v2, i2 = first_argmax(jnp.where(lane == i1, NEG, le))
    e2 = jnp.exp(v2 - v1)
    w1 = p_sel / (1.0 + e2)
    w2 = w1 * e2
    rw_ref[0] = jnp.where(lane == 0, w1, 0.0) + jnp.where(lane == 1, w2, 0.0)

    @pl.when((pl.program_id(0) == 0) & (pl.program_id(1) == 0))
    def _():
        cnt_ref[...] = jnp.zeros_like(cnt_ref)

    onehot = jnp.where((lane == i1) | (lane == i2), 1.0, 0.0)
    before = _dot(tri_ref[...], onehot.astype(BF16)) + cnt_ref[0:1, :]
    r1 = jnp.sum(jnp.where(lane == i1, before, 0.0), axis=-1, keepdims=True)
    r2 = jnp.sum(jnp.where(lane == i2, before, 0.0), axis=-1, keepdims=True)
    cnt_ref[...] = cnt_ref[...] + jnp.sum(onehot, axis=0, keepdims=True)
    ridx = (jnp.where(lane == 0, i1, 0) + jnp.where(lane == 1, i2, 0)
            + jnp.where(lane == 2, r1.astype(jnp.int32), 0)
            + jnp.where(lane == 3, r2.astype(jnp.int32), 0))
    ridx_ref[0] = ridx


def _post(x, yb, n1, sh1, sc1, g1, n2, sh2, sc2, wc, cw, woc, woa, wo, wr, br, tm):
    B, S, _ = x.shape
    tri = jnp.asarray(np.tril(np.ones((tm, tm), np.float32), -1), BF16)
    const = lambda *shape: pl.BlockSpec(shape, lambda b, i: (0,) * len(shape))
    perb = pl.BlockSpec((1, 1, D_MODEL), lambda b, i: (b, 0, 0))
    tok = lambda w: pl.BlockSpec((1, tm, w), lambda b, i: (b, i, 0))
    return pl.pallas_call(
        _post_kernel,
        out_shape=(jax.ShapeDtypeStruct((B, S, D_MODEL), F32),
                   jax.ShapeDtypeStruct((B, S, D_MODEL // 2), jnp.uint32),
                   jax.ShapeDtypeStruct((B, S, LANES), jnp.int32),
                   jax.ShapeDtypeStruct((B, S, LANES), F32),
                   jax.ShapeDtypeStruct((8, LANES), F32)),
        grid=(B, S // tm),
        in_specs=[tok(D_MODEL), tok(ATTN_WIDTH), const(1, D_MODEL), perb, perb, perb,
                  const(1, D_MODEL), perb, perb,
                  const(D_MODEL, 3 * CONV_WIDTH + 2 * D_MODEL),
                  const(8, CONV_WIDTH),
                  const(CONV_WIDTH, D_MODEL), const(ATTN_WIDTH, D_MODEL),
                  const(D_MODEL, D_MODEL),
                  const(2, D_MODEL, LANES), const(1, LANES), const(tm, tm)],
        out_specs=(tok(D_MODEL), tok(D_MODEL // 2), tok(LANES), tok(LANES), const(8, LANES)),
        scratch_shapes=[pltpu.VMEM((8, CONV_WIDTH), F32)],
        compiler_params=pltpu.CompilerParams(
            dimension_semantics=("arbitrary", "arbitrary"), vmem_limit_bytes=VMEM_LIMIT),
        name="post",
    )(x, yb, n1, sh1, sc1, g1, n2, sh2, sc2, wc, cw, woc, woa, wo, wr, br, tri)


SC_CORES = 2
SC_SUBCORES = 16
SC_WORKERS = SC_CORES * SC_SUBCORES
SC_CHUNK = 64
ROW_WORDS = D_MODEL // 2


def _sc_mesh():
    return plsc.VectorSubcoreMesh(core_axis_name="c", subcore_axis_name="s",
                                  num_cores=SC_CORES, num_subcores=SC_SUBCORES)


def _dispatch_body(rows_hbm, idx_hbm, xs_hbm, idx_v, rows_v, *, n_chunks):
    wid = lax.axis_index("s") * SC_CORES + lax.axis_index("c")
    pltpu.sync_copy(idx_hbm.at[wid], idx_v)
    base = wid * (n_chunks * SC_CHUNK)

    @pl.loop(0, n_chunks)
    def _(j):
        pltpu.sync_copy(rows_hbm.at[pl.ds(base + j * SC_CHUNK, SC_CHUNK)], rows_v)
        pltpu.sync_copy(rows_v, xs_hbm.at[idx_v.at[2 * j]])
        pltpu.sync_copy(rows_v, xs_hbm.at[idx_v.at[2 * j + 1]])


def _dispatch(rows, idx, n_slots):
    T = rows.shape[0]
    n_chunks = T // (SC_WORKERS * SC_CHUNK)
    return pl.kernel(
        functools.partial(_dispatch_body, n_chunks=n_chunks),
        out_type=jax.ShapeDtypeStruct((n_slots, ROW_WORDS), jnp.uint32),
        mesh=_sc_mesh(),
        scratch_types=[pltpu.VMEM((2 * n_chunks, SC_CHUNK), jnp.int32),
                       pltpu.VMEM((SC_CHUNK, ROW_WORDS), jnp.uint32)],
        name="dispatch",
    )(rows, idx)


def _collect_body(ys_hbm, idx_hbm, g1_hbm, g2_hbm, idx_v, rows_v, *, n_chunks):
    wid = lax.axis_index("s") * SC_CORES + lax.axis_index("c")
    pltpu.sync_copy(idx_hbm.at[wid], idx_v)
    base = wid * (n_chunks * SC_CHUNK)

    @pl.loop(0, n_chunks)
    def _(j):
        dst = pl.ds(base + j * SC_CHUNK, SC_CHUNK)
        pltpu.sync_copy(ys_hbm.at[idx_v.at[2 * j]], rows_v)
        pltpu.sync_copy(rows_v, g1_hbm.at[dst])
        pltpu.sync_copy(ys_hbm.at[idx_v.at[2 * j + 1]], rows_v)
        pltpu.sync_copy(rows_v, g2_hbm.at[dst])


def _collect(ys, idx, T):
    n_chunks = T // (SC_WORKERS * SC_CHUNK)
    out = jax.ShapeDtypeStruct((T, ROW_WORDS), jnp.uint32)
    return pl.kernel(
        functools.partial(_collect_body, n_chunks=n_chunks),
        out_type=(out, out),
        mesh=_sc_mesh(),
        scratch_types=[pltpu.VMEM((2 * n_chunks, SC_CHUNK), jnp.int32),
                       pltpu.VMEM((SC_CHUNK, ROW_WORDS), jnp.uint32)],
        name="collect",
    )(ys, idx)


def _moe_kernel(te_ref, nt_ref, xs_ref, wg_ref, wu_ref, wd_ref, ys_ref):
    @pl.when(pl.program_id(0) < nt_ref[0])
    def _():
        left, right = _unpack_bf16_pairs(xs_ref[...])
        xb = jnp.concatenate([left.astype(BF16), right.astype(BF16)], axis=1)
        g = _dot(xb, wg_ref[0])
        u = _dot(xb, wu_ref[0])
        a = (g * _sigmoid(g) * u).astype(BF16)
        ys_ref[...] = _pack_bf16_pairs(_dot(a, wd_ref[0]))


def _moe(xs, tile_expert, n_tiles, wg, wu, wd, tm):
    n_slots = xs.shape[0]
    row_blk = lambda i, te, nt: (jnp.minimum(i, nt[0] - 1), 0)
    w_blk = lambda i, te, nt: (te[i], 0, 0)
    return pl.pallas_call(
        _moe_kernel,
        out_shape=jax.ShapeDtypeStruct((n_slots, ROW_WORDS), jnp.uint32),
        grid_spec=pltpu.PrefetchScalarGridSpec(
            num_scalar_prefetch=2,
            grid=(n_slots // tm,),
            in_specs=[pl.BlockSpec((tm, ROW_WORDS), row_blk),
                      pl.BlockSpec((1, D_MODEL, D_EXPERT), w_blk),
                      pl.BlockSpec((1, D_MODEL, D_EXPERT), w_blk),
                      pl.BlockSpec((1, D_EXPERT, D_MODEL), w_blk)],
            out_specs=pl.BlockSpec((tm, ROW_WORDS), row_blk)),
        compiler_params=pltpu.CompilerParams(
            dimension_semantics=("arbitrary",), vmem_limit_bytes=VMEM_LIMIT),
        name="moe",
    )(tile_expert, n_tiles, xs, wg, wu, wd)


def _final_kernel(x1_ref, g1_ref, g2_ref, rw_ref, gate_ref, o_ref):
    rw = rw_ref[0]
    w1 = rw[:, 0:1]
    w2 = rw[:, 1:2]
    a_l, a_r = _unpack_bf16_pairs(g1_ref[0])
    b_l, b_r = _unpack_bf16_pairs(g2_ref[0])
    moe = jnp.concatenate([w1 * a_l + w2 * b_l, w1 * a_r + w2 * b_r], axis=1)
    o_ref[0] = x1_ref[0] + gate_ref[0] * moe


def _final(x1, g1, g2, rw, gate2, tm):
    B, S, _ = x1.shape
    tok = lambda w: pl.BlockSpec((1, tm, w), lambda b, i: (b, i, 0))
    return pl.pallas_call(
        _final_kernel,
        out_shape=jax.ShapeDtypeStruct((B, S, D_MODEL), F32),
        grid=(B, S // tm),
        in_specs=[tok(D_MODEL), tok(ROW_WORDS), tok(ROW_WORDS), tok(LANES),
                  pl.BlockSpec((1, 1, D_MODEL), lambda b, i: (b, 0, 0))],
        out_specs=tok(D_MODEL),
        compiler_params=pltpu.CompilerParams(
            dimension_semantics=("arbitrary", "arbitrary"), vmem_limit_bytes=VMEM_LIMIT),
        name="final",
    )(x1, g1, g2, rw, gate2)


def _pick(n, pref):
    t = min(n, pref)
    assert n % t == 0, (n, t)
    return t


def _route_plan(ridx, counts, tm_e, T):
    counts = counts.astype(jnp.int32)
    tiles = (counts + tm_e - 1) // tm_e
    tile_end = jnp.cumsum(tiles)
    offs = (tile_end - tiles) * tm_e
    slot1 = offs[ridx[:, 0]] + ridx[:, 2]
    slot2 = offs[ridx[:, 1]] + ridx[:, 3]
    n_chunks = T // (SC_WORKERS * SC_CHUNK)
    idx = jnp.stack([slot1.reshape(SC_WORKERS, n_chunks, SC_CHUNK),
                     slot2.reshape(SC_WORKERS, n_chunks, SC_CHUNK)], axis=2)
    idx = idx.reshape(SC_WORKERS, 2 * n_chunks, SC_CHUNK)
    n_tiles_max = 2 * T // tm_e + N_EXPERTS
    tile_expert = jnp.searchsorted(tile_end, jnp.arange(n_tiles_max, dtype=jnp.int32), side="right")
    tile_expert = jnp.minimum(tile_expert, N_EXPERTS - 1).astype(jnp.int32)
    return idx, tile_expert, tile_end[-1:].astype(jnp.int32), n_tiles_max * tm_e


def _layer(x, c, w_ada, b_ada, norm1_w, w_in, b_forget, conv_w, q_norm_w, k_norm_w,
           w_out_conv, w_out_attn, w_o, norm2_w, w_rg, b_rg, w_re, b_re, w_gate, w_up, w_down):
    B, S, _ = x.shape
    T = B * S
    assert T % (SC_WORKERS * SC_CHUNK) == 0, T
    mod = _ada(c, w_ada, b_ada.reshape(1, -1)).reshape(B, 6, 1, D_MODEL)
    shift1, scale1, gate1, shift2, scale2, gate2 = (mod[:, t] for t in range(6))

    cuts = np.cumsum([0, CONV_WIDTH, CONV_WIDTH, CONV_WIDTH, ATTN_WIDTH, ATTN_WIDTH, ATTN_WIDTH,
                      N_HEADS, D_MODEL, D_MODEL])
    w_conv3 = w_in[:, cuts[0]:cuts[3]]
    w_qvT = jnp.concatenate([w_in[:, cuts[3]:cuts[4]], w_in[:, cuts[5]:cuts[6]]], axis=1).T.astype(BF16)
    w_k = w_in[:, cuts[4]:cuts[5]].astype(BF16)
    w_f = jnp.pad(w_in[:, cuts[6]:cuts[7]], ((0, 0), (0, LANES - N_HEADS))).astype(BF16)
    b_f = jnp.pad(b_forget, (0, LANES - N_HEADS)).reshape(1, LANES)
    w_cgg = jnp.concatenate([w_conv3, w_in[:, cuts[7]:cuts[9]]], axis=1).astype(BF16)

    tm_qkv = _pick(S, 512)
    qnwT = jnp.broadcast_to((jnp.tile(q_norm_w, N_HEADS) * (LOG2E * HEAD_DIM ** -0.5))[:, None],
                            (ATTN_WIDTH, LANES))
    qT_aug, k_aug, vT = _qkv(x, norm1_w.reshape(1, -1), shift1, scale1, w_qvT, w_k, w_f, b_f,
                             qnwT, jnp.tile(k_norm_w, N_HEADS).reshape(1, -1), tm_qkv)

    tq = _pick(S, 512)
    y_b = _attention(qT_aug, k_aug, vT, tq, _pick(tq, 256))

    w_r = jnp.pad(jnp.concatenate([w_re, w_rg], axis=1),
                  ((0, 0), (0, LANES - N_EXPERTS - N_GROUPS)))
    w_r_hi = w_r.astype(BF16)
    w_r_lo = (w_r - w_r_hi.astype(F32)).astype(BF16)
    b_r = jnp.pad(jnp.concatenate([b_re, b_rg]), (0, LANES - N_EXPERTS - N_GROUPS)).reshape(1, LANES)
    cw = jnp.pad(conv_w, ((0, 8 - CONV_K), (0, 0)))
    tm_post = _pick(S, 256)
    x1, h2p, ridx, rw, counts = _post(x, y_b, norm1_w.reshape(1, -1), shift1, scale1, gate1,
                                      norm2_w.reshape(1, -1), shift2, scale2,
                                      w_cgg, cw, w_out_conv.astype(BF16), w_out_attn.astype(BF16),
                                      w_o.astype(BF16), jnp.stack([w_r_hi, w_r_lo]), b_r, tm_post)

    tm_e = 256
    idx, tile_expert, n_tiles, n_slots = _route_plan(
        ridx.reshape(T, LANES)[:, :4], counts[0, :N_EXPERTS], tm_e, T)
    xs = _dispatch(h2p.reshape(T, ROW_WORDS), idx, n_slots)
    ys = _moe(xs, tile_expert, n_tiles, w_gate.astype(BF16), w_up.astype(BF16),
              w_down.astype(BF16), tm_e)
    g1, g2 = _collect(ys, idx, T)
    return _final(x1, g1.reshape(B, S, ROW_WORDS), g2.reshape(B, S, ROW_WORDS), rw, gate2,
                  _pick(S, 512))


def kernel(x, c, w_ada, b_ada, norm1_w, w_in, b_forget, conv_w, q_norm_w, k_norm_w, w_out_conv,
           w_out_attn, w_o, norm2_w, w_router_group, b_router_group, w_router_expert,
           b_router_expert, w_gate, w_up, w_down):
    for l in range(w_ada.shape[0]):
        x = _layer(x, c, w_ada[l], b_ada[l], norm1_w[l], w_in[l], b_forget[l], conv_w[l],
                   q_norm_w[l], k_norm_w[l], w_out_conv[l], w_out_attn[l], w_o[l], norm2_w[l],
                   w_router_group[l], b_router_group[l], w_router_expert[l], b_router_expert[l],
                   w_gate[l], w_up[l], w_down[l])
    return x
```

```python
import functools

import jax
import jax.numpy as jnp
import numpy as np
from jax import lax
from jax.experimental import pallas as pl
from jax.experimental.pallas import tpu as pltpu
from jax.experimental.pallas import tpu_sc as plsc

D_MODEL = 1024
CONV_WIDTH = 512
CONV_K = 3
N_HEADS = 8
HEAD_DIM = 64
ATTN_WIDTH = N_HEADS * HEAD_DIM
N_PAIRS = N_HEADS // 2
N_GROUPS = 4
EXPERTS_PER_GROUP = 8
N_EXPERTS = N_GROUPS * EXPERTS_PER_GROUP
D_EXPERT = 256
EPS = 1e-6
LANES = 128
AUG = 2 * LANES
VROWS = HEAD_DIM + 16
NEG = -1e30
LOG2E = 1.4426950408889634

F32 = jnp.float32
BF16 = jnp.bfloat16
VMEM_LIMIT = 56 * 1024 * 1024


def _sigmoid(z):
    return 1.0 / (1.0 + jnp.exp(-z))


def _split3(z):
    hi = z.astype(BF16)
    r = z - hi.astype(F32)
    mid = r.astype(BF16)
    lo = (r - mid.astype(F32)).astype(BF16)
    return hi, mid, lo


def _dot(a, b):
    return jnp.dot(a, b, preferred_element_type=F32)


def _modulated_norm(x, nw, shift, scale):
    ms = jnp.mean(x * x, axis=-1, keepdims=True)
    return (x * lax.rsqrt(ms + EPS) * nw) * (1.0 + scale) + shift


def _ada_kernel(c_ref, w_ref, b_ref, o_ref):
    c = c_ref[...]
    a = c * _sigmoid(c)
    o_ref[...] = jnp.dot(a, w_ref[...], precision=lax.Precision.HIGHEST,
                         preferred_element_type=F32) + b_ref[...]


def _ada(c, w_ada, b_ada):
    B = c.shape[0]
    n = w_ada.shape[1] // D_MODEL
    return pl.pallas_call(
        _ada_kernel,
        out_shape=jax.ShapeDtypeStruct((B, n * D_MODEL), F32),
        grid=(n,),
        in_specs=[pl.BlockSpec((B, D_MODEL), lambda j: (0, 0)),
                  pl.BlockSpec((D_MODEL, D_MODEL), lambda j: (0, j)),
                  pl.BlockSpec((1, D_MODEL), lambda j: (0, j))],
        out_specs=pl.BlockSpec((B, D_MODEL), lambda j: (0, j)),
        compiler_params=pltpu.CompilerParams(dimension_semantics=("arbitrary",)),
        name="ada",
    )(c, w_ada, b_ada)


_NT = (((1,), (1,)), ((), ()))


def _lane_tile(a, width):
    return jnp.concatenate([a] * (width // a.shape[1]), axis=1)


def _qkv_kernel(x_ref, nw_ref, sh_ref, sc_ref, wqv_ref, wk_ref, wf_ref, bf_ref, qnw_ref, knw_ref,
                gsum_ref, pq_ref, pk_ref, cq_ref, ck_ref,
                qT_ref, k_ref, vT_ref, carry_ref):
    tm = x_ref.shape[1]
    h = _modulated_norm(x_ref[0], nw_ref[...], sh_ref[0], sc_ref[0])
    hb = h.astype(BF16)
    qvT = lax.dot_general(wqv_ref[...], hb, _NT, preferred_element_type=F32)
    k = _dot(hb, wk_ref[...])

    heads = []
    for hd in range(N_HEADS):
        z = qvT[hd * HEAD_DIM:(hd + 1) * HEAD_DIM]
        heads.append(z * lax.rsqrt(jnp.mean(z * z, axis=0, keepdims=True) + EPS))
    qnT = jnp.concatenate(heads, axis=0) * _lane_tile(qnw_ref[...], tm)

    kk = k * k
    kk_hi = kk.astype(BF16)
    kk_lo = (kk - kk_hi.astype(F32)).astype(BF16)
    ss = _dot(kk_hi, gsum_ref[...]) + _dot(kk_lo, gsum_ref[...])
    kn = k * lax.rsqrt(ss * (1.0 / HEAD_DIM) + EPS) * knw_ref[...]

    fl = _dot(hb, wf_ref[...]) + bf_ref[...]
    lf = jnp.minimum(fl, 0.0) - jnp.log(1.0 + jnp.exp(-jnp.abs(fl)))

    @pl.when(pl.program_id(1) == 0)
    def _():
        carry_ref[...] = jnp.zeros_like(carry_ref)

    row = lax.broadcasted_iota(jnp.int32, (tm, LANES), 0)
    cum = lf
    s = 1
    while s < tm:
        cum = cum + jnp.where(row >= s, pltpu.roll(cum, s, 0), 0.0)
        s *= 2
    cum = cum + carry_ref[7:8, :]
    carry_ref[...] = cum[tm - 8:, :]

    hi, mid, lo = _split3(cum * LOG2E)
    eqT = (lax.dot_general(pq_ref[0], hi, _NT, preferred_element_type=F32)
           + lax.dot_general(pq_ref[1], mid, _NT, preferred_element_type=F32)
           + lax.dot_general(pq_ref[2], lo, _NT, preferred_element_type=F32)
           + _lane_tile(cq_ref[...], tm))
    ek = _dot(hi, pk_ref[0]) + _dot(mid, pk_ref[1]) + _dot(lo, pk_ref[2]) + ck_ref[...]
    for j in range(N_PAIRS):
        qT_ref[0, j, :LANES, :] = qnT[j * LANES:(j + 1) * LANES].astype(BF16)
        qT_ref[0, j, LANES:, :] = eqT[j * LANES:(j + 1) * LANES].astype(BF16)
        for t in range(2):
            r0 = ATTN_WIDTH + (2 * j + t) * HEAD_DIM
            vT_ref[0, j, t * VROWS:t * VROWS + HEAD_DIM, :] = qvT[r0:r0 + HEAD_DIM].astype(BF16)
            vT_ref[0, j, t * VROWS + HEAD_DIM:(t + 1) * VROWS, :] = jnp.ones((VROWS - HEAD_DIM, tm), BF16)
        k_ref[0, :, j * AUG:j * AUG + LANES] = kn[:, j * LANES:(j + 1) * LANES].astype(BF16)
        k_ref[0, :, j * AUG + LANES:(j + 1) * AUG] = ek[:, j * LANES:(j + 1) * LANES].astype(BF16)


def _bias_placement():
    pq = np.zeros((3, N_PAIRS * LANES, LANES), np.float32)
    pk = np.zeros((3, LANES, N_PAIRS * LANES), np.float32)
    cq = np.zeros((N_PAIRS * LANES, LANES), np.float32)
    ck = np.zeros((1, N_PAIRS * LANES), np.float32)
    for hd in range(N_HEADS):
        base = (hd // 2) * LANES + 6 * (hd % 2)
        for p in range(3):
            pq[p, base + p, hd] = 1.0
            pk[p, hd, base + 3 + p] = -1.0
            cq[base + 3 + p, :] = 1.0
            ck[0, base + p] = 1.0
    return (jnp.asarray(pq, BF16), jnp.asarray(pk, BF16), jnp.asarray(cq), jnp.asarray(ck))


def _qkv(x, nw, shift, scale, wqvT, wk, wf, bf, qnwT, knw, tm):
    B, S, _ = x.shape
    gsum = jnp.asarray(np.kron(np.eye(N_HEADS), np.ones((HEAD_DIM, HEAD_DIM))), BF16)
    pq, pk, cq, ck = _bias_placement()
    const = lambda *shape: pl.BlockSpec(shape, lambda b, i: (0,) * len(shape),
                                        pipeline_mode=pl.Buffered(1))
    return pl.pallas_call(
        _qkv_kernel,
        out_shape=(jax.ShapeDtypeStruct((B, N_PAIRS, AUG, S), BF16),
                   jax.ShapeDtypeStruct((B, S, N_PAIRS * AUG), BF16),
                   jax.ShapeDtypeStruct((B, N_PAIRS, 2 * VROWS, S), BF16)),
        grid=(B, S // tm),
        in_specs=[pl.BlockSpec((1, tm, D_MODEL), lambda b, i: (b, i, 0)),
                  const(1, D_MODEL),
                  pl.BlockSpec((1, 1, D_MODEL), lambda b, i: (b, 0, 0)),
                  pl.BlockSpec((1, 1, D_MODEL), lambda b, i: (b, 0, 0)),
                  const(2 * ATTN_WIDTH, D_MODEL),
                  const(D_MODEL, ATTN_WIDTH),
                  const(D_MODEL, LANES),
                  const(1, LANES),
                  const(ATTN_WIDTH, LANES),
                  const(1, ATTN_WIDTH),
                  const(ATTN_WIDTH, ATTN_WIDTH),
                  const(3, N_PAIRS * LANES, LANES),
                  const(3, LANES, N_PAIRS * LANES),
                  const(N_PAIRS * LANES, LANES),
                  const(1, N_PAIRS * LANES)],
        out_specs=(pl.BlockSpec((1, N_PAIRS, AUG, tm), lambda b, i: (b, 0, 0, i)),
                   pl.BlockSpec((1, tm, N_PAIRS * AUG), lambda b, i: (b, i, 0)),
                   pl.BlockSpec((1, N_PAIRS, 2 * VROWS, tm), lambda b, i: (b, 0, 0, i))),
        scratch_shapes=[pltpu.VMEM((8, LANES), F32)],
        compiler_params=pltpu.CompilerParams(
            dimension_semantics=("arbitrary", "arbitrary"), vmem_limit_bytes=VMEM_LIMIT),
        name="qkv",
    )(x, nw, shift, scale, wqvT, wk, wf, bf, qnwT, knw, gsum, pq, pk, cq, ck)


def _attn_kernel(qT_ref, k_ref, vT_ref, o_ref, qq_ref, s_ref, smax_ref, m_ref, acc_ref, *, tq, tk, cw):
    qi = pl.program_id(2)
    n = tq // cw
    chains = [(t, c) for t in range(2) for c in range(n)]
    qT = qT_ref[0, 0]
    feat = lax.broadcasted_iota(jnp.int32, (AUG, tq), 0)
    for t in range(2):
        keep = ((feat >= t * HEAD_DIM) & (feat < (t + 1) * HEAD_DIM)) | \
               ((feat >= LANES + 6 * t) & (feat < LANES + 6 * (t + 1)))
        qh = jnp.where(keep, qT, jnp.zeros_like(qT))
        for c in range(n):
            qq_ref[t * n + c] = qh[:, c * cw:(c + 1) * cw]
    kpos = lax.broadcasted_iota(jnp.int32, (tk, cw), 0)
    qpos = lax.broadcasted_iota(jnp.int32, (tk, cw), 1)

    def scores(j, slot):
        start = pl.multiple_of(j * tk, tk)
        for ci in range(len(chains)):
            s = _dot(k_ref[0, pl.ds(start, tk), :], qq_ref[ci])
            s_ref[slot, ci] = s
            smax_ref[slot, ci] = jnp.broadcast_to(jnp.max(s, axis=0, keepdims=True), (8, cw))

    def absorb(j, slot, diagonal=False):
        start = pl.multiple_of(j * tk, tk)
        for ci, (t, c) in enumerate(chains):
            vj = vT_ref[0, 0, t * VROWS:(t + 1) * VROWS, pl.ds(start, tk)]
            s = s_ref[slot, ci]
            if diagonal:
                s = jnp.where(kpos <= qpos + c * cw, s, NEG)
                smax = jnp.max(s, axis=0, keepdims=True)
            else:
                smax = smax_ref[slot, ci, 0:1]
            m = m_ref[ci, 0:1]
            m_new = jnp.maximum(m, smax)
            p = jnp.exp2(s - m_new).astype(BF16)
            acc_ref[ci] = jnp.exp2(m - m_new) * acc_ref[ci] + _dot(vj, p)
            m_ref[ci] = jnp.broadcast_to(m_new, (8, cw))

    m_ref[...] = jnp.full(m_ref.shape, NEG, F32)
    acc_ref[...] = jnp.zeros(acc_ref.shape, F32)
    scores(0, 0)

    def two_blocks(jj, _):
        j = 2 * jj
        scores(j + 1, 1)
        absorb(j, 0)
        scores(j + 2, 0)
        absorb(j + 1, 1)
        return 0

    lax.fori_loop(0, qi // 2, two_blocks, 0)
    odd = qi % 2 == 1

    @pl.when(odd)
    def _():
        scores(qi, 1)
        absorb(qi - 1, 0)
        absorb(qi, 1, diagonal=True)

    @pl.when(jnp.logical_not(odd))
    def _():
        absorb(qi, 0, diagonal=True)

    outs = [acc_ref[ci, :HEAD_DIM] / acc_ref[ci, HEAD_DIM:HEAD_DIM + 1] for ci in range(len(chains))]
    oT = jnp.concatenate([jnp.concatenate(outs[:n], axis=1), jnp.concatenate(outs[n:], axis=1)], axis=0)
    o_ref[0] = oT.T.astype(BF16)


def _attention(qT_aug, k_aug, vT, tq, cw):
    B, S, _ = k_aug.shape
    n_chains = 2 * tq // cw
    return pl.pallas_call(
        functools.partial(_attn_kernel, tq=tq, tk=tq, cw=cw),
        scratch_shapes=[pltpu.VMEM((n_chains, AUG, cw), BF16),
                        pltpu.VMEM((2, n_chains, tq, cw), F32),
                        pltpu.VMEM((2, n_chains, 8, cw), F32),
                        pltpu.VMEM((n_chains, 8, cw), F32),
                        pltpu.VMEM((n_chains, VROWS, cw), F32)],
        out_shape=jax.ShapeDtypeStruct((B, S, ATTN_WIDTH), BF16),
        grid=(B, N_PAIRS, S // tq),
        in_specs=[pl.BlockSpec((1, 1, AUG, tq), lambda b, j, i: (b, j, 0, i)),
                  pl.BlockSpec((1, S, AUG), lambda b, j, i: (b, 0, j)),
                  pl.BlockSpec((1, 1, 2 * VROWS, S), lambda b, j, i: (b, j, 0, 0))],
        out_specs=pl.BlockSpec((1, tq, LANES), lambda b, j, i: (b, i, j)),
        compiler_params=pltpu.CompilerParams(
            dimension_semantics=("arbitrary", "arbitrary", "arbitrary"),
            vmem_limit_bytes=VMEM_LIMIT),
        name="attn",
    )(qT_aug, k_aug, vT)


def _pack_bf16_pairs(z):
    w = z.shape[1] // 2
    bits = pltpu.bitcast(z.astype(BF16).astype(F32), jnp.uint32)
    return bits[:, :w] | (bits[:, w:] >> 16)


def _unpack_bf16_pairs(p):
    return (pltpu.bitcast(p & jnp.uint32(0xFFFF0000), F32), pltpu.bitcast(p << 16, F32))


def _post_kernel(x_ref, yb_ref, n1_ref, sh1_ref, sc1_ref, g1_ref, n2_ref, sh2_ref, sc2_ref,
                 wc_ref, cw_ref, woc_ref, woa_ref, wo_ref, wr_ref, br_ref, tri_ref,
                 x1_ref, h2_ref, ridx_ref, rw_ref, cnt_ref, carry_ref):
    tm = x_ref.shape[1]
    x = x_ref[0]
    hb = _modulated_norm(x, n1_ref[...], sh1_ref[0], sc1_ref[0]).astype(BF16)
    pc = _dot(hb, wc_ref[...])
    x_in = pc[:, :CONV_WIDTH]
    conv_b = pc[:, CONV_WIDTH:2 * CONV_WIDTH]
    conv_c = pc[:, 2 * CONV_WIDTH:3 * CONV_WIDTH]
    gate_c = pc[:, 3 * CONV_WIDTH:3 * CONV_WIDTH + D_MODEL]
    gate_a = pc[:, 3 * CONV_WIDTH + D_MODEL:]

    @pl.when(pl.program_id(1) == 0)
    def _():
        carry_ref[...] = jnp.zeros_like(carry_ref)

    u = conv_c * x_in
    prev = carry_ref[...]
    carry_ref[...] = u[tm - 8:, :]
    row8 = lax.broadcasted_iota(jnp.int32, (8, CONV_WIDTH), 0)

    def shifted(k):
        r = pltpu.roll(u, k, 0)
        top = jnp.where(row8 < k, pltpu.roll(prev, k, 0), r[:8])
        return jnp.concatenate([top, r[8:]], axis=0)

    cw = cw_ref[...]
    conv = cw[0:1] * shifted(2) + cw[1:2] * shifted(1) + cw[2:3] * u
    y_a = (conv_b * conv).astype(BF16)
    p_a = _dot(y_a, woc_ref[...])
    p_b = _dot(yb_ref[0], woa_ref[...])
    merged = (_sigmoid(gate_c) * p_a + _sigmoid(gate_a) * p_b).astype(BF16)
    x1 = x + g1_ref[0] * _dot(merged, wo_ref[...])
    x1_ref[0] = x1

    h2 = _modulated_norm(x1, n2_ref[...], sh2_ref[0], sc2_ref[0])
    h2_ref[0] = _pack_bf16_pairs(h2)

    h_hi = h2.astype(BF16)
    h_lo = (h2 - h_hi.astype(F32)).astype(BF16)
    lg = (_dot(h_hi, wr_ref[0]) + _dot(h_lo, wr_ref[0]) + _dot(h_hi, wr_ref[1])) + br_ref[...]
    lane = lax.broadcasted_iota(jnp.int32, (tm, LANES), 1)
    big = jnp.int32(1 << 20)

    def first_argmax(vals):
        mx = jnp.max(vals, axis=-1, keepdims=True)
        idx = jnp.min(jnp.where(vals == mx, lane, big), axis=-1, keepdims=True)
        return mx, idx

    is_g = (lane >= N_EXPERTS) & (lane < N_EXPERTS + N_GROUPS)
    g_mx, g_lane = first_argmax(jnp.where(is_g, lg, NEG))
    p_sel = 1.0 / jnp.sum(jnp.where(is_g, jnp.exp(lg - g_mx), 0.0), axis=-1, keepdims=True)
    g_idx = g_lane - N_EXPERTS
    in_g = (lane >= g_idx * EXPERTS_PER_GROUP) & (lane < (g_idx + 1) * EXPERTS_PER_GROUP)
    le = jnp.where(in_g, lg, NEG)
    v1, i1 = first_argmax(le)
    v2, i2 = first_argmax(jnp.where(lane == i1, NEG, le))
    e2 = jnp.exp(v2 - v1)
    w1 = p_sel / (1.0 + e2)
    w2 = w1 * e2
    rw_ref[0] = jnp.where(lane == 0, w1, 0.0) + jnp.where(lane == 1, w2, 0.0)

    @pl.when((pl.program_id(0) == 0) & (pl.program_id(1) == 0))
    def _():
        cnt_ref[...] = jnp.zeros_like(cnt_ref)

    onehot = jnp.where((lane == i1) | (lane == i2), 1.0, 0.0)
    before = _dot(tri_ref[...], onehot.astype(BF16)) + cnt_ref[0:1, :]
    r1 = jnp.sum(jnp.where(lane == i1, before, 0.0), axis=-1, keepdims=True)
    r2 = jnp.sum(jnp.where(lane == i2, before, 0.0), axis=-1, keepdims=True)
    cnt_ref[...] = cnt_ref[...] + jnp.sum(onehot, axis=0, keepdims=True)
    ridx = (jnp.where(lane == 0, i1, 0) + jnp.where(lane == 1, i2, 0)
            + jnp.where(lane == 2, r1.astype(jnp.int32), 0)
            + jnp.where(lane == 3, r2.astype(jnp.int32), 0))
    ridx_ref[0] = ridx


def _post(x, yb, n1, sh1, sc1, g1, n2, sh2, sc2, wc, cw, woc, woa, wo, wr, br, tm):
    B, S, _ = x.shape
    tri = jnp.asarray(np.tril(np.ones((tm, tm), np.float32), -1), BF16)
    const = lambda *shape: pl.BlockSpec(shape, lambda b, i: (0,) * len(shape),
                                        pipeline_mode=pl.Buffered(1))
    perb = pl.BlockSpec((1, 1, D_MODEL), lambda b, i: (b, 0, 0))
    tok = lambda w: pl.BlockSpec((1, tm, w), lambda b, i: (b, i, 0))
    return pl.pallas_call(
        _post_kernel,
        out_shape=(jax.ShapeDtypeStruct((B, S, D_MODEL), F32),
                   jax.ShapeDtypeStruct((B, S, D_MODEL // 2), jnp.uint32),
                   jax.ShapeDtypeStruct((B, S, LANES), jnp.int32),
                   jax.ShapeDtypeStruct((B, S, LANES), F32),
                   jax.ShapeDtypeStruct((8, LANES), F32)),
        grid=(B, S // tm),
        in_specs=[tok(D_MODEL), tok(ATTN_WIDTH), const(1, D_MODEL), perb, perb, perb,
                  const(1, D_MODEL), perb, perb,
                  const(D_MODEL, 3 * CONV_WIDTH + 2 * D_MODEL),
                  const(8, CONV_WIDTH),
                  const(CONV_WIDTH, D_MODEL), const(ATTN_WIDTH, D_MODEL),
                  const(D_MODEL, D_MODEL),
                  const(2, D_MODEL, LANES), const(1, LANES), const(tm, tm)],
        out_specs=(tok(D_MODEL), tok(D_MODEL // 2), tok(LANES), tok(LANES),
                   pl.BlockSpec((8, LANES), lambda b, i: (0, 0))),
        scratch_shapes=[pltpu.VMEM((8, CONV_WIDTH), F32)],
        compiler_params=pltpu.CompilerParams(
            dimension_semantics=("arbitrary", "arbitrary"), vmem_limit_bytes=VMEM_LIMIT),
        name="post",
    )(x, yb, n1, sh1, sc1, g1, n2, sh2, sc2, wc, cw, woc, woa, wo, wr, br, tri)


SC_CORES = 2
SC_SUBCORES = 16
SC_WORKERS = SC_CORES * SC_SUBCORES
SC_CHUNK = 64
ROW_WORDS = D_MODEL // 2


def _sc_mesh():
    return plsc.VectorSubcoreMesh(core_axis_name="c", subcore_axis_name="s",
                                  num_cores=SC_CORES, num_subcores=SC_SUBCORES)


def _dispatch_body(rows_hbm, idx_hbm, xs_hbm, idx_v, rows_v, *, n_chunks):
    wid = lax.axis_index("s") * SC_CORES + lax.axis_index("c")
    pltpu.sync_copy(idx_hbm.at[wid], idx_v)
    base = wid * (n_chunks * SC_CHUNK)

    @pl.loop(0, n_chunks)
    def _(j):
        pltpu.sync_copy(rows_hbm.at[pl.ds(base + j * SC_CHUNK, SC_CHUNK)], rows_v)
        pltpu.sync_copy(rows_v, xs_hbm.at[idx_v.at[2 * j]])
        pltpu.sync_copy(rows_v, xs_hbm.at[idx_v.at[2 * j + 1]])


def _dispatch(rows, idx, n_slots):
    T = rows.shape[0]
    n_chunks = T // (SC_WORKERS * SC_CHUNK)
    return pl.kernel(
        functools.partial(_dispatch_body, n_chunks=n_chunks),
        out_type=jax.ShapeDtypeStruct((n_slots, ROW_WORDS), jnp.uint32),
        mesh=_sc_mesh(),
        scratch_types=[pltpu.VMEM((2 * n_chunks, SC_CHUNK), jnp.int32),
                       pltpu.VMEM((SC_CHUNK, ROW_WORDS), jnp.uint32)],
        name="dispatch",
    )(rows, idx)


def _collect_body(ys_hbm, idx_hbm, g1_hbm, g2_hbm, idx_v, rows_v, *, n_chunks):
    wid = lax.axis_index("s") * SC_CORES + lax.axis_index("c")
    pltpu.sync_copy(idx_hbm.at[wid], idx_v)
    base = wid * (n_chunks * SC_CHUNK)

    @pl.loop(0, n_chunks)
    def _(j):
        dst = pl.ds(base + j * SC_CHUNK, SC_CHUNK)
        pltpu.sync_copy(ys_hbm.at[idx_v.at[2 * j]], rows_v)
        pltpu.sync_copy(rows_v, g1_hbm.at[dst])
        pltpu.sync_copy(ys_hbm.at[idx_v.at[2 * j + 1]], rows_v)
        pltpu.sync_copy(rows_v, g2_hbm.at[dst])


def _collect(ys, idx, T):
    n_chunks = T // (SC_WORKERS * SC_CHUNK)
    out = jax.ShapeDtypeStruct((T, ROW_WORDS), jnp.uint32)
    return pl.kernel(
        functools.partial(_collect_body, n_chunks=n_chunks),
        out_type=(out, out),
        mesh=_sc_mesh(),
        scratch_types=[pltpu.VMEM((2 * n_chunks, SC_CHUNK), jnp.int32),
                       pltpu.VMEM((SC_CHUNK, ROW_WORDS), jnp.uint32)],
        name="collect",
    )(ys, idx)


def _moe_kernel(te_ref, nt_ref, xs_ref, wg_ref, wu_ref, wd_ref, ys_ref):
    @pl.when(pl.program_id(0) < nt_ref[0])
    def _():
        left, right = _unpack_bf16_pairs(xs_ref[...])
        xb = jnp.concatenate([left.astype(BF16), right.astype(BF16)], axis=1)
        g = _dot(xb, wg_ref[0])
        u = _dot(xb, wu_ref[0])
        a = (g * _sigmoid(g) * u).astype(BF16)
        ys_ref[...] = _pack_bf16_pairs(_dot(a, wd_ref[0]))


def _moe(xs, tile_expert, n_tiles, wg, wu, wd, tm):
    n_slots = xs.shape[0]
    row_blk = lambda i, te, nt: (jnp.minimum(i, nt[0] - 1), 0)
    w_blk = lambda i, te, nt: (te[i], 0, 0)
    return pl.pallas_call(
        _moe_kernel,
        out_shape=jax.ShapeDtypeStruct((n_slots, ROW_WORDS), jnp.uint32),
        grid_spec=pltpu.PrefetchScalarGridSpec(
            num_scalar_prefetch=2,
            grid=(n_slots // tm,),
            in_specs=[pl.BlockSpec((tm, ROW_WORDS), row_blk),
                      pl.BlockSpec((1, D_MODEL, D_EXPERT), w_blk),
                      pl.BlockSpec((1, D_MODEL, D_EXPERT), w_blk),
                      pl.BlockSpec((1, D_EXPERT, D_MODEL), w_blk)],
            out_specs=pl.BlockSpec((tm, ROW_WORDS), row_blk)),
        compiler_params=pltpu.CompilerParams(
            dimension_semantics=("arbitrary",), vmem_limit_bytes=VMEM_LIMIT),
        name="moe",
    )(tile_expert, n_tiles, xs, wg, wu, wd)


def _final_kernel(x1_ref, g1_ref, g2_ref, rw_ref, gate_ref, o_ref):
    rw = rw_ref[0]
    w1 = rw[:, 0:1]
    w2 = rw[:, 1:2]
    a_l, a_r = _unpack_bf16_pairs(g1_ref[0])
    b_l, b_r = _unpack_bf16_pairs(g2_ref[0])
    moe = jnp.concatenate([w1 * a_l + w2 * b_l, w1 * a_r + w2 * b_r], axis=1)
    o_ref[0] = x1_ref[0] + gate_ref[0] * moe


def _final(x1, g1, g2, rw, gate2, tm):
    B, S, _ = x1.shape
    tok = lambda w: pl.BlockSpec((1, tm, w), lambda b, i: (b, i, 0))
    return pl.pallas_call(
        _final_kernel,
        out_shape=jax.ShapeDtypeStruct((B, S, D_MODEL), F32),
        grid=(B, S // tm),
        in_specs=[tok(D_MODEL), tok(ROW_WORDS), tok(ROW_WORDS), tok(LANES),
                  pl.BlockSpec((1, 1, D_MODEL), lambda b, i: (b, 0, 0))],
        out_specs=tok(D_MODEL),
        compiler_params=pltpu.CompilerParams(
            dimension_semantics=("arbitrary", "arbitrary"), vmem_limit_bytes=VMEM_LIMIT),
        name="final",
    )(x1, g1, g2, rw, gate2)


def _pick(n, pref):
    t = min(n, pref)
    assert n % t == 0, (n, t)
    return t


def _route_plan(ridx, counts, tm_e, T):
    counts = counts.astype(jnp.int32)
    tiles = (counts + tm_e - 1) // tm_e
    tile_end = jnp.cumsum(tiles)
    offs = (tile_end - tiles) * tm_e
    slot1 = offs[ridx[:, 0]] + ridx[:, 2]
    slot2 = offs[ridx[:, 1]] + ridx[:, 3]
    n_chunks = T // (SC_WORKERS * SC_CHUNK)
    idx = jnp.stack([slot1.reshape(SC_WORKERS, n_chunks, SC_CHUNK),
                     slot2.reshape(SC_WORKERS, n_chunks, SC_CHUNK)], axis=2)
    idx = idx.reshape(SC_WORKERS, 2 * n_chunks, SC_CHUNK)
    n_tiles_max = 2 * T // tm_e + N_EXPERTS
    tile_ids = jnp.arange(n_tiles_max, dtype=jnp.int32)
    tile_expert = jnp.sum((tile_end[None, :] <= tile_ids[:, None]).astype(jnp.int32), axis=1)
    tile_expert = jnp.minimum(tile_expert, N_EXPERTS - 1)
    return idx, tile_expert, tile_end[-1:].astype(jnp.int32), n_tiles_max * tm_e


def _layer(x, c, w_ada, b_ada, norm1_w, w_in, b_forget, conv_w, q_norm_w, k_norm_w,
           w_out_conv, w_out_attn, w_o, norm2_w, w_rg, b_rg, w_re, b_re, w_gate, w_up, w_down):
    B, S, _ = x.shape
    T = B * S
    assert T % (SC_WORKERS * SC_CHUNK) == 0, T
    mod = _ada(c, w_ada, b_ada.reshape(1, -1)).reshape(B, 6, 1, D_MODEL)
    shift1, scale1, gate1, shift2, scale2, gate2 = (mod[:, t] for t in range(6))

    cuts = np.cumsum([0, CONV_WIDTH, CONV_WIDTH, CONV_WIDTH, ATTN_WIDTH, ATTN_WIDTH, ATTN_WIDTH,
                      N_HEADS, D_MODEL, D_MODEL])
    w_conv3 = w_in[:, cuts[0]:cuts[3]]
    w_qvT = jnp.concatenate([w_in[:, cuts[3]:cuts[4]], w_in[:, cuts[5]:cuts[6]]], axis=1).T.astype(BF16)
    w_k = w_in[:, cuts[4]:cuts[5]].astype(BF16)
    w_f = jnp.pad(w_in[:, cuts[6]:cuts[7]], ((0, 0), (0, LANES - N_HEADS))).astype(BF16)
    b_f = jnp.pad(b_forget, (0, LANES - N_HEADS)).reshape(1, LANES)
    w_cgg = jnp.concatenate([w_conv3, w_in[:, cuts[7]:cuts[9]]], axis=1).astype(BF16)

    tm_qkv = _pick(S, 512)
    qnwT = jnp.broadcast_to((jnp.tile(q_norm_w, N_HEADS) * (LOG2E * HEAD_DIM ** -0.5))[:, None],
                            (ATTN_WIDTH, LANES))
    qT_aug, k_aug, vT = _qkv(x, norm1_w.reshape(1, -1), shift1, scale1, w_qvT, w_k, w_f, b_f,
                             qnwT, jnp.tile(k_norm_w, N_HEADS).reshape(1, -1), tm_qkv)

    tq = _pick(S, 512)
    y_b = _attention(qT_aug, k_aug, vT, tq, _pick(tq, 256))

    w_r = jnp.pad(jnp.concatenate([w_re, w_rg], axis=1),
                  ((0, 0), (0, LANES - N_EXPERTS - N_GROUPS)))
    w_r_hi = w_r.astype(BF16)
    w_r_lo = (w_r - w_r_hi.astype(F32)).astype(BF16)
    b_r = jnp.pad(jnp.concatenate([b_re, b_rg]), (0, LANES - N_EXPERTS - N_GROUPS)).reshape(1, LANES)
    cw = jnp.pad(conv_w, ((0, 8 - CONV_K), (0, 0)))
    tm_post = _pick(S, 512)
    x1, h2p, ridx, rw, counts = _post(x, y_b, norm1_w.reshape(1, -1), shift1, scale1, gate1,
                                      norm2_w.reshape(1, -1), shift2, scale2,
                                      w_cgg, cw, w_out_conv.astype(BF16), w_out_attn.astype(BF16),
                                      w_o.astype(BF16), jnp.stack([w_r_hi, w_r_lo]), b_r, tm_post)

    tm_e = 256
    idx, tile_expert, n_tiles, n_slots = _route_plan(
        ridx.reshape(T, LANES)[:, :4], counts[0, :N_EXPERTS], tm_e, T)
    xs = _dispatch(h2p.reshape(T, ROW_WORDS), idx, n_slots)
    ys = _moe(xs, tile_expert, n_tiles, w_gate.astype(BF16), w_up.astype(BF16),
              w_down.astype(BF16), tm_e)
    g1, g2 = _collect(ys, idx, T)
    return _final(x1, g1.reshape(B, S, ROW_WORDS), g2.reshape(B, S, ROW_WORDS), rw, gate2,
                  _pick(S, 512))


def kernel(x, c, w_ada, b_ada, norm1_w, w_in, b_forget, conv_w, q_norm_w, k_norm_w, w_out_conv,
           w_out_attn, w_o, norm2_w, w_router_group, b_router_group, w_router_expert,
           b_router_expert, w_gate, w_up, w_down):
    for l in range(w_ada.shape[0]):
        x = _layer(x, c, w_ada[l], b_ada[l], norm1_w[l], w_in[l], b_forget[l], conv_w[l],
                   q_norm_w[l], k_norm_w[l], w_out_conv[l], w_out_attn[l], w_o[l], norm2_w[l],
                   w_router_group[l], b_router_group[l], w_router_expert[l], b_router_expert[l],
                   w_gate[l], w_up[l], w_down[l])
    return x
```

```python
import functools

import jax
import jax.numpy as jnp
import numpy as np
from jax import lax
from jax.experimental import pallas as pl
from jax.experimental.pallas import tpu as pltpu
from jax.experimental.pallas import tpu_sc as plsc

D_MODEL = 1024
CONV_WIDTH = 512
CONV_K = 3
N_HEADS = 8
HEAD_DIM = 64
ATTN_WIDTH = N_HEADS * HEAD_DIM
N_PAIRS = N_HEADS // 2
N_GROUPS = 4
EXPERTS_PER_GROUP = 8
N_EXPERTS = N_GROUPS * EXPERTS_PER_GROUP
D_EXPERT = 256
EPS = 1e-6
LANES = 128
AUG = 2 * LANES
BIAS_W = 6
VROWS = HEAD_DIM + 16
NEG = -1e30
LOG2E = 1.4426950408889634

F32 = jnp.float32
BF16 = jnp.bfloat16
VMEM_LIMIT = 56 * 1024 * 1024


def _sigmoid(z):
    return 1.0 / (1.0 + jnp.exp(-z))


def _split3(z):
    hi = z.astype(BF16)
    r = z - hi.astype(F32)
    mid = r.astype(BF16)
    lo = (r - mid.astype(F32)).astype(BF16)
    return hi, mid, lo


def _dot(a, b):
    return jnp.dot(a, b, preferred_element_type=F32)


def _modulated_norm(x, nw, shift, scale):
    ms = jnp.mean(x * x, axis=-1, keepdims=True)
    return (x * lax.rsqrt(ms + EPS) * nw) * (1.0 + scale) + shift


def _ada_kernel(c_ref, w_ref, b_ref, o_ref):
    c = c_ref[...]
    a = c * _sigmoid(c)
    o_ref[...] = jnp.dot(a, w_ref[...], precision=lax.Precision.HIGHEST,
                         preferred_element_type=F32) + b_ref[...]


def _ada(c, w_ada, b_ada):
    B = c.shape[0]
    n = w_ada.shape[1] // D_MODEL
    return pl.pallas_call(
        _ada_kernel,
        out_shape=jax.ShapeDtypeStruct((B, n * D_MODEL), F32),
        grid=(n,),
        in_specs=[pl.BlockSpec((B, D_MODEL), lambda j: (0, 0)),
                  pl.BlockSpec((D_MODEL, D_MODEL), lambda j: (0, j)),
                  pl.BlockSpec((1, D_MODEL), lambda j: (0, j))],
        out_specs=pl.BlockSpec((B, D_MODEL), lambda j: (0, j)),
        compiler_params=pltpu.CompilerParams(dimension_semantics=("arbitrary",)),
        name="ada",
    )(c, w_ada, b_ada)


_NT = (((1,), (1,)), ((), ()))


def _lane_tile(a, width):
    return jnp.concatenate([a] * (width // a.shape[1]), axis=1)


def _qkv_kernel(x_ref, nw_ref, sh_ref, sc_ref, wqv_ref, wk_ref, wf_ref, bf_ref, qnw_ref, knw_ref,
                gsum_ref, pq_ref, pk_ref, cq_ref, ck_ref,
                qT_ref, k_ref, vT_ref, carry_ref):
    tm = x_ref.shape[1]
    h = _modulated_norm(x_ref[0], nw_ref[...], sh_ref[0], sc_ref[0])
    hb = h.astype(BF16)
    qvT = lax.dot_general(wqv_ref[...], hb, _NT, preferred_element_type=F32)
    k = _dot(hb, wk_ref[...])

    heads = []
    for hd in range(N_HEADS):
        z = qvT[hd * HEAD_DIM:(hd + 1) * HEAD_DIM]
        heads.append(z * lax.rsqrt(jnp.mean(z * z, axis=0, keepdims=True) + EPS))
    qnT = jnp.concatenate(heads, axis=0) * _lane_tile(qnw_ref[...], tm)

    ss = _dot((k * k).astype(BF16), gsum_ref[...])
    kn = k * lax.rsqrt(ss * (1.0 / HEAD_DIM) + EPS) * knw_ref[...]

    fl = _dot(hb, wf_ref[...]) + bf_ref[...]
    lf = jnp.minimum(fl, 0.0) - jnp.log(1.0 + jnp.exp(-jnp.abs(fl)))

    @pl.when(pl.program_id(1) == 0)
    def _():
        carry_ref[...] = jnp.zeros_like(carry_ref)

    row = lax.broadcasted_iota(jnp.int32, (tm, LANES), 0)
    cum = lf
    s = 1
    while s < tm:
        cum = cum + jnp.where(row >= s, pltpu.roll(cum, s, 0), 0.0)
        s *= 2
    cum = cum + carry_ref[7:8, :]
    carry_ref[...] = cum[tm - 8:, :]

    parts = jnp.concatenate(_split3(cum * LOG2E), axis=1)
    eqT = (lax.dot_general(pq_ref[...], parts, _NT, preferred_element_type=F32)
           + _lane_tile(cq_ref[...], tm)).astype(BF16)
    ek = (_dot(parts, pk_ref[...]) + ck_ref[...]).astype(BF16)
    for j in range(N_PAIRS):
        qT_ref[0, j, :LANES, :] = qnT[j * LANES:(j + 1) * LANES].astype(BF16)
        qT_ref[0, j, LANES:, :] = eqT
        for t in range(2):
            r0 = ATTN_WIDTH + (2 * j + t) * HEAD_DIM
            vT_ref[0, j, t * VROWS:t * VROWS + HEAD_DIM, :] = qvT[r0:r0 + HEAD_DIM].astype(BF16)
            vT_ref[0, j, t * VROWS + HEAD_DIM:(t + 1) * VROWS, :] = jnp.ones((VROWS - HEAD_DIM, tm), BF16)
        k_ref[0, :, j * AUG:j * AUG + LANES] = kn[:, j * LANES:(j + 1) * LANES].astype(BF16)
        k_ref[0, :, j * AUG + LANES:(j + 1) * AUG] = ek


def _bias_placement():
    pq = np.zeros((LANES, 3 * LANES), np.float32)
    pk = np.zeros((3 * LANES, LANES), np.float32)
    cq = np.zeros((LANES, LANES), np.float32)
    ck = np.zeros((1, LANES), np.float32)
    for hd in range(N_HEADS):
        base = BIAS_W * hd
        for p in range(3):
            pq[base + p, p * LANES + hd] = 1.0
            pk[p * LANES + hd, base + 3 + p] = -1.0
            cq[base + 3 + p, :] = 1.0
            ck[0, base + p] = 1.0
    return (jnp.asarray(pq, BF16), jnp.asarray(pk, BF16), jnp.asarray(cq), jnp.asarray(ck))


def _qkv(x, nw, shift, scale, wqvT, wk, wf, bf, qnwT, knw, tm):
    B, S, _ = x.shape
    gsum = jnp.asarray(np.kron(np.eye(N_HEADS), np.ones((HEAD_DIM, HEAD_DIM))), BF16)
    pq, pk, cq, ck = _bias_placement()
    const = lambda *shape: pl.BlockSpec(shape, lambda b, i: (0,) * len(shape),
                                        pipeline_mode=pl.Buffered(1))
    return pl.pallas_call(
        _qkv_kernel,
        out_shape=(jax.ShapeDtypeStruct((B, N_PAIRS, AUG, S), BF16),
                   jax.ShapeDtypeStruct((B, S, N_PAIRS * AUG), BF16),
                   jax.ShapeDtypeStruct((B, N_PAIRS, 2 * VROWS, S), BF16)),
        grid=(B, S // tm),
        in_specs=[pl.BlockSpec((1, tm, D_MODEL), lambda b, i: (b, i, 0)),
                  const(1, D_MODEL),
                  pl.BlockSpec((1, 1, D_MODEL), lambda b, i: (b, 0, 0)),
                  pl.BlockSpec((1, 1, D_MODEL), lambda b, i: (b, 0, 0)),
                  const(2 * ATTN_WIDTH, D_MODEL),
                  const(D_MODEL, ATTN_WIDTH),
                  const(D_MODEL, LANES),
                  const(1, LANES),
                  const(ATTN_WIDTH, LANES),
                  const(1, ATTN_WIDTH),
                  const(ATTN_WIDTH, ATTN_WIDTH),
                  const(LANES, 3 * LANES),
                  const(3 * LANES, LANES),
                  const(LANES, LANES),
                  const(1, LANES)],
        out_specs=(pl.BlockSpec((1, N_PAIRS, AUG, tm), lambda b, i: (b, 0, 0, i)),
                   pl.BlockSpec((1, tm, N_PAIRS * AUG), lambda b, i: (b, i, 0)),
                   pl.BlockSpec((1, N_PAIRS, 2 * VROWS, tm), lambda b, i: (b, 0, 0, i))),
        scratch_shapes=[pltpu.VMEM((8, LANES), F32)],
        compiler_params=pltpu.CompilerParams(
            dimension_semantics=("arbitrary", "arbitrary"), vmem_limit_bytes=VMEM_LIMIT),
        name="qkv",
    )(x, nw, shift, scale, wqvT, wk, wf, bf, qnwT, knw, gsum, pq, pk, cq, ck)


def _attn_kernel(qT_ref, k_ref, vT_ref, o_ref, qq_ref, s_ref, smax_ref, m_ref, acc_ref, *, tq, tk, cw):
    qi = pl.program_id(2)
    n = tq // cw
    chains = [(t, c) for t in range(2) for c in range(n)]
    qT = qT_ref[0, 0]
    feat = lax.broadcasted_iota(jnp.int32, (AUG, tq), 0)
    for t in range(2):
        bias0 = LANES + BIAS_W * (2 * pl.program_id(1) + t)
        keep = ((feat >= t * HEAD_DIM) & (feat < (t + 1) * HEAD_DIM)) | \
               ((feat >= bias0) & (feat < bias0 + BIAS_W))
        qh = jnp.where(keep, qT, jnp.zeros_like(qT))
        for c in range(n):
            qq_ref[t * n + c] = qh[:, c * cw:(c + 1) * cw]
    kpos = lax.broadcasted_iota(jnp.int32, (tk, cw), 0)
    qpos = lax.broadcasted_iota(jnp.int32, (tk, cw), 1)

    def scores(j, slot):
        start = pl.multiple_of(j * tk, tk)
        for ci in range(len(chains)):
            s = _dot(k_ref[0, pl.ds(start, tk), :], qq_ref[ci])
            s_ref[slot, ci] = s
            smax_ref[slot, ci] = jnp.broadcast_to(jnp.max(s, axis=0, keepdims=True), (8, cw))

    def absorb(j, slot, diagonal=False):
        start = pl.multiple_of(j * tk, tk)
        for ci, (t, c) in enumerate(chains):
            vj = vT_ref[0, 0, t * VROWS:(t + 1) * VROWS, pl.ds(start, tk)]
            s = s_ref[slot, ci]
            if diagonal:
                s = jnp.where(kpos <= qpos + c * cw, s, NEG)
                smax = jnp.max(s, axis=0, keepdims=True)
            else:
                smax = smax_ref[slot, ci, 0:1]
            m = m_ref[ci, 0:1]
            m_new = jnp.maximum(m, smax)
            p = jnp.exp2(s - m_new).astype(BF16)
            acc_ref[ci] = jnp.exp2(m - m_new) * acc_ref[ci] + _dot(vj, p)
            m_ref[ci] = jnp.broadcast_to(m_new, (8, cw))

    m_ref[...] = jnp.full(m_ref.shape, NEG, F32)
    acc_ref[...] = jnp.zeros(acc_ref.shape, F32)
    scores(0, 0)

    def two_blocks(jj, _):
        j = 2 * jj
        scores(j + 1, 1)
        absorb(j, 0)
        scores(j + 2, 0)
        absorb(j + 1, 1)
        return 0

    lax.fori_loop(0, qi // 2, two_blocks, 0)
    odd = qi % 2 == 1

    @pl.when(odd)
    def _():
        scores(qi, 1)
        absorb(qi - 1, 0)
        absorb(qi, 1, diagonal=True)

    @pl.when(jnp.logical_not(odd))
    def _():
        absorb(qi, 0, diagonal=True)

    outs = [acc_ref[ci, :HEAD_DIM] / acc_ref[ci, HEAD_DIM:HEAD_DIM + 1] for ci in range(len(chains))]
    oT = jnp.concatenate([jnp.concatenate(outs[:n], axis=1), jnp.concatenate(outs[n:], axis=1)], axis=0)
    o_ref[0] = oT.T.astype(BF16)


def _attention(qT_aug, k_aug, vT, tq, cw):
    B, S, _ = k_aug.shape
    n_chains = 2 * tq // cw
    return pl.pallas_call(
        functools.partial(_attn_kernel, tq=tq, tk=tq, cw=cw),
        scratch_shapes=[pltpu.VMEM((n_chains, AUG, cw), BF16),
                        pltpu.VMEM((2, n_chains, tq, cw), F32),
                        pltpu.VMEM((2, n_chains, 8, cw), F32),
                        pltpu.VMEM((n_chains, 8, cw), F32),
                        pltpu.VMEM((n_chains, VROWS, cw), F32)],
        out_shape=jax.ShapeDtypeStruct((B, S, ATTN_WIDTH), BF16),
        grid=(B, N_PAIRS, S // tq),
        in_specs=[pl.BlockSpec((1, 1, AUG, tq), lambda b, j, i: (b, j, 0, i)),
                  pl.BlockSpec((1, S, AUG), lambda b, j, i: (b, 0, j)),
                  pl.BlockSpec((1, 1, 2 * VROWS, S), lambda b, j, i: (b, j, 0, 0))],
        out_specs=pl.BlockSpec((1, tq, LANES), lambda b, j, i: (b, i, j)),
        compiler_params=pltpu.CompilerParams(
            dimension_semantics=("arbitrary", "arbitrary", "arbitrary"),
            vmem_limit_bytes=VMEM_LIMIT),
        name="attn",
    )(qT_aug, k_aug, vT)


def _pack_bf16_pairs(z):
    w = z.shape[1] // 2
    bits = pltpu.bitcast(z.astype(BF16).astype(F32), jnp.uint32)
    return bits[:, :w] | (bits[:, w:] >> 16)


def _unpack_bf16_pairs(p):
    return (pltpu.bitcast(p & jnp.uint32(0xFFFF0000), F32), pltpu.bitcast(p << 16, F32))


POST_ROWS = 256


def _post_kernel(x_ref, yb_ref, n1_ref, sh1_ref, sc1_ref, g1_ref, n2_ref, sh2_ref, sc2_ref,
                 wc_ref, cw_ref, woc_ref, woa_ref, wo_ref, wr_ref, br_ref, tri_ref,
                 x1_ref, h2_ref, ridx_ref, rw_ref, cnt_ref, carry_ref):
    tm = x_ref.shape[1]
    n_grp = max(tm // POST_ROWS, 1)
    rows = tm // n_grp
    lane = lax.broadcasted_iota(jnp.int32, (rows, LANES), 1)
    row8 = lax.broadcasted_iota(jnp.int32, (8, CONV_WIDTH), 0)
    big = jnp.int32(1 << 20)

    @pl.when(pl.program_id(1) == 0)
    def _():
        carry_ref[...] = jnp.zeros_like(carry_ref)

    @pl.when((pl.program_id(0) == 0) & (pl.program_id(1) == 0))
    def _():
        cnt_ref[...] = jnp.zeros_like(cnt_ref)

    st = [dict(rs=pl.ds(g * rows, rows)) for g in range(n_grp)]

    def conv_in(d):
        d["x"] = x_ref[0, d["rs"], :]
        d["hb"] = _modulated_norm(d["x"], n1_ref[...], sh1_ref[0], sc1_ref[0]).astype(BF16)
        d["x_in"] = _dot(d["hb"], wc_ref[:, :CONV_WIDTH])
        d["conv_c"] = _dot(d["hb"], wc_ref[:, 2 * CONV_WIDTH:3 * CONV_WIDTH])
        d["conv_b"] = _dot(d["hb"], wc_ref[:, CONV_WIDTH:2 * CONV_WIDTH])

    def conv(d, prev):
        u = d.pop("conv_c") * d.pop("x_in")
        d["u_tail"] = u[rows - 8:, :]

        def shifted(k):
            r = pltpu.roll(u, k, 0)
            top = jnp.where(row8 < k, pltpu.roll(prev, k, 0), r[:8])
            return jnp.concatenate([top, r[8:]], axis=0)

        cw = cw_ref[...]
        cv = cw[0:1] * shifted(2) + cw[1:2] * shifted(1) + cw[2:3] * u
        d["y_a"] = (d.pop("conv_b") * cv).astype(BF16)

    def gates(d):
        d["p_b"] = _dot(yb_ref[0, d["rs"], :], woa_ref[...])
        d["gate_c"] = _dot(d["hb"], wc_ref[:, 3 * CONV_WIDTH:3 * CONV_WIDTH + D_MODEL])
        d["gate_a"] = _dot(d.pop("hb"), wc_ref[:, 3 * CONV_WIDTH + D_MODEL:])

    def branch_a(d):
        d["p_a"] = _dot(d.pop("y_a"), woc_ref[...])

    def merge(d):
        d["merged"] = (_sigmoid(d.pop("gate_c")) * d.pop("p_a")
                       + _sigmoid(d.pop("gate_a")) * d.pop("p_b")).astype(BF16)

    def out_proj(d):
        d["o"] = _dot(d.pop("merged"), wo_ref[...])

    def residual(d):
        x1 = d.pop("x") + g1_ref[0] * d.pop("o")
        x1_ref[0, d["rs"], :] = x1
        h2 = _modulated_norm(x1, n2_ref[...], sh2_ref[0], sc2_ref[0])
        h2_ref[0, d["rs"], :] = _pack_bf16_pairs(h2)
        d["h_hi"] = h2.astype(BF16)
        d["h_lo"] = (h2 - d["h_hi"].astype(F32)).astype(BF16)

    def router(d):
        h_hi = d.pop("h_hi")
        d["lg"] = (_dot(h_hi, wr_ref[0]) + _dot(d.pop("h_lo"), wr_ref[0]) + _dot(h_hi, wr_ref[1])) \
            + br_ref[...]

    def first_argmax(vals):
        mx = jnp.max(vals, axis=-1, keepdims=True)
        idx = jnp.min(jnp.where(vals == mx, lane, big), axis=-1, keepdims=True)
        return mx, idx

    def route(d):
        lg = d.pop("lg")
        is_g = (lane >= N_EXPERTS) & (lane < N_EXPERTS + N_GROUPS)
        g_mx, g_lane = first_argmax(jnp.where(is_g, lg, NEG))
        p_sel = 1.0 / jnp.sum(jnp.where(is_g, jnp.exp(lg - g_mx), 0.0), axis=-1, keepdims=True)
        g_idx = g_lane - N_EXPERTS
        in_g = (lane >= g_idx * EXPERTS_PER_GROUP) & (lane < (g_idx + 1) * EXPERTS_PER_GROUP)
        le = jnp.where(in_g, lg, NEG)
        v1, i1 = first_argmax(le)
        v2, i2 = first_argmax(jnp.where(lane == i1, NEG, le))
        e2 = jnp.exp(v2 - v1)
        w1 = p_sel / (1.0 + e2)
        w2 = w1 * e2
        rw_ref[0, d["rs"], :] = jnp.where(lane == 0, w1, 0.0) + jnp.where(lane == 1, w2, 0.0)
        d["i1"], d["i2"] = i1, i2
        d["onehot"] = jnp.where((lane == i1) | (lane == i2), 1.0, 0.0)

    stages = [conv_in, None, gates, branch_a, merge, out_proj, residual, router, route]
    lag = 2
    for step in range(len(stages) + lag * (n_grp - 1)):
        for g, d in enumerate(st):
            k = step - lag * g
            if 0 <= k < len(stages):
                if stages[k] is None:
                    conv(d, carry_ref[...] if g == 0 else st[g - 1]["u_tail"])
                else:
                    stages[k](d)
    carry_ref[...] = st[-1]["u_tail"]

    onehot = jnp.concatenate([d["onehot"] for d in st], axis=0)
    before = _dot(tri_ref[...], onehot.astype(BF16)) + cnt_ref[0:1, :]
    cnt_ref[...] = cnt_ref[...] + jnp.sum(onehot, axis=0, keepdims=True)
    for g, d in enumerate(st):
        bg = before[g * rows:(g + 1) * rows]
        i1, i2 = d["i1"], d["i2"]
        r1 = jnp.sum(jnp.where(lane == i1, bg, 0.0), axis=-1, keepdims=True)
        r2 = jnp.sum(jnp.where(lane == i2, bg, 0.0), axis=-1, keepdims=True)
        ridx_ref[0, d["rs"], :] = (jnp.where(lane == 0, i1, 0) + jnp.where(lane == 1, i2, 0)
                                   + jnp.where(lane == 2, r1.astype(jnp.int32), 0)
                                   + jnp.where(lane == 3, r2.astype(jnp.int32), 0))


def _post(x, yb, n1, sh1, sc1, g1, n2, sh2, sc2, wc, cw, woc, woa, wo, wr, br, tm):
    B, S, _ = x.shape
    tri = jnp.asarray(np.tril(np.ones((tm, tm), np.float32), -1), BF16)
    const = lambda *shape: pl.BlockSpec(shape, lambda b, i: (0,) * len(shape),
                                        pipeline_mode=pl.Buffered(1))
    perb = pl.BlockSpec((1, 1, D_MODEL), lambda b, i: (b, 0, 0))
    tok = lambda w: pl.BlockSpec((1, tm, w), lambda b, i: (b, i, 0))
    return pl.pallas_call(
        _post_kernel,
        out_shape=(jax.ShapeDtypeStruct((B, S, D_MODEL), F32),
                   jax.ShapeDtypeStruct((B, S, D_MODEL // 2), jnp.uint32),
                   jax.ShapeDtypeStruct((B, S, LANES), jnp.int32),
                   jax.ShapeDtypeStruct((B, S, LANES), F32),
                   jax.ShapeDtypeStruct((8, LANES), F32)),
        grid=(B, S // tm),
        in_specs=[tok(D_MODEL), tok(ATTN_WIDTH), const(1, D_MODEL), perb, perb, perb,
                  const(1, D_MODEL), perb, perb,
                  const(D_MODEL, 3 * CONV_WIDTH + 2 * D_MODEL),
                  const(8, CONV_WIDTH),
                  const(CONV_WIDTH, D_MODEL), const(ATTN_WIDTH, D_MODEL),
                  const(D_MODEL, D_MODEL),
                  const(2, D_MODEL, LANES), const(1, LANES), const(tm, tm)],
        out_specs=(tok(D_MODEL), tok(D_MODEL // 2), tok(LANES), tok(LANES),
                   pl.BlockSpec((8, LANES), lambda b, i: (0, 0))),
        scratch_shapes=[pltpu.VMEM((8, CONV_WIDTH), F32)],
        compiler_params=pltpu.CompilerParams(
            dimension_semantics=("arbitrary", "arbitrary"), vmem_limit_bytes=VMEM_LIMIT),
        name="post",
    )(x, yb, n1, sh1, sc1, g1, n2, sh2, sc2, wc, cw, woc, woa, wo, wr, br, tri)


SC_CORES = 2
SC_SUBCORES = 16
SC_WORKERS = SC_CORES * SC_SUBCORES
SC_CHUNK = 64
ROW_WORDS = D_MODEL // 2


def _sc_mesh():
    return plsc.VectorSubcoreMesh(core_axis_name="c", subcore_axis_name="s",
                                  num_cores=SC_CORES, num_subcores=SC_SUBCORES)


def _dispatch_body(rows_hbm, idx_hbm, xs_hbm, idx_v, rows_v, *, n_chunks):
    wid = lax.axis_index("s") * SC_CORES + lax.axis_index("c")
    pltpu.sync_copy(idx_hbm.at[wid], idx_v)
    base = wid * (n_chunks * SC_CHUNK)

    @pl.loop(0, n_chunks)
    def _(j):
        pltpu.sync_copy(rows_hbm.at[pl.ds(base + j * SC_CHUNK, SC_CHUNK)], rows_v)
        pltpu.sync_copy(rows_v, xs_hbm.at[idx_v.at[2 * j]])
        pltpu.sync_copy(rows_v, xs_hbm.at[idx_v.at[2 * j + 1]])


def _dispatch(rows, idx, n_slots):
    T = rows.shape[0]
    n_chunks = T // (SC_WORKERS * SC_CHUNK)
    return pl.kernel(
        functools.partial(_dispatch_body, n_chunks=n_chunks),
        out_type=jax.ShapeDtypeStruct((n_slots, ROW_WORDS), jnp.uint32),
        mesh=_sc_mesh(),
        scratch_types=[pltpu.VMEM((2 * n_chunks, SC_CHUNK), jnp.int32),
                       pltpu.VMEM((SC_CHUNK, ROW_WORDS), jnp.uint32)],
        name="dispatch",
    )(rows, idx)


def _collect_body(ys_hbm, idx_hbm, g1_hbm, g2_hbm, idx_v, rows_v, *, n_chunks):
    wid = lax.axis_index("s") * SC_CORES + lax.axis_index("c")
    pltpu.sync_copy(idx_hbm.at[wid], idx_v)
    base = wid * (n_chunks * SC_CHUNK)

    @pl.loop(0, n_chunks)
    def _(j):
        dst = pl.ds(base + j * SC_CHUNK, SC_CHUNK)
        pltpu.sync_copy(ys_hbm.at[idx_v.at[2 * j]], rows_v)
        pltpu.sync_copy(rows_v, g1_hbm.at[dst])
        pltpu.sync_copy(ys_hbm.at[idx_v.at[2 * j + 1]], rows_v)
        pltpu.sync_copy(rows_v, g2_hbm.at[dst])


def _collect(ys, idx, T):
    n_chunks = T // (SC_WORKERS * SC_CHUNK)
    out = jax.ShapeDtypeStruct((T, ROW_WORDS), jnp.uint32)
    return pl.kernel(
        functools.partial(_collect_body, n_chunks=n_chunks),
        out_type=(out, out),
        mesh=_sc_mesh(),
        scratch_types=[pltpu.VMEM((2 * n_chunks, SC_CHUNK), jnp.int32),
                       pltpu.VMEM((SC_CHUNK, ROW_WORDS), jnp.uint32)],
        name="collect",
    )(ys, idx)


def _moe_kernel(te_ref, nt_ref, xs_ref, wg_ref, wu_ref, wd_ref, ys_ref):
    @pl.when(pl.program_id(0) < nt_ref[0])
    def _():
        left, right = _unpack_bf16_pairs(xs_ref[...])
        xb = jnp.concatenate([left.astype(BF16), right.astype(BF16)], axis=1)
        g = _dot(xb, wg_ref[0])
        u = _dot(xb, wu_ref[0])
        a = (g * _sigmoid(g) * u).astype(BF16)
        ys_ref[...] = _pack_bf16_pairs(_dot(a, wd_ref[0]))


def _moe(xs, tile_expert, n_tiles, wg, wu, wd, tm):
    n_slots = xs.shape[0]
    row_blk = lambda i, te, nt: (jnp.minimum(i, nt[0] - 1), 0)
    w_blk = lambda i, te, nt: (te[i], 0, 0)
    return pl.pallas_call(
        _moe_kernel,
        out_shape=jax.ShapeDtypeStruct((n_slots, ROW_WORDS), jnp.uint32),
        grid_spec=pltpu.PrefetchScalarGridSpec(
            num_scalar_prefetch=2,
            grid=(n_slots // tm,),
            in_specs=[pl.BlockSpec((tm, ROW_WORDS), row_blk),
                      pl.BlockSpec((1, D_MODEL, D_EXPERT), w_blk),
                      pl.BlockSpec((1, D_MODEL, D_EXPERT), w_blk),
                      pl.BlockSpec((1, D_EXPERT, D_MODEL), w_blk)],
            out_specs=pl.BlockSpec((tm, ROW_WORDS), row_blk)),
        compiler_params=pltpu.CompilerParams(
            dimension_semantics=("arbitrary",), vmem_limit_bytes=VMEM_LIMIT),
        name="moe",
    )(tile_expert, n_tiles, xs, wg, wu, wd)


def _final_kernel(x1_ref, g1_ref, g2_ref, rw_ref, gate_ref, o_ref):
    rw = rw_ref[0]
    w1 = rw[:, 0:1]
    w2 = rw[:, 1:2]
    a_l, a_r = _unpack_bf16_pairs(g1_ref[0])
    b_l, b_r = _unpack_bf16_pairs(g2_ref[0])
    moe = jnp.concatenate([w1 * a_l + w2 * b_l, w1 * a_r + w2 * b_r], axis=1)
    o_ref[0] = x1_ref[0] + gate_ref[0] * moe


def _final(x1, g1, g2, rw, gate2, tm):
    B, S, _ = x1.shape
    tok = lambda w: pl.BlockSpec((1, tm, w), lambda b, i: (b, i, 0))
    return pl.pallas_call(
        _final_kernel,
        out_shape=jax.ShapeDtypeStruct((B, S, D_MODEL), F32),
        grid=(B, S // tm),
        in_specs=[tok(D_MODEL), tok(ROW_WORDS), tok(ROW_WORDS), tok(LANES),
                  pl.BlockSpec((1, 1, D_MODEL), lambda b, i: (b, 0, 0))],
        out_specs=tok(D_MODEL),
        compiler_params=pltpu.CompilerParams(
            dimension_semantics=("arbitrary", "arbitrary"), vmem_limit_bytes=VMEM_LIMIT),
        name="final",
    )(x1, g1, g2, rw, gate2)


def _pick(n, pref):
    t = min(n, pref)
    assert n % t == 0, (n, t)
    return t


def _route_plan(ridx, counts, tm_e, T):
    counts = counts.astype(jnp.int32)
    tiles = (counts + tm_e - 1) // tm_e
    tile_end = jnp.cumsum(tiles)
    offs = (tile_end - tiles) * tm_e
    slot1 = offs[ridx[:, 0]] + ridx[:, 2]
    slot2 = offs[ridx[:, 1]] + ridx[:, 3]
    n_chunks = T // (SC_WORKERS * SC_CHUNK)
    idx = jnp.stack([slot1.reshape(SC_WORKERS, n_chunks, SC_CHUNK),
                     slot2.reshape(SC_WORKERS, n_chunks, SC_CHUNK)], axis=2)
    idx = idx.reshape(SC_WORKERS, 2 * n_chunks, SC_CHUNK)
    n_tiles_max = 2 * T // tm_e + N_EXPERTS
    tile_ids = jnp.arange(n_tiles_max, dtype=jnp.int32)
    tile_expert = jnp.sum((tile_end[None, :] <= tile_ids[:, None]).astype(jnp.int32), axis=1)
    tile_expert = jnp.minimum(tile_expert, N_EXPERTS - 1)
    return idx, tile_expert, tile_end[-1:].astype(jnp.int32), n_tiles_max * tm_e


def _layer(x, c, w_ada, b_ada, norm1_w, w_in, b_forget, conv_w, q_norm_w, k_norm_w,
           w_out_conv, w_out_attn, w_o, norm2_w, w_rg, b_rg, w_re, b_re, w_gate, w_up, w_down):
    B, S, _ = x.shape
    T = B * S
    assert T % (SC_WORKERS * SC_CHUNK) == 0, T
    mod = _ada(c, w_ada, b_ada.reshape(1, -1)).reshape(B, 6, 1, D_MODEL)
    shift1, scale1, gate1, shift2, scale2, gate2 = (mod[:, t] for t in range(6))

    cuts = np.cumsum([0, CONV_WIDTH, CONV_WIDTH, CONV_WIDTH, ATTN_WIDTH, ATTN_WIDTH, ATTN_WIDTH,
                      N_HEADS, D_MODEL, D_MODEL])
    w_conv3 = w_in[:, cuts[0]:cuts[3]]
    w_qvT = jnp.concatenate([w_in[:, cuts[3]:cuts[4]], w_in[:, cuts[5]:cuts[6]]], axis=1).T.astype(BF16)
    w_k = w_in[:, cuts[4]:cuts[5]].astype(BF16)
    w_f = jnp.pad(w_in[:, cuts[6]:cuts[7]], ((0, 0), (0, LANES - N_HEADS))).astype(BF16)
    b_f = jnp.pad(b_forget, (0, LANES - N_HEADS)).reshape(1, LANES)
    w_cgg = jnp.concatenate([w_conv3, w_in[:, cuts[7]:cuts[9]]], axis=1).astype(BF16)

    tm_qkv = _pick(S, 512)
    qnwT = jnp.broadcast_to((jnp.tile(q_norm_w, N_HEADS) * (LOG2E * HEAD_DIM ** -0.5))[:, None],
                            (ATTN_WIDTH, LANES))
    qT_aug, k_aug, vT = _qkv(x, norm1_w.reshape(1, -1), shift1, scale1, w_qvT, w_k, w_f, b_f,
                             qnwT, jnp.tile(k_norm_w, N_HEADS).reshape(1, -1), tm_qkv)

    tq = _pick(S, 512)
    y_b = _attention(qT_aug, k_aug, vT, tq, _pick(tq, 256))

    w_r = jnp.pad(jnp.concatenate([w_re, w_rg], axis=1),
                  ((0, 0), (0, LANES - N_EXPERTS - N_GROUPS)))
    w_r_hi = w_r.astype(BF16)
    w_r_lo = (w_r - w_r_hi.astype(F32)).astype(BF16)
    b_r = jnp.pad(jnp.concatenate([b_re, b_rg]), (0, LANES - N_EXPERTS - N_GROUPS)).reshape(1, LANES)
    cw = jnp.pad(conv_w, ((0, 8 - CONV_K), (0, 0)))
    tm_post = _pick(S, 1024)
    x1, h2p, ridx, rw, counts = _post(x, y_b, norm1_w.reshape(1, -1), shift1, scale1, gate1,
                                      norm2_w.reshape(1, -1), shift2, scale2,
                                      w_cgg, cw, w_out_conv.astype(BF16), w_out_attn.astype(BF16),
                                      w_o.astype(BF16), jnp.stack([w_r_hi, w_r_lo]), b_r, tm_post)

    tm_e = 512
    idx, tile_expert, n_tiles, n_slots = _route_plan(
        ridx.reshape(T, LANES)[:, :4], counts[0, :N_EXPERTS], tm_e, T)
    xs = _dispatch(h2p.reshape(T, ROW_WORDS), idx, n_slots)
    ys = _moe(xs, tile_expert, n_tiles, w_gate.astype(BF16), w_up.astype(BF16),
              w_down.astype(BF16), tm_e)
    g1, g2 = _collect(ys, idx, T)
    return _final(x1, g1.reshape(B, S, ROW_WORDS), g2.reshape(B, S, ROW_WORDS), rw, gate2,
                  _pick(S, 512))


def kernel(x, c, w_ada, b_ada, norm1_w, w_in, b_forget, conv_w, q_norm_w, k_norm_w, w_out_conv,
           w_out_attn, w_o, norm2_w, w_router_group, b_router_group, w_router_expert,
           b_router_expert, w_gate, w_up, w_down):
    for l in range(w_ada.shape[0]):
        x = _layer(x, c, w_ada[l], b_ada[l], norm1_w[l], w_in[l], b_forget[l], conv_w[l],
                   q_norm_w[l], k_norm_w[l], w_out_conv[l], w_out_attn[l], w_o[l], norm2_w[l],
                   w_router_group[l], b_router_group[l], w_router_expert[l], b_router_expert[l],
                   w_gate[l], w_up[l], w_down[l])
    return x
```

```python
import functools

import jax
import jax.numpy as jnp
import numpy as np
from jax import lax
from jax.experimental import pallas as pl
from jax.experimental.pallas import tpu as pltpu
from jax.experimental.pallas import tpu_sc as plsc

D_MODEL = 1024
CONV_WIDTH = 512
CONV_K = 3
N_HEADS = 8
HEAD_DIM = 64
ATTN_WIDTH = N_HEADS * HEAD_DIM
N_PAIRS = N_HEADS // 2
N_GROUPS = 4
EXPERTS_PER_GROUP = 8
N_EXPERTS = N_GROUPS * EXPERTS_PER_GROUP
D_EXPERT = 256
EPS = 1e-6
LANES = 128
AUG = 2 * LANES
BIAS_W = 6
VROWS = HEAD_DIM + 16
NEG = -1e30
LOG2E = 1.4426950408889634

F32 = jnp.float32
BF16 = jnp.bfloat16
VMEM_LIMIT = 56 * 1024 * 1024


def _sigmoid(z):
    return 1.0 / (1.0 + jnp.exp(-z))


def _split3(z):
    hi = z.astype(BF16)
    r = z - hi.astype(F32)
    mid = r.astype(BF16)
    lo = (r - mid.astype(F32)).astype(BF16)
    return hi, mid, lo


def _dot(a, b):
    return jnp.dot(a, b, preferred_element_type=F32)


def _modulated_norm(x, nw, shift, scale):
    ms = jnp.mean(x * x, axis=-1, keepdims=True)
    return (x * lax.rsqrt(ms + EPS) * nw) * (1.0 + scale) + shift


def _ada_kernel(c_ref, w_ref, b_ref, o_ref):
    c = c_ref[...]
    a = c * _sigmoid(c)
    o_ref[...] = jnp.dot(a, w_ref[...], precision=lax.Precision.HIGHEST,
                         preferred_element_type=F32) + b_ref[...]


def _ada(c, w_ada, b_ada):
    B = c.shape[0]
    n = w_ada.shape[1] // D_MODEL
    return pl.pallas_call(
        _ada_kernel,
        out_shape=jax.ShapeDtypeStruct((B, n * D_MODEL), F32),
        grid=(n,),
        in_specs=[pl.BlockSpec((B, D_MODEL), lambda j: (0, 0)),
                  pl.BlockSpec((D_MODEL, D_MODEL), lambda j: (0, j)),
                  pl.BlockSpec((1, D_MODEL), lambda j: (0, j))],
        out_specs=pl.BlockSpec((B, D_MODEL), lambda j: (0, j)),
        compiler_params=pltpu.CompilerParams(dimension_semantics=("arbitrary",)),
        name="ada",
    )(c, w_ada, b_ada)


_NT = (((1,), (1,)), ((), ()))


def _lane_tile(a, width):
    return jnp.concatenate([a] * (width // a.shape[1]), axis=1)


def _qkv_kernel(x_ref, nw_ref, sh_ref, sc_ref, wqv_ref, wk_ref, wf_ref, bf_ref, qnw_ref, knw_ref,
                gsum_ref, pq_ref, pk_ref, cq_ref, ck_ref,
                qT_ref, k_ref, vT_ref, carry_ref):
    tm = x_ref.shape[1]
    h = _modulated_norm(x_ref[0], nw_ref[...], sh_ref[0], sc_ref[0])
    hb = h.astype(BF16)
    qvT = lax.dot_general(wqv_ref[...], hb, _NT, preferred_element_type=F32)
    k = _dot(hb, wk_ref[...])

    heads = []
    for hd in range(N_HEADS):
        z = qvT[hd * HEAD_DIM:(hd + 1) * HEAD_DIM]
        heads.append(z * lax.rsqrt(jnp.mean(z * z, axis=0, keepdims=True) + EPS))
    qnT = jnp.concatenate(heads, axis=0) * _lane_tile(qnw_ref[...], tm)

    ss = _dot((k * k).astype(BF16), gsum_ref[...])
    kn = k * lax.rsqrt(ss * (1.0 / HEAD_DIM) + EPS) * knw_ref[...]

    fl = _dot(hb, wf_ref[...]) + bf_ref[...]
    lf = jnp.minimum(fl, 0.0) - jnp.log(1.0 + jnp.exp(-jnp.abs(fl)))

    @pl.when(pl.program_id(1) == 0)
    def _():
        carry_ref[...] = jnp.zeros_like(carry_ref)

    row = lax.broadcasted_iota(jnp.int32, (tm, LANES), 0)
    cum = lf
    s = 1
    while s < tm:
        cum = cum + jnp.where(row >= s, pltpu.roll(cum, s, 0), 0.0)
        s *= 2
    cum = cum + carry_ref[7:8, :]
    carry_ref[...] = cum[tm - 8:, :]

    parts = jnp.concatenate(_split3(cum * LOG2E), axis=1)
    eqT = (lax.dot_general(pq_ref[...], parts, _NT, preferred_element_type=F32)
           + _lane_tile(cq_ref[...], tm)).astype(BF16)
    ek = (_dot(parts, pk_ref[...]) + ck_ref[...]).astype(BF16)
    for j in range(N_PAIRS):
        qT_ref[0, j, :LANES, :] = qnT[j * LANES:(j + 1) * LANES].astype(BF16)
        qT_ref[0, j, LANES:, :] = eqT
        for t in range(2):
            r0 = ATTN_WIDTH + (2 * j + t) * HEAD_DIM
            vT_ref[0, j, t * VROWS:t * VROWS + HEAD_DIM, :] = qvT[r0:r0 + HEAD_DIM].astype(BF16)
            vT_ref[0, j, t * VROWS + HEAD_DIM:(t + 1) * VROWS, :] = jnp.ones((VROWS - HEAD_DIM, tm), BF16)
        k_ref[0, :, j * AUG:j * AUG + LANES] = kn[:, j * LANES:(j + 1) * LANES].astype(BF16)
        k_ref[0, :, j * AUG + LANES:(j + 1) * AUG] = ek


def _bias_placement():
    pq = np.zeros((LANES, 3 * LANES), np.float32)
    pk = np.zeros((3 * LANES, LANES), np.float32)
    cq = np.zeros((LANES, LANES), np.float32)
    ck = np.zeros((1, LANES), np.float32)
    for hd in range(N_HEADS):
        base = BIAS_W * hd
        for p in range(3):
            pq[base + p, p * LANES + hd] = 1.0
            pk[p * LANES + hd, base + 3 + p] = -1.0
            cq[base + 3 + p, :] = 1.0
            ck[0, base + p] = 1.0
    return (jnp.asarray(pq, BF16), jnp.asarray(pk, BF16), jnp.asarray(cq), jnp.asarray(ck))


def _qkv(x, nw, shift, scale, wqvT, wk, wf, bf, qnwT, knw, tm):
    B, S, _ = x.shape
    gsum = jnp.asarray(np.kron(np.eye(N_HEADS), np.ones((HEAD_DIM, HEAD_DIM))), BF16)
    pq, pk, cq, ck = _bias_placement()
    const = lambda *shape: pl.BlockSpec(shape, lambda b, i: (0,) * len(shape),
                                        pipeline_mode=pl.Buffered(1))
    return pl.pallas_call(
        _qkv_kernel,
        out_shape=(jax.ShapeDtypeStruct((B, N_PAIRS, AUG, S), BF16),
                   jax.ShapeDtypeStruct((B, S, N_PAIRS * AUG), BF16),
                   jax.ShapeDtypeStruct((B, N_PAIRS, 2 * VROWS, S), BF16)),
        grid=(B, S // tm),
        in_specs=[pl.BlockSpec((1, tm, D_MODEL), lambda b, i: (b, i, 0)),
                  const(1, D_MODEL),
                  pl.BlockSpec((1, 1, D_MODEL), lambda b, i: (b, 0, 0)),
                  pl.BlockSpec((1, 1, D_MODEL), lambda b, i: (b, 0, 0)),
                  const(2 * ATTN_WIDTH, D_MODEL),
                  const(D_MODEL, ATTN_WIDTH),
                  const(D_MODEL, LANES),
                  const(1, LANES),
                  const(ATTN_WIDTH, LANES),
                  const(1, ATTN_WIDTH),
                  const(ATTN_WIDTH, ATTN_WIDTH),
                  const(LANES, 3 * LANES),
                  const(3 * LANES, LANES),
                  const(LANES, LANES),
                  const(1, LANES)],
        out_specs=(pl.BlockSpec((1, N_PAIRS, AUG, tm), lambda b, i: (b, 0, 0, i)),
                   pl.BlockSpec((1, tm, N_PAIRS * AUG), lambda b, i: (b, i, 0)),
                   pl.BlockSpec((1, N_PAIRS, 2 * VROWS, tm), lambda b, i: (b, 0, 0, i))),
        scratch_shapes=[pltpu.VMEM((8, LANES), F32)],
        compiler_params=pltpu.CompilerParams(
            dimension_semantics=("arbitrary", "arbitrary"), vmem_limit_bytes=VMEM_LIMIT),
        name="qkv",
    )(x, nw, shift, scale, wqvT, wk, wf, bf, qnwT, knw, gsum, pq, pk, cq, ck)


def _attn_kernel(qT_ref, k_ref, vT_ref, o_ref, qq_ref, s_ref, smax_ref, m_ref, acc_ref, *, tq, cw):
    S = k_ref.shape[1]
    nq = S // tq
    n = tq // cw
    chains = [(t, c) for t in range(2) for c in range(n)]
    n_items = nq * (nq + 1) // 2

    feat = lax.broadcasted_iota(jnp.int32, (AUG, tq), 0)
    for t in range(2):
        bias0 = LANES + BIAS_W * (2 * pl.program_id(1) + t)
        keep = ((feat >= t * HEAD_DIM) & (feat < (t + 1) * HEAD_DIM)) | \
               ((feat >= bias0) & (feat < bias0 + BIAS_W))
        for i in range(nq):
            qT = qT_ref[0, 0, :, i * tq:(i + 1) * tq]
            qq_ref[t, :, i * tq:(i + 1) * tq] = jnp.where(keep, qT, jnp.zeros_like(qT))
    kpos = lax.broadcasted_iota(jnp.int32, (tq, cw), 0)
    qpos = lax.broadcasted_iota(jnp.int32, (tq, cw), 1)

    def following(item):
        qi, j = item
        last = j == qi
        return jnp.where(last, jnp.minimum(qi + 1, nq - 1), qi), jnp.where(last, 0, j + 1)

    def scores(item, slot):
        qi, j = item
        k_blk = k_ref[0, pl.ds(pl.multiple_of(j * tq, tq), tq), :]
        for ci, (t, c) in enumerate(chains):
            q0 = pl.multiple_of(qi * tq + c * cw, cw)
            s = _dot(k_blk, qq_ref[t, :, pl.ds(q0, cw)])
            s_ref[slot, ci] = s
            smax_ref[slot, ci] = jnp.broadcast_to(jnp.max(s, axis=0, keepdims=True), (8, cw))

    def absorb(item, slot, diagonal):
        qi, j = item
        start = pl.multiple_of(j * tq, tq)
        for ci, (t, c) in enumerate(chains):
            vj = vT_ref[0, 0, t * VROWS:(t + 1) * VROWS, pl.ds(start, tq)]
            s = s_ref[slot, ci]
            if diagonal:
                s = jnp.where(kpos <= qpos + c * cw, s, NEG)
                smax = jnp.max(s, axis=0, keepdims=True)
            else:
                smax = smax_ref[slot, ci, 0:1]
            m = jnp.where(j == 0, NEG, m_ref[ci, 0:1])
            m_new = jnp.maximum(m, smax)
            p = jnp.exp2(s - m_new).astype(BF16)
            acc_ref[ci] = jnp.exp2(m - m_new) * acc_ref[ci] + _dot(vj, p)
            m_ref[ci] = jnp.broadcast_to(m_new, (8, cw))

    def finish(item):
        qi, _ = item
        outs = [acc_ref[ci, :HEAD_DIM] / acc_ref[ci, HEAD_DIM:HEAD_DIM + 1] for ci in range(len(chains))]
        oT = jnp.concatenate([jnp.concatenate(outs[:n], axis=1), jnp.concatenate(outs[n:], axis=1)],
                             axis=0)
        o_ref[0, pl.ds(pl.multiple_of(qi * tq, tq), tq), :] = oT.T.astype(BF16)

    def stage(item, slot, prefetch=True):
        nxt = following(item)
        diag = item[1] == item[0]

        @pl.when(diag)
        def _():
            if prefetch:
                scores(nxt, 1 - slot)
            absorb(item, slot, True)
            finish(item)

        @pl.when(jnp.logical_not(diag))
        def _():
            if prefetch:
                scores(nxt, 1 - slot)
            absorb(item, slot, False)

        return nxt

    m_ref[...] = jnp.full(m_ref.shape, NEG, F32)
    acc_ref[...] = jnp.zeros(acc_ref.shape, F32)
    first = (jnp.int32(0), jnp.int32(0))
    scores(first, 0)

    def two_items(_, item):
        return stage(stage(item, 0), 1)

    n_pairs = n_items // 2 if n_items % 2 else n_items // 2 - 1
    item = lax.fori_loop(0, n_pairs, two_items, first)
    if n_items % 2:
        stage(item, 0, prefetch=False)
    else:
        item = stage(item, 0)
        stage(item, 1, prefetch=False)


def _attention(qT_aug, k_aug, vT, tq, cw):
    B, S, _ = k_aug.shape
    n_chains = 2 * tq // cw
    return pl.pallas_call(
        functools.partial(_attn_kernel, tq=tq, cw=cw),
        scratch_shapes=[pltpu.VMEM((2, AUG, S), BF16),
                        pltpu.VMEM((2, n_chains, tq, cw), F32),
                        pltpu.VMEM((2, n_chains, 8, cw), F32),
                        pltpu.VMEM((n_chains, 8, cw), F32),
                        pltpu.VMEM((n_chains, VROWS, cw), F32)],
        out_shape=jax.ShapeDtypeStruct((B, S, ATTN_WIDTH), BF16),
        grid=(B, N_PAIRS),
        in_specs=[pl.BlockSpec((1, 1, AUG, S), lambda b, j: (b, j, 0, 0)),
                  pl.BlockSpec((1, S, AUG), lambda b, j: (b, 0, j)),
                  pl.BlockSpec((1, 1, 2 * VROWS, S), lambda b, j: (b, j, 0, 0))],
        out_specs=pl.BlockSpec((1, S, LANES), lambda b, j: (b, 0, j)),
        compiler_params=pltpu.CompilerParams(
            dimension_semantics=("arbitrary", "arbitrary"), vmem_limit_bytes=VMEM_LIMIT),
        name="attn",
    )(qT_aug, k_aug, vT)


def _pack_bf16_pairs(z):
    w = z.shape[1] // 2
    bits = pltpu.bitcast(z.astype(BF16).astype(F32), jnp.uint32)
    return bits[:, :w] | (bits[:, w:] >> 16)


def _unpack_bf16_pairs(p):
    return (pltpu.bitcast(p & jnp.uint32(0xFFFF0000), F32), pltpu.bitcast(p << 16, F32))


POST_ROWS = 256


def _post_kernel(x_ref, yb_ref, n1_ref, sh1_ref, sc1_ref, g1_ref, n2_ref, sh2_ref, sc2_ref,
                 wc_ref, cw_ref, woc_ref, woa_ref, wo_ref, wr_ref, br_ref, tri_ref,
                 x1_ref, h2_ref, ridx_ref, rw_ref, cnt_ref, carry_ref):
    tm = x_ref.shape[1]
    n_grp = max(tm // POST_ROWS, 1)
    rows = tm // n_grp
    lane = lax.broadcasted_iota(jnp.int32, (rows, LANES), 1)
    row8 = lax.broadcasted_iota(jnp.int32, (8, CONV_WIDTH), 0)
    big = jnp.int32(1 << 20)

    @pl.when(pl.program_id(1) == 0)
    def _():
        carry_ref[...] = jnp.zeros_like(carry_ref)

    @pl.when((pl.program_id(0) == 0) & (pl.program_id(1) == 0))
    def _():
        cnt_ref[...] = jnp.zeros_like(cnt_ref)

    st = [dict(rs=pl.ds(g * rows, rows)) for g in range(n_grp)]

    def conv_in(d):
        d["x"] = x_ref[0, d["rs"], :]
        d["hb"] = _modulated_norm(d["x"], n1_ref[...], sh1_ref[0], sc1_ref[0]).astype(BF16)
        d["x_in"] = _dot(d["hb"], wc_ref[:, :CONV_WIDTH])
        d["conv_c"] = _dot(d["hb"], wc_ref[:, 2 * CONV_WIDTH:3 * CONV_WIDTH])
        d["conv_b"] = _dot(d["hb"], wc_ref[:, CONV_WIDTH:2 * CONV_WIDTH])

    def conv(d, prev):
        u = d.pop("conv_c") * d.pop("x_in")
        d["u_tail"] = u[rows - 8:, :]

        def shifted(k):
            r = pltpu.roll(u, k, 0)
            top = jnp.where(row8 < k, pltpu.roll(prev, k, 0), r[:8])
            return jnp.concatenate([top, r[8:]], axis=0)

        cw = cw_ref[...]
        cv = cw[0:1] * shifted(2) + cw[1:2] * shifted(1) + cw[2:3] * u
        d["y_a"] = (d.pop("conv_b") * cv).astype(BF16)

    def gates(d):
        d["p_b"] = _dot(yb_ref[0, d["rs"], :], woa_ref[...])
        d["gate_c"] = _dot(d["hb"], wc_ref[:, 3 * CONV_WIDTH:3 * CONV_WIDTH + D_MODEL])
        d["gate_a"] = _dot(d.pop("hb"), wc_ref[:, 3 * CONV_WIDTH + D_MODEL:])

    def branch_a(d):
        d["p_a"] = _dot(d.pop("y_a"), woc_ref[...])

    def merge(d):
        d["merged"] = (_sigmoid(d.pop("gate_c")) * d.pop("p_a")
                       + _sigmoid(d.pop("gate_a")) * d.pop("p_b")).astype(BF16)

    def out_proj(d):
        d["o"] = _dot(d.pop("merged"), wo_ref[...])

    def residual(d):
        x1 = d.pop("x") + g1_ref[0] * d.pop("o")
        x1_ref[0, d["rs"], :] = x1
        h2 = _modulated_norm(x1, n2_ref[...], sh2_ref[0], sc2_ref[0])
        h2_ref[0, d["rs"], :] = _pack_bf16_pairs(h2)
        d["h_hi"] = h2.astype(BF16)
        d["h_lo"] = (h2 - d["h_hi"].astype(F32)).astype(BF16)

    def router(d):
        h_hi = d.pop("h_hi")
        d["lg"] = (_dot(h_hi, wr_ref[0]) + _dot(d.pop("h_lo"), wr_ref[0]) + _dot(h_hi, wr_ref[1])) \
            + br_ref[...]

    def first_argmax(vals):
        mx = jnp.max(vals, axis=-1, keepdims=True)
        idx = jnp.min(jnp.where(vals == mx, lane, big), axis=-1, keepdims=True)
        return mx, idx

    def route(d):
        lg = d.pop("lg")
        is_g = (lane >= N_EXPERTS) & (lane < N_EXPERTS + N_GROUPS)
        g_mx, g_lane = first_argmax(jnp.where(is_g, lg, NEG))
        p_sel = 1.0 / jnp.sum(jnp.where(is_g, jnp.exp(lg - g_mx), 0.0), axis=-1, keepdims=True)
        g_idx = g_lane - N_EXPERTS
        in_g = (lane >= g_idx * EXPERTS_PER_GROUP) & (lane < (g_idx + 1) * EXPERTS_PER_GROUP)
        le = jnp.where(in_g, lg, NEG)
        v1, i1 = first_argmax(le)
        v2, i2 = first_argmax(jnp.where(lane == i1, NEG, le))
        e2 = jnp.exp(v2 - v1)
        w1 = p_sel / (1.0 + e2)
        w2 = w1 * e2
        rw_ref[0, d["rs"], :] = jnp.where(lane == 0, w1, 0.0) + jnp.where(lane == 1, w2, 0.0)
        d["i1"], d["i2"] = i1, i2
        d["onehot"] = jnp.where((lane == i1) | (lane == i2), 1.0, 0.0)

    stages = [conv_in, None, gates, branch_a, merge, out_proj, residual, router, route]
    lag = 2
    for step in range(len(stages) + lag * (n_grp - 1)):
        for g, d in enumerate(st):
            k = step - lag * g
            if 0 <= k < len(stages):
                if stages[k] is None:
                    conv(d, carry_ref[...] if g == 0 else st[g - 1]["u_tail"])
                else:
                    stages[k](d)
    carry_ref[...] = st[-1]["u_tail"]

    onehot = jnp.concatenate([d["onehot"] for d in st], axis=0)
    before = _dot(tri_ref[...], onehot.astype(BF16)) + cnt_ref[0:1, :]
    cnt_ref[...] = cnt_ref[...] + jnp.sum(onehot, axis=0, keepdims=True)
    for g, d in enumerate(st):
        bg = before[g * rows:(g + 1) * rows]
        i1, i2 = d["i1"], d["i2"]
        r1 = jnp.sum(jnp.where(lane == i1, bg, 0.0), axis=-1, keepdims=True)
        r2 = jnp.sum(jnp.where(lane == i2, bg, 0.0), axis=-1, keepdims=True)
        ridx_ref[0, d["rs"], :] = (jnp.where(lane == 0, i1, 0) + jnp.where(lane == 1, i2, 0)
                                   + jnp.where(lane == 2, r1.astype(jnp.int32), 0)
                                   + jnp.where(lane == 3, r2.astype(jnp.int32), 0))


def _post(x, yb, n1, sh1, sc1, g1, n2, sh2, sc2, wc, cw, woc, woa, wo, wr, br, tm):
    B, S, _ = x.shape
    tri = jnp.asarray(np.tril(np.ones((tm, tm), np.float32), -1), BF16)
    const = lambda *shape: pl.BlockSpec(shape, lambda b, i: (0,) * len(shape),
                                        pipeline_mode=pl.Buffered(1))
    perb = pl.BlockSpec((1, 1, D_MODEL), lambda b, i: (b, 0, 0))
    tok = lambda w: pl.BlockSpec((1, tm, w), lambda b, i: (b, i, 0))
    return pl.pallas_call(
        _post_kernel,
        out_shape=(jax.ShapeDtypeStruct((B, S, D_MODEL), F32),
                   jax.ShapeDtypeStruct((B, S, D_MODEL // 2), jnp.uint32),
                   jax.ShapeDtypeStruct((B, S, LANES), jnp.int32),
                   jax.ShapeDtypeStruct((B, S, LANES), F32),
                   jax.ShapeDtypeStruct((8, LANES), F32)),
        grid=(B, S // tm),
        in_specs=[tok(D_MODEL), tok(ATTN_WIDTH), const(1, D_MODEL), perb, perb, perb,
                  const(1, D_MODEL), perb, perb,
                  const(D_MODEL, 3 * CONV_WIDTH + 2 * D_MODEL),
                  const(8, CONV_WIDTH),
                  const(CONV_WIDTH, D_MODEL), const(ATTN_WIDTH, D_MODEL),
                  const(D_MODEL, D_MODEL),
                  const(2, D_MODEL, LANES), const(1, LANES), const(tm, tm)],
        out_specs=(tok(D_MODEL), tok(D_MODEL // 2), tok(LANES), tok(LANES),
                   pl.BlockSpec((8, LANES), lambda b, i: (0, 0))),
        scratch_shapes=[pltpu.VMEM((8, CONV_WIDTH), F32)],
        compiler_params=pltpu.CompilerParams(
            dimension_semantics=("arbitrary", "arbitrary"), vmem_limit_bytes=VMEM_LIMIT),
        name="post",
    )(x, yb, n1, sh1, sc1, g1, n2, sh2, sc2, wc, cw, woc, woa, wo, wr, br, tri)


SC_CORES = 2
SC_SUBCORES = 16
SC_WORKERS = SC_CORES * SC_SUBCORES
SC_CHUNK = 64
ROW_WORDS = D_MODEL // 2


def _sc_mesh():
    return plsc.VectorSubcoreMesh(core_axis_name="c", subcore_axis_name="s",
                                  num_cores=SC_CORES, num_subcores=SC_SUBCORES)


def _dispatch_body(rows_hbm, idx_hbm, xs_hbm, idx_v, rows_v, *, n_chunks):
    wid = lax.axis_index("s") * SC_CORES + lax.axis_index("c")
    pltpu.sync_copy(idx_hbm.at[wid], idx_v)
    base = wid * (n_chunks * SC_CHUNK)

    @pl.loop(0, n_chunks)
    def _(j):
        pltpu.sync_copy(rows_hbm.at[pl.ds(base + j * SC_CHUNK, SC_CHUNK)], rows_v)
        pltpu.sync_copy(rows_v, xs_hbm.at[idx_v.at[2 * j]])
        pltpu.sync_copy(rows_v, xs_hbm.at[idx_v.at[2 * j + 1]])


def _dispatch(rows, idx, n_slots):
    T = rows.shape[0]
    n_chunks = T // (SC_WORKERS * SC_CHUNK)
    return pl.kernel(
        functools.partial(_dispatch_body, n_chunks=n_chunks),
        out_type=jax.ShapeDtypeStruct((n_slots, ROW_WORDS), jnp.uint32),
        mesh=_sc_mesh(),
        scratch_types=[pltpu.VMEM((2 * n_chunks, SC_CHUNK), jnp.int32),
                       pltpu.VMEM((SC_CHUNK, ROW_WORDS), jnp.uint32)],
        name="dispatch",
    )(rows, idx)


def _collect_body(ys_hbm, idx_hbm, g1_hbm, g2_hbm, idx_v, rows_v, *, n_chunks):
    wid = lax.axis_index("s") * SC_CORES + lax.axis_index("c")
    pltpu.sync_copy(idx_hbm.at[wid], idx_v)
    base = wid * (n_chunks * SC_CHUNK)

    @pl.loop(0, n_chunks)
    def _(j):
        dst = pl.ds(base + j * SC_CHUNK, SC_CHUNK)
        pltpu.sync_copy(ys_hbm.at[idx_v.at[2 * j]], rows_v)
        pltpu.sync_copy(rows_v, g1_hbm.at[dst])
        pltpu.sync_copy(ys_hbm.at[idx_v.at[2 * j + 1]], rows_v)
        pltpu.sync_copy(rows_v, g2_hbm.at[dst])


def _collect(ys, idx, T):
    n_chunks = T // (SC_WORKERS * SC_CHUNK)
    out = jax.ShapeDtypeStruct((T, ROW_WORDS), jnp.uint32)
    return pl.kernel(
        functools.partial(_collect_body, n_chunks=n_chunks),
        out_type=(out, out),
        mesh=_sc_mesh(),
        scratch_types=[pltpu.VMEM((2 * n_chunks, SC_CHUNK), jnp.int32),
                       pltpu.VMEM((SC_CHUNK, ROW_WORDS), jnp.uint32)],
        name="collect",
    )(ys, idx)


def _moe_kernel(te_ref, nt_ref, xs_ref, wg_ref, wu_ref, wd_ref, ys_ref):
    @pl.when(pl.program_id(0) < nt_ref[0])
    def _():
        left, right = _unpack_bf16_pairs(xs_ref[...])
        xb = jnp.concatenate([left.astype(BF16), right.astype(BF16)], axis=1)
        g = _dot(xb, wg_ref[0])
        u = _dot(xb, wu_ref[0])
        a = (g * _sigmoid(g) * u).astype(BF16)
        ys_ref[...] = _pack_bf16_pairs(_dot(a, wd_ref[0]))


def _moe(xs, tile_expert, n_tiles, wg, wu, wd, tm):
    n_slots = xs.shape[0]
    row_blk = lambda i, te, nt: (jnp.minimum(i, nt[0] - 1), 0)
    w_blk = lambda i, te, nt: (te[i], 0, 0)
    return pl.pallas_call(
        _moe_kernel,
        out_shape=jax.ShapeDtypeStruct((n_slots, ROW_WORDS), jnp.uint32),
        grid_spec=pltpu.PrefetchScalarGridSpec(
            num_scalar_prefetch=2,
            grid=(n_slots // tm,),
            in_specs=[pl.BlockSpec((tm, ROW_WORDS), row_blk),
                      pl.BlockSpec((1, D_MODEL, D_EXPERT), w_blk),
                      pl.BlockSpec((1, D_MODEL, D_EXPERT), w_blk),
                      pl.BlockSpec((1, D_EXPERT, D_MODEL), w_blk)],
            out_specs=pl.BlockSpec((tm, ROW_WORDS), row_blk)),
        compiler_params=pltpu.CompilerParams(
            dimension_semantics=("arbitrary",), vmem_limit_bytes=VMEM_LIMIT),
        name="moe",
    )(tile_expert, n_tiles, xs, wg, wu, wd)


def _final_kernel(x1_ref, g1_ref, g2_ref, rw_ref, gate_ref, o_ref):
    rw = rw_ref[0]
    w1 = rw[:, 0:1]
    w2 = rw[:, 1:2]
    a_l, a_r = _unpack_bf16_pairs(g1_ref[0])
    b_l, b_r = _unpack_bf16_pairs(g2_ref[0])
    moe = jnp.concatenate([w1 * a_l + w2 * b_l, w1 * a_r + w2 * b_r], axis=1)
    o_ref[0] = x1_ref[0] + gate_ref[0] * moe


def _final(x1, g1, g2, rw, gate2, tm):
    B, S, _ = x1.shape
    tok = lambda w: pl.BlockSpec((1, tm, w), lambda b, i: (b, i, 0))
    return pl.pallas_call(
        _final_kernel,
        out_shape=jax.ShapeDtypeStruct((B, S, D_MODEL), F32),
        grid=(B, S // tm),
        in_specs=[tok(D_MODEL), tok(ROW_WORDS), tok(ROW_WORDS), tok(LANES),
                  pl.BlockSpec((1, 1, D_MODEL), lambda b, i: (b, 0, 0))],
        out_specs=tok(D_MODEL),
        compiler_params=pltpu.CompilerParams(
            dimension_semantics=("arbitrary", "arbitrary"), vmem_limit_bytes=VMEM_LIMIT),
        name="final",
    )(x1, g1, g2, rw, gate2)


def _pick(n, pref):
    t = min(n, pref)
    assert n % t == 0, (n, t)
    return t


def _route_plan(ridx, counts, tm_e, T):
    counts = counts.astype(jnp.int32)
    tiles = (counts + tm_e - 1) // tm_e
    tile_end = jnp.cumsum(tiles)
    offs = (tile_end - tiles) * tm_e
    slot1 = offs[ridx[:, 0]] + ridx[:, 2]
    slot2 = offs[ridx[:, 1]] + ridx[:, 3]
    n_chunks = T // (SC_WORKERS * SC_CHUNK)
    idx = jnp.stack([slot1.reshape(SC_WORKERS, n_chunks, SC_CHUNK),
                     slot2.reshape(SC_WORKERS, n_chunks, SC_CHUNK)], axis=2)
    idx = idx.reshape(SC_WORKERS, 2 * n_chunks, SC_CHUNK)
    n_tiles_max = 2 * T // tm_e + N_EXPERTS
    tile_ids = jnp.arange(n_tiles_max, dtype=jnp.int32)
    tile_expert = jnp.sum((tile_end[None, :] <= tile_ids[:, None]).astype(jnp.int32), axis=1)
    tile_expert = jnp.minimum(tile_expert, N_EXPERTS - 1)
    return idx, tile_expert, tile_end[-1:].astype(jnp.int32), n_tiles_max * tm_e


def _layer(x, c, w_ada, b_ada, norm1_w, w_in, b_forget, conv_w, q_norm_w, k_norm_w,
           w_out_conv, w_out_attn, w_o, norm2_w, w_rg, b_rg, w_re, b_re, w_gate, w_up, w_down):
    B, S, _ = x.shape
    T = B * S
    assert T % (SC_WORKERS * SC_CHUNK) == 0, T
    mod = _ada(c, w_ada, b_ada.reshape(1, -1)).reshape(B, 6, 1, D_MODEL)
    shift1, scale1, gate1, shift2, scale2, gate2 = (mod[:, t] for t in range(6))

    cuts = np.cumsum([0, CONV_WIDTH, CONV_WIDTH, CONV_WIDTH, ATTN_WIDTH, ATTN_WIDTH, ATTN_WIDTH,
                      N_HEADS, D_MODEL, D_MODEL])
    w_conv3 = w_in[:, cuts[0]:cuts[3]]
    w_qvT = jnp.concatenate([w_in[:, cuts[3]:cuts[4]], w_in[:, cuts[5]:cuts[6]]], axis=1).T.astype(BF16)
    w_k = w_in[:, cuts[4]:cuts[5]].astype(BF16)
    w_f = jnp.pad(w_in[:, cuts[6]:cuts[7]], ((0, 0), (0, LANES - N_HEADS))).astype(BF16)
    b_f = jnp.pad(b_forget, (0, LANES - N_HEADS)).reshape(1, LANES)
    w_cgg = jnp.concatenate([w_conv3, w_in[:, cuts[7]:cuts[9]]], axis=1).astype(BF16)

    tm_qkv = _pick(S, 512)
    qnwT = jnp.broadcast_to((jnp.tile(q_norm_w, N_HEADS) * (LOG2E * HEAD_DIM ** -0.5))[:, None],
                            (ATTN_WIDTH, LANES))
    qT_aug, k_aug, vT = _qkv(x, norm1_w.reshape(1, -1), shift1, scale1, w_qvT, w_k, w_f, b_f,
                             qnwT, jnp.tile(k_norm_w, N_HEADS).reshape(1, -1), tm_qkv)

    tq = _pick(S, 512)
    y_b = _attention(qT_aug, k_aug, vT, tq, _pick(tq, 256))

    w_r = jnp.pad(jnp.concatenate([w_re, w_rg], axis=1),
                  ((0, 0), (0, LANES - N_EXPERTS - N_GROUPS)))
    w_r_hi = w_r.astype(BF16)
    w_r_lo = (w_r - w_r_hi.astype(F32)).astype(BF16)
    b_r = jnp.pad(jnp.concatenate([b_re, b_rg]), (0, LANES - N_EXPERTS - N_GROUPS)).reshape(1, LANES)
    cw = jnp.pad(conv_w, ((0, 8 - CONV_K), (0, 0)))
    tm_post = _pick(S, 1024)
    x1, h2p, ridx, rw, counts = _post(x, y_b, norm1_w.reshape(1, -1), shift1, scale1, gate1,
                                      norm2_w.reshape(1, -1), shift2, scale2,
                                      w_cgg, cw, w_out_conv.astype(BF16), w_out_attn.astype(BF16),
                                      w_o.astype(BF16), jnp.stack([w_r_hi, w_r_lo]), b_r, tm_post)

    tm_e = 512
    idx, tile_expert, n_tiles, n_slots = _route_plan(
        ridx.reshape(T, LANES)[:, :4], counts[0, :N_EXPERTS], tm_e, T)
    xs = _dispatch(h2p.reshape(T, ROW_WORDS), idx, n_slots)
    ys = _moe(xs, tile_expert, n_tiles, w_gate.astype(BF16), w_up.astype(BF16),
              w_down.astype(BF16), tm_e)
    g1, g2 = _collect(ys, idx, T)
    return _final(x1, g1.reshape(B, S, ROW_WORDS), g2.reshape(B, S, ROW_WORDS), rw, gate2,
                  _pick(S, 512))


def kernel(x, c, w_ada, b_ada, norm1_w, w_in, b_forget, conv_w, q_norm_w, k_norm_w, w_out_conv,
           w_out_attn, w_o, norm2_w, w_router_group, b_router_group, w_router_expert,
           b_router_expert, w_gate, w_up, w_down):
    for l in range(w_ada.shape[0]):
        x = _layer(x, c, w_ada[l], b_ada[l], norm1_w[l], w_in[l], b_forget[l], conv_w[l],
                   q_norm_w[l], k_norm_w[l], w_out_conv[l], w_out_attn[l], w_o[l], norm2_w[l],
                   w_router_group[l], b_router_group[l], w_router_expert[l], b_router_expert[l],
                   w_gate[l], w_up[l], w_down[l])
    return x
```

```python
import functools

import jax
import jax.numpy as jnp
import numpy as np
from jax import lax
from jax.experimental import pallas as pl
from jax.experimental.pallas import tpu as pltpu
from jax.experimental.pallas import tpu_sc as plsc

D_MODEL = 1024
CONV_WIDTH = 512
CONV_K = 3
N_HEADS = 8
HEAD_DIM = 64
ATTN_WIDTH = N_HEADS * HEAD_DIM
N_PAIRS = N_HEADS // 2
N_GROUPS = 4
EXPERTS_PER_GROUP = 8
N_EXPERTS = N_GROUPS * EXPERTS_PER_GROUP
D_EXPERT = 256
EPS = 1e-6
LANES = 128
AUG = 2 * LANES
BIAS_W = 6
VROWS = HEAD_DIM + 16
NEG = -1e30
LOG2E = 1.4426950408889634

F32 = jnp.float32
BF16 = jnp.bfloat16
VMEM_LIMIT = 56 * 1024 * 1024


def _sigmoid(z):
    return 1.0 / (1.0 + jnp.exp(-z))


def _split3(z):
    hi = z.astype(BF16)
    r = z - hi.astype(F32)
    mid = r.astype(BF16)
    lo = (r - mid.astype(F32)).astype(BF16)
    return hi, mid, lo


def _dot(a, b):
    return jnp.dot(a, b, preferred_element_type=F32)


def _modulated_norm(x, nw, shift, scale):
    ms = jnp.mean(x * x, axis=-1, keepdims=True)
    return (x * lax.rsqrt(ms + EPS) * nw) * (1.0 + scale) + shift


def _ada_kernel(c_ref, w_ref, b_ref, o_ref):
    c = c_ref[...]
    a = c * _sigmoid(c)
    o_ref[...] = jnp.dot(a, w_ref[...], precision=lax.Precision.HIGHEST,
                         preferred_element_type=F32) + b_ref[...]


def _ada(c, w_ada, b_ada):
    B = c.shape[0]
    n = w_ada.shape[1] // D_MODEL
    return pl.pallas_call(
        _ada_kernel,
        out_shape=jax.ShapeDtypeStruct((B, n * D_MODEL), F32),
        grid=(n,),
        in_specs=[pl.BlockSpec((B, D_MODEL), lambda j: (0, 0)),
                  pl.BlockSpec((D_MODEL, D_MODEL), lambda j: (0, j)),
                  pl.BlockSpec((1, D_MODEL), lambda j: (0, j))],
        out_specs=pl.BlockSpec((B, D_MODEL), lambda j: (0, j)),
        compiler_params=pltpu.CompilerParams(dimension_semantics=("arbitrary",)),
        name="ada",
    )(c, w_ada, b_ada)


_NT = (((1,), (1,)), ((), ()))


def _lane_tile(a, width):
    return jnp.concatenate([a] * (width // a.shape[1]), axis=1)


def _qkv_kernel(x_ref, nw_ref, sh_ref, sc_ref, wqv_ref, wk_ref, wf_ref, bf_ref, qnw_ref, knw_ref,
                gsum_ref, pq_ref, pk_ref, cq_ref, ck_ref,
                qT_ref, k_ref, vT_ref, carry_ref):
    tm = x_ref.shape[1]
    h = _modulated_norm(x_ref[0], nw_ref[...], sh_ref[0], sc_ref[0])
    hb = h.astype(BF16)
    qvT = lax.dot_general(wqv_ref[...], hb, _NT, preferred_element_type=F32)
    k = _dot(hb, wk_ref[...])

    heads = []
    for hd in range(N_HEADS):
        z = qvT[hd * HEAD_DIM:(hd + 1) * HEAD_DIM]
        heads.append(z * lax.rsqrt(jnp.mean(z * z, axis=0, keepdims=True) + EPS))
    qnT = jnp.concatenate(heads, axis=0) * _lane_tile(qnw_ref[...], tm)

    ss = _dot((k * k).astype(BF16), gsum_ref[...])
    kn = k * lax.rsqrt(ss * (1.0 / HEAD_DIM) + EPS) * knw_ref[...]

    fl = _dot(hb, wf_ref[...]) + bf_ref[...]
    lf = jnp.minimum(fl, 0.0) - jnp.log(1.0 + jnp.exp(-jnp.abs(fl)))

    @pl.when(pl.program_id(1) == 0)
    def _():
        carry_ref[...] = jnp.zeros_like(carry_ref)

    row = lax.broadcasted_iota(jnp.int32, (tm, LANES), 0)
    cum = lf
    s = 1
    while s < tm:
        cum = cum + jnp.where(row >= s, pltpu.roll(cum, s, 0), 0.0)
        s *= 2
    cum = cum + carry_ref[7:8, :]
    carry_ref[...] = cum[tm - 8:, :]

    parts = jnp.concatenate(_split3(cum * LOG2E), axis=1)
    eqT = (lax.dot_general(pq_ref[...], parts, _NT, preferred_element_type=F32)
           + _lane_tile(cq_ref[...], tm)).astype(BF16)
    ek = (_dot(parts, pk_ref[...]) + ck_ref[...]).astype(BF16)
    for j in range(N_PAIRS):
        qT_ref[0, j, :LANES, :] = qnT[j * LANES:(j + 1) * LANES].astype(BF16)
        qT_ref[0, j, LANES:, :] = eqT
        for t in range(2):
            r0 = ATTN_WIDTH + (2 * j + t) * HEAD_DIM
            vT_ref[0, j, t * VROWS:t * VROWS + HEAD_DIM, :] = qvT[r0:r0 + HEAD_DIM].astype(BF16)
            vT_ref[0, j, t * VROWS + HEAD_DIM:(t + 1) * VROWS, :] = jnp.ones((VROWS - HEAD_DIM, tm), BF16)
        k_ref[0, :, j * AUG:j * AUG + LANES] = kn[:, j * LANES:(j + 1) * LANES].astype(BF16)
        k_ref[0, :, j * AUG + LANES:(j + 1) * AUG] = ek


def _bias_placement():
    pq = np.zeros((LANES, 3 * LANES), np.float32)
    pk = np.zeros((3 * LANES, LANES), np.float32)
    cq = np.zeros((LANES, LANES), np.float32)
    ck = np.zeros((1, LANES), np.float32)
    for hd in range(N_HEADS):
        base = BIAS_W * hd
        for p in range(3):
            pq[base + p, p * LANES + hd] = 1.0
            pk[p * LANES + hd, base + 3 + p] = -1.0
            cq[base + 3 + p, :] = 1.0
            ck[0, base + p] = 1.0
    return (jnp.asarray(pq, BF16), jnp.asarray(pk, BF16), jnp.asarray(cq), jnp.asarray(ck))


def _qkv(x, nw, shift, scale, wqvT, wk, wf, bf, qnwT, knw, tm):
    B, S, _ = x.shape
    gsum = jnp.asarray(np.kron(np.eye(N_HEADS), np.ones((HEAD_DIM, HEAD_DIM))), BF16)
    pq, pk, cq, ck = _bias_placement()
    const = lambda *shape: pl.BlockSpec(shape, lambda b, i: (0,) * len(shape),
                                        pipeline_mode=pl.Buffered(1))
    return pl.pallas_call(
        _qkv_kernel,
        out_shape=(jax.ShapeDtypeStruct((B, N_PAIRS, AUG, S), BF16),
                   jax.ShapeDtypeStruct((B, S, N_PAIRS * AUG), BF16),
                   jax.ShapeDtypeStruct((B, N_PAIRS, 2 * VROWS, S), BF16)),
        grid=(B, S // tm),
        in_specs=[pl.BlockSpec((1, tm, D_MODEL), lambda b, i: (b, i, 0)),
                  const(1, D_MODEL),
                  pl.BlockSpec((1, 1, D_MODEL), lambda b, i: (b, 0, 0)),
                  pl.BlockSpec((1, 1, D_MODEL), lambda b, i: (b, 0, 0)),
                  const(2 * ATTN_WIDTH, D_MODEL),
                  const(D_MODEL, ATTN_WIDTH),
                  const(D_MODEL, LANES),
                  const(1, LANES),
                  const(ATTN_WIDTH, LANES),
                  const(1, ATTN_WIDTH),
                  const(ATTN_WIDTH, ATTN_WIDTH),
                  const(LANES, 3 * LANES),
                  const(3 * LANES, LANES),
                  const(LANES, LANES),
                  const(1, LANES)],
        out_specs=(pl.BlockSpec((1, N_PAIRS, AUG, tm), lambda b, i: (b, 0, 0, i)),
                   pl.BlockSpec((1, tm, N_PAIRS * AUG), lambda b, i: (b, i, 0)),
                   pl.BlockSpec((1, N_PAIRS, 2 * VROWS, tm), lambda b, i: (b, 0, 0, i))),
        scratch_shapes=[pltpu.VMEM((8, LANES), F32)],
        compiler_params=pltpu.CompilerParams(
            dimension_semantics=("arbitrary", "arbitrary"), vmem_limit_bytes=VMEM_LIMIT),
        name="qkv",
    )(x, nw, shift, scale, wqvT, wk, wf, bf, qnwT, knw, gsum, pq, pk, cq, ck)


def _attn_kernel(qT_ref, k_ref, vT_ref, o_ref, qq_ref, s_ref, smax_ref, m_ref, acc_ref, *, tq, tk, cw):
    qi = pl.program_id(2)
    n = tq // cw
    chains = [(t, c) for t in range(2) for c in range(n)]
    qT = qT_ref[0, 0]
    feat = lax.broadcasted_iota(jnp.int32, (AUG, tq), 0)
    for t in range(2):
        bias0 = LANES + BIAS_W * (2 * pl.program_id(1) + t)
        keep = ((feat >= t * HEAD_DIM) & (feat < (t + 1) * HEAD_DIM)) | \
               ((feat >= bias0) & (feat < bias0 + BIAS_W))
        qh = jnp.where(keep, qT, jnp.zeros_like(qT))
        for c in range(n):
            qq_ref[t * n + c] = qh[:, c * cw:(c + 1) * cw]
    kpos = lax.broadcasted_iota(jnp.int32, (tk, cw), 0)
    qpos = lax.broadcasted_iota(jnp.int32, (tk, cw), 1)

    def scores(j, slot):
        start = pl.multiple_of(j * tk, tk)
        for ci in range(len(chains)):
            s = _dot(k_ref[0, pl.ds(start, tk), :], qq_ref[ci])
            s_ref[slot, ci] = s
            smax_ref[slot, ci] = jnp.broadcast_to(jnp.max(s, axis=0, keepdims=True), (8, cw))

    def absorb(j, slot, diagonal=False):
        start = pl.multiple_of(j * tk, tk)
        for ci, (t, c) in enumerate(chains):
            vj = vT_ref[0, 0, t * VROWS:(t + 1) * VROWS, pl.ds(start, tk)]
            s = s_ref[slot, ci]
            if diagonal:
                s = jnp.where(kpos <= qpos + c * cw, s, NEG)
                smax = jnp.max(s, axis=0, keepdims=True)
            else:
                smax = smax_ref[slot, ci, 0:1]
            m = m_ref[ci, 0:1]
            m_new = jnp.maximum(m, smax)
            p = jnp.exp2(s - m_new).astype(BF16)
            acc_ref[ci] = jnp.exp2(m - m_new) * acc_ref[ci] + _dot(vj, p)
            m_ref[ci] = jnp.broadcast_to(m_new, (8, cw))

    m_ref[...] = jnp.full(m_ref.shape, NEG, F32)
    acc_ref[...] = jnp.zeros(acc_ref.shape, F32)
    scores(0, 0)

    def two_blocks(jj, _):
        j = 2 * jj
        scores(j + 1, 1)
        absorb(j, 0)
        scores(j + 2, 0)
        absorb(j + 1, 1)
        return 0

    lax.fori_loop(0, qi // 2, two_blocks, 0)
    odd = qi % 2 == 1

    @pl.when(odd)
    def _():
        scores(qi, 1)
        absorb(qi - 1, 0)
        absorb(qi, 1, diagonal=True)

    @pl.when(jnp.logical_not(odd))
    def _():
        absorb(qi, 0, diagonal=True)

    outs = [acc_ref[ci, :HEAD_DIM] / acc_ref[ci, HEAD_DIM:HEAD_DIM + 1] for ci in range(len(chains))]
    oT = jnp.concatenate([jnp.concatenate(outs[:n], axis=1), jnp.concatenate(outs[n:], axis=1)], axis=0)
    o_ref[0] = oT.T.astype(BF16)


def _attention(qT_aug, k_aug, vT, tq, cw):
    B, S, _ = k_aug.shape
    n_chains = 2 * tq // cw
    return pl.pallas_call(
        functools.partial(_attn_kernel, tq=tq, tk=tq, cw=cw),
        scratch_shapes=[pltpu.VMEM((n_chains, AUG, cw), BF16),
                        pltpu.VMEM((2, n_chains, tq, cw), F32),
                        pltpu.VMEM((2, n_chains, 8, cw), F32),
                        pltpu.VMEM((n_chains, 8, cw), F32),
                        pltpu.VMEM((n_chains, VROWS, cw), F32)],
        out_shape=jax.ShapeDtypeStruct((B, S, ATTN_WIDTH), BF16),
        grid=(B, N_PAIRS, S // tq),
        in_specs=[pl.BlockSpec((1, 1, AUG, tq), lambda b, j, i: (b, j, 0, i)),
                  pl.BlockSpec((1, S, AUG), lambda b, j, i: (b, 0, j)),
                  pl.BlockSpec((1, 1, 2 * VROWS, S), lambda b, j, i: (b, j, 0, 0))],
        out_specs=pl.BlockSpec((1, tq, LANES), lambda b, j, i: (b, i, j)),
        compiler_params=pltpu.CompilerParams(
            dimension_semantics=("arbitrary", "arbitrary", "arbitrary"),
            vmem_limit_bytes=VMEM_LIMIT),
        name="attn",
    )(qT_aug, k_aug, vT)


def _pack_bf16_pairs(z):
    w = z.shape[1] // 2
    bits = pltpu.bitcast(z.astype(BF16).astype(F32), jnp.uint32)
    return bits[:, :w] | (bits[:, w:] >> 16)


def _unpack_bf16_pairs(p):
    return (pltpu.bitcast(p & jnp.uint32(0xFFFF0000), F32), pltpu.bitcast(p << 16, F32))


POST_ROWS = 256


def _post_kernel(x_ref, yb_ref, n1_ref, sh1_ref, sc1_ref, g1_ref, n2_ref, sh2_ref, sc2_ref,
                 wc_ref, cw_ref, woc_ref, woa_ref, wo_ref, wr_ref, br_ref, tri_ref,
                 x1_ref, h2_ref, ridx_ref, rw_ref, cnt_ref, carry_ref):
    tm = x_ref.shape[1]
    n_grp = max(tm // POST_ROWS, 1)
    rows = tm // n_grp
    lane = lax.broadcasted_iota(jnp.int32, (rows, LANES), 1)
    row8 = lax.broadcasted_iota(jnp.int32, (8, CONV_WIDTH), 0)
    big = jnp.int32(1 << 20)

    @pl.when(pl.program_id(1) == 0)
    def _():
        carry_ref[...] = jnp.zeros_like(carry_ref)

    @pl.when((pl.program_id(0) == 0) & (pl.program_id(1) == 0))
    def _():
        cnt_ref[...] = jnp.zeros_like(cnt_ref)

    st = [dict(rs=pl.ds(g * rows, rows)) for g in range(n_grp)]

    def conv_in(d):
        d["x"] = x_ref[0, d["rs"], :]
        d["hb"] = _modulated_norm(d["x"], n1_ref[...], sh1_ref[0], sc1_ref[0]).astype(BF16)
        d["x_in"] = _dot(d["hb"], wc_ref[:, :CONV_WIDTH])
        d["conv_c"] = _dot(d["hb"], wc_ref[:, 2 * CONV_WIDTH:3 * CONV_WIDTH])
        d["conv_b"] = _dot(d["hb"], wc_ref[:, CONV_WIDTH:2 * CONV_WIDTH])

    def conv(d, prev):
        u = d.pop("conv_c") * d.pop("x_in")
        d["u_tail"] = u[rows - 8:, :]

        def shifted(k):
            r = pltpu.roll(u, k, 0)
            top = jnp.where(row8 < k, pltpu.roll(prev, k, 0), r[:8])
            return jnp.concatenate([top, r[8:]], axis=0)

        cw = cw_ref[...]
        cv = cw[0:1] * shifted(2) + cw[1:2] * shifted(1) + cw[2:3] * u
        d["y_a"] = (d.pop("conv_b") * cv).astype(BF16)

    def gates(d):
        d["p_b"] = _dot(yb_ref[0, d["rs"], :], woa_ref[...])
        d["gate_c"] = _dot(d["hb"], wc_ref[:, 3 * CONV_WIDTH:3 * CONV_WIDTH + D_MODEL])
        d["gate_a"] = _dot(d.pop("hb"), wc_ref[:, 3 * CONV_WIDTH + D_MODEL:])

    def branch_a(d):
        d["p_a"] = _dot(d.pop("y_a"), woc_ref[...])

    def merge(d):
        d["merged"] = (_sigmoid(d.pop("gate_c")) * d.pop("p_a")
                       + _sigmoid(d.pop("gate_a")) * d.pop("p_b")).astype(BF16)

    def out_proj(d):
        d["o"] = _dot(d.pop("merged"), wo_ref[...])

    def residual(d):
        x1 = d.pop("x") + g1_ref[0] * d.pop("o")
        x1_ref[0, d["rs"], :] = x1
        h2 = _modulated_norm(x1, n2_ref[...], sh2_ref[0], sc2_ref[0])
        h2_ref[0, d["rs"], :] = _pack_bf16_pairs(h2)
        d["h_hi"] = h2.astype(BF16)
        d["h_lo"] = (h2 - d["h_hi"].astype(F32)).astype(BF16)

    def router(d):
        h_hi = d.pop("h_hi")
        d["lg"] = (_dot(h_hi, wr_ref[0]) + _dot(d.pop("h_lo"), wr_ref[0]) + _dot(h_hi, wr_ref[1])) \
            + br_ref[...]

    def first_argmax(vals):
        mx = jnp.max(vals, axis=-1, keepdims=True)
        idx = jnp.min(jnp.where(vals == mx, lane, big), axis=-1, keepdims=True)
        return mx, idx

    def route(d):
        lg = d.pop("lg")
        is_g = (lane >= N_EXPERTS) & (lane < N_EXPERTS + N_GROUPS)
        g_mx, g_lane = first_argmax(jnp.where(is_g, lg, NEG))
        p_sel = 1.0 / jnp.sum(jnp.where(is_g, jnp.exp(lg - g_mx), 0.0), axis=-1, keepdims=True)
        g_idx = g_lane - N_EXPERTS
        in_g = (lane >= g_idx * EXPERTS_PER_GROUP) & (lane < (g_idx + 1) * EXPERTS_PER_GROUP)
        le = jnp.where(in_g, lg, NEG)
        v1, i1 = first_argmax(le)
        v2, i2 = first_argmax(jnp.where(lane == i1, NEG, le))
        e2 = jnp.exp(v2 - v1)
        w1 = p_sel / (1.0 + e2)
        w2 = w1 * e2
        rw_ref[0, d["rs"], :] = jnp.where(lane == 0, w1, 0.0) + jnp.where(lane == 1, w2, 0.0)
        d["i1"], d["i2"] = i1, i2
        d["onehot"] = jnp.where((lane == i1) | (lane == i2), 1.0, 0.0)

    stages = [conv_in, None, gates, branch_a, merge, out_proj, residual, router, route]
    lag = 2
    for step in range(len(stages) + lag * (n_grp - 1)):
        for g, d in enumerate(st):
            k = step - lag * g
            if 0 <= k < len(stages):
                if stages[k] is None:
                    conv(d, carry_ref[...] if g == 0 else st[g - 1]["u_tail"])
                else:
                    stages[k](d)
    carry_ref[...] = st[-1]["u_tail"]

    onehot = jnp.concatenate([d["onehot"] for d in st], axis=0)
    before = _dot(tri_ref[...], onehot.astype(BF16)) + cnt_ref[0:1, :]
    cnt_ref[...] = cnt_ref[...] + jnp.sum(onehot, axis=0, keepdims=True)
    for g, d in enumerate(st):
        bg = before[g * rows:(g + 1) * rows]
        i1, i2 = d["i1"], d["i2"]
        r1 = jnp.sum(jnp.where(lane == i1, bg, 0.0), axis=-1, keepdims=True)
        r2 = jnp.sum(jnp.where(lane == i2, bg, 0.0), axis=-1, keepdims=True)
        rec = (jnp.where(lane == 0, i1.astype(F32), 0.0) + jnp.where(lane == 1, i2.astype(F32), 0.0)
               + jnp.where(lane == 2, r1, 0.0) + jnp.where(lane == 3, r2, 0.0))
        ridx_ref[0, :, d["rs"]] = rec.T[:8].astype(jnp.int32)


def _post(x, yb, n1, sh1, sc1, g1, n2, sh2, sc2, wc, cw, woc, woa, wo, wr, br, tm):
    B, S, _ = x.shape
    tri = jnp.asarray(np.tril(np.ones((tm, tm), np.float32), -1), BF16)
    const = lambda *shape: pl.BlockSpec(shape, lambda b, i: (0,) * len(shape),
                                        pipeline_mode=pl.Buffered(1))
    perb = pl.BlockSpec((1, 1, D_MODEL), lambda b, i: (b, 0, 0))
    tok = lambda w: pl.BlockSpec((1, tm, w), lambda b, i: (b, i, 0))
    return pl.pallas_call(
        _post_kernel,
        out_shape=(jax.ShapeDtypeStruct((B, S, D_MODEL), F32),
                   jax.ShapeDtypeStruct((B, S, D_MODEL // 2), jnp.uint32),
                   jax.ShapeDtypeStruct((B, 8, S), jnp.int32),
                   jax.ShapeDtypeStruct((B, S, LANES), F32),
                   jax.ShapeDtypeStruct((8, LANES), F32)),
        grid=(B, S // tm),
        in_specs=[tok(D_MODEL), tok(ATTN_WIDTH), const(1, D_MODEL), perb, perb, perb,
                  const(1, D_MODEL), perb, perb,
                  const(D_MODEL, 3 * CONV_WIDTH + 2 * D_MODEL),
                  const(8, CONV_WIDTH),
                  const(CONV_WIDTH, D_MODEL), const(ATTN_WIDTH, D_MODEL),
                  const(D_MODEL, D_MODEL),
                  const(2, D_MODEL, LANES), const(1, LANES), const(tm, tm)],
        out_specs=(tok(D_MODEL), tok(D_MODEL // 2),
                   pl.BlockSpec((1, 8, tm), lambda b, i: (b, 0, i)), tok(LANES),
                   pl.BlockSpec((8, LANES), lambda b, i: (0, 0))),
        scratch_shapes=[pltpu.VMEM((8, CONV_WIDTH), F32)],
        compiler_params=pltpu.CompilerParams(
            dimension_semantics=("arbitrary", "arbitrary"), vmem_limit_bytes=VMEM_LIMIT),
        name="post",
    )(x, yb, n1, sh1, sc1, g1, n2, sh2, sc2, wc, cw, woc, woa, wo, wr, br, tri)


SC_CORES = 2
SC_SUBCORES = 16
SC_WORKERS = SC_CORES * SC_SUBCORES
SC_CHUNK = 64
ROW_WORDS = D_MODEL // 2


def _sc_mesh():
    return plsc.VectorSubcoreMesh(core_axis_name="c", subcore_axis_name="s",
                                  num_cores=SC_CORES, num_subcores=SC_SUBCORES)


def _dispatch_body(rows_hbm, idx1_hbm, idx2_hbm, xs_hbm, idx1_v, idx2_v, rows_v, *, n_chunks):
    wid = lax.axis_index("s") * SC_CORES + lax.axis_index("c")
    pltpu.sync_copy(idx1_hbm.at[wid], idx1_v)
    pltpu.sync_copy(idx2_hbm.at[wid], idx2_v)
    base = wid * (n_chunks * SC_CHUNK)

    @pl.loop(0, n_chunks)
    def _(j):
        pltpu.sync_copy(rows_hbm.at[pl.ds(base + j * SC_CHUNK, SC_CHUNK)], rows_v)
        pltpu.sync_copy(rows_v, xs_hbm.at[idx1_v.at[j]])
        pltpu.sync_copy(rows_v, xs_hbm.at[idx2_v.at[j]])


def _sc_scratch(n_chunks):
    return [pltpu.VMEM((n_chunks, SC_CHUNK), jnp.int32), pltpu.VMEM((n_chunks, SC_CHUNK), jnp.int32),
            pltpu.VMEM((SC_CHUNK, ROW_WORDS), jnp.uint32)]


def _dispatch(rows, idx1, idx2, n_slots):
    n_chunks = idx1.shape[1]
    return pl.kernel(
        functools.partial(_dispatch_body, n_chunks=n_chunks),
        out_type=jax.ShapeDtypeStruct((n_slots, ROW_WORDS), jnp.uint32),
        mesh=_sc_mesh(),
        scratch_types=_sc_scratch(n_chunks),
        name="dispatch",
    )(rows, idx1, idx2)


def _collect_body(ys_hbm, idx1_hbm, idx2_hbm, g1_hbm, g2_hbm, idx1_v, idx2_v, rows_v, *, n_chunks):
    wid = lax.axis_index("s") * SC_CORES + lax.axis_index("c")
    pltpu.sync_copy(idx1_hbm.at[wid], idx1_v)
    pltpu.sync_copy(idx2_hbm.at[wid], idx2_v)
    base = wid * (n_chunks * SC_CHUNK)

    @pl.loop(0, n_chunks)
    def _(j):
        dst = pl.ds(base + j * SC_CHUNK, SC_CHUNK)
        pltpu.sync_copy(ys_hbm.at[idx1_v.at[j]], rows_v)
        pltpu.sync_copy(rows_v, g1_hbm.at[dst])
        pltpu.sync_copy(ys_hbm.at[idx2_v.at[j]], rows_v)
        pltpu.sync_copy(rows_v, g2_hbm.at[dst])


def _collect(ys, idx1, idx2):
    n_chunks = idx1.shape[1]
    out = jax.ShapeDtypeStruct((SC_WORKERS * n_chunks * SC_CHUNK, ROW_WORDS), jnp.uint32)
    return pl.kernel(
        functools.partial(_collect_body, n_chunks=n_chunks),
        out_type=(out, out),
        mesh=_sc_mesh(),
        scratch_types=_sc_scratch(n_chunks),
        name="collect",
    )(ys, idx1, idx2)


def _moe_kernel(te_ref, nt_ref, xs_ref, wg_ref, wu_ref, wd_ref, ys_ref):
    @pl.when(pl.program_id(0) < nt_ref[0])
    def _():
        left, right = _unpack_bf16_pairs(xs_ref[...])
        xb = jnp.concatenate([left.astype(BF16), right.astype(BF16)], axis=1)
        g = _dot(xb, wg_ref[0].astype(BF16))
        u = _dot(xb, wu_ref[0].astype(BF16))
        a = (g * _sigmoid(g) * u).astype(BF16)
        ys_ref[...] = _pack_bf16_pairs(_dot(a, wd_ref[0].astype(BF16)))


def _moe(xs, tile_expert, n_tiles, wg, wu, wd, tm):
    n_slots = xs.shape[0]
    row_blk = lambda i, te, nt: (jnp.minimum(i, nt[0] - 1), 0)
    w_blk = lambda i, te, nt: (te[i], 0, 0)
    return pl.pallas_call(
        _moe_kernel,
        out_shape=jax.ShapeDtypeStruct((n_slots, ROW_WORDS), jnp.uint32),
        grid_spec=pltpu.PrefetchScalarGridSpec(
            num_scalar_prefetch=2,
            grid=(n_slots // tm,),
            in_specs=[pl.BlockSpec((tm, ROW_WORDS), row_blk),
                      pl.BlockSpec((1, D_MODEL, D_EXPERT), w_blk),
                      pl.BlockSpec((1, D_MODEL, D_EXPERT), w_blk),
                      pl.BlockSpec((1, D_EXPERT, D_MODEL), w_blk)],
            out_specs=pl.BlockSpec((tm, ROW_WORDS), row_blk)),
        compiler_params=pltpu.CompilerParams(
            dimension_semantics=("arbitrary",), vmem_limit_bytes=VMEM_LIMIT),
        name="moe",
    )(tile_expert, n_tiles, xs, wg, wu, wd)


def _final_kernel(x1_ref, g1_ref, g2_ref, rw_ref, gate_ref, o_ref):
    rw = rw_ref[0]
    w1 = rw[:, 0:1]
    w2 = rw[:, 1:2]
    a_l, a_r = _unpack_bf16_pairs(g1_ref[0])
    b_l, b_r = _unpack_bf16_pairs(g2_ref[0])
    moe = jnp.concatenate([w1 * a_l + w2 * b_l, w1 * a_r + w2 * b_r], axis=1)
    o_ref[0] = x1_ref[0] + gate_ref[0] * moe


def _final(x1, g1, g2, rw, gate2, tm):
    B, S, _ = x1.shape
    tok = lambda w: pl.BlockSpec((1, tm, w), lambda b, i: (b, i, 0))
    return pl.pallas_call(
        _final_kernel,
        out_shape=jax.ShapeDtypeStruct((B, S, D_MODEL), F32),
        grid=(B, S // tm),
        in_specs=[tok(D_MODEL), tok(ROW_WORDS), tok(ROW_WORDS), tok(LANES),
                  pl.BlockSpec((1, 1, D_MODEL), lambda b, i: (b, 0, 0))],
        out_specs=tok(D_MODEL),
        compiler_params=pltpu.CompilerParams(
            dimension_semantics=("arbitrary", "arbitrary"), vmem_limit_bytes=VMEM_LIMIT),
        name="final",
    )(x1, g1, g2, rw, gate2)


def _pick(n, pref):
    t = min(n, pref)
    assert n % t == 0, (n, t)
    return t


def _slots_kernel(offs_ref, r_ref, o_ref):
    r = r_ref[0]
    base = jnp.zeros_like(r)
    for e in range(N_EXPERTS):
        base = jnp.where(r == e, offs_ref[e], base)
    o_ref[0] = base + pltpu.roll(r, 6, 0)


def _slots(ridx, offs):
    B, _, S = ridx.shape
    return pl.pallas_call(
        _slots_kernel,
        out_shape=jax.ShapeDtypeStruct((B, 8, S), jnp.int32),
        grid_spec=pltpu.PrefetchScalarGridSpec(
            num_scalar_prefetch=1, grid=(B,),
            in_specs=[pl.BlockSpec((1, 8, S), lambda b, offs: (b, 0, 0))],
            out_specs=pl.BlockSpec((1, 8, S), lambda b, offs: (b, 0, 0))),
        compiler_params=pltpu.CompilerParams(dimension_semantics=("arbitrary",)),
        name="slots",
    )(offs, ridx)


def _route_plan(ridx, counts, tm_e, T):
    counts = counts.astype(jnp.int32)
    tiles = (counts + tm_e - 1) // tm_e
    tile_end = jnp.cumsum(tiles)
    offs = (tile_end - tiles) * tm_e
    slots = _slots(ridx, offs)
    n_chunks = T // (SC_WORKERS * SC_CHUNK)
    idx1 = slots[:, 0, :].reshape(SC_WORKERS, n_chunks, SC_CHUNK)
    idx2 = slots[:, 1, :].reshape(SC_WORKERS, n_chunks, SC_CHUNK)
    n_tiles_max = 2 * T // tm_e + N_EXPERTS
    tile_ids = jnp.arange(n_tiles_max, dtype=jnp.int32)
    tile_expert = jnp.sum((tile_end[None, :] <= tile_ids[:, None]).astype(jnp.int32), axis=1)
    tile_expert = jnp.minimum(tile_expert, N_EXPERTS - 1)
    return idx1, idx2, tile_expert, tile_end[-1:].astype(jnp.int32), n_tiles_max * tm_e


def _layer(x, c, w_ada, b_ada, norm1_w, w_in, b_forget, conv_w, q_norm_w, k_norm_w,
           w_out_conv, w_out_attn, w_o, norm2_w, w_rg, b_rg, w_re, b_re, w_gate, w_up, w_down):
    B, S, _ = x.shape
    T = B * S
    assert T % (SC_WORKERS * SC_CHUNK) == 0, T
    mod = _ada(c, w_ada, b_ada.reshape(1, -1)).reshape(B, 6, 1, D_MODEL)
    shift1, scale1, gate1, shift2, scale2, gate2 = (mod[:, t] for t in range(6))

    cuts = np.cumsum([0, CONV_WIDTH, CONV_WIDTH, CONV_WIDTH, ATTN_WIDTH, ATTN_WIDTH, ATTN_WIDTH,
                      N_HEADS, D_MODEL, D_MODEL])
    w_conv3 = w_in[:, cuts[0]:cuts[3]]
    w_qvT = jnp.concatenate([w_in[:, cuts[3]:cuts[4]], w_in[:, cuts[5]:cuts[6]]], axis=1).T.astype(BF16)
    w_k = w_in[:, cuts[4]:cuts[5]].astype(BF16)
    w_f = jnp.pad(w_in[:, cuts[6]:cuts[7]], ((0, 0), (0, LANES - N_HEADS))).astype(BF16)
    b_f = jnp.pad(b_forget, (0, LANES - N_HEADS)).reshape(1, LANES)
    w_cgg = jnp.concatenate([w_conv3, w_in[:, cuts[7]:cuts[9]]], axis=1).astype(BF16)

    tm_qkv = _pick(S, 512)
    qnwT = jnp.broadcast_to((jnp.tile(q_norm_w, N_HEADS) * (LOG2E * HEAD_DIM ** -0.5))[:, None],
                            (ATTN_WIDTH, LANES))
    qT_aug, k_aug, vT = _qkv(x, norm1_w.reshape(1, -1), shift1, scale1, w_qvT, w_k, w_f, b_f,
                             qnwT, jnp.tile(k_norm_w, N_HEADS).reshape(1, -1), tm_qkv)

    tq = _pick(S, 512)
    y_b = _attention(qT_aug, k_aug, vT, tq, _pick(tq, 256))

    w_r = jnp.pad(jnp.concatenate([w_re, w_rg], axis=1),
                  ((0, 0), (0, LANES - N_EXPERTS - N_GROUPS)))
    w_r_hi = w_r.astype(BF16)
    w_r_lo = (w_r - w_r_hi.astype(F32)).astype(BF16)
    b_r = jnp.pad(jnp.concatenate([b_re, b_rg]), (0, LANES - N_EXPERTS - N_GROUPS)).reshape(1, LANES)
    cw = jnp.pad(conv_w, ((0, 8 - CONV_K), (0, 0)))
    tm_post = _pick(S, 1024)
    x1, h2p, ridx, rw, counts = _post(x, y_b, norm1_w.reshape(1, -1), shift1, scale1, gate1,
                                      norm2_w.reshape(1, -1), shift2, scale2,
                                      w_cgg, cw, w_out_conv.astype(BF16), w_out_attn.astype(BF16),
                                      w_o.astype(BF16), jnp.stack([w_r_hi, w_r_lo]), b_r, tm_post)

    tm_e = 512
    idx1, idx2, tile_expert, n_tiles, n_slots = _route_plan(ridx, counts[0, :N_EXPERTS], tm_e, T)
    xs = _dispatch(h2p.reshape(T, ROW_WORDS), idx1, idx2, n_slots)
    ys = _moe(xs, tile_expert, n_tiles, w_gate, w_up, w_down, tm_e)
    g1, g2 = _collect(ys, idx1, idx2)
    return _final(x1, g1.reshape(B, S, ROW_WORDS), g2.reshape(B, S, ROW_WORDS), rw, gate2,
                  _pick(S, 512))


def kernel(x, c, w_ada, b_ada, norm1_w, w_in, b_forget, conv_w, q_norm_w, k_norm_w, w_out_conv,
           w_out_attn, w_o, norm2_w, w_router_group, b_router_group, w_router_expert,
           b_router_expert, w_gate, w_up, w_down):
    for l in range(w_ada.shape[0]):
        x = _layer(x, c, w_ada[l], b_ada[l], norm1_w[l], w_in[l], b_forget[l], conv_w[l],
                   q_norm_w[l], k_norm_w[l], w_out_conv[l], w_out_attn[l], w_o[l], norm2_w[l],
                   w_router_group[l], b_router_group[l], w_router_expert[l], b_router_expert[l],
                   w_gate[l], w_up[l], w_down[l])
    return x
```

```python
import functools

import jax
import jax.numpy as jnp
import numpy as np
from jax import lax
from jax.experimental import pallas as pl
from jax.experimental.pallas import tpu as pltpu
from jax.experimental.pallas import tpu_sc as plsc

D_MODEL = 1024
CONV_WIDTH = 512
CONV_K = 3
N_HEADS = 8
HEAD_DIM = 64
ATTN_WIDTH = N_HEADS * HEAD_DIM
N_PAIRS = N_HEADS // 2
N_GROUPS = 4
EXPERTS_PER_GROUP = 8
N_EXPERTS = N_GROUPS * EXPERTS_PER_GROUP
D_EXPERT = 256
EPS = 1e-6
LANES = 128
AUG = 2 * LANES
BIAS_W = 6
VROWS = HEAD_DIM + 16
NEG = -1e30
LOG2E = 1.4426950408889634

F32 = jnp.float32
BF16 = jnp.bfloat16
VMEM_LIMIT = 56 * 1024 * 1024


def _sigmoid(z):
    return 1.0 / (1.0 + jnp.exp(-z))


def _split3(z):
    hi = z.astype(BF16)
    r = z - hi.astype(F32)
    mid = r.astype(BF16)
    lo = (r - mid.astype(F32)).astype(BF16)
    return hi, mid, lo


def _dot(a, b):
    return jnp.dot(a, b, preferred_element_type=F32)


def _modulated_norm(x, nw, shift, scale):
    ms = jnp.mean(x * x, axis=-1, keepdims=True)
    return (x * lax.rsqrt(ms + EPS) * nw) * (1.0 + scale) + shift


def _ada_kernel(c_ref, w_ref, b_ref, o_ref):
    c = c_ref[...]
    a = c * _sigmoid(c)
    o_ref[...] = jnp.dot(a, w_ref[...], precision=lax.Precision.HIGHEST,
                         preferred_element_type=F32) + b_ref[...]


def _ada(c, w_ada, b_ada):
    B = c.shape[0]
    n = w_ada.shape[1] // D_MODEL
    return pl.pallas_call(
        _ada_kernel,
        out_shape=jax.ShapeDtypeStruct((B, n * D_MODEL), F32),
        grid=(n,),
        in_specs=[pl.BlockSpec((B, D_MODEL), lambda j: (0, 0)),
                  pl.BlockSpec((D_MODEL, D_MODEL), lambda j: (0, j)),
                  pl.BlockSpec((1, D_MODEL), lambda j: (0, j))],
        out_specs=pl.BlockSpec((B, D_MODEL), lambda j: (0, j)),
        compiler_params=pltpu.CompilerParams(dimension_semantics=("arbitrary",)),
        name="ada",
    )(c, w_ada, b_ada)


_NT = (((1,), (1,)), ((), ()))


def _lane_tile(a, width):
    return jnp.concatenate([a] * (width // a.shape[1]), axis=1)


def _qkv_kernel(x_ref, nw_ref, sh_ref, sc_ref, wqv_ref, wk_ref, wf_ref, bf_ref, qnw_ref, knw_ref,
                gsum_ref, pq_ref, pk_ref, cq_ref, ck_ref,
                qT_ref, k_ref, vT_ref, carry_ref):
    tm = x_ref.shape[1]
    h = _modulated_norm(x_ref[0], nw_ref[...], sh_ref[0], sc_ref[0])
    hb = h.astype(BF16)
    qvT = lax.dot_general(wqv_ref[...], hb, _NT, preferred_element_type=F32)
    k = _dot(hb, wk_ref[...])

    heads = []
    for hd in range(N_HEADS):
        z = qvT[hd * HEAD_DIM:(hd + 1) * HEAD_DIM]
        heads.append(z * lax.rsqrt(jnp.mean(z * z, axis=0, keepdims=True) + EPS))
    qnT = jnp.concatenate(heads, axis=0) * _lane_tile(qnw_ref[...], tm)

    ss = _dot((k * k).astype(BF16), gsum_ref[...])
    kn = k * lax.rsqrt(ss * (1.0 / HEAD_DIM) + EPS) * knw_ref[...]

    fl = _dot(hb, wf_ref[...]) + bf_ref[...]
    lf = jnp.minimum(fl, 0.0) - jnp.log(1.0 + jnp.exp(-jnp.abs(fl)))

    @pl.when(pl.program_id(1) == 0)
    def _():
        carry_ref[...] = jnp.zeros_like(carry_ref)

    row = lax.broadcasted_iota(jnp.int32, (tm, LANES), 0)
    cum = lf
    s = 1
    while s < tm:
        cum = cum + jnp.where(row >= s, pltpu.roll(cum, s, 0), 0.0)
        s *= 2
    cum = cum + carry_ref[7:8, :]
    carry_ref[...] = cum[tm - 8:, :]

    parts = jnp.concatenate(_split3(cum * LOG2E), axis=1)
    eqT = (lax.dot_general(pq_ref[...], parts, _NT, preferred_element_type=F32)
           + _lane_tile(cq_ref[...], tm)).astype(BF16)
    ek = (_dot(parts, pk_ref[...]) + ck_ref[...]).astype(BF16)
    for j in range(N_PAIRS):
        qT_ref[0, j, :LANES, :] = qnT[j * LANES:(j + 1) * LANES].astype(BF16)
        qT_ref[0, j, LANES:, :] = eqT
        for t in range(2):
            r0 = ATTN_WIDTH + (2 * j + t) * HEAD_DIM
            vT_ref[0, j, t * VROWS:t * VROWS + HEAD_DIM, :] = qvT[r0:r0 + HEAD_DIM].astype(BF16)
            vT_ref[0, j, t * VROWS + HEAD_DIM:(t + 1) * VROWS, :] = jnp.ones((VROWS - HEAD_DIM, tm), BF16)
        k_ref[0, :, j * AUG:j * AUG + LANES] = kn[:, j * LANES:(j + 1) * LANES].astype(BF16)
        k_ref[0, :, j * AUG + LANES:(j + 1) * AUG] = ek


def _bias_placement():
    pq = np.zeros((LANES, 3 * LANES), np.float32)
    pk = np.zeros((3 * LANES, LANES), np.float32)
    cq = np.zeros((LANES, LANES), np.float32)
    ck = np.zeros((1, LANES), np.float32)
    for hd in range(N_HEADS):
        base = BIAS_W * hd
        for p in range(3):
            pq[base + p, p * LANES + hd] = 1.0
            pk[p * LANES + hd, base + 3 + p] = -1.0
            cq[base + 3 + p, :] = 1.0
            ck[0, base + p] = 1.0
    return (jnp.asarray(pq, BF16), jnp.asarray(pk, BF16), jnp.asarray(cq), jnp.asarray(ck))


def _qkv(x, nw, shift, scale, wqvT, wk, wf, bf, qnwT, knw, tm, b0, B):
    S = x.shape[1]
    gsum = jnp.asarray(np.kron(np.eye(N_HEADS), np.ones((HEAD_DIM, HEAD_DIM))), BF16)
    pq, pk, cq, ck = _bias_placement()
    const = lambda *shape: pl.BlockSpec(shape, lambda b, i: (0,) * len(shape),
                                        pipeline_mode=pl.Buffered(1))
    return pl.pallas_call(
        _qkv_kernel,
        out_shape=(jax.ShapeDtypeStruct((B, N_PAIRS, AUG, S), BF16),
                   jax.ShapeDtypeStruct((B, S, N_PAIRS * AUG), BF16),
                   jax.ShapeDtypeStruct((B, N_PAIRS, 2 * VROWS, S), BF16)),
        grid=(B, S // tm),
        in_specs=[pl.BlockSpec((1, tm, D_MODEL), lambda b, i: (b + b0, i, 0)),
                  const(1, D_MODEL),
                  pl.BlockSpec((1, 1, D_MODEL), lambda b, i: (b + b0, 0, 0)),
                  pl.BlockSpec((1, 1, D_MODEL), lambda b, i: (b + b0, 0, 0)),
                  const(2 * ATTN_WIDTH, D_MODEL),
                  const(D_MODEL, ATTN_WIDTH),
                  const(D_MODEL, LANES),
                  const(1, LANES),
                  const(ATTN_WIDTH, LANES),
                  const(1, ATTN_WIDTH),
                  const(ATTN_WIDTH, ATTN_WIDTH),
                  const(LANES, 3 * LANES),
                  const(3 * LANES, LANES),
                  const(LANES, LANES),
                  const(1, LANES)],
        out_specs=(pl.BlockSpec((1, N_PAIRS, AUG, tm), lambda b, i: (b, 0, 0, i)),
                   pl.BlockSpec((1, tm, N_PAIRS * AUG), lambda b, i: (b, i, 0)),
                   pl.BlockSpec((1, N_PAIRS, 2 * VROWS, tm), lambda b, i: (b, 0, 0, i))),
        scratch_shapes=[pltpu.VMEM((8, LANES), F32)],
        compiler_params=pltpu.CompilerParams(
            dimension_semantics=("arbitrary", "arbitrary"), vmem_limit_bytes=VMEM_LIMIT),
        name="qkv",
    )(x, nw, shift, scale, wqvT, wk, wf, bf, qnwT, knw, gsum, pq, pk, cq, ck)


def _attn_kernel(qT_ref, k_ref, vT_ref, o_ref, qq_ref, s_ref, smax_ref, m_ref, acc_ref, *, tq, tk, cw):
    qi = pl.program_id(2)
    n = tq // cw
    chains = [(t, c) for t in range(2) for c in range(n)]
    qT = qT_ref[0, 0]
    feat = lax.broadcasted_iota(jnp.int32, (AUG, tq), 0)
    for t in range(2):
        bias0 = LANES + BIAS_W * (2 * pl.program_id(1) + t)
        keep = ((feat >= t * HEAD_DIM) & (feat < (t + 1) * HEAD_DIM)) | \
               ((feat >= bias0) & (feat < bias0 + BIAS_W))
        qh = jnp.where(keep, qT, jnp.zeros_like(qT))
        for c in range(n):
            qq_ref[t * n + c] = qh[:, c * cw:(c + 1) * cw]
    kpos = lax.broadcasted_iota(jnp.int32, (tk, cw), 0)
    qpos = lax.broadcasted_iota(jnp.int32, (tk, cw), 1)

    def scores(j, slot):
        start = pl.multiple_of(j * tk, tk)
        for ci in range(len(chains)):
            s = _dot(k_ref[0, pl.ds(start, tk), :], qq_ref[ci])
            s_ref[slot, ci] = s
            smax_ref[slot, ci] = jnp.broadcast_to(jnp.max(s, axis=0, keepdims=True), (8, cw))

    def absorb(j, slot, diagonal=False):
        start = pl.multiple_of(j * tk, tk)
        for ci, (t, c) in enumerate(chains):
            vj = vT_ref[0, 0, t * VROWS:(t + 1) * VROWS, pl.ds(start, tk)]
            s = s_ref[slot, ci]
            if diagonal:
                s = jnp.where(kpos <= qpos + c * cw, s, NEG)
                smax = jnp.max(s, axis=0, keepdims=True)
            else:
                smax = smax_ref[slot, ci, 0:1]
            m = m_ref[ci, 0:1]
            m_new = jnp.maximum(m, smax)
            p = jnp.exp2(s - m_new).astype(BF16)
            acc_ref[ci] = jnp.exp2(m - m_new) * acc_ref[ci] + _dot(vj, p)
            m_ref[ci] = jnp.broadcast_to(m_new, (8, cw))

    m_ref[...] = jnp.full(m_ref.shape, NEG, F32)
    acc_ref[...] = jnp.zeros(acc_ref.shape, F32)
    scores(0, 0)

    def two_blocks(jj, _):
        j = 2 * jj
        scores(j + 1, 1)
        absorb(j, 0)
        scores(j + 2, 0)
        absorb(j + 1, 1)
        return 0

    lax.fori_loop(0, qi // 2, two_blocks, 0)
    odd = qi % 2 == 1

    @pl.when(odd)
    def _():
        scores(qi, 1)
        absorb(qi - 1, 0)
        absorb(qi, 1, diagonal=True)

    @pl.when(jnp.logical_not(odd))
    def _():
        absorb(qi, 0, diagonal=True)

    outs = [acc_ref[ci, :HEAD_DIM] / acc_ref[ci, HEAD_DIM:HEAD_DIM + 1] for ci in range(len(chains))]
    oT = jnp.concatenate([jnp.concatenate(outs[:n], axis=1), jnp.concatenate(outs[n:], axis=1)], axis=0)
    o_ref[0] = oT.T.astype(BF16)


def _attention(qT_aug, k_aug, vT, tq, cw):
    B, S, _ = k_aug.shape
    n_chains = 2 * tq // cw
    return pl.pallas_call(
        functools.partial(_attn_kernel, tq=tq, tk=tq, cw=cw),
        scratch_shapes=[pltpu.VMEM((n_chains, AUG, cw), BF16),
                        pltpu.VMEM((2, n_chains, tq, cw), F32),
                        pltpu.VMEM((2, n_chains, 8, cw), F32),
                        pltpu.VMEM((n_chains, 8, cw), F32),
                        pltpu.VMEM((n_chains, VROWS, cw), F32)],
        out_shape=jax.ShapeDtypeStruct((B, S, ATTN_WIDTH), BF16),
        grid=(B, N_PAIRS, S // tq),
        in_specs=[pl.BlockSpec((1, 1, AUG, tq), lambda b, j, i: (b, j, 0, i)),
                  pl.BlockSpec((1, S, AUG), lambda b, j, i: (b, 0, j)),
                  pl.BlockSpec((1, 1, 2 * VROWS, S), lambda b, j, i: (b, j, 0, 0))],
        out_specs=pl.BlockSpec((1, tq, LANES), lambda b, j, i: (b, i, j)),
        compiler_params=pltpu.CompilerParams(
            dimension_semantics=("arbitrary", "arbitrary", "arbitrary"),
            vmem_limit_bytes=VMEM_LIMIT),
        name="attn",
    )(qT_aug, k_aug, vT)


def _pack_bf16_pairs(z):
    w = z.shape[1] // 2
    bits = pltpu.bitcast(z.astype(BF16).astype(F32), jnp.uint32)
    return bits[:, :w] | (bits[:, w:] >> 16)


def _unpack_bf16_pairs(p):
    return (pltpu.bitcast(p & jnp.uint32(0xFFFF0000), F32), pltpu.bitcast(p << 16, F32))


POST_ROWS = 256


def _post_kernel(x_ref, yb_ref, n1_ref, sh1_ref, sc1_ref, g1_ref, n2_ref, sh2_ref, sc2_ref,
                 wc_ref, cw_ref, woc_ref, woa_ref, wo_ref, wr_ref, br_ref, tri_ref,
                 x1_ref, h2_ref, ridx_ref, rw_ref, cnt_ref, carry_ref):
    tm = x_ref.shape[1]
    n_grp = max(tm // POST_ROWS, 1)
    rows = tm // n_grp
    lane = lax.broadcasted_iota(jnp.int32, (rows, LANES), 1)
    row8 = lax.broadcasted_iota(jnp.int32, (8, CONV_WIDTH), 0)
    big = jnp.int32(1 << 20)

    @pl.when(pl.program_id(1) == 0)
    def _():
        carry_ref[...] = jnp.zeros_like(carry_ref)

    @pl.when((pl.program_id(0) == 0) & (pl.program_id(1) == 0))
    def _():
        cnt_ref[...] = jnp.zeros_like(cnt_ref)

    st = [dict(rs=pl.ds(g * rows, rows)) for g in range(n_grp)]

    def conv_in(d):
        d["x"] = x_ref[0, d["rs"], :]
        d["hb"] = _modulated_norm(d["x"], n1_ref[...], sh1_ref[0], sc1_ref[0]).astype(BF16)
        d["x_in"] = _dot(d["hb"], wc_ref[:, :CONV_WIDTH])
        d["conv_c"] = _dot(d["hb"], wc_ref[:, 2 * CONV_WIDTH:3 * CONV_WIDTH])
        d["conv_b"] = _dot(d["hb"], wc_ref[:, CONV_WIDTH:2 * CONV_WIDTH])

    def conv(d, prev):
        u = d.pop("conv_c") * d.pop("x_in")
        d["u_tail"] = u[rows - 8:, :]

        def shifted(k):
            r = pltpu.roll(u, k, 0)
            top = jnp.where(row8 < k, pltpu.roll(prev, k, 0), r[:8])
            return jnp.concatenate([top, r[8:]], axis=0)

        cw = cw_ref[...]
        cv = cw[0:1] * shifted(2) + cw[1:2] * shifted(1) + cw[2:3] * u
        d["y_a"] = (d.pop("conv_b") * cv).astype(BF16)

    def gates(d):
        d["p_b"] = _dot(yb_ref[0, d["rs"], :], woa_ref[...])
        d["gate_c"] = _dot(d["hb"], wc_ref[:, 3 * CONV_WIDTH:3 * CONV_WIDTH + D_MODEL])
        d["gate_a"] = _dot(d.pop("hb"), wc_ref[:, 3 * CONV_WIDTH + D_MODEL:])

    def branch_a(d):
        d["p_a"] = _dot(d.pop("y_a"), woc_ref[...])

    def merge(d):
        d["merged"] = (_sigmoid(d.pop("gate_c")) * d.pop("p_a")
                       + _sigmoid(d.pop("gate_a")) * d.pop("p_b")).astype(BF16)

    def out_proj(d):
        d["o"] = _dot(d.pop("merged"), wo_ref[...])

    def residual(d):
        x1 = d.pop("x") + g1_ref[0] * d.pop("o")
        x1_ref[0, d["rs"], :] = x1
        h2 = _modulated_norm(x1, n2_ref[...], sh2_ref[0], sc2_ref[0])
        h2_ref[0, d["rs"], :] = _pack_bf16_pairs(h2)
        d["h_hi"] = h2.astype(BF16)
        d["h_lo"] = (h2 - d["h_hi"].astype(F32)).astype(BF16)

    def router(d):
        h_hi = d.pop("h_hi")
        d["lg"] = (_dot(h_hi, wr_ref[0]) + _dot(d.pop("h_lo"), wr_ref[0]) + _dot(h_hi, wr_ref[1])) \
            + br_ref[...]

    def first_argmax(vals):
        mx = jnp.max(vals, axis=-1, keepdims=True)
        idx = jnp.min(jnp.where(vals == mx, lane, big), axis=-1, keepdims=True)
        return mx, idx

    def route(d):
        lg = d.pop("lg")
        is_g = (lane >= N_EXPERTS) & (lane < N_EXPERTS + N_GROUPS)
        g_mx, g_lane = first_argmax(jnp.where(is_g, lg, NEG))
        p_sel = 1.0 / jnp.sum(jnp.where(is_g, jnp.exp(lg - g_mx), 0.0), axis=-1, keepdims=True)
        g_idx = g_lane - N_EXPERTS
        in_g = (lane >= g_idx * EXPERTS_PER_GROUP) & (lane < (g_idx + 1) * EXPERTS_PER_GROUP)
        le = jnp.where(in_g, lg, NEG)
        v1, i1 = first_argmax(le)
        v2, i2 = first_argmax(jnp.where(lane == i1, NEG, le))
        e2 = jnp.exp(v2 - v1)
        w1 = p_sel / (1.0 + e2)
        w2 = w1 * e2
        rw_ref[0, d["rs"], :] = jnp.where(lane == 0, w1, 0.0) + jnp.where(lane == 1, w2, 0.0)
        d["i1"], d["i2"] = i1, i2
        d["onehot"] = jnp.where((lane == i1) | (lane == i2), 1.0, 0.0)

    stages = [conv_in, None, gates, branch_a, merge, out_proj, residual, router, route]
    lag = 2
    for step in range(len(stages) + lag * (n_grp - 1)):
        for g, d in enumerate(st):
            k = step - lag * g
            if 0 <= k < len(stages):
                if stages[k] is None:
                    conv(d, carry_ref[...] if g == 0 else st[g - 1]["u_tail"])
                else:
                    stages[k](d)
    carry_ref[...] = st[-1]["u_tail"]

    onehot = jnp.concatenate([d["onehot"] for d in st], axis=0)
    before = _dot(tri_ref[...], onehot.astype(BF16)) + cnt_ref[0:1, :]
    cnt_ref[...] = cnt_ref[...] + jnp.sum(onehot, axis=0, keepdims=True)
    for g, d in enumerate(st):
        bg = before[g * rows:(g + 1) * rows]
        i1, i2 = d["i1"], d["i2"]
        r1 = jnp.sum(jnp.where(lane == i1, bg, 0.0), axis=-1, keepdims=True)
        r2 = jnp.sum(jnp.where(lane == i2, bg, 0.0), axis=-1, keepdims=True)
        rec = (jnp.where(lane == 0, i1.astype(F32), 0.0) + jnp.where(lane == 1, i2.astype(F32), 0.0)
               + jnp.where(lane == 2, r1, 0.0) + jnp.where(lane == 3, r2, 0.0))
        ridx_ref[0, :, d["rs"]] = rec.T[:8].astype(jnp.int32)


def _post(x, yb, n1, sh1, sc1, g1, n2, sh2, sc2, wc, cw, woc, woa, wo, wr, br, tm, b0):
    B, S, _ = yb.shape
    tri = jnp.asarray(np.tril(np.ones((tm, tm), np.float32), -1), BF16)
    const = lambda *shape: pl.BlockSpec(shape, lambda b, i: (0,) * len(shape),
                                        pipeline_mode=pl.Buffered(1))
    perb = pl.BlockSpec((1, 1, D_MODEL), lambda b, i: (b + b0, 0, 0))
    tok = lambda w: pl.BlockSpec((1, tm, w), lambda b, i: (b, i, 0))
    return pl.pallas_call(
        _post_kernel,
        out_shape=(jax.ShapeDtypeStruct((B, S, D_MODEL), F32),
                   jax.ShapeDtypeStruct((B, S, D_MODEL // 2), jnp.uint32),
                   jax.ShapeDtypeStruct((B, 8, S), jnp.int32),
                   jax.ShapeDtypeStruct((B, S, LANES), F32),
                   jax.ShapeDtypeStruct((8, LANES), F32)),
        grid=(B, S // tm),
        in_specs=[pl.BlockSpec((1, tm, D_MODEL), lambda b, i: (b + b0, i, 0)), tok(ATTN_WIDTH),
                  const(1, D_MODEL), perb, perb, perb,
                  const(1, D_MODEL), perb, perb,
                  const(D_MODEL, 3 * CONV_WIDTH + 2 * D_MODEL),
                  const(8, CONV_WIDTH),
                  const(CONV_WIDTH, D_MODEL), const(ATTN_WIDTH, D_MODEL),
                  const(D_MODEL, D_MODEL),
                  const(2, D_MODEL, LANES), const(1, LANES), const(tm, tm)],
        out_specs=(tok(D_MODEL), tok(D_MODEL // 2),
                   pl.BlockSpec((1, 8, tm), lambda b, i: (b, 0, i)), tok(LANES),
                   pl.BlockSpec((8, LANES), lambda b, i: (0, 0))),
        scratch_shapes=[pltpu.VMEM((8, CONV_WIDTH), F32)],
        compiler_params=pltpu.CompilerParams(
            dimension_semantics=("arbitrary", "arbitrary"), vmem_limit_bytes=VMEM_LIMIT),
        name="post",
    )(x, yb, n1, sh1, sc1, g1, n2, sh2, sc2, wc, cw, woc, woa, wo, wr, br, tri)


SC_CORES = 2
SC_SUBCORES = 16
SC_WORKERS = SC_CORES * SC_SUBCORES
SC_CHUNK = 64
ROW_WORDS = D_MODEL // 2


def _sc_mesh():
    return plsc.VectorSubcoreMesh(core_axis_name="c", subcore_axis_name="s",
                                  num_cores=SC_CORES, num_subcores=SC_SUBCORES)


def _dispatch_body(rows_hbm, idx1_hbm, idx2_hbm, xs_hbm, idx1_v, idx2_v, rows_v, *, n_chunks):
    wid = lax.axis_index("s") * SC_CORES + lax.axis_index("c")
    pltpu.sync_copy(idx1_hbm.at[wid], idx1_v)
    pltpu.sync_copy(idx2_hbm.at[wid], idx2_v)
    base = wid * (n_chunks * SC_CHUNK)

    @pl.loop(0, n_chunks)
    def _(j):
        pltpu.sync_copy(rows_hbm.at[pl.ds(base + j * SC_CHUNK, SC_CHUNK)], rows_v)
        pltpu.sync_copy(rows_v, xs_hbm.at[idx1_v.at[j]])
        pltpu.sync_copy(rows_v, xs_hbm.at[idx2_v.at[j]])


def _sc_scratch(n_chunks):
    return [pltpu.VMEM((n_chunks, SC_CHUNK), jnp.int32), pltpu.VMEM((n_chunks, SC_CHUNK), jnp.int32),
            pltpu.VMEM((SC_CHUNK, ROW_WORDS), jnp.uint32)]


def _dispatch(rows, idx1, idx2, n_slots):
    n_chunks = idx1.shape[1]
    return pl.kernel(
        functools.partial(_dispatch_body, n_chunks=n_chunks),
        out_type=jax.ShapeDtypeStruct((n_slots, ROW_WORDS), jnp.uint32),
        mesh=_sc_mesh(),
        scratch_types=_sc_scratch(n_chunks),
        name="dispatch",
    )(rows, idx1, idx2)


def _collect_body(ys_hbm, idx1_hbm, idx2_hbm, g1_hbm, g2_hbm, idx1_v, idx2_v, rows_v, *, n_chunks):
    wid = lax.axis_index("s") * SC_CORES + lax.axis_index("c")
    pltpu.sync_copy(idx1_hbm.at[wid], idx1_v)
    pltpu.sync_copy(idx2_hbm.at[wid], idx2_v)
    base = wid * (n_chunks * SC_CHUNK)

    @pl.loop(0, n_chunks)
    def _(j):
        dst = pl.ds(base + j * SC_CHUNK, SC_CHUNK)
        pltpu.sync_copy(ys_hbm.at[idx1_v.at[j]], rows_v)
        pltpu.sync_copy(rows_v, g1_hbm.at[dst])
        pltpu.sync_copy(ys_hbm.at[idx2_v.at[j]], rows_v)
        pltpu.sync_copy(rows_v, g2_hbm.at[dst])


def _collect(ys, idx1, idx2):
    n_chunks = idx1.shape[1]
    out = jax.ShapeDtypeStruct((SC_WORKERS * n_chunks * SC_CHUNK, ROW_WORDS), jnp.uint32)
    return pl.kernel(
        functools.partial(_collect_body, n_chunks=n_chunks),
        out_type=(out, out),
        mesh=_sc_mesh(),
        scratch_types=_sc_scratch(n_chunks),
        name="collect",
    )(ys, idx1, idx2)


def _moe_kernel(te_ref, nt_ref, xs_ref, wg_ref, wu_ref, wd_ref, ys_ref):
    @pl.when(pl.program_id(0) < nt_ref[0])
    def _():
        left, right = _unpack_bf16_pairs(xs_ref[...])
        xb = jnp.concatenate([left.astype(BF16), right.astype(BF16)], axis=1)
        g = _dot(xb, wg_ref[0].astype(BF16))
        u = _dot(xb, wu_ref[0].astype(BF16))
        a = (g * _sigmoid(g) * u).astype(BF16)
        ys_ref[...] = _pack_bf16_pairs(_dot(a, wd_ref[0].astype(BF16)))


def _moe(xs, tile_expert, n_tiles, wg, wu, wd, tm):
    n_slots = xs.shape[0]
    row_blk = lambda i, te, nt: (jnp.minimum(i, nt[0] - 1), 0)
    w_blk = lambda i, te, nt: (te[i], 0, 0)
    return pl.pallas_call(
        _moe_kernel,
        out_shape=jax.ShapeDtypeStruct((n_slots, ROW_WORDS), jnp.uint32),
        grid_spec=pltpu.PrefetchScalarGridSpec(
            num_scalar_prefetch=2,
            grid=(n_slots // tm,),
            in_specs=[pl.BlockSpec((tm, ROW_WORDS), row_blk),
                      pl.BlockSpec((1, D_MODEL, D_EXPERT), w_blk),
                      pl.BlockSpec((1, D_MODEL, D_EXPERT), w_blk),
                      pl.BlockSpec((1, D_EXPERT, D_MODEL), w_blk)],
            out_specs=pl.BlockSpec((tm, ROW_WORDS), row_blk)),
        compiler_params=pltpu.CompilerParams(
            dimension_semantics=("arbitrary",), vmem_limit_bytes=VMEM_LIMIT),
        name="moe",
    )(tile_expert, n_tiles, xs, wg, wu, wd)


def _final_kernel(x1_ref, g1_ref, g2_ref, rw_ref, gate_ref, *rest):
    o_ref = rest[-1]
    rw = rw_ref[0]
    w1 = rw[:, 0:1]
    w2 = rw[:, 1:2]
    a_l, a_r = _unpack_bf16_pairs(g1_ref[0])
    b_l, b_r = _unpack_bf16_pairs(g2_ref[0])
    moe = jnp.concatenate([w1 * a_l + w2 * b_l, w1 * a_r + w2 * b_r], axis=1)
    o_ref[0] = x1_ref[0] + gate_ref[0] * moe


def _final(x1, g1, g2, rw, gate2, tm, b0, out_prev):
    Bg, S, _ = x1.shape
    tok = lambda w: pl.BlockSpec((1, tm, w), lambda b, i: (b, i, 0))
    prev = () if out_prev is None else (out_prev,)
    return pl.pallas_call(
        _final_kernel,
        out_shape=jax.ShapeDtypeStruct((gate2.shape[0], S, D_MODEL), F32),
        grid=(Bg, S // tm),
        in_specs=[tok(D_MODEL), tok(ROW_WORDS), tok(ROW_WORDS), tok(LANES),
                  pl.BlockSpec((1, 1, D_MODEL), lambda b, i: (b + b0, 0, 0))]
        + [pl.BlockSpec(memory_space=pl.ANY)] * len(prev),
        out_specs=pl.BlockSpec((1, tm, D_MODEL), lambda b, i: (b + b0, i, 0)),
        input_output_aliases={5: 0} if prev else {},
        compiler_params=pltpu.CompilerParams(
            dimension_semantics=("arbitrary", "arbitrary"), vmem_limit_bytes=VMEM_LIMIT),
        name="final",
    )(x1, g1, g2, rw, gate2, *prev)


def _pick(n, pref):
    t = min(n, pref)
    assert n % t == 0, (n, t)
    return t


def _slots_kernel(offs_ref, r_ref, o_ref):
    r = r_ref[0]
    base = jnp.zeros_like(r)
    for e in range(N_EXPERTS):
        base = jnp.where(r == e, offs_ref[e], base)
    o_ref[0] = base + pltpu.roll(r, 6, 0)


def _slots(ridx, offs):
    B, _, S = ridx.shape
    return pl.pallas_call(
        _slots_kernel,
        out_shape=jax.ShapeDtypeStruct((B, 8, S), jnp.int32),
        grid_spec=pltpu.PrefetchScalarGridSpec(
            num_scalar_prefetch=1, grid=(B,),
            in_specs=[pl.BlockSpec((1, 8, S), lambda b, offs: (b, 0, 0))],
            out_specs=pl.BlockSpec((1, 8, S), lambda b, offs: (b, 0, 0))),
        compiler_params=pltpu.CompilerParams(dimension_semantics=("arbitrary",)),
        name="slots",
    )(offs, ridx)


def _route_plan(ridx, counts, tm_e, T):
    counts = counts.astype(jnp.int32)
    tiles = (counts + tm_e - 1) // tm_e
    tile_end = jnp.cumsum(tiles)
    offs = (tile_end - tiles) * tm_e
    slots = _slots(ridx, offs)
    n_chunks = T // (SC_WORKERS * SC_CHUNK)
    idx1 = slots[:, 0, :].reshape(SC_WORKERS, n_chunks, SC_CHUNK)
    idx2 = slots[:, 1, :].reshape(SC_WORKERS, n_chunks, SC_CHUNK)
    n_tiles_max = 2 * T // tm_e + N_EXPERTS
    tile_ids = jnp.arange(n_tiles_max, dtype=jnp.int32)
    tile_expert = jnp.sum((tile_end[None, :] <= tile_ids[:, None]).astype(jnp.int32), axis=1)
    tile_expert = jnp.minimum(tile_expert, N_EXPERTS - 1)
    return idx1, idx2, tile_expert, tile_end[-1:].astype(jnp.int32), n_tiles_max * tm_e


def _layer(x, c, w_ada, b_ada, norm1_w, w_in, b_forget, conv_w, q_norm_w, k_norm_w,
           w_out_conv, w_out_attn, w_o, norm2_w, w_rg, b_rg, w_re, b_re, w_gate, w_up, w_down):
    B, S, _ = x.shape
    n_grp = 2 if B % 2 == 0 and (B // 2 * S) % (SC_WORKERS * SC_CHUNK) == 0 else 1
    Bg = B // n_grp
    T = Bg * S
    assert T % (SC_WORKERS * SC_CHUNK) == 0, T
    mod = _ada(c, w_ada, b_ada.reshape(1, -1)).reshape(B, 6, 1, D_MODEL)
    shift1, scale1, gate1, shift2, scale2, gate2 = (mod[:, t] for t in range(6))

    cuts = np.cumsum([0, CONV_WIDTH, CONV_WIDTH, CONV_WIDTH, ATTN_WIDTH, ATTN_WIDTH, ATTN_WIDTH,
                      N_HEADS, D_MODEL, D_MODEL])
    w_conv3 = w_in[:, cuts[0]:cuts[3]]
    w_qvT = jnp.concatenate([w_in[:, cuts[3]:cuts[4]], w_in[:, cuts[5]:cuts[6]]], axis=1).T.astype(BF16)
    w_k = w_in[:, cuts[4]:cuts[5]].astype(BF16)
    w_f = jnp.pad(w_in[:, cuts[6]:cuts[7]], ((0, 0), (0, LANES - N_HEADS))).astype(BF16)
    b_f = jnp.pad(b_forget, (0, LANES - N_HEADS)).reshape(1, LANES)
    w_cgg = jnp.concatenate([w_conv3, w_in[:, cuts[7]:cuts[9]]], axis=1).astype(BF16)

    tm_qkv = _pick(S, 512)
    qnwT = jnp.broadcast_to((jnp.tile(q_norm_w, N_HEADS) * (LOG2E * HEAD_DIM ** -0.5))[:, None],
                            (ATTN_WIDTH, LANES))
    tq = _pick(S, 512)
    w_r = jnp.pad(jnp.concatenate([w_re, w_rg], axis=1),
                  ((0, 0), (0, LANES - N_EXPERTS - N_GROUPS)))
    w_r_hi = w_r.astype(BF16)
    w_r_lo = (w_r - w_r_hi.astype(F32)).astype(BF16)
    b_r = jnp.pad(jnp.concatenate([b_re, b_rg]), (0, LANES - N_EXPERTS - N_GROUPS)).reshape(1, LANES)
    cw = jnp.pad(conv_w, ((0, 8 - CONV_K), (0, 0)))
    tm_post = _pick(S, 1024)
    tm_e = 512
    w_oc, w_oa, w_ob, w_rs = (w_out_conv.astype(BF16), w_out_attn.astype(BF16), w_o.astype(BF16),
                              jnp.stack([w_r_hi, w_r_lo]))

    routed = []
    for g in range(n_grp):
        b0 = g * Bg
        qT_aug, k_aug, vT = _qkv(x, norm1_w.reshape(1, -1), shift1, scale1, w_qvT, w_k, w_f, b_f,
                                 qnwT, jnp.tile(k_norm_w, N_HEADS).reshape(1, -1), tm_qkv, b0, Bg)
        y_b = _attention(qT_aug, k_aug, vT, tq, _pick(tq, 256))
        x1, h2p, ridx, rw, counts = _post(x, y_b, norm1_w.reshape(1, -1), shift1, scale1, gate1,
                                          norm2_w.reshape(1, -1), shift2, scale2,
                                          w_cgg, cw, w_oc, w_oa, w_ob, w_rs, b_r, tm_post, b0)
        idx1, idx2, tile_expert, n_tiles, n_slots = _route_plan(ridx, counts[0, :N_EXPERTS], tm_e, T)
        xs = _dispatch(h2p.reshape(T, ROW_WORDS), idx1, idx2, n_slots)
        routed.append((x1, rw, idx1, idx2, tile_expert, n_tiles, xs))

    gathered = []
    for x1, rw, idx1, idx2, tile_expert, n_tiles, xs in routed:
        ys = _moe(xs, tile_expert, n_tiles, w_gate, w_up, w_down, tm_e)
        gathered.append(_collect(ys, idx1, idx2))

    out = None
    for g, ((x1, rw, *_), (g1, g2)) in enumerate(zip(routed, gathered)):
        out = _final(x1, g1.reshape(Bg, S, ROW_WORDS), g2.reshape(Bg, S, ROW_WORDS), rw, gate2,
                     _pick(S, 512), g * Bg, out)
    return out


def kernel(x, c, w_ada, b_ada, norm1_w, w_in, b_forget, conv_w, q_norm_w, k_norm_w, w_out_conv,
           w_out_attn, w_o, norm2_w, w_router_group, b_router_group, w_router_expert,
           b_router_expert, w_gate, w_up, w_down):
    for l in range(w_ada.shape[0]):
        x = _layer(x, c, w_ada[l], b_ada[l], norm1_w[l], w_in[l], b_forget[l], conv_w[l],
                   q_norm_w[l], k_norm_w[l], w_out_conv[l], w_out_attn[l], w_o[l], norm2_w[l],
                   w_router_group[l], b_router_group[l], w_router_expert[l], b_router_expert[l],
                   w_gate[l], w_up[l], w_down[l])
    return x
```

```python
import functools

import jax
import jax.numpy as jnp
import numpy as np
from jax import lax
from jax.experimental import pallas as pl
from jax.experimental.pallas import tpu as pltpu
from jax.experimental.pallas import tpu_sc as plsc

D_MODEL = 1024
CONV_WIDTH = 512
CONV_K = 3
N_HEADS = 8
HEAD_DIM = 64
ATTN_WIDTH = N_HEADS * HEAD_DIM
N_PAIRS = N_HEADS // 2
N_GROUPS = 4
EXPERTS_PER_GROUP = 8
N_EXPERTS = N_GROUPS * EXPERTS_PER_GROUP
D_EXPERT = 256
EPS = 1e-6
LANES = 128
AUG = 2 * LANES
BIAS_W = 6
VROWS = HEAD_DIM + 16
NEG = -1e30
LOG2E = 1.4426950408889634

F32 = jnp.float32
BF16 = jnp.bfloat16
VMEM_LIMIT = 56 * 1024 * 1024


def _sigmoid(z):
    return 1.0 / (1.0 + jnp.exp(-z))


def _split3(z):
    hi = z.astype(BF16)
    r = z - hi.astype(F32)
    mid = r.astype(BF16)
    lo = (r - mid.astype(F32)).astype(BF16)
    return hi, mid, lo


def _dot(a, b):
    return jnp.dot(a, b, preferred_element_type=F32)


def _modulated_norm(x, nw, shift, scale):
    ms = jnp.mean(x * x, axis=-1, keepdims=True)
    return (x * lax.rsqrt(ms + EPS) * nw) * (1.0 + scale) + shift


def _ada_kernel(c_ref, w_ref, b_ref, o_ref):
    c = c_ref[...]
    a = c * _sigmoid(c)
    o_ref[...] = jnp.dot(a, w_ref[...], precision=lax.Precision.HIGHEST,
                         preferred_element_type=F32) + b_ref[...]


def _ada(c, w_ada, b_ada):
    B = c.shape[0]
    n = w_ada.shape[1] // D_MODEL
    return pl.pallas_call(
        _ada_kernel,
        out_shape=jax.ShapeDtypeStruct((B, n * D_MODEL), F32),
        grid=(n,),
        in_specs=[pl.BlockSpec((B, D_MODEL), lambda j: (0, 0)),
                  pl.BlockSpec((D_MODEL, D_MODEL), lambda j: (0, j)),
                  pl.BlockSpec((1, D_MODEL), lambda j: (0, j))],
        out_specs=pl.BlockSpec((B, D_MODEL), lambda j: (0, j)),
        compiler_params=pltpu.CompilerParams(dimension_semantics=("arbitrary",)),
        name="ada",
    )(c, w_ada, b_ada)


_NT = (((1,), (1,)), ((), ()))


def _lane_tile(a, width):
    return jnp.concatenate([a] * (width // a.shape[1]), axis=1)


def _qkv_kernel(x_ref, nw_ref, sh_ref, sc_ref, wqv_ref, wk_ref, wf_ref, bf_ref, qnw_ref, knw_ref,
                gsum_ref, pq_ref, pk_ref, cq_ref, ck_ref,
                qT_ref, k_ref, vT_ref, carry_ref):
    tm = x_ref.shape[1]
    h = _modulated_norm(x_ref[0], nw_ref[...], sh_ref[0], sc_ref[0])
    hb = h.astype(BF16)
    qvT = lax.dot_general(wqv_ref[...], hb, _NT, preferred_element_type=F32)
    k = _dot(hb, wk_ref[...])

    heads = []
    for hd in range(N_HEADS):
        z = qvT[hd * HEAD_DIM:(hd + 1) * HEAD_DIM]
        heads.append(z * lax.rsqrt(jnp.mean(z * z, axis=0, keepdims=True) + EPS))
    qnT = jnp.concatenate(heads, axis=0) * _lane_tile(qnw_ref[...], tm)

    ss = _dot((k * k).astype(BF16), gsum_ref[...])
    kn = k * lax.rsqrt(ss * (1.0 / HEAD_DIM) + EPS) * knw_ref[...]

    fl = _dot(hb, wf_ref[...]) + bf_ref[...]
    lf = jnp.minimum(fl, 0.0) - jnp.log(1.0 + jnp.exp(-jnp.abs(fl)))

    @pl.when(pl.program_id(1) == 0)
    def _():
        carry_ref[...] = jnp.zeros_like(carry_ref)

    row = lax.broadcasted_iota(jnp.int32, (tm, LANES), 0)
    cum = lf
    s = 1
    while s < tm:
        cum = cum + jnp.where(row >= s, pltpu.roll(cum, s, 0), 0.0)
        s *= 2
    cum = cum + carry_ref[7:8, :]
    carry_ref[...] = cum[tm - 8:, :]

    parts = jnp.concatenate(_split3(cum * LOG2E), axis=1)
    eqT = (lax.dot_general(pq_ref[...], parts, _NT, preferred_element_type=F32)
           + _lane_tile(cq_ref[...], tm)).astype(BF16)
    ek = (_dot(parts, pk_ref[...]) + ck_ref[...]).astype(BF16)
    for j in range(N_PAIRS):
        qT_ref[0, j, :LANES, :] = qnT[j * LANES:(j + 1) * LANES].astype(BF16)
        qT_ref[0, j, LANES:, :] = eqT
        for t in range(2):
            r0 = ATTN_WIDTH + (2 * j + t) * HEAD_DIM
            vT_ref[0, j, t * VROWS:t * VROWS + HEAD_DIM, :] = qvT[r0:r0 + HEAD_DIM].astype(BF16)
            vT_ref[0, j, t * VROWS + HEAD_DIM:(t + 1) * VROWS, :] = jnp.ones((VROWS - HEAD_DIM, tm), BF16)
        k_ref[0, :, j * AUG:j * AUG + LANES] = kn[:, j * LANES:(j + 1) * LANES].astype(BF16)
        k_ref[0, :, j * AUG + LANES:(j + 1) * AUG] = ek


def _bias_placement():
    pq = np.zeros((LANES, 3 * LANES), np.float32)
    pk = np.zeros((3 * LANES, LANES), np.float32)
    cq = np.zeros((LANES, LANES), np.float32)
    ck = np.zeros((1, LANES), np.float32)
    for hd in range(N_HEADS):
        base = BIAS_W * hd
        for p in range(3):
            pq[base + p, p * LANES + hd] = 1.0
            pk[p * LANES + hd, base + 3 + p] = -1.0
            cq[base + 3 + p, :] = 1.0
            ck[0, base + p] = 1.0
    return (jnp.asarray(pq, BF16), jnp.asarray(pk, BF16), jnp.asarray(cq), jnp.asarray(ck))


def _qkv(x, nw, shift, scale, wqvT, wk, wf, bf, qnwT, knw, tm, b0, B):
    S = x.shape[1]
    gsum = jnp.asarray(np.kron(np.eye(N_HEADS), np.ones((HEAD_DIM, HEAD_DIM))), BF16)
    pq, pk, cq, ck = _bias_placement()
    const = lambda *shape: pl.BlockSpec(shape, lambda b, i: (0,) * len(shape),
                                        pipeline_mode=pl.Buffered(1))
    return pl.pallas_call(
        _qkv_kernel,
        out_shape=(jax.ShapeDtypeStruct((B, N_PAIRS, AUG, S), BF16),
                   jax.ShapeDtypeStruct((B, S, N_PAIRS * AUG), BF16),
                   jax.ShapeDtypeStruct((B, N_PAIRS, 2 * VROWS, S), BF16)),
        grid=(B, S // tm),
        in_specs=[pl.BlockSpec((1, tm, D_MODEL), lambda b, i: (b + b0, i, 0)),
                  const(1, D_MODEL),
                  pl.BlockSpec((1, 1, D_MODEL), lambda b, i: (b + b0, 0, 0)),
                  pl.BlockSpec((1, 1, D_MODEL), lambda b, i: (b + b0, 0, 0)),
                  const(2 * ATTN_WIDTH, D_MODEL),
                  const(D_MODEL, ATTN_WIDTH),
                  const(D_MODEL, LANES),
                  const(1, LANES),
                  const(ATTN_WIDTH, LANES),
                  const(1, ATTN_WIDTH),
                  const(ATTN_WIDTH, ATTN_WIDTH),
                  const(LANES, 3 * LANES),
                  const(3 * LANES, LANES),
                  const(LANES, LANES),
                  const(1, LANES)],
        out_specs=(pl.BlockSpec((1, N_PAIRS, AUG, tm), lambda b, i: (b, 0, 0, i)),
                   pl.BlockSpec((1, tm, N_PAIRS * AUG), lambda b, i: (b, i, 0)),
                   pl.BlockSpec((1, N_PAIRS, 2 * VROWS, tm), lambda b, i: (b, 0, 0, i))),
        scratch_shapes=[pltpu.VMEM((8, LANES), F32)],
        compiler_params=pltpu.CompilerParams(
            dimension_semantics=("arbitrary", "arbitrary"), vmem_limit_bytes=VMEM_LIMIT),
        name="qkv",
    )(x, nw, shift, scale, wqvT, wk, wf, bf, qnwT, knw, gsum, pq, pk, cq, ck)


def _attn_kernel(qT_ref, k_ref, vT_ref, o_ref, qq_ref, s_ref, smax_ref, m_ref, acc_ref, *, tq, tk, cw):
    qi = pl.program_id(2)
    n = tq // cw
    chains = [(t, c) for t in range(2) for c in range(n)]
    qT = qT_ref[0, 0]
    feat = lax.broadcasted_iota(jnp.int32, (AUG, tq), 0)
    for t in range(2):
        bias0 = LANES + BIAS_W * (2 * pl.program_id(1) + t)
        keep = ((feat >= t * HEAD_DIM) & (feat < (t + 1) * HEAD_DIM)) | \
               ((feat >= bias0) & (feat < bias0 + BIAS_W))
        qh = jnp.where(keep, qT, jnp.zeros_like(qT))
        for c in range(n):
            qq_ref[t * n + c] = qh[:, c * cw:(c + 1) * cw]
    kpos = lax.broadcasted_iota(jnp.int32, (tk, cw), 0)
    qpos = lax.broadcasted_iota(jnp.int32, (tk, cw), 1)

    def scores(j, slot):
        start = pl.multiple_of(j * tk, tk)
        for ci in range(len(chains)):
            s = _dot(k_ref[0, pl.ds(start, tk), :], qq_ref[ci])
            s_ref[slot, ci] = s
            smax_ref[slot, ci] = jnp.broadcast_to(jnp.max(s, axis=0, keepdims=True), (8, cw))

    def absorb(j, slot, diagonal=False):
        start = pl.multiple_of(j * tk, tk)
        for ci, (t, c) in enumerate(chains):
            vj = vT_ref[0, 0, t * VROWS:(t + 1) * VROWS, pl.ds(start, tk)]
            s = s_ref[slot, ci]
            if diagonal:
                s = jnp.where(kpos <= qpos + c * cw, s, NEG)
                smax = jnp.max(s, axis=0, keepdims=True)
            else:
                smax = smax_ref[slot, ci, 0:1]
            m = m_ref[ci, 0:1]
            m_new = jnp.maximum(m, smax)
            p = jnp.exp2(s - m_new).astype(BF16)
            acc_ref[ci] = jnp.exp2(m - m_new) * acc_ref[ci] + _dot(vj, p)
            m_ref[ci] = jnp.broadcast_to(m_new, (8, cw))

    m_ref[...] = jnp.full(m_ref.shape, NEG, F32)
    acc_ref[...] = jnp.zeros(acc_ref.shape, F32)
    scores(0, 0)

    def two_blocks(jj, _):
        j = 2 * jj
        scores(j + 1, 1)
        absorb(j, 0)
        scores(j + 2, 0)
        absorb(j + 1, 1)
        return 0

    lax.fori_loop(0, qi // 2, two_blocks, 0)
    odd = qi % 2 == 1

    @pl.when(odd)
    def _():
        scores(qi, 1)
        absorb(qi - 1, 0)
        absorb(qi, 1, diagonal=True)

    @pl.when(jnp.logical_not(odd))
    def _():
        absorb(qi, 0, diagonal=True)

    outs = [acc_ref[ci, :HEAD_DIM] / acc_ref[ci, HEAD_DIM:HEAD_DIM + 1] for ci in range(len(chains))]
    oT = jnp.concatenate([jnp.concatenate(outs[:n], axis=1), jnp.concatenate(outs[n:], axis=1)], axis=0)
    o_ref[0] = oT.T.astype(BF16)


def _attention(qT_aug, k_aug, vT, tq, cw):
    B, S, _ = k_aug.shape
    n_chains = 2 * tq // cw
    return pl.pallas_call(
        functools.partial(_attn_kernel, tq=tq, tk=tq, cw=cw),
        scratch_shapes=[pltpu.VMEM((n_chains, AUG, cw), BF16),
                        pltpu.VMEM((2, n_chains, tq, cw), F32),
                        pltpu.VMEM((2, n_chains, 8, cw), F32),
                        pltpu.VMEM((n_chains, 8, cw), F32),
                        pltpu.VMEM((n_chains, VROWS, cw), F32)],
        out_shape=jax.ShapeDtypeStruct((B, S, ATTN_WIDTH), BF16),
        grid=(B, N_PAIRS, S // tq),
        in_specs=[pl.BlockSpec((1, 1, AUG, tq), lambda b, j, i: (b, j, 0, i)),
                  pl.BlockSpec((1, S, AUG), lambda b, j, i: (b, 0, j)),
                  pl.BlockSpec((1, 1, 2 * VROWS, S), lambda b, j, i: (b, j, 0, 0))],
        out_specs=pl.BlockSpec((1, tq, LANES), lambda b, j, i: (b, i, j)),
        compiler_params=pltpu.CompilerParams(
            dimension_semantics=("arbitrary", "arbitrary", "arbitrary"),
            vmem_limit_bytes=VMEM_LIMIT),
        name="attn",
    )(qT_aug, k_aug, vT)


def _pack_bf16_pairs(z):
    w = z.shape[1] // 2
    bits = pltpu.bitcast(z.astype(BF16).astype(F32), jnp.uint32)
    return bits[:, :w] | (bits[:, w:] >> 16)


def _unpack_bf16_pairs(p):
    return (pltpu.bitcast(p & jnp.uint32(0xFFFF0000), F32), pltpu.bitcast(p << 16, F32))


POST_ROWS = 256


def _post_kernel(x_ref, yb_ref, n1_ref, sh1_ref, sc1_ref, g1_ref, n2_ref, sh2_ref, sc2_ref,
                 wc_ref, cw_ref, woc_ref, woa_ref, wo_ref, wr_ref, br_ref, tri_ref,
                 x1_ref, h2_ref, ridx_ref, rw_ref, cnt_ref, carry_ref):
    tm = x_ref.shape[1]
    n_grp = max(tm // POST_ROWS, 1)
    rows = tm // n_grp
    lane = lax.broadcasted_iota(jnp.int32, (rows, LANES), 1)
    row8 = lax.broadcasted_iota(jnp.int32, (8, CONV_WIDTH), 0)
    big = jnp.int32(1 << 20)

    @pl.when(pl.program_id(1) == 0)
    def _():
        carry_ref[...] = jnp.zeros_like(carry_ref)

    @pl.when((pl.program_id(0) == 0) & (pl.program_id(1) == 0))
    def _():
        cnt_ref[...] = jnp.zeros_like(cnt_ref)

    st = [dict(rs=pl.ds(g * rows, rows)) for g in range(n_grp)]

    def conv_in(d):
        d["x"] = x_ref[0, d["rs"], :]
        d["hb"] = _modulated_norm(d["x"], n1_ref[...], sh1_ref[0], sc1_ref[0]).astype(BF16)
        d["x_in"] = _dot(d["hb"], wc_ref[:, :CONV_WIDTH])
        d["conv_c"] = _dot(d["hb"], wc_ref[:, 2 * CONV_WIDTH:3 * CONV_WIDTH])
        d["conv_b"] = _dot(d["hb"], wc_ref[:, CONV_WIDTH:2 * CONV_WIDTH])

    def conv(d, prev):
        u = d.pop("conv_c") * d.pop("x_in")
        d["u_tail"] = u[rows - 8:, :]

        def shifted(k):
            r = pltpu.roll(u, k, 0)
            top = jnp.where(row8 < k, pltpu.roll(prev, k, 0), r[:8])
            return jnp.concatenate([top, r[8:]], axis=0)

        cw = cw_ref[...]
        cv = cw[0:1] * shifted(2) + cw[1:2] * shifted(1) + cw[2:3] * u
        d["y_a"] = (d.pop("conv_b") * cv).astype(BF16)

    def gates(d):
        d["p_b"] = _dot(yb_ref[0, d["rs"], :], woa_ref[...])
        d["gate_c"] = _dot(d["hb"], wc_ref[:, 3 * CONV_WIDTH:3 * CONV_WIDTH + D_MODEL])
        d["gate_a"] = _dot(d.pop("hb"), wc_ref[:, 3 * CONV_WIDTH + D_MODEL:])

    def branch_a(d):
        d["p_a"] = _dot(d.pop("y_a"), woc_ref[...])

    def merge(d):
        d["merged"] = (_sigmoid(d.pop("gate_c")) * d.pop("p_a")
                       + _sigmoid(d.pop("gate_a")) * d.pop("p_b")).astype(BF16)

    def out_proj(d):
        d["o"] = _dot(d.pop("merged"), wo_ref[...])

    def residual(d):
        x1 = d.pop("x") + g1_ref[0] * d.pop("o")
        x1_ref[0, d["rs"], :] = x1
        h2 = _modulated_norm(x1, n2_ref[...], sh2_ref[0], sc2_ref[0])
        h2_ref[0, d["rs"], :] = _pack_bf16_pairs(h2)
        d["h_hi"] = h2.astype(BF16)
        d["h_lo"] = (h2 - d["h_hi"].astype(F32)).astype(BF16)

    def router(d):
        h_hi = d.pop("h_hi")
        d["lg"] = (_dot(h_hi, wr_ref[0]) + _dot(d.pop("h_lo"), wr_ref[0]) + _dot(h_hi, wr_ref[1])) \
            + br_ref[...]

    def first_argmax(vals):
        mx = jnp.max(vals, axis=-1, keepdims=True)
        idx = jnp.min(jnp.where(vals == mx, lane, big), axis=-1, keepdims=True)
        return mx, idx

    def route(d):
        lg = d.pop("lg")
        is_g = (lane >= N_EXPERTS) & (lane < N_EXPERTS + N_GROUPS)
        g_mx, g_lane = first_argmax(jnp.where(is_g, lg, NEG))
        p_sel = 1.0 / jnp.sum(jnp.where(is_g, jnp.exp(lg - g_mx), 0.0), axis=-1, keepdims=True)
        g_idx = g_lane - N_EXPERTS
        in_g = (lane >= g_idx * EXPERTS_PER_GROUP) & (lane < (g_idx + 1) * EXPERTS_PER_GROUP)
        le = jnp.where(in_g, lg, NEG)
        v1, i1 = first_argmax(le)
        v2, i2 = first_argmax(jnp.where(lane == i1, NEG, le))
        e2 = jnp.exp(v2 - v1)
        w1 = p_sel / (1.0 + e2)
        w2 = w1 * e2
        rw_ref[0, d["rs"], :] = jnp.where(lane == 0, w1, 0.0) + jnp.where(lane == 1, w2, 0.0)
        d["i1"], d["i2"] = i1, i2
        d["onehot"] = jnp.where((lane == i1) | (lane == i2), 1.0, 0.0)

    stages = [conv_in, None, gates, branch_a, merge, out_proj, residual, router, route]
    lag = 2
    for step in range(len(stages) + lag * (n_grp - 1)):
        for g, d in enumerate(st):
            k = step - lag * g
            if 0 <= k < len(stages):
                if stages[k] is None:
                    conv(d, carry_ref[...] if g == 0 else st[g - 1]["u_tail"])
                else:
                    stages[k](d)
    carry_ref[...] = st[-1]["u_tail"]

    onehot = jnp.concatenate([d["onehot"] for d in st], axis=0)
    before = _dot(tri_ref[...], onehot.astype(BF16)) + cnt_ref[0:1, :]
    cnt_ref[...] = cnt_ref[...] + jnp.sum(onehot, axis=0, keepdims=True)
    for g, d in enumerate(st):
        bg = before[g * rows:(g + 1) * rows]
        i1, i2 = d["i1"], d["i2"]
        r1 = jnp.sum(jnp.where(lane == i1, bg, 0.0), axis=-1, keepdims=True)
        r2 = jnp.sum(jnp.where(lane == i2, bg, 0.0), axis=-1, keepdims=True)
        rec = (jnp.where(lane == 0, i1.astype(F32), 0.0) + jnp.where(lane == 1, i2.astype(F32), 0.0)
               + jnp.where(lane == 2, r1, 0.0) + jnp.where(lane == 3, r2, 0.0))
        ridx_ref[0, :, d["rs"]] = rec.T[:8].astype(jnp.int32)


def _post(x, yb, n1, sh1, sc1, g1, n2, sh2, sc2, wc, cw, woc, woa, wo, wr, br, tm, b0):
    B, S, _ = yb.shape
    tri = jnp.asarray(np.tril(np.ones((tm, tm), np.float32), -1), BF16)
    const = lambda *shape: pl.BlockSpec(shape, lambda b, i: (0,) * len(shape),
                                        pipeline_mode=pl.Buffered(1))
    perb = pl.BlockSpec((1, 1, D_MODEL), lambda b, i: (b + b0, 0, 0))
    tok = lambda w: pl.BlockSpec((1, tm, w), lambda b, i: (b, i, 0))
    return pl.pallas_call(
        _post_kernel,
        out_shape=(jax.ShapeDtypeStruct((B, S, D_MODEL), F32),
                   jax.ShapeDtypeStruct((B, S, D_MODEL // 2), jnp.uint32),
                   jax.ShapeDtypeStruct((B, 8, S), jnp.int32),
                   jax.ShapeDtypeStruct((B, S, LANES), F32),
                   jax.ShapeDtypeStruct((8, LANES), F32)),
        grid=(B, S // tm),
        in_specs=[pl.BlockSpec((1, tm, D_MODEL), lambda b, i: (b + b0, i, 0)), tok(ATTN_WIDTH),
                  const(1, D_MODEL), perb, perb, perb,
                  const(1, D_MODEL), perb, perb,
                  const(D_MODEL, 3 * CONV_WIDTH + 2 * D_MODEL),
                  const(8, CONV_WIDTH),
                  const(CONV_WIDTH, D_MODEL), const(ATTN_WIDTH, D_MODEL),
                  const(D_MODEL, D_MODEL),
                  const(2, D_MODEL, LANES), const(1, LANES), const(tm, tm)],
        out_specs=(tok(D_MODEL), tok(D_MODEL // 2),
                   pl.BlockSpec((1, 8, tm), lambda b, i: (b, 0, i)), tok(LANES),
                   pl.BlockSpec((8, LANES), lambda b, i: (0, 0))),
        scratch_shapes=[pltpu.VMEM((8, CONV_WIDTH), F32)],
        compiler_params=pltpu.CompilerParams(
            dimension_semantics=("arbitrary", "arbitrary"), vmem_limit_bytes=VMEM_LIMIT),
        name="post",
    )(x, yb, n1, sh1, sc1, g1, n2, sh2, sc2, wc, cw, woc, woa, wo, wr, br, tri)


SC_CORES = 2
SC_SUBCORES = 16
SC_WORKERS = SC_CORES * SC_SUBCORES
SC_CHUNK = 64
ROW_WORDS = D_MODEL // 2


def _sc_mesh():
    return plsc.VectorSubcoreMesh(core_axis_name="c", subcore_axis_name="s",
                                  num_cores=SC_CORES, num_subcores=SC_SUBCORES)


def _dispatch_body(rows_hbm, idx1_hbm, idx2_hbm, xs_hbm, idx1_v, idx2_v, rows_v, *, n_chunks):
    wid = lax.axis_index("s") * SC_CORES + lax.axis_index("c")
    pltpu.sync_copy(idx1_hbm.at[wid], idx1_v)
    pltpu.sync_copy(idx2_hbm.at[wid], idx2_v)
    base = wid * (n_chunks * SC_CHUNK)

    @pl.loop(0, n_chunks)
    def _(j):
        pltpu.sync_copy(rows_hbm.at[pl.ds(base + j * SC_CHUNK, SC_CHUNK)], rows_v)
        pltpu.sync_copy(rows_v, xs_hbm.at[idx1_v.at[j]])
        pltpu.sync_copy(rows_v, xs_hbm.at[idx2_v.at[j]])


def _sc_scratch(n_chunks):
    return [pltpu.VMEM((n_chunks, SC_CHUNK), jnp.int32), pltpu.VMEM((n_chunks, SC_CHUNK), jnp.int32),
            pltpu.VMEM((SC_CHUNK, ROW_WORDS), jnp.uint32)]


def _dispatch(rows, idx1, idx2, n_slots):
    n_chunks = idx1.shape[1]
    return pl.kernel(
        functools.partial(_dispatch_body, n_chunks=n_chunks),
        out_type=jax.ShapeDtypeStruct((n_slots, ROW_WORDS), jnp.uint32),
        mesh=_sc_mesh(),
        scratch_types=_sc_scratch(n_chunks),
        name="dispatch",
    )(rows, idx1, idx2)


def _collect_body(ys_hbm, idx1_hbm, idx2_hbm, g1_hbm, g2_hbm, idx1_v, idx2_v, rows_v, *, n_chunks):
    wid = lax.axis_index("s") * SC_CORES + lax.axis_index("c")
    pltpu.sync_copy(idx1_hbm.at[wid], idx1_v)
    pltpu.sync_copy(idx2_hbm.at[wid], idx2_v)
    base = wid * (n_chunks * SC_CHUNK)

    @pl.loop(0, n_chunks)
    def _(j):
        dst = pl.ds(base + j * SC_CHUNK, SC_CHUNK)
        pltpu.sync_copy(ys_hbm.at[idx1_v.at[j]], rows_v)
        pltpu.sync_copy(rows_v, g1_hbm.at[dst])
        pltpu.sync_copy(ys_hbm.at[idx2_v.at[j]], rows_v)
        pltpu.sync_copy(rows_v, g2_hbm.at[dst])


def _collect(ys, idx1, idx2):
    n_chunks = idx1.shape[1]
    out = jax.ShapeDtypeStruct((SC_WORKERS * n_chunks * SC_CHUNK, ROW_WORDS), jnp.uint32)
    return pl.kernel(
        functools.partial(_collect_body, n_chunks=n_chunks),
        out_type=(out, out),
        mesh=_sc_mesh(),
        scratch_types=_sc_scratch(n_chunks),
        name="collect",
    )(ys, idx1, idx2)


def _moe_kernel(te_ref, nt_ref, xs_ref, wg_ref, wu_ref, wd_ref, ys_ref):
    @pl.when(pl.program_id(0) < nt_ref[0])
    def _():
        left, right = _unpack_bf16_pairs(xs_ref[...])
        xb = jnp.concatenate([left.astype(BF16), right.astype(BF16)], axis=1)
        g = _dot(xb, wg_ref[0].astype(BF16))
        u = _dot(xb, wu_ref[0].astype(BF16))
        a = (g * _sigmoid(g) * u).astype(BF16)
        ys_ref[...] = _pack_bf16_pairs(_dot(a, wd_ref[0].astype(BF16)))


def _moe(xs, tile_expert, n_tiles, wg, wu, wd, tm):
    n_slots = xs.shape[0]
    row_blk = lambda i, te, nt: (jnp.minimum(i, nt[0] - 1), 0)
    w_blk = lambda i, te, nt: (te[i], 0, 0)
    return pl.pallas_call(
        _moe_kernel,
        out_shape=jax.ShapeDtypeStruct((n_slots, ROW_WORDS), jnp.uint32),
        grid_spec=pltpu.PrefetchScalarGridSpec(
            num_scalar_prefetch=2,
            grid=(n_slots // tm,),
            in_specs=[pl.BlockSpec((tm, ROW_WORDS), row_blk),
                      pl.BlockSpec((1, D_MODEL, D_EXPERT), w_blk),
                      pl.BlockSpec((1, D_MODEL, D_EXPERT), w_blk),
                      pl.BlockSpec((1, D_EXPERT, D_MODEL), w_blk)],
            out_specs=pl.BlockSpec((tm, ROW_WORDS), row_blk)),
        compiler_params=pltpu.CompilerParams(
            dimension_semantics=("arbitrary",), vmem_limit_bytes=VMEM_LIMIT),
        name="moe",
    )(tile_expert, n_tiles, xs, wg, wu, wd)


def _final_kernel(x1_ref, g1_ref, g2_ref, rw_ref, gate_ref, *rest):
    o_ref = rest[-1]
    rw = rw_ref[0]
    w1 = rw[:, 0:1]
    w2 = rw[:, 1:2]
    a_l, a_r = _unpack_bf16_pairs(g1_ref[0])
    b_l, b_r = _unpack_bf16_pairs(g2_ref[0])
    moe = jnp.concatenate([w1 * a_l + w2 * b_l, w1 * a_r + w2 * b_r], axis=1)
    o_ref[0] = x1_ref[0] + gate_ref[0] * moe


def _final(x1, g1, g2, rw, gate2, tm, b0, out_prev, after):
    Bg, S, _ = x1.shape
    tok = lambda w: pl.BlockSpec((1, tm, w), lambda b, i: (b, i, 0))
    extra = (() if out_prev is None else (out_prev,)) + tuple(after)
    return pl.pallas_call(
        _final_kernel,
        out_shape=jax.ShapeDtypeStruct((gate2.shape[0], S, D_MODEL), F32),
        grid=(Bg, S // tm),
        in_specs=[tok(D_MODEL), tok(ROW_WORDS), tok(ROW_WORDS), tok(LANES),
                  pl.BlockSpec((1, 1, D_MODEL), lambda b, i: (b + b0, 0, 0))]
        + [pl.BlockSpec(memory_space=pl.ANY)] * len(extra),
        out_specs=pl.BlockSpec((1, tm, D_MODEL), lambda b, i: (b + b0, i, 0)),
        input_output_aliases={} if out_prev is None else {5: 0},
        compiler_params=pltpu.CompilerParams(
            dimension_semantics=("arbitrary", "arbitrary"), vmem_limit_bytes=VMEM_LIMIT),
        name="final",
    )(x1, g1, g2, rw, gate2, *extra)


def _pick(n, pref):
    t = min(n, pref)
    assert n % t == 0, (n, t)
    return t


def _slots_kernel(offs_ref, r_ref, o_ref):
    r = r_ref[0]
    base = jnp.zeros_like(r)
    for e in range(N_EXPERTS):
        base = jnp.where(r == e, offs_ref[e], base)
    o_ref[0] = base + pltpu.roll(r, 6, 0)


def _slots(ridx, offs):
    B, _, S = ridx.shape
    return pl.pallas_call(
        _slots_kernel,
        out_shape=jax.ShapeDtypeStruct((B, 8, S), jnp.int32),
        grid_spec=pltpu.PrefetchScalarGridSpec(
            num_scalar_prefetch=1, grid=(B,),
            in_specs=[pl.BlockSpec((1, 8, S), lambda b, offs: (b, 0, 0))],
            out_specs=pl.BlockSpec((1, 8, S), lambda b, offs: (b, 0, 0))),
        compiler_params=pltpu.CompilerParams(dimension_semantics=("arbitrary",)),
        name="slots",
    )(offs, ridx)


def _route_plan(ridx, counts, tm_e, T):
    counts = counts.astype(jnp.int32)
    tiles = (counts + tm_e - 1) // tm_e
    tile_end = jnp.cumsum(tiles)
    offs = (tile_end - tiles) * tm_e
    slots = _slots(ridx, offs)
    n_chunks = T // (SC_WORKERS * SC_CHUNK)
    idx1 = slots[:, 0, :].reshape(SC_WORKERS, n_chunks, SC_CHUNK)
    idx2 = slots[:, 1, :].reshape(SC_WORKERS, n_chunks, SC_CHUNK)
    n_tiles_max = 2 * T // tm_e + N_EXPERTS
    tile_ids = jnp.arange(n_tiles_max, dtype=jnp.int32)
    tile_expert = jnp.sum((tile_end[None, :] <= tile_ids[:, None]).astype(jnp.int32), axis=1)
    tile_expert = jnp.minimum(tile_expert, N_EXPERTS - 1)
    return idx1, idx2, tile_expert, tile_end[-1:].astype(jnp.int32), n_tiles_max * tm_e


def _layer(x, c, w_ada, b_ada, norm1_w, w_in, b_forget, conv_w, q_norm_w, k_norm_w,
           w_out_conv, w_out_attn, w_o, norm2_w, w_rg, b_rg, w_re, b_re, w_gate, w_up, w_down):
    B, S, _ = x.shape
    n_grp = 2 if B % 2 == 0 and (B // 2 * S) % (SC_WORKERS * SC_CHUNK) == 0 else 1
    Bg = B // n_grp
    T = Bg * S
    assert T % (SC_WORKERS * SC_CHUNK) == 0, T
    mod = _ada(c, w_ada, b_ada.reshape(1, -1)).reshape(B, 6, 1, D_MODEL)
    shift1, scale1, gate1, shift2, scale2, gate2 = (mod[:, t] for t in range(6))

    cuts = np.cumsum([0, CONV_WIDTH, CONV_WIDTH, CONV_WIDTH, ATTN_WIDTH, ATTN_WIDTH, ATTN_WIDTH,
                      N_HEADS, D_MODEL, D_MODEL])
    w_conv3 = w_in[:, cuts[0]:cuts[3]]
    w_qvT = jnp.concatenate([w_in[:, cuts[3]:cuts[4]], w_in[:, cuts[5]:cuts[6]]], axis=1).T.astype(BF16)
    w_k = w_in[:, cuts[4]:cuts[5]].astype(BF16)
    w_f = jnp.pad(w_in[:, cuts[6]:cuts[7]], ((0, 0), (0, LANES - N_HEADS))).astype(BF16)
    b_f = jnp.pad(b_forget, (0, LANES - N_HEADS)).reshape(1, LANES)
    w_cgg = jnp.concatenate([w_conv3, w_in[:, cuts[7]:cuts[9]]], axis=1).astype(BF16)

    tm_qkv = _pick(S, 512)
    qnwT = jnp.broadcast_to((jnp.tile(q_norm_w, N_HEADS) * (LOG2E * HEAD_DIM ** -0.5))[:, None],
                            (ATTN_WIDTH, LANES))
    tq = _pick(S, 512)
    w_r = jnp.pad(jnp.concatenate([w_re, w_rg], axis=1),
                  ((0, 0), (0, LANES - N_EXPERTS - N_GROUPS)))
    w_r_hi = w_r.astype(BF16)
    w_r_lo = (w_r - w_r_hi.astype(F32)).astype(BF16)
    b_r = jnp.pad(jnp.concatenate([b_re, b_rg]), (0, LANES - N_EXPERTS - N_GROUPS)).reshape(1, LANES)
    cw = jnp.pad(conv_w, ((0, 8 - CONV_K), (0, 0)))
    tm_post = _pick(S, 1024)
    tm_e = 512
    w_oc, w_oa, w_ob, w_rs = (w_out_conv.astype(BF16), w_out_attn.astype(BF16), w_o.astype(BF16),
                              jnp.stack([w_r_hi, w_r_lo]))

    routed = []
    for g in range(n_grp):
        b0 = g * Bg
        qT_aug, k_aug, vT = _qkv(x, norm1_w.reshape(1, -1), shift1, scale1, w_qvT, w_k, w_f, b_f,
                                 qnwT, jnp.tile(k_norm_w, N_HEADS).reshape(1, -1), tm_qkv, b0, Bg)
        y_b = _attention(qT_aug, k_aug, vT, tq, _pick(tq, 256))
        x1, h2p, ridx, rw, counts = _post(x, y_b, norm1_w.reshape(1, -1), shift1, scale1, gate1,
                                          norm2_w.reshape(1, -1), shift2, scale2,
                                          w_cgg, cw, w_oc, w_oa, w_ob, w_rs, b_r, tm_post, b0)
        idx1, idx2, tile_expert, n_tiles, n_slots = _route_plan(ridx, counts[0, :N_EXPERTS], tm_e, T)
        xs = _dispatch(h2p.reshape(T, ROW_WORDS), idx1, idx2, n_slots)
        routed.append((x1, rw, idx1, idx2, tile_expert, n_tiles, xs))

    gathered = []
    for x1, rw, idx1, idx2, tile_expert, n_tiles, xs in routed:
        ys = _moe(xs, tile_expert, n_tiles, w_gate, w_up, w_down, tm_e)
        gathered.append(_collect(ys, idx1, idx2) + (ys,))

    out = None
    for g, ((x1, rw, *_), (g1, g2, _)) in enumerate(zip(routed, gathered)):
        later = [ys for (_, _, ys) in gathered[g + 1:]]
        out = _final(x1, g1.reshape(Bg, S, ROW_WORDS), g2.reshape(Bg, S, ROW_WORDS), rw, gate2,
                     _pick(S, 512), g * Bg, out, later)
    return out


def kernel(x, c, w_ada, b_ada, norm1_w, w_in, b_forget, conv_w, q_norm_w, k_norm_w, w_out_conv,
           w_out_attn, w_o, norm2_w, w_router_group, b_router_group, w_router_expert,
           b_router_expert, w_gate, w_up, w_down):
    for l in range(w_ada.shape[0]):
        x = _layer(x, c, w_ada[l], b_ada[l], norm1_w[l], w_in[l], b_forget[l], conv_w[l],
                   q_norm_w[l], k_norm_w[l], w_out_conv[l], w_out_attn[l], w_o[l], norm2_w[l],
                   w_router_group[l], b_router_group[l], w_router_expert[l], b_router_expert[l],
                   w_gate[l], w_up[l], w_down[l])
    return x
```

```python
import functools

import jax
import jax.numpy as jnp
import numpy as np
from jax import lax
from jax.experimental import pallas as pl
from jax.experimental.pallas import tpu as pltpu
from jax.experimental.pallas import tpu_sc as plsc

D_MODEL = 1024
CONV_WIDTH = 512
CONV_K = 3
N_HEADS = 8
HEAD_DIM = 64
ATTN_WIDTH = N_HEADS * HEAD_DIM
N_PAIRS = N_HEADS // 2
N_GROUPS = 4
EXPERTS_PER_GROUP = 8
N_EXPERTS = N_GROUPS * EXPERTS_PER_GROUP
D_EXPERT = 256
EPS = 1e-6
LANES = 128
AUG = 2 * LANES
BIAS_W = 6
VROWS = HEAD_DIM + 16
NEG = -1e30
LOG2E = 1.4426950408889634

F32 = jnp.float32
BF16 = jnp.bfloat16
VMEM_LIMIT = 56 * 1024 * 1024


def _sigmoid(z):
    return 1.0 / (1.0 + jnp.exp(-z))


def _split3(z):
    hi = z.astype(BF16)
    r = z - hi.astype(F32)
    mid = r.astype(BF16)
    lo = (r - mid.astype(F32)).astype(BF16)
    return hi, mid, lo


def _dot(a, b):
    return jnp.dot(a, b, preferred_element_type=F32)


def _modulated_norm(x, nw, shift, scale):
    ms = jnp.mean(x * x, axis=-1, keepdims=True)
    return (x * lax.rsqrt(ms + EPS) * nw) * (1.0 + scale) + shift


def _ada_kernel(c_ref, w_ref, b_ref, o_ref):
    c = c_ref[...]
    a = c * _sigmoid(c)
    o_ref[...] = jnp.dot(a, w_ref[...], precision=lax.Precision.HIGHEST,
                         preferred_element_type=F32) + b_ref[...]


def _ada(c, w_ada, b_ada):
    B = c.shape[0]
    n = w_ada.shape[1] // D_MODEL
    return pl.pallas_call(
        _ada_kernel,
        out_shape=jax.ShapeDtypeStruct((B, n * D_MODEL), F32),
        grid=(n,),
        in_specs=[pl.BlockSpec((B, D_MODEL), lambda j: (0, 0)),
                  pl.BlockSpec((D_MODEL, D_MODEL), lambda j: (0, j)),
                  pl.BlockSpec((1, D_MODEL), lambda j: (0, j))],
        out_specs=pl.BlockSpec((B, D_MODEL), lambda j: (0, j)),
        compiler_params=pltpu.CompilerParams(dimension_semantics=("arbitrary",)),
        name="ada",
    )(c, w_ada, b_ada)


_NT = (((1,), (1,)), ((), ()))


def _lane_tile(a, width):
    return jnp.concatenate([a] * (width // a.shape[1]), axis=1)


def _qkv_kernel(x_ref, nw_ref, sh_ref, sc_ref, wqv_ref, wk_ref, wf_ref, bf_ref, qnw_ref, knw_ref,
                gsum_ref, pq_ref, pk_ref, cq_ref, ck_ref,
                qT_ref, k_ref, vT_ref, carry_ref):
    tm = x_ref.shape[1]
    h = _modulated_norm(x_ref[0], nw_ref[...], sh_ref[0], sc_ref[0])
    hb = h.astype(BF16)
    qvT = lax.dot_general(wqv_ref[...], hb, _NT, preferred_element_type=F32)
    k = _dot(hb, wk_ref[...])

    heads = []
    for hd in range(N_HEADS):
        z = qvT[hd * HEAD_DIM:(hd + 1) * HEAD_DIM]
        heads.append(z * lax.rsqrt(jnp.mean(z * z, axis=0, keepdims=True) + EPS))
    qnT = jnp.concatenate(heads, axis=0) * _lane_tile(qnw_ref[...], tm)

    ss = _dot((k * k).astype(BF16), gsum_ref[...])
    kn = k * lax.rsqrt(ss * (1.0 / HEAD_DIM) + EPS) * knw_ref[...]

    fl = _dot(hb, wf_ref[...]) + bf_ref[...]
    lf = jnp.minimum(fl, 0.0) - jnp.log(1.0 + jnp.exp(-jnp.abs(fl)))

    @pl.when(pl.program_id(1) == 0)
    def _():
        carry_ref[...] = jnp.zeros_like(carry_ref)

    row = lax.broadcasted_iota(jnp.int32, (tm, LANES), 0)
    cum = lf
    s = 1
    while s < tm:
        cum = cum + jnp.where(row >= s, pltpu.roll(cum, s, 0), 0.0)
        s *= 2
    cum = cum + carry_ref[7:8, :]
    carry_ref[...] = cum[tm - 8:, :]

    parts = jnp.concatenate(_split3(cum * LOG2E), axis=1)
    eqT = (lax.dot_general(pq_ref[...], parts, _NT, preferred_element_type=F32)
           + _lane_tile(cq_ref[...], tm)).astype(BF16)
    ek = (_dot(parts, pk_ref[...]) + ck_ref[...]).astype(BF16)
    for j in range(N_PAIRS):
        qT_ref[0, j, :LANES, :] = qnT[j * LANES:(j + 1) * LANES].astype(BF16)
        qT_ref[0, j, LANES:, :] = eqT
        for t in range(2):
            r0 = ATTN_WIDTH + (2 * j + t) * HEAD_DIM
            vT_ref[0, j, t * VROWS:t * VROWS + HEAD_DIM, :] = qvT[r0:r0 + HEAD_DIM].astype(BF16)
            vT_ref[0, j, t * VROWS + HEAD_DIM:(t + 1) * VROWS, :] = jnp.ones((VROWS - HEAD_DIM, tm), BF16)
        k_ref[0, :, j * AUG:j * AUG + LANES] = kn[:, j * LANES:(j + 1) * LANES].astype(BF16)
        k_ref[0, :, j * AUG + LANES:(j + 1) * AUG] = ek


def _bias_placement():
    pq = np.zeros((LANES, 3 * LANES), np.float32)
    pk = np.zeros((3 * LANES, LANES), np.float32)
    cq = np.zeros((LANES, LANES), np.float32)
    ck = np.zeros((1, LANES), np.float32)
    for hd in range(N_HEADS):
        base = BIAS_W * hd
        for p in range(3):
            pq[base + p, p * LANES + hd] = 1.0
            pk[p * LANES + hd, base + 3 + p] = -1.0
            cq[base + 3 + p, :] = 1.0
            ck[0, base + p] = 1.0
    return (jnp.asarray(pq, BF16), jnp.asarray(pk, BF16), jnp.asarray(cq), jnp.asarray(ck))


def _qkv(x, nw, shift, scale, wqvT, wk, wf, bf, qnwT, knw, tm, b0, B):
    S = x.shape[1]
    gsum = jnp.asarray(np.kron(np.eye(N_HEADS), np.ones((HEAD_DIM, HEAD_DIM))), BF16)
    pq, pk, cq, ck = _bias_placement()
    const = lambda *shape: pl.BlockSpec(shape, lambda b, i: (0,) * len(shape),
                                        pipeline_mode=pl.Buffered(1))
    return pl.pallas_call(
        _qkv_kernel,
        out_shape=(jax.ShapeDtypeStruct((B, N_PAIRS, AUG, S), BF16),
                   jax.ShapeDtypeStruct((B, S, N_PAIRS * AUG), BF16),
                   jax.ShapeDtypeStruct((B, N_PAIRS, 2 * VROWS, S), BF16)),
        grid=(B, S // tm),
        in_specs=[pl.BlockSpec((1, tm, D_MODEL), lambda b, i: (b + b0, i, 0)),
                  const(1, D_MODEL),
                  pl.BlockSpec((1, 1, D_MODEL), lambda b, i: (b + b0, 0, 0)),
                  pl.BlockSpec((1, 1, D_MODEL), lambda b, i: (b + b0, 0, 0)),
                  const(2 * ATTN_WIDTH, D_MODEL),
                  const(D_MODEL, ATTN_WIDTH),
                  const(D_MODEL, LANES),
                  const(1, LANES),
                  const(ATTN_WIDTH, LANES),
                  const(1, ATTN_WIDTH),
                  const(ATTN_WIDTH, ATTN_WIDTH),
                  const(LANES, 3 * LANES),
                  const(3 * LANES, LANES),
                  const(LANES, LANES),
                  const(1, LANES)],
        out_specs=(pl.BlockSpec((1, N_PAIRS, AUG, tm), lambda b, i: (b, 0, 0, i)),
                   pl.BlockSpec((1, tm, N_PAIRS * AUG), lambda b, i: (b, i, 0)),
                   pl.BlockSpec((1, N_PAIRS, 2 * VROWS, tm), lambda b, i: (b, 0, 0, i))),
        scratch_shapes=[pltpu.VMEM((8, LANES), F32)],
        compiler_params=pltpu.CompilerParams(
            dimension_semantics=("arbitrary", "arbitrary"), vmem_limit_bytes=VMEM_LIMIT),
        name="qkv",
    )(x, nw, shift, scale, wqvT, wk, wf, bf, qnwT, knw, gsum, pq, pk, cq, ck)


def _attn_kernel(qT_ref, k_ref, vT_ref, o_ref, qq_ref, s_ref, smax_ref, m_ref, acc_ref, *, tq, cw):
    i = pl.program_id(2)
    n = tq // cw
    chains = [(a, t, c) for a in range(2) for t in range(2) for c in range(n)]
    second = [ci for ci, (a, _, _) in enumerate(chains) if a == 1]
    feat = lax.broadcasted_iota(jnp.int32, (AUG, tq), 0)
    for t in range(2):
        bias0 = LANES + BIAS_W * (2 * pl.program_id(1) + t)
        keep = ((feat >= t * HEAD_DIM) & (feat < (t + 1) * HEAD_DIM)) | \
               ((feat >= bias0) & (feat < bias0 + BIAS_W))
        for a in range(2):
            qT = qT_ref[0, 0, :, a * tq:(a + 1) * tq]
            qh = jnp.where(keep, qT, jnp.zeros_like(qT))
            for c in range(n):
                qq_ref[chains.index((a, t, c))] = qh[:, c * cw:(c + 1) * cw]
    kpos = lax.broadcasted_iota(jnp.int32, (tq, cw), 0)
    qpos = lax.broadcasted_iota(jnp.int32, (tq, cw), 1)

    def scores(j, slot, which):
        k_blk = k_ref[0, pl.ds(pl.multiple_of(j * tq, tq), tq), :]
        for ci in which:
            s = _dot(k_blk, qq_ref[ci])
            s_ref[slot, ci] = s
            smax_ref[slot, ci] = jnp.broadcast_to(jnp.max(s, axis=0, keepdims=True), (8, cw))

    def absorb(j, slot, which, diagonal=()):
        start = pl.multiple_of(j * tq, tq)
        for ci in which:
            _, t, c = chains[ci]
            vj = vT_ref[0, 0, t * VROWS:(t + 1) * VROWS, pl.ds(start, tq)]
            s = s_ref[slot, ci]
            if ci in diagonal:
                s = jnp.where(kpos <= qpos + c * cw, s, NEG)
                smax = jnp.max(s, axis=0, keepdims=True)
            else:
                smax = smax_ref[slot, ci, 0:1]
            m = m_ref[ci, 0:1]
            m_new = jnp.maximum(m, smax)
            p = jnp.exp2(s - m_new).astype(BF16)
            acc_ref[ci] = jnp.exp2(m - m_new) * acc_ref[ci] + _dot(vj, p)
            m_ref[ci] = jnp.broadcast_to(m_new, (8, cw))

    every = list(range(len(chains)))
    first = [ci for ci in every if ci not in second]
    m_ref[...] = jnp.full(m_ref.shape, NEG, F32)
    acc_ref[...] = jnp.zeros(acc_ref.shape, F32)
    scores(0, 0, every)

    def two_blocks(jj, _):
        j = 2 * jj
        scores(j + 1, 1, every)
        absorb(j, 0, every)
        scores(j + 2, 0, every)
        absorb(j + 1, 1, every)
        return 0

    lax.fori_loop(0, i, two_blocks, 0)
    scores(2 * i + 1, 1, second)
    absorb(2 * i, 0, every, diagonal=first)
    absorb(2 * i + 1, 1, second, diagonal=second)

    for a in range(2):
        outs = [acc_ref[ci, :HEAD_DIM] / acc_ref[ci, HEAD_DIM:HEAD_DIM + 1]
                for ci in every if chains[ci][0] == a]
        oT = jnp.concatenate([jnp.concatenate(outs[:n], axis=1), jnp.concatenate(outs[n:], axis=1)],
                             axis=0)
        o_ref[0, a * tq:(a + 1) * tq, :] = oT.T.astype(BF16)


def _attention(qT_aug, k_aug, vT, tq, cw):
    B, S, _ = k_aug.shape
    n_chains = 4 * tq // cw
    return pl.pallas_call(
        functools.partial(_attn_kernel, tq=tq, cw=cw),
        scratch_shapes=[pltpu.VMEM((n_chains, AUG, cw), BF16),
                        pltpu.VMEM((2, n_chains, tq, cw), F32),
                        pltpu.VMEM((2, n_chains, 8, cw), F32),
                        pltpu.VMEM((n_chains, 8, cw), F32),
                        pltpu.VMEM((n_chains, VROWS, cw), F32)],
        out_shape=jax.ShapeDtypeStruct((B, S, ATTN_WIDTH), BF16),
        grid=(B, N_PAIRS, S // (2 * tq)),
        in_specs=[pl.BlockSpec((1, 1, AUG, 2 * tq), lambda b, j, i: (b, j, 0, i)),
                  pl.BlockSpec((1, S, AUG), lambda b, j, i: (b, 0, j)),
                  pl.BlockSpec((1, 1, 2 * VROWS, S), lambda b, j, i: (b, j, 0, 0))],
        out_specs=pl.BlockSpec((1, 2 * tq, LANES), lambda b, j, i: (b, i, j)),
        compiler_params=pltpu.CompilerParams(
            dimension_semantics=("arbitrary", "arbitrary", "arbitrary"),
            vmem_limit_bytes=VMEM_LIMIT),
        name="attn",
    )(qT_aug, k_aug, vT)


def _pack_bf16_pairs(z):
    w = z.shape[1] // 2
    bits = pltpu.bitcast(z.astype(BF16).astype(F32), jnp.uint32)
    return bits[:, :w] | (bits[:, w:] >> 16)


def _unpack_bf16_pairs(p):
    return (pltpu.bitcast(p & jnp.uint32(0xFFFF0000), F32), pltpu.bitcast(p << 16, F32))


POST_ROWS = 256


def _post_kernel(x_ref, yb_ref, n1_ref, sh1_ref, sc1_ref, g1_ref, n2_ref, sh2_ref, sc2_ref,
                 wc_ref, cw_ref, woc_ref, woa_ref, wo_ref, wr_ref, br_ref, tri_ref,
                 x1_ref, h2_ref, ridx_ref, rw_ref, cnt_ref, carry_ref):
    tm = x_ref.shape[1]
    n_grp = max(tm // POST_ROWS, 1)
    rows = tm // n_grp
    lane = lax.broadcasted_iota(jnp.int32, (rows, LANES), 1)
    row8 = lax.broadcasted_iota(jnp.int32, (8, CONV_WIDTH), 0)
    big = jnp.int32(1 << 20)

    @pl.when(pl.program_id(1) == 0)
    def _():
        carry_ref[...] = jnp.zeros_like(carry_ref)

    @pl.when((pl.program_id(0) == 0) & (pl.program_id(1) == 0))
    def _():
        cnt_ref[...] = jnp.zeros_like(cnt_ref)

    st = [dict(rs=pl.ds(g * rows, rows)) for g in range(n_grp)]

    def conv_in(d):
        d["x"] = x_ref[0, d["rs"], :]
        d["hb"] = _modulated_norm(d["x"], n1_ref[...], sh1_ref[0], sc1_ref[0]).astype(BF16)
        d["x_in"] = _dot(d["hb"], wc_ref[:, :CONV_WIDTH])
        d["conv_c"] = _dot(d["hb"], wc_ref[:, 2 * CONV_WIDTH:3 * CONV_WIDTH])
        d["conv_b"] = _dot(d["hb"], wc_ref[:, CONV_WIDTH:2 * CONV_WIDTH])

    def conv(d, prev):
        u = d.pop("conv_c") * d.pop("x_in")
        d["u_tail"] = u[rows - 8:, :]

        def shifted(k):
            r = pltpu.roll(u, k, 0)
            top = jnp.where(row8 < k, pltpu.roll(prev, k, 0), r[:8])
            return jnp.concatenate([top, r[8:]], axis=0)

        cw = cw_ref[...]
        cv = cw[0:1] * shifted(2) + cw[1:2] * shifted(1) + cw[2:3] * u
        d["y_a"] = (d.pop("conv_b") * cv).astype(BF16)

    def gates(d):
        d["p_b"] = _dot(yb_ref[0, d["rs"], :], woa_ref[...])
        d["gate_c"] = _dot(d["hb"], wc_ref[:, 3 * CONV_WIDTH:3 * CONV_WIDTH + D_MODEL])
        d["gate_a"] = _dot(d.pop("hb"), wc_ref[:, 3 * CONV_WIDTH + D_MODEL:])

    def branch_a(d):
        d["p_a"] = _dot(d.pop("y_a"), woc_ref[...])

    def merge(d):
        d["merged"] = (_sigmoid(d.pop("gate_c")) * d.pop("p_a")
                       + _sigmoid(d.pop("gate_a")) * d.pop("p_b")).astype(BF16)

    def out_proj(d):
        d["o"] = _dot(d.pop("merged"), wo_ref[...])

    def residual(d):
        x1 = d.pop("x") + g1_ref[0] * d.pop("o")
        x1_ref[0, d["rs"], :] = x1
        h2 = _modulated_norm(x1, n2_ref[...], sh2_ref[0], sc2_ref[0])
        h2_ref[0, d["rs"], :] = _pack_bf16_pairs(h2)
        d["h_hi"] = h2.astype(BF16)
        d["h_lo"] = (h2 - d["h_hi"].astype(F32)).astype(BF16)

    def router(d):
        h_hi = d.pop("h_hi")
        d["lg"] = (_dot(h_hi, wr_ref[0]) + _dot(d.pop("h_lo"), wr_ref[0]) + _dot(h_hi, wr_ref[1])) \
            + br_ref[...]

    def first_argmax(vals):
        mx = jnp.max(vals, axis=-1, keepdims=True)
        idx = jnp.min(jnp.where(vals == mx, lane, big), axis=-1, keepdims=True)
        return mx, idx

    def route(d):
        lg = d.pop("lg")
        is_g = (lane >= N_EXPERTS) & (lane < N_EXPERTS + N_GROUPS)
        g_mx, g_lane = first_argmax(jnp.where(is_g, lg, NEG))
        p_sel = 1.0 / jnp.sum(jnp.where(is_g, jnp.exp(lg - g_mx), 0.0), axis=-1, keepdims=True)
        g_idx = g_lane - N_EXPERTS
        in_g = (lane >= g_idx * EXPERTS_PER_GROUP) & (lane < (g_idx + 1) * EXPERTS_PER_GROUP)
        le = jnp.where(in_g, lg, NEG)
        v1, i1 = first_argmax(le)
        v2, i2 = first_argmax(jnp.where(lane == i1, NEG, le))
        e2 = jnp.exp(v2 - v1)
        w1 = p_sel / (1.0 + e2)
        w2 = w1 * e2
        rw_ref[0, d["rs"], :] = jnp.where(lane == 0, w1, 0.0) + jnp.where(lane == 1, w2, 0.0)
        d["i1"], d["i2"] = i1, i2
        d["onehot"] = jnp.where((lane == i1) | (lane == i2), 1.0, 0.0)

    stages = [conv_in, None, gates, branch_a, merge, out_proj, residual, router, route]
    lag = 2
    for step in range(len(stages) + lag * (n_grp - 1)):
        for g, d in enumerate(st):
            k = step - lag * g
            if 0 <= k < len(stages):
                if stages[k] is None:
                    conv(d, carry_ref[...] if g == 0 else st[g - 1]["u_tail"])
                else:
                    stages[k](d)
    carry_ref[...] = st[-1]["u_tail"]

    onehot = jnp.concatenate([d["onehot"] for d in st], axis=0)
    before = _dot(tri_ref[...], onehot.astype(BF16)) + cnt_ref[0:1, :]
    cnt_ref[...] = cnt_ref[...] + jnp.sum(onehot, axis=0, keepdims=True)
    for g, d in enumerate(st):
        bg = before[g * rows:(g + 1) * rows]
        i1, i2 = d["i1"], d["i2"]
        r1 = jnp.sum(jnp.where(lane == i1, bg, 0.0), axis=-1, keepdims=True)
        r2 = jnp.sum(jnp.where(lane == i2, bg, 0.0), axis=-1, keepdims=True)
        rec = (jnp.where(lane == 0, i1.astype(F32), 0.0) + jnp.where(lane == 1, i2.astype(F32), 0.0)
               + jnp.where(lane == 2, r1, 0.0) + jnp.where(lane == 3, r2, 0.0))
        ridx_ref[0, :, d["rs"]] = rec.T[:8].astype(jnp.int32)


def _post(x, yb, n1, sh1, sc1, g1, n2, sh2, sc2, wc, cw, woc, woa, wo, wr, br, tm, b0):
    B, S, _ = yb.shape
    tri = jnp.asarray(np.tril(np.ones((tm, tm), np.float32), -1), BF16)
    const = lambda *shape: pl.BlockSpec(shape, lambda b, i: (0,) * len(shape),
                                        pipeline_mode=pl.Buffered(1))
    perb = pl.BlockSpec((1, 1, D_MODEL), lambda b, i: (b + b0, 0, 0))
    tok = lambda w: pl.BlockSpec((1, tm, w), lambda b, i: (b, i, 0))
    return pl.pallas_call(
        _post_kernel,
        out_shape=(jax.ShapeDtypeStruct((B, S, D_MODEL), F32),
                   jax.ShapeDtypeStruct((B, S, D_MODEL // 2), jnp.uint32),
                   jax.ShapeDtypeStruct((B, 8, S), jnp.int32),
                   jax.ShapeDtypeStruct((B, S, LANES), F32),
                   jax.ShapeDtypeStruct((8, LANES), F32)),
        grid=(B, S // tm),
        in_specs=[pl.BlockSpec((1, tm, D_MODEL), lambda b, i: (b + b0, i, 0)), tok(ATTN_WIDTH),
                  const(1, D_MODEL), perb, perb, perb,
                  const(1, D_MODEL), perb, perb,
                  const(D_MODEL, 3 * CONV_WIDTH + 2 * D_MODEL),
                  const(8, CONV_WIDTH),
                  const(CONV_WIDTH, D_MODEL), const(ATTN_WIDTH, D_MODEL),
                  const(D_MODEL, D_MODEL),
                  const(2, D_MODEL, LANES), const(1, LANES), const(tm, tm)],
        out_specs=(tok(D_MODEL), tok(D_MODEL // 2),
                   pl.BlockSpec((1, 8, tm), lambda b, i: (b, 0, i)), tok(LANES),
                   pl.BlockSpec((8, LANES), lambda b, i: (0, 0))),
        scratch_shapes=[pltpu.VMEM((8, CONV_WIDTH), F32)],
        compiler_params=pltpu.CompilerParams(
            dimension_semantics=("arbitrary", "arbitrary"), vmem_limit_bytes=VMEM_LIMIT),
        name="post",
    )(x, yb, n1, sh1, sc1, g1, n2, sh2, sc2, wc, cw, woc, woa, wo, wr, br, tri)


SC_CORES = 2
SC_SUBCORES = 16
SC_WORKERS = SC_CORES * SC_SUBCORES
SC_CHUNK = 64
ROW_WORDS = D_MODEL // 2


def _sc_mesh():
    return plsc.VectorSubcoreMesh(core_axis_name="c", subcore_axis_name="s",
                                  num_cores=SC_CORES, num_subcores=SC_SUBCORES)


def _dispatch_body(rows_hbm, idx1_hbm, idx2_hbm, xs_hbm, idx1_v, idx2_v, rows_v, *, n_chunks):
    wid = lax.axis_index("s") * SC_CORES + lax.axis_index("c")
    pltpu.sync_copy(idx1_hbm.at[wid], idx1_v)
    pltpu.sync_copy(idx2_hbm.at[wid], idx2_v)
    base = wid * (n_chunks * SC_CHUNK)

    @pl.loop(0, n_chunks)
    def _(j):
        pltpu.sync_copy(rows_hbm.at[pl.ds(base + j * SC_CHUNK, SC_CHUNK)], rows_v)
        pltpu.sync_copy(rows_v, xs_hbm.at[idx1_v.at[j]])
        pltpu.sync_copy(rows_v, xs_hbm.at[idx2_v.at[j]])


def _sc_scratch(n_chunks):
    return [pltpu.VMEM((n_chunks, SC_CHUNK), jnp.int32), pltpu.VMEM((n_chunks, SC_CHUNK), jnp.int32),
            pltpu.VMEM((SC_CHUNK, ROW_WORDS), jnp.uint32)]


def _dispatch(rows, idx1, idx2, n_slots):
    n_chunks = idx1.shape[1]
    return pl.kernel(
        functools.partial(_dispatch_body, n_chunks=n_chunks),
        out_type=jax.ShapeDtypeStruct((n_slots, ROW_WORDS), jnp.uint32),
        mesh=_sc_mesh(),
        scratch_types=_sc_scratch(n_chunks),
        name="dispatch",
    )(rows, idx1, idx2)


def _collect_body(ys_hbm, idx1_hbm, idx2_hbm, g1_hbm, g2_hbm, idx1_v, idx2_v, rows_v, *, n_chunks):
    wid = lax.axis_index("s") * SC_CORES + lax.axis_index("c")
    pltpu.sync_copy(idx1_hbm.at[wid], idx1_v)
    pltpu.sync_copy(idx2_hbm.at[wid], idx2_v)
    base = wid * (n_chunks * SC_CHUNK)

    @pl.loop(0, n_chunks)
    def _(j):
        dst = pl.ds(base + j * SC_CHUNK, SC_CHUNK)
        pltpu.sync_copy(ys_hbm.at[idx1_v.at[j]], rows_v)
        pltpu.sync_copy(rows_v, g1_hbm.at[dst])
        pltpu.sync_copy(ys_hbm.at[idx2_v.at[j]], rows_v)
        pltpu.sync_copy(rows_v, g2_hbm.at[dst])


def _collect(ys, idx1, idx2):
    n_chunks = idx1.shape[1]
    out = jax.ShapeDtypeStruct((SC_WORKERS * n_chunks * SC_CHUNK, ROW_WORDS), jnp.uint32)
    return pl.kernel(
        functools.partial(_collect_body, n_chunks=n_chunks),
        out_type=(out, out),
        mesh=_sc_mesh(),
        scratch_types=_sc_scratch(n_chunks),
        name="collect",
    )(ys, idx1, idx2)


def _moe_kernel(te_ref, nt_ref, xs_ref, wg_ref, wu_ref, wd_ref, ys_ref):
    @pl.when(pl.program_id(0) < nt_ref[0])
    def _():
        left, right = _unpack_bf16_pairs(xs_ref[...])
        xb = jnp.concatenate([left.astype(BF16), right.astype(BF16)], axis=1)
        g = _dot(xb, wg_ref[0].astype(BF16))
        u = _dot(xb, wu_ref[0].astype(BF16))
        a = (g * _sigmoid(g) * u).astype(BF16)
        ys_ref[...] = _pack_bf16_pairs(_dot(a, wd_ref[0].astype(BF16)))


def _moe(xs, tile_expert, n_tiles, wg, wu, wd, tm):
    n_slots = xs.shape[0]
    row_blk = lambda i, te, nt: (jnp.minimum(i, nt[0] - 1), 0)
    w_blk = lambda i, te, nt: (te[i], 0, 0)
    return pl.pallas_call(
        _moe_kernel,
        out_shape=jax.ShapeDtypeStruct((n_slots, ROW_WORDS), jnp.uint32),
        grid_spec=pltpu.PrefetchScalarGridSpec(
            num_scalar_prefetch=2,
            grid=(n_slots // tm,),
            in_specs=[pl.BlockSpec((tm, ROW_WORDS), row_blk),
                      pl.BlockSpec((1, D_MODEL, D_EXPERT), w_blk),
                      pl.BlockSpec((1, D_MODEL, D_EXPERT), w_blk),
                      pl.BlockSpec((1, D_EXPERT, D_MODEL), w_blk)],
            out_specs=pl.BlockSpec((tm, ROW_WORDS), row_blk)),
        compiler_params=pltpu.CompilerParams(
            dimension_semantics=("arbitrary",), vmem_limit_bytes=VMEM_LIMIT),
        name="moe",
    )(tile_expert, n_tiles, xs, wg, wu, wd)


def _final_kernel(x1_ref, g1_ref, g2_ref, rw_ref, gate_ref, *rest):
    o_ref = rest[-1]
    rw = rw_ref[0]
    w1 = rw[:, 0:1]
    w2 = rw[:, 1:2]
    a_l, a_r = _unpack_bf16_pairs(g1_ref[0])
    b_l, b_r = _unpack_bf16_pairs(g2_ref[0])
    moe = jnp.concatenate([w1 * a_l + w2 * b_l, w1 * a_r + w2 * b_r], axis=1)
    o_ref[0] = x1_ref[0] + gate_ref[0] * moe


def _final(x1, g1, g2, rw, gate2, tm, b0, out_prev, after):
    Bg, S, _ = x1.shape
    tok = lambda w: pl.BlockSpec((1, tm, w), lambda b, i: (b, i, 0))
    extra = (() if out_prev is None else (out_prev,)) + tuple(after)
    return pl.pallas_call(
        _final_kernel,
        out_shape=jax.ShapeDtypeStruct((gate2.shape[0], S, D_MODEL), F32),
        grid=(Bg, S // tm),
        in_specs=[tok(D_MODEL), tok(ROW_WORDS), tok(ROW_WORDS), tok(LANES),
                  pl.BlockSpec((1, 1, D_MODEL), lambda b, i: (b + b0, 0, 0))]
        + [pl.BlockSpec(memory_space=pl.ANY)] * len(extra),
        out_specs=pl.BlockSpec((1, tm, D_MODEL), lambda b, i: (b + b0, i, 0)),
        input_output_aliases={} if out_prev is None else {5: 0},
        compiler_params=pltpu.CompilerParams(
            dimension_semantics=("arbitrary", "arbitrary"), vmem_limit_bytes=VMEM_LIMIT),
        name="final",
    )(x1, g1, g2, rw, gate2, *extra)


def _pick(n, pref):
    t = min(n, pref)
    assert n % t == 0, (n, t)
    return t


def _slots_kernel(offs_ref, r_ref, o_ref):
    r = r_ref[0]
    base = jnp.zeros_like(r)
    for e in range(N_EXPERTS):
        base = jnp.where(r == e, offs_ref[e], base)
    o_ref[0] = base + pltpu.roll(r, 6, 0)


def _slots(ridx, offs):
    B, _, S = ridx.shape
    return pl.pallas_call(
        _slots_kernel,
        out_shape=jax.ShapeDtypeStruct((B, 8, S), jnp.int32),
        grid_spec=pltpu.PrefetchScalarGridSpec(
            num_scalar_prefetch=1, grid=(B,),
            in_specs=[pl.BlockSpec((1, 8, S), lambda b, offs: (b, 0, 0))],
            out_specs=pl.BlockSpec((1, 8, S), lambda b, offs: (b, 0, 0))),
        compiler_params=pltpu.CompilerParams(dimension_semantics=("arbitrary",)),
        name="slots",
    )(offs, ridx)


def _route_plan(ridx, counts, tm_e, T):
    counts = counts.astype(jnp.int32)
    tiles = (counts + tm_e - 1) // tm_e
    tile_end = jnp.cumsum(tiles)
    offs = (tile_end - tiles) * tm_e
    slots = _slots(ridx, offs)
    n_chunks = T // (SC_WORKERS * SC_CHUNK)
    idx1 = slots[:, 0, :].reshape(SC_WORKERS, n_chunks, SC_CHUNK)
    idx2 = slots[:, 1, :].reshape(SC_WORKERS, n_chunks, SC_CHUNK)
    n_tiles_max = 2 * T // tm_e + N_EXPERTS
    tile_ids = jnp.arange(n_tiles_max, dtype=jnp.int32)
    tile_expert = jnp.sum((tile_end[None, :] <= tile_ids[:, None]).astype(jnp.int32), axis=1)
    tile_expert = jnp.minimum(tile_expert, N_EXPERTS - 1)
    return idx1, idx2, tile_expert, tile_end[-1:].astype(jnp.int32), n_tiles_max * tm_e


def _layer(x, c, w_ada, b_ada, norm1_w, w_in, b_forget, conv_w, q_norm_w, k_norm_w,
           w_out_conv, w_out_attn, w_o, norm2_w, w_rg, b_rg, w_re, b_re, w_gate, w_up, w_down):
    B, S, _ = x.shape
    n_grp = 2 if B % 2 == 0 and (B // 2 * S) % (SC_WORKERS * SC_CHUNK) == 0 else 1
    Bg = B // n_grp
    T = Bg * S
    assert T % (SC_WORKERS * SC_CHUNK) == 0, T
    mod = _ada(c, w_ada, b_ada.reshape(1, -1)).reshape(B, 6, 1, D_MODEL)
    shift1, scale1, gate1, shift2, scale2, gate2 = (mod[:, t] for t in range(6))

    cuts = np.cumsum([0, CONV_WIDTH, CONV_WIDTH, CONV_WIDTH, ATTN_WIDTH, ATTN_WIDTH, ATTN_WIDTH,
                      N_HEADS, D_MODEL, D_MODEL])
    w_conv3 = w_in[:, cuts[0]:cuts[3]]
    w_qvT = jnp.concatenate([w_in[:, cuts[3]:cuts[4]], w_in[:, cuts[5]:cuts[6]]], axis=1).T.astype(BF16)
    w_k = w_in[:, cuts[4]:cuts[5]].astype(BF16)
    w_f = jnp.pad(w_in[:, cuts[6]:cuts[7]], ((0, 0), (0, LANES - N_HEADS))).astype(BF16)
    b_f = jnp.pad(b_forget, (0, LANES - N_HEADS)).reshape(1, LANES)
    w_cgg = jnp.concatenate([w_conv3, w_in[:, cuts[7]:cuts[9]]], axis=1).astype(BF16)

    tm_qkv = _pick(S, 512)
    qnwT = jnp.broadcast_to((jnp.tile(q_norm_w, N_HEADS) * (LOG2E * HEAD_DIM ** -0.5))[:, None],
                            (ATTN_WIDTH, LANES))
    tq = _pick(S // 2, 512)
    w_r = jnp.pad(jnp.concatenate([w_re, w_rg], axis=1),
                  ((0, 0), (0, LANES - N_EXPERTS - N_GROUPS)))
    w_r_hi = w_r.astype(BF16)
    w_r_lo = (w_r - w_r_hi.astype(F32)).astype(BF16)
    b_r = jnp.pad(jnp.concatenate([b_re, b_rg]), (0, LANES - N_EXPERTS - N_GROUPS)).reshape(1, LANES)
    cw = jnp.pad(conv_w, ((0, 8 - CONV_K), (0, 0)))
    tm_post = _pick(S, 1024)
    tm_e = 512
    w_oc, w_oa, w_ob, w_rs = (w_out_conv.astype(BF16), w_out_attn.astype(BF16), w_o.astype(BF16),
                              jnp.stack([w_r_hi, w_r_lo]))

    routed = []
    for g in range(n_grp):
        b0 = g * Bg
        qT_aug, k_aug, vT = _qkv(x, norm1_w.reshape(1, -1), shift1, scale1, w_qvT, w_k, w_f, b_f,
                                 qnwT, jnp.tile(k_norm_w, N_HEADS).reshape(1, -1), tm_qkv, b0, Bg)
        y_b = _attention(qT_aug, k_aug, vT, tq, _pick(tq, 256))
        x1, h2p, ridx, rw, counts = _post(x, y_b, norm1_w.reshape(1, -1), shift1, scale1, gate1,
                                          norm2_w.reshape(1, -1), shift2, scale2,
                                          w_cgg, cw, w_oc, w_oa, w_ob, w_rs, b_r, tm_post, b0)
        idx1, idx2, tile_expert, n_tiles, n_slots = _route_plan(ridx, counts[0, :N_EXPERTS], tm_e, T)
        xs = _dispatch(h2p.reshape(T, ROW_WORDS), idx1, idx2, n_slots)
        routed.append((x1, rw, idx1, idx2, tile_expert, n_tiles, xs))

    gathered = []
    for x1, rw, idx1, idx2, tile_expert, n_tiles, xs in routed:
        ys = _moe(xs, tile_expert, n_tiles, w_gate, w_up, w_down, tm_e)
        gathered.append(_collect(ys, idx1, idx2) + (ys,))

    out = None
    for g, ((x1, rw, *_), (g1, g2, _)) in enumerate(zip(routed, gathered)):
        later = [ys for (_, _, ys) in gathered[g + 1:]]
        out = _final(x1, g1.reshape(Bg, S, ROW_WORDS), g2.reshape(Bg, S, ROW_WORDS), rw, gate2,
                     _pick(S, 512), g * Bg, out, later)
    return out


def kernel(x, c, w_ada, b_ada, norm1_w, w_in, b_forget, conv_w, q_norm_w, k_norm_w, w_out_conv,
           w_out_attn, w_o, norm2_w, w_router_group, b_router_group, w_router_expert,
           b_router_expert, w_gate, w_up, w_down):
    for l in range(w_ada.shape[0]):
        x = _layer(x, c, w_ada[l], b_ada[l], norm1_w[l], w_in[l], b_forget[l], conv_w[l],
                   q_norm_w[l], k_norm_w[l], w_out_conv[l], w_out_attn[l], w_o[l], norm2_w[l],
                   w_router_group[l], b_router_group[l], w_router_expert[l], b_router_expert[l],
                   w_gate[l], w_up[l], w_down[l])
    return x
```

```python
import functools

import jax
import jax.numpy as jnp
import numpy as np
from jax import lax
from jax.experimental import pallas as pl
from jax.experimental.pallas import tpu as pltpu
from jax.experimental.pallas import tpu_sc as plsc

D_MODEL = 1024
CONV_WIDTH = 512
CONV_K = 3
N_HEADS = 8
HEAD_DIM = 64
ATTN_WIDTH = N_HEADS * HEAD_DIM
N_PAIRS = N_HEADS // 2
N_GROUPS = 4
EXPERTS_PER_GROUP = 8
N_EXPERTS = N_GROUPS * EXPERTS_PER_GROUP
D_EXPERT = 256
EPS = 1e-6
LANES = 128
AUG = 2 * LANES
BIAS_W = 6
VROWS = HEAD_DIM + 16
NEG = -1e30
LOG2E = 1.4426950408889634

F32 = jnp.float32
BF16 = jnp.bfloat16
VMEM_LIMIT = 56 * 1024 * 1024


def _sigmoid(z):
    return 1.0 / (1.0 + jnp.exp(-z))


def _split3(z):
    hi = z.astype(BF16)
    r = z - hi.astype(F32)
    mid = r.astype(BF16)
    lo = (r - mid.astype(F32)).astype(BF16)
    return hi, mid, lo


def _dot(a, b):
    return jnp.dot(a, b, preferred_element_type=F32)


def _modulated_norm(x, nw, shift, scale):
    ms = jnp.mean(x * x, axis=-1, keepdims=True)
    return (x * lax.rsqrt(ms + EPS) * nw) * (1.0 + scale) + shift


def _ada_kernel(c_ref, w_ref, b_ref, o_ref):
    c = c_ref[...]
    a = c * _sigmoid(c)
    o_ref[...] = jnp.dot(a, w_ref[...], precision=lax.Precision.HIGHEST,
                         preferred_element_type=F32) + b_ref[...]


def _ada(c, w_ada, b_ada):
    B = c.shape[0]
    n = w_ada.shape[1] // D_MODEL
    return pl.pallas_call(
        _ada_kernel,
        out_shape=jax.ShapeDtypeStruct((B, n * D_MODEL), F32),
        grid=(n,),
        in_specs=[pl.BlockSpec((B, D_MODEL), lambda j: (0, 0)),
                  pl.BlockSpec((D_MODEL, D_MODEL), lambda j: (0, j)),
                  pl.BlockSpec((1, D_MODEL), lambda j: (0, j))],
        out_specs=pl.BlockSpec((B, D_MODEL), lambda j: (0, j)),
        compiler_params=pltpu.CompilerParams(dimension_semantics=("arbitrary",)),
        name="ada",
    )(c, w_ada, b_ada)


_NT = (((1,), (1,)), ((), ()))


def _lane_tile(a, width):
    return jnp.concatenate([a] * (width // a.shape[1]), axis=1)


def _qkv_kernel(x_ref, nw_ref, sh_ref, sc_ref, wqv_ref, wk_ref, wf_ref, bf_ref, qnw_ref, knw_ref,
                gsum_ref, pq_ref, pk_ref, cq_ref, ck_ref,
                qT_ref, k_ref, vT_ref, carry_ref):
    tm = x_ref.shape[1]
    h = _modulated_norm(x_ref[0], nw_ref[...], sh_ref[0], sc_ref[0])
    hb = h.astype(BF16)
    qvT = lax.dot_general(wqv_ref[...], hb, _NT, preferred_element_type=F32)
    k = _dot(hb, wk_ref[...])

    heads = []
    for hd in range(N_HEADS):
        z = qvT[hd * HEAD_DIM:(hd + 1) * HEAD_DIM]
        heads.append(z * lax.rsqrt(jnp.mean(z * z, axis=0, keepdims=True) + EPS))
    qnT = jnp.concatenate(heads, axis=0) * _lane_tile(qnw_ref[...], tm)

    ss = _dot((k * k).astype(BF16), gsum_ref[...])
    kn = k * lax.rsqrt(ss * (1.0 / HEAD_DIM) + EPS) * knw_ref[...]

    fl = _dot(hb, wf_ref[...]) + bf_ref[...]
    lf = jnp.minimum(fl, 0.0) - jnp.log(1.0 + jnp.exp(-jnp.abs(fl)))

    @pl.when(pl.program_id(1) == 0)
    def _():
        carry_ref[...] = jnp.zeros_like(carry_ref)

    row = lax.broadcasted_iota(jnp.int32, (tm, LANES), 0)
    cum = lf
    s = 1
    while s < tm:
        cum = cum + jnp.where(row >= s, pltpu.roll(cum, s, 0), 0.0)
        s *= 2
    cum = cum + carry_ref[7:8, :]
    carry_ref[...] = cum[tm - 8:, :]

    parts = jnp.concatenate(_split3(cum * LOG2E), axis=1)
    eqT = (lax.dot_general(pq_ref[...], parts, _NT, preferred_element_type=F32)
           + _lane_tile(cq_ref[...], tm)).astype(BF16)
    ek = (_dot(parts, pk_ref[...]) + ck_ref[...]).astype(BF16)
    for j in range(N_PAIRS):
        qT_ref[0, j, :LANES, :] = qnT[j * LANES:(j + 1) * LANES].astype(BF16)
        qT_ref[0, j, LANES:, :] = eqT
        for t in range(2):
            r0 = ATTN_WIDTH + (2 * j + t) * HEAD_DIM
            vT_ref[0, j, t * VROWS:t * VROWS + HEAD_DIM, :] = qvT[r0:r0 + HEAD_DIM].astype(BF16)
            vT_ref[0, j, t * VROWS + HEAD_DIM:(t + 1) * VROWS, :] = jnp.ones((VROWS - HEAD_DIM, tm), BF16)
        k_ref[0, :, j * AUG:j * AUG + LANES] = kn[:, j * LANES:(j + 1) * LANES].astype(BF16)
        k_ref[0, :, j * AUG + LANES:(j + 1) * AUG] = ek


def _bias_placement():
    pq = np.zeros((LANES, 3 * LANES), np.float32)
    pk = np.zeros((3 * LANES, LANES), np.float32)
    cq = np.zeros((LANES, LANES), np.float32)
    ck = np.zeros((1, LANES), np.float32)
    for hd in range(N_HEADS):
        base = BIAS_W * hd
        for p in range(3):
            pq[base + p, p * LANES + hd] = 1.0
            pk[p * LANES + hd, base + 3 + p] = -1.0
            cq[base + 3 + p, :] = 1.0
            ck[0, base + p] = 1.0
    return (jnp.asarray(pq, BF16), jnp.asarray(pk, BF16), jnp.asarray(cq), jnp.asarray(ck))


def _qkv(x, nw, shift, scale, wqvT, wk, wf, bf, qnwT, knw, tm, b0, B):
    S = x.shape[1]
    gsum = jnp.asarray(np.kron(np.eye(N_HEADS), np.ones((HEAD_DIM, HEAD_DIM))), BF16)
    pq, pk, cq, ck = _bias_placement()
    const = lambda *shape: pl.BlockSpec(shape, lambda b, i: (0,) * len(shape),
                                        pipeline_mode=pl.Buffered(1))
    return pl.pallas_call(
        _qkv_kernel,
        out_shape=(jax.ShapeDtypeStruct((B, N_PAIRS, AUG, S), BF16),
                   jax.ShapeDtypeStruct((B, S, N_PAIRS * AUG), BF16),
                   jax.ShapeDtypeStruct((B, N_PAIRS, 2 * VROWS, S), BF16)),
        grid=(B, S // tm),
        in_specs=[pl.BlockSpec((1, tm, D_MODEL), lambda b, i: (b + b0, i, 0)),
                  const(1, D_MODEL),
                  pl.BlockSpec((1, 1, D_MODEL), lambda b, i: (b + b0, 0, 0)),
                  pl.BlockSpec((1, 1, D_MODEL), lambda b, i: (b + b0, 0, 0)),
                  const(2 * ATTN_WIDTH, D_MODEL),
                  const(D_MODEL, ATTN_WIDTH),
                  const(D_MODEL, LANES),
                  const(1, LANES),
                  const(ATTN_WIDTH, LANES),
                  const(1, ATTN_WIDTH),
                  const(ATTN_WIDTH, ATTN_WIDTH),
                  const(LANES, 3 * LANES),
                  const(3 * LANES, LANES),
                  const(LANES, LANES),
                  const(1, LANES)],
        out_specs=(pl.BlockSpec((1, N_PAIRS, AUG, tm), lambda b, i: (b, 0, 0, i)),
                   pl.BlockSpec((1, tm, N_PAIRS * AUG), lambda b, i: (b, i, 0)),
                   pl.BlockSpec((1, N_PAIRS, 2 * VROWS, tm), lambda b, i: (b, 0, 0, i))),
        scratch_shapes=[pltpu.VMEM((8, LANES), F32)],
        compiler_params=pltpu.CompilerParams(
            dimension_semantics=("arbitrary", "arbitrary"), vmem_limit_bytes=VMEM_LIMIT),
        name="qkv",
    )(x, nw, shift, scale, wqvT, wk, wf, bf, qnwT, knw, gsum, pq, pk, cq, ck)


def _attn_kernel(qT_ref, k_ref, vT_ref, *rest, tq, cw):
    o_ref, qq_ref, s_ref, smax_ref, m_ref, acc_ref = rest[-6:]
    i = pl.program_id(2)
    n = tq // cw
    chains = [(a, t, c) for a in range(2) for t in range(2) for c in range(n)]
    second = [ci for ci, (a, _, _) in enumerate(chains) if a == 1]
    feat = lax.broadcasted_iota(jnp.int32, (AUG, tq), 0)
    for t in range(2):
        bias0 = LANES + BIAS_W * (2 * pl.program_id(1) + t)
        keep = ((feat >= t * HEAD_DIM) & (feat < (t + 1) * HEAD_DIM)) | \
               ((feat >= bias0) & (feat < bias0 + BIAS_W))
        for a in range(2):
            qT = qT_ref[0, 0, :, a * tq:(a + 1) * tq]
            qh = jnp.where(keep, qT, jnp.zeros_like(qT))
            for c in range(n):
                qq_ref[chains.index((a, t, c))] = qh[:, c * cw:(c + 1) * cw]
    kpos = lax.broadcasted_iota(jnp.int32, (tq, cw), 0)
    qpos = lax.broadcasted_iota(jnp.int32, (tq, cw), 1)

    def scores(j, slot, which):
        k_blk = k_ref[0, pl.ds(pl.multiple_of(j * tq, tq), tq), :]
        for ci in which:
            s = _dot(k_blk, qq_ref[ci])
            s_ref[slot, ci] = s
            smax_ref[slot, ci] = jnp.broadcast_to(jnp.max(s, axis=0, keepdims=True), (8, cw))

    def absorb(j, slot, which, diagonal=()):
        start = pl.multiple_of(j * tq, tq)
        for ci in which:
            _, t, c = chains[ci]
            vj = vT_ref[0, 0, t * VROWS:(t + 1) * VROWS, pl.ds(start, tq)]
            s = s_ref[slot, ci]
            if ci in diagonal:
                s = jnp.where(kpos <= qpos + c * cw, s, NEG)
                smax = jnp.max(s, axis=0, keepdims=True)
            else:
                smax = smax_ref[slot, ci, 0:1]
            m = m_ref[ci, 0:1]
            m_new = jnp.maximum(m, smax)
            p = jnp.exp2(s - m_new).astype(BF16)
            acc_ref[ci] = jnp.exp2(m - m_new) * acc_ref[ci] + _dot(vj, p)
            m_ref[ci] = jnp.broadcast_to(m_new, (8, cw))

    every = list(range(len(chains)))
    first = [ci for ci in every if ci not in second]
    m_ref[...] = jnp.full(m_ref.shape, NEG, F32)
    acc_ref[...] = jnp.zeros(acc_ref.shape, F32)
    scores(0, 0, every)

    def two_blocks(jj, _):
        j = 2 * jj
        scores(j + 1, 1, every)
        absorb(j, 0, every)
        scores(j + 2, 0, every)
        absorb(j + 1, 1, every)
        return 0

    lax.fori_loop(0, i, two_blocks, 0)
    scores(2 * i + 1, 1, second)
    absorb(2 * i, 0, every, diagonal=first)
    absorb(2 * i + 1, 1, second, diagonal=second)

    for a in range(2):
        outs = [acc_ref[ci, :HEAD_DIM] / acc_ref[ci, HEAD_DIM:HEAD_DIM + 1]
                for ci in every if chains[ci][0] == a]
        oT = jnp.concatenate([jnp.concatenate(outs[:n], axis=1), jnp.concatenate(outs[n:], axis=1)],
                             axis=0)
        o_ref[0, a * tq:(a + 1) * tq, :] = oT.T.astype(BF16)


def _attention(qT_aug, k_aug, vT, tq, cw, after):
    B, S, _ = k_aug.shape
    n_chains = 4 * tq // cw
    return pl.pallas_call(
        functools.partial(_attn_kernel, tq=tq, cw=cw),
        scratch_shapes=[pltpu.VMEM((n_chains, AUG, cw), BF16),
                        pltpu.VMEM((2, n_chains, tq, cw), F32),
                        pltpu.VMEM((2, n_chains, 8, cw), F32),
                        pltpu.VMEM((n_chains, 8, cw), F32),
                        pltpu.VMEM((n_chains, VROWS, cw), F32)],
        out_shape=jax.ShapeDtypeStruct((B, S, ATTN_WIDTH), BF16),
        grid=(B, N_PAIRS, S // (2 * tq)),
        in_specs=[pl.BlockSpec((1, 1, AUG, 2 * tq), lambda b, j, i: (b, j, 0, i)),
                  pl.BlockSpec((1, S, AUG), lambda b, j, i: (b, 0, j)),
                  pl.BlockSpec((1, 1, 2 * VROWS, S), lambda b, j, i: (b, j, 0, 0))]
        + [pl.BlockSpec(memory_space=pl.ANY)] * len(after),
        out_specs=pl.BlockSpec((1, 2 * tq, LANES), lambda b, j, i: (b, i, j)),
        compiler_params=pltpu.CompilerParams(
            dimension_semantics=("arbitrary", "arbitrary", "arbitrary"),
            vmem_limit_bytes=VMEM_LIMIT),
        name="attn",
    )(qT_aug, k_aug, vT, *after)


def _pack_bf16_pairs(z):
    w = z.shape[1] // 2
    bits = pltpu.bitcast(z.astype(BF16).astype(F32), jnp.uint32)
    return bits[:, :w] | (bits[:, w:] >> 16)


def _unpack_bf16_pairs(p):
    return (pltpu.bitcast(p & jnp.uint32(0xFFFF0000), F32), pltpu.bitcast(p << 16, F32))


POST_ROWS = 256


def _post_kernel(x_ref, yb_ref, n1_ref, sh1_ref, sc1_ref, g1_ref, n2_ref, sh2_ref, sc2_ref,
                 wc_ref, cw_ref, woc_ref, woa_ref, wo_ref, wr_ref, br_ref, tri_ref,
                 x1_ref, h2_ref, ridx_ref, rw_ref, cnt_ref, carry_ref):
    tm = x_ref.shape[1]
    n_grp = max(tm // POST_ROWS, 1)
    rows = tm // n_grp
    lane = lax.broadcasted_iota(jnp.int32, (rows, LANES), 1)
    row8 = lax.broadcasted_iota(jnp.int32, (8, CONV_WIDTH), 0)
    big = jnp.int32(1 << 20)

    @pl.when(pl.program_id(1) == 0)
    def _():
        carry_ref[...] = jnp.zeros_like(carry_ref)

    @pl.when((pl.program_id(0) == 0) & (pl.program_id(1) == 0))
    def _():
        cnt_ref[...] = jnp.zeros_like(cnt_ref)

    st = [dict(rs=pl.ds(g * rows, rows)) for g in range(n_grp)]

    def conv_in(d):
        d["x"] = x_ref[0, d["rs"], :]
        d["hb"] = _modulated_norm(d["x"], n1_ref[...], sh1_ref[0], sc1_ref[0]).astype(BF16)
        d["x_in"] = _dot(d["hb"], wc_ref[:, :CONV_WIDTH])
        d["conv_c"] = _dot(d["hb"], wc_ref[:, 2 * CONV_WIDTH:3 * CONV_WIDTH])
        d["conv_b"] = _dot(d["hb"], wc_ref[:, CONV_WIDTH:2 * CONV_WIDTH])

    def conv(d, prev):
        u = d.pop("conv_c") * d.pop("x_in")
        d["u_tail"] = u[rows - 8:, :]

        def shifted(k):
            r = pltpu.roll(u, k, 0)
            top = jnp.where(row8 < k, pltpu.roll(prev, k, 0), r[:8])
            return jnp.concatenate([top, r[8:]], axis=0)

        cw = cw_ref[...]
        cv = cw[0:1] * shifted(2) + cw[1:2] * shifted(1) + cw[2:3] * u
        d["y_a"] = (d.pop("conv_b") * cv).astype(BF16)

    def gates(d):
        d["p_b"] = _dot(yb_ref[0, d["rs"], :], woa_ref[...])
        d["gate_c"] = _dot(d["hb"], wc_ref[:, 3 * CONV_WIDTH:3 * CONV_WIDTH + D_MODEL])
        d["gate_a"] = _dot(d.pop("hb"), wc_ref[:, 3 * CONV_WIDTH + D_MODEL:])

    def branch_a(d):
        d["p_a"] = _dot(d.pop("y_a"), woc_ref[...])

    def merge(d):
        d["merged"] = (_sigmoid(d.pop("gate_c")) * d.pop("p_a")
                       + _sigmoid(d.pop("gate_a")) * d.pop("p_b")).astype(BF16)

    def out_proj(d):
        d["o"] = _dot(d.pop("merged"), wo_ref[...])

    def residual(d):
        x1 = d.pop("x") + g1_ref[0] * d.pop("o")
        x1_ref[0, d["rs"], :] = x1
        h2 = _modulated_norm(x1, n2_ref[...], sh2_ref[0], sc2_ref[0])
        h2_ref[0, d["rs"], :] = _pack_bf16_pairs(h2)
        d["h_hi"] = h2.astype(BF16)
        d["h_lo"] = (h2 - d["h_hi"].astype(F32)).astype(BF16)

    def router(d):
        both = _dot(d.pop("h_hi"), wr_ref[...])
        d["lg"] = (both[:, :LANES] + both[:, LANES:] + _dot(d.pop("h_lo"), wr_ref[:, :LANES])) \
            + br_ref[...]

    def first_argmax(vals):
        mx = jnp.max(vals, axis=-1, keepdims=True)
        idx = jnp.min(jnp.where(vals == mx, lane, big), axis=-1, keepdims=True)
        return mx, idx

    def route(d):
        lg = d.pop("lg")
        is_g = (lane >= N_EXPERTS) & (lane < N_EXPERTS + N_GROUPS)
        g_mx, g_lane = first_argmax(jnp.where(is_g, lg, NEG))
        p_sel = 1.0 / jnp.sum(jnp.where(is_g, jnp.exp(lg - g_mx), 0.0), axis=-1, keepdims=True)
        g_idx = g_lane - N_EXPERTS
        in_g = (lane >= g_idx * EXPERTS_PER_GROUP) & (lane < (g_idx + 1) * EXPERTS_PER_GROUP)
        le = jnp.where(in_g, lg, NEG)
        v1, i1 = first_argmax(le)
        v2, i2 = first_argmax(jnp.where(lane == i1, NEG, le))
        e2 = jnp.exp(v2 - v1)
        w1 = p_sel / (1.0 + e2)
        w2 = w1 * e2
        rw_ref[0, d["rs"], :] = jnp.where(lane == 0, w1, 0.0) + jnp.where(lane == 1, w2, 0.0)
        d["i1"], d["i2"] = i1, i2
        d["onehot"] = jnp.where((lane == i1) | (lane == i2), 1.0, 0.0)

    def rank(d):
        onehot = d.pop("onehot")
        before = _dot(tri_ref[...], onehot.astype(BF16)) + cnt_ref[0:1, :]
        cnt_ref[...] = cnt_ref[...] + jnp.sum(onehot, axis=0, keepdims=True)
        i1, i2 = d.pop("i1"), d.pop("i2")
        r1 = jnp.sum(jnp.where(lane == i1, before, 0.0), axis=-1, keepdims=True)
        r2 = jnp.sum(jnp.where(lane == i2, before, 0.0), axis=-1, keepdims=True)
        rec = (jnp.where(lane == 0, i1.astype(F32), 0.0) + jnp.where(lane == 1, i2.astype(F32), 0.0)
               + jnp.where(lane == 2, r1, 0.0) + jnp.where(lane == 3, r2, 0.0))
        ridx_ref[0, :, d["rs"]] = rec.T[:8].astype(jnp.int32)

    stages = [conv_in, None, gates, branch_a, merge, out_proj, residual, router, route, rank]
    lag = 2
    for step in range(len(stages) + lag * (n_grp - 1)):
        for g, d in enumerate(st):
            k = step - lag * g
            if 0 <= k < len(stages):
                if stages[k] is None:
                    conv(d, carry_ref[...] if g == 0 else st[g - 1]["u_tail"])
                else:
                    stages[k](d)
    carry_ref[...] = st[-1]["u_tail"]


def _post(x, yb, n1, sh1, sc1, g1, n2, sh2, sc2, wc, cw, woc, woa, wo, wr, br, tm, b0):
    B, S, _ = yb.shape
    rows = tm // max(tm // POST_ROWS, 1)
    tri = jnp.asarray(np.tril(np.ones((rows, rows), np.float32), -1), BF16)
    const = lambda *shape: pl.BlockSpec(shape, lambda b, i: (0,) * len(shape),
                                        pipeline_mode=pl.Buffered(1))
    perb = pl.BlockSpec((1, 1, D_MODEL), lambda b, i: (b + b0, 0, 0))
    tok = lambda w: pl.BlockSpec((1, tm, w), lambda b, i: (b, i, 0))
    return pl.pallas_call(
        _post_kernel,
        out_shape=(jax.ShapeDtypeStruct((B, S, D_MODEL), F32),
                   jax.ShapeDtypeStruct((B, S, D_MODEL // 2), jnp.uint32),
                   jax.ShapeDtypeStruct((B, 8, S), jnp.int32),
                   jax.ShapeDtypeStruct((B, S, LANES), F32),
                   jax.ShapeDtypeStruct((8, LANES), F32)),
        grid=(B, S // tm),
        in_specs=[pl.BlockSpec((1, tm, D_MODEL), lambda b, i: (b + b0, i, 0)), tok(ATTN_WIDTH),
                  const(1, D_MODEL), perb, perb, perb,
                  const(1, D_MODEL), perb, perb,
                  const(D_MODEL, 3 * CONV_WIDTH + 2 * D_MODEL),
                  const(8, CONV_WIDTH),
                  const(CONV_WIDTH, D_MODEL), const(ATTN_WIDTH, D_MODEL),
                  const(D_MODEL, D_MODEL),
                  const(D_MODEL, 2 * LANES), const(1, LANES), const(rows, rows)],
        out_specs=(tok(D_MODEL), tok(D_MODEL // 2),
                   pl.BlockSpec((1, 8, tm), lambda b, i: (b, 0, i)), tok(LANES),
                   pl.BlockSpec((8, LANES), lambda b, i: (0, 0))),
        scratch_shapes=[pltpu.VMEM((8, CONV_WIDTH), F32)],
        compiler_params=pltpu.CompilerParams(
            dimension_semantics=("arbitrary", "arbitrary"), vmem_limit_bytes=VMEM_LIMIT),
        name="post",
    )(x, yb, n1, sh1, sc1, g1, n2, sh2, sc2, wc, cw, woc, woa, wo, wr, br, tri)


SC_CORES = 2
SC_SUBCORES = 16
SC_WORKERS = SC_CORES * SC_SUBCORES
SC_CHUNK = 64
ROW_WORDS = D_MODEL // 2


def _sc_mesh():
    return plsc.VectorSubcoreMesh(core_axis_name="c", subcore_axis_name="s",
                                  num_cores=SC_CORES, num_subcores=SC_SUBCORES)


def _dispatch_body(rows_hbm, idx1_hbm, idx2_hbm, xs_hbm, idx1_v, idx2_v, rows_v, *, n_chunks):
    wid = lax.axis_index("s") * SC_CORES + lax.axis_index("c")
    pltpu.sync_copy(idx1_hbm.at[wid], idx1_v)
    pltpu.sync_copy(idx2_hbm.at[wid], idx2_v)
    base = wid * (n_chunks * SC_CHUNK)

    @pl.loop(0, n_chunks)
    def _(j):
        pltpu.sync_copy(rows_hbm.at[pl.ds(base + j * SC_CHUNK, SC_CHUNK)], rows_v)
        pltpu.sync_copy(rows_v, xs_hbm.at[idx1_v.at[j]])
        pltpu.sync_copy(rows_v, xs_hbm.at[idx2_v.at[j]])


def _sc_scratch(n_chunks):
    return [pltpu.VMEM((n_chunks, SC_CHUNK), jnp.int32), pltpu.VMEM((n_chunks, SC_CHUNK), jnp.int32),
            pltpu.VMEM((SC_CHUNK, ROW_WORDS), jnp.uint32)]


def _dispatch(rows, idx1, idx2, n_slots):
    n_chunks = idx1.shape[1]
    return pl.kernel(
        functools.partial(_dispatch_body, n_chunks=n_chunks),
        out_type=jax.ShapeDtypeStruct((n_slots, ROW_WORDS), jnp.uint32),
        mesh=_sc_mesh(),
        scratch_types=_sc_scratch(n_chunks),
        name="dispatch",
    )(rows, idx1, idx2)


def _collect_body(ys_hbm, idx1_hbm, idx2_hbm, g1_hbm, g2_hbm, idx1_v, idx2_v, rows_v, *, n_chunks):
    wid = lax.axis_index("s") * SC_CORES + lax.axis_index("c")
    pltpu.sync_copy(idx1_hbm.at[wid], idx1_v)
    pltpu.sync_copy(idx2_hbm.at[wid], idx2_v)
    base = wid * (n_chunks * SC_CHUNK)

    @pl.loop(0, n_chunks)
    def _(j):
        dst = pl.ds(base + j * SC_CHUNK, SC_CHUNK)
        pltpu.sync_copy(ys_hbm.at[idx1_v.at[j]], rows_v)
        pltpu.sync_copy(rows_v, g1_hbm.at[dst])
        pltpu.sync_copy(ys_hbm.at[idx2_v.at[j]], rows_v)
        pltpu.sync_copy(rows_v, g2_hbm.at[dst])


def _collect(ys, idx1, idx2):
    n_chunks = idx1.shape[1]
    out = jax.ShapeDtypeStruct((SC_WORKERS * n_chunks * SC_CHUNK, ROW_WORDS), jnp.uint32)
    return pl.kernel(
        functools.partial(_collect_body, n_chunks=n_chunks),
        out_type=(out, out),
        mesh=_sc_mesh(),
        scratch_types=_sc_scratch(n_chunks),
        name="collect",
    )(ys, idx1, idx2)


def _moe_kernel(te_ref, nt_ref, xs_ref, wg_ref, wu_ref, wd_ref, *rest):
    ys_ref = rest[-1]

    @pl.when(pl.program_id(0) < nt_ref[0])
    def _():
        left, right = _unpack_bf16_pairs(xs_ref[...])
        xb = jnp.concatenate([left.astype(BF16), right.astype(BF16)], axis=1)
        g = _dot(xb, wg_ref[0].astype(BF16))
        u = _dot(xb, wu_ref[0].astype(BF16))
        a = (g * _sigmoid(g) * u).astype(BF16)
        ys_ref[...] = _pack_bf16_pairs(_dot(a, wd_ref[0].astype(BF16)))


def _moe(xs, tile_expert, n_tiles, wg, wu, wd, tm, after):
    n_slots = xs.shape[0]
    row_blk = lambda i, te, nt: (jnp.minimum(i, nt[0] - 1), 0)
    w_blk = lambda i, te, nt: (te[i], 0, 0)
    return pl.pallas_call(
        _moe_kernel,
        out_shape=jax.ShapeDtypeStruct((n_slots, ROW_WORDS), jnp.uint32),
        grid_spec=pltpu.PrefetchScalarGridSpec(
            num_scalar_prefetch=2,
            grid=(n_slots // tm,),
            in_specs=[pl.BlockSpec((tm, ROW_WORDS), row_blk),
                      pl.BlockSpec((1, D_MODEL, D_EXPERT), w_blk),
                      pl.BlockSpec((1, D_MODEL, D_EXPERT), w_blk),
                      pl.BlockSpec((1, D_EXPERT, D_MODEL), w_blk)]
            + [pl.BlockSpec(memory_space=pl.ANY)] * len(after),
            out_specs=pl.BlockSpec((tm, ROW_WORDS), row_blk)),
        compiler_params=pltpu.CompilerParams(
            dimension_semantics=("arbitrary",), vmem_limit_bytes=VMEM_LIMIT),
        name="moe",
    )(tile_expert, n_tiles, xs, wg, wu, wd, *after)


def _final_kernel(x1_ref, g1_ref, g2_ref, rw_ref, gate_ref, *rest):
    o_ref = rest[-1]
    rw = rw_ref[0]
    w1 = rw[:, 0:1]
    w2 = rw[:, 1:2]
    a_l, a_r = _unpack_bf16_pairs(g1_ref[0])
    b_l, b_r = _unpack_bf16_pairs(g2_ref[0])
    moe = jnp.concatenate([w1 * a_l + w2 * b_l, w1 * a_r + w2 * b_r], axis=1)
    o_ref[0] = x1_ref[0] + gate_ref[0] * moe


def _final(x1, g1, g2, rw, gate2, tm, b0, out_prev, after):
    Bg, S, _ = x1.shape
    tok = lambda w: pl.BlockSpec((1, tm, w), lambda b, i: (b, i, 0))
    extra = (() if out_prev is None else (out_prev,)) + tuple(after)
    return pl.pallas_call(
        _final_kernel,
        out_shape=jax.ShapeDtypeStruct((gate2.shape[0], S, D_MODEL), F32),
        grid=(Bg, S // tm),
        in_specs=[tok(D_MODEL), tok(ROW_WORDS), tok(ROW_WORDS), tok(LANES),
                  pl.BlockSpec((1, 1, D_MODEL), lambda b, i: (b + b0, 0, 0))]
        + [pl.BlockSpec(memory_space=pl.ANY)] * len(extra),
        out_specs=pl.BlockSpec((1, tm, D_MODEL), lambda b, i: (b + b0, i, 0)),
        input_output_aliases={} if out_prev is None else {5: 0},
        compiler_params=pltpu.CompilerParams(
            dimension_semantics=("arbitrary", "arbitrary"), vmem_limit_bytes=VMEM_LIMIT),
        name="final",
    )(x1, g1, g2, rw, gate2, *extra)


def _pick(n, pref):
    t = min(n, pref)
    assert n % t == 0, (n, t)
    return t


def _slots_kernel(offs_ref, r_ref, o_ref):
    r = r_ref[0]
    base = jnp.zeros_like(r)
    for e in range(N_EXPERTS):
        base = jnp.where(r == e, offs_ref[e], base)
    o_ref[0] = base + pltpu.roll(r, 6, 0)


def _slots(ridx, offs):
    B, _, S = ridx.shape
    return pl.pallas_call(
        _slots_kernel,
        out_shape=jax.ShapeDtypeStruct((B, 8, S), jnp.int32),
        grid_spec=pltpu.PrefetchScalarGridSpec(
            num_scalar_prefetch=1, grid=(B,),
            in_specs=[pl.BlockSpec((1, 8, S), lambda b, offs: (b, 0, 0))],
            out_specs=pl.BlockSpec((1, 8, S), lambda b, offs: (b, 0, 0))),
        compiler_params=pltpu.CompilerParams(dimension_semantics=("arbitrary",)),
        name="slots",
    )(offs, ridx)


def _route_plan(ridx, counts, tm_e, T):
    counts = counts.astype(jnp.int32)
    tiles = (counts + tm_e - 1) // tm_e
    tile_end = jnp.cumsum(tiles)
    offs = (tile_end - tiles) * tm_e
    slots = _slots(ridx, offs)
    n_chunks = T // (SC_WORKERS * SC_CHUNK)
    idx1 = slots[:, 0, :].reshape(SC_WORKERS, n_chunks, SC_CHUNK)
    idx2 = slots[:, 1, :].reshape(SC_WORKERS, n_chunks, SC_CHUNK)
    n_tiles_max = 2 * T // tm_e + N_EXPERTS
    tile_ids = jnp.arange(n_tiles_max, dtype=jnp.int32)
    tile_expert = jnp.sum((tile_end[None, :] <= tile_ids[:, None]).astype(jnp.int32), axis=1)
    tile_expert = jnp.minimum(tile_expert, N_EXPERTS - 1)
    return idx1, idx2, tile_expert, tile_end[-1:].astype(jnp.int32), n_tiles_max * tm_e


def _layer(x, c, w_ada, b_ada, norm1_w, w_in, b_forget, conv_w, q_norm_w, k_norm_w,
           w_out_conv, w_out_attn, w_o, norm2_w, w_rg, b_rg, w_re, b_re, w_gate, w_up, w_down):
    B, S, _ = x.shape
    n_grp = 2 if B % 2 == 0 and (B // 2 * S) % (SC_WORKERS * SC_CHUNK) == 0 else 1
    Bg = B // n_grp
    T = Bg * S
    assert T % (SC_WORKERS * SC_CHUNK) == 0, T
    mod = _ada(c, w_ada, b_ada.reshape(1, -1)).reshape(B, 6, 1, D_MODEL)
    shift1, scale1, gate1, shift2, scale2, gate2 = (mod[:, t] for t in range(6))

    cuts = np.cumsum([0, CONV_WIDTH, CONV_WIDTH, CONV_WIDTH, ATTN_WIDTH, ATTN_WIDTH, ATTN_WIDTH,
                      N_HEADS, D_MODEL, D_MODEL])
    w_conv3 = w_in[:, cuts[0]:cuts[3]]
    w_qvT = jnp.concatenate([w_in[:, cuts[3]:cuts[4]], w_in[:, cuts[5]:cuts[6]]], axis=1).T.astype(BF16)
    w_k = w_in[:, cuts[4]:cuts[5]].astype(BF16)
    w_f = jnp.pad(w_in[:, cuts[6]:cuts[7]], ((0, 0), (0, LANES - N_HEADS))).astype(BF16)
    b_f = jnp.pad(b_forget, (0, LANES - N_HEADS)).reshape(1, LANES)
    w_cgg = jnp.concatenate([w_conv3, w_in[:, cuts[7]:cuts[9]]], axis=1).astype(BF16)

    tm_qkv = _pick(S, 512)
    qnwT = jnp.broadcast_to((jnp.tile(q_norm_w, N_HEADS) * (LOG2E * HEAD_DIM ** -0.5))[:, None],
                            (ATTN_WIDTH, LANES))
    tq = _pick(S // 2, 512)
    w_r = jnp.pad(jnp.concatenate([w_re, w_rg], axis=1),
                  ((0, 0), (0, LANES - N_EXPERTS - N_GROUPS)))
    w_r_hi = w_r.astype(BF16)
    w_r_lo = (w_r - w_r_hi.astype(F32)).astype(BF16)
    b_r = jnp.pad(jnp.concatenate([b_re, b_rg]), (0, LANES - N_EXPERTS - N_GROUPS)).reshape(1, LANES)
    cw = jnp.pad(conv_w, ((0, 8 - CONV_K), (0, 0)))
    tm_post = _pick(S, 1024)
    tm_e = 512
    w_oc, w_oa, w_ob, w_rs = (w_out_conv.astype(BF16), w_out_attn.astype(BF16), w_o.astype(BF16),
                              jnp.concatenate([w_r_hi, w_r_lo], axis=1))

    def experts(grp, after):
        x1, rw, idx1, idx2, tile_expert, n_tiles, xs = grp
        ys = _moe(xs, tile_expert, n_tiles, w_gate, w_up, w_down, tm_e, after)
        return ys, _collect(ys, idx1, idx2)

    def finish(g, grp, gathered, out, after):
        g1, g2 = gathered
        return _final(grp[0], g1.reshape(Bg, S, ROW_WORDS), g2.reshape(Bg, S, ROW_WORDS), grp[1],
                      gate2, _pick(S, 512), g * Bg, out, after)

    out, prev, prev_gathered, ys_prev = None, None, None, None
    for g in range(n_grp):
        b0 = g * Bg
        qT_aug, k_aug, vT = _qkv(x, norm1_w.reshape(1, -1), shift1, scale1, w_qvT, w_k, w_f, b_f,
                                 qnwT, jnp.tile(k_norm_w, N_HEADS).reshape(1, -1), tm_qkv, b0, Bg)
        if prev is not None:
            ys_prev, prev_gathered = experts(prev, [qT_aug])
        y_b = _attention(qT_aug, k_aug, vT, tq, _pick(tq, 256),
                         [] if ys_prev is None else [ys_prev])
        x1, h2p, ridx, rw, counts = _post(x, y_b, norm1_w.reshape(1, -1), shift1, scale1, gate1,
                                          norm2_w.reshape(1, -1), shift2, scale2,
                                          w_cgg, cw, w_oc, w_oa, w_ob, w_rs, b_r, tm_post, b0)
        idx1, idx2, tile_expert, n_tiles, n_slots = _route_plan(ridx, counts[0, :N_EXPERTS], tm_e, T)
        xs = _dispatch(h2p.reshape(T, ROW_WORDS), idx1, idx2, n_slots)
        if prev is not None:
            out = finish(g - 1, prev, prev_gathered, out, [idx1])
        prev = (x1, rw, idx1, idx2, tile_expert, n_tiles, xs)
    _, last_gathered = experts(prev, [] if out is None else [out])
    return finish(n_grp - 1, prev, last_gathered, out, [])


def kernel(x, c, w_ada, b_ada, norm1_w, w_in, b_forget, conv_w, q_norm_w, k_norm_w, w_out_conv,
           w_out_attn, w_o, norm2_w, w_router_group, b_router_group, w_router_expert,
           b_router_expert, w_gate, w_up, w_down):
    for l in range(w_ada.shape[0]):
        x = _layer(x, c, w_ada[l], b_ada[l], norm1_w[l], w_in[l], b_forget[l], conv_w[l],
                   q_norm_w[l], k_norm_w[l], w_out_conv[l], w_out_attn[l], w_o[l], norm2_w[l],
                   w_router_group[l], b_router_group[l], w_router_expert[l], b_router_expert[l],
                   w_gate[l], w_up[l], w_down[l])
    return x
```

```python
import functools

import jax
import jax.numpy as jnp
import numpy as np
from jax import lax
from jax.experimental import pallas as pl
from jax.experimental.pallas import tpu as pltpu
from jax.experimental.pallas import tpu_sc as plsc

D_MODEL = 1024
CONV_WIDTH = 512
CONV_K = 3
N_HEADS = 8
HEAD_DIM = 64
ATTN_WIDTH = N_HEADS * HEAD_DIM
N_PAIRS = N_HEADS // 2
N_GROUPS = 4
EXPERTS_PER_GROUP = 8
N_EXPERTS = N_GROUPS * EXPERTS_PER_GROUP
D_EXPERT = 256
EPS = 1e-6
LANES = 128
AUG = 2 * LANES
BIAS_W = 6
VROWS = HEAD_DIM + 16
NEG = -1e30
LOG2E = 1.4426950408889634

F32 = jnp.float32
BF16 = jnp.bfloat16
VMEM_LIMIT = 56 * 1024 * 1024


def _sigmoid(z):
    return 1.0 / (1.0 + jnp.exp(-z))


def _split3(z):
    hi = z.astype(BF16)
    r = z - hi.astype(F32)
    mid = r.astype(BF16)
    lo = (r - mid.astype(F32)).astype(BF16)
    return hi, mid, lo


def _dot(a, b):
    return jnp.dot(a, b, preferred_element_type=F32)


def _modulated_norm(x, nw, shift, scale):
    ms = jnp.mean(x * x, axis=-1, keepdims=True)
    return (x * lax.rsqrt(ms + EPS) * nw) * (1.0 + scale) + shift


def _ada_kernel(c_ref, w_ref, b_ref, o_ref):
    c = c_ref[...]
    a = c * _sigmoid(c)
    o_ref[...] = jnp.dot(a, w_ref[...], precision=lax.Precision.HIGHEST,
                         preferred_element_type=F32) + b_ref[...]


def _ada(c, w_ada, b_ada):
    B = c.shape[0]
    n = w_ada.shape[1] // D_MODEL
    return pl.pallas_call(
        _ada_kernel,
        out_shape=jax.ShapeDtypeStruct((B, n * D_MODEL), F32),
        grid=(n,),
        in_specs=[pl.BlockSpec((B, D_MODEL), lambda j: (0, 0)),
                  pl.BlockSpec((D_MODEL, D_MODEL), lambda j: (0, j)),
                  pl.BlockSpec((1, D_MODEL), lambda j: (0, j))],
        out_specs=pl.BlockSpec((B, D_MODEL), lambda j: (0, j)),
        compiler_params=pltpu.CompilerParams(dimension_semantics=("arbitrary",)),
        name="ada",
    )(c, w_ada, b_ada)


QKV_ROWS = 256
_NT = (((1,), (1,)), ((), ()))


def _lane_tile(a, width):
    return jnp.concatenate([a] * (width // a.shape[1]), axis=1)


def _qkv_kernel(x_ref, nw_ref, sh_ref, sc_ref, wqv_ref, wk_ref, wf_ref, bf_ref, qnw_ref, knw_ref,
                gsum_ref, pq_ref, pk_ref, cq_ref, ck_ref,
                qT_ref, k_ref, vT_ref, carry_ref):
    tm = x_ref.shape[1]
    n_grp = max(tm // QKV_ROWS, 1)
    rows = tm // n_grp
    row = lax.broadcasted_iota(jnp.int32, (rows, LANES), 0)

    @pl.when(pl.program_id(1) == 0)
    def _():
        carry_ref[...] = jnp.zeros_like(carry_ref)

    st = [dict(rs=pl.ds(g * rows, rows)) for g in range(n_grp)]

    def project(d):
        h = _modulated_norm(x_ref[0, d["rs"], :], nw_ref[...], sh_ref[0], sc_ref[0])
        hb = h.astype(BF16)
        d["qvT"] = lax.dot_general(wqv_ref[...], hb, _NT, preferred_element_type=F32)
        d["k"] = _dot(hb, wk_ref[...])
        d["fl"] = _dot(hb, wf_ref[...]) + bf_ref[...]

    def norms(d):
        qvT = d["qvT"]
        heads = []
        for hd in range(N_HEADS):
            z = qvT[hd * HEAD_DIM:(hd + 1) * HEAD_DIM]
            heads.append(z * lax.rsqrt(jnp.mean(z * z, axis=0, keepdims=True) + EPS))
        qnT = jnp.concatenate(heads, axis=0) * _lane_tile(qnw_ref[...], rows)
        for j in range(N_PAIRS):
            qT_ref[0, j, :LANES, d["rs"]] = qnT[j * LANES:(j + 1) * LANES].astype(BF16)
            for t in range(2):
                r0 = ATTN_WIDTH + (2 * j + t) * HEAD_DIM
                vT_ref[0, j, t * VROWS:t * VROWS + HEAD_DIM, d["rs"]] = qvT[r0:r0 + HEAD_DIM].astype(BF16)
                vT_ref[0, j, t * VROWS + HEAD_DIM:(t + 1) * VROWS, d["rs"]] = \
                    jnp.ones((VROWS - HEAD_DIM, rows), BF16)
        del d["qvT"]
        k = d.pop("k")
        ss = _dot((k * k).astype(BF16), gsum_ref[...])
        kn = k * lax.rsqrt(ss * (1.0 / HEAD_DIM) + EPS) * knw_ref[...]
        for j in range(N_PAIRS):
            k_ref[0, d["rs"], j * AUG:j * AUG + LANES] = kn[:, j * LANES:(j + 1) * LANES].astype(BF16)

    def forget(d, before):
        fl = d.pop("fl")
        cum = jnp.minimum(fl, 0.0) - jnp.log(1.0 + jnp.exp(-jnp.abs(fl)))
        s = 1
        while s < rows:
            cum = cum + jnp.where(row >= s, pltpu.roll(cum, s, 0), 0.0)
            s *= 2
        cum = cum + before[7:8, :]
        d["tail"] = cum[rows - 8:, :]
        d["parts"] = jnp.concatenate(_split3(cum * LOG2E), axis=1)

    def bias(d):
        parts = d.pop("parts")
        eqT = (lax.dot_general(pq_ref[...], parts, _NT, preferred_element_type=F32)
               + _lane_tile(cq_ref[...], rows)).astype(BF16)
        ek = (_dot(parts, pk_ref[...]) + ck_ref[...]).astype(BF16)
        for j in range(N_PAIRS):
            qT_ref[0, j, LANES:, d["rs"]] = eqT
            k_ref[0, d["rs"], j * AUG + LANES:(j + 1) * AUG] = ek

    stages = [project, norms, None, bias]
    lag = 1
    for step in range(len(stages) + lag * (n_grp - 1)):
        for g, d in enumerate(st):
            kk = step - lag * g
            if 0 <= kk < len(stages):
                if stages[kk] is None:
                    forget(d, carry_ref[...] if g == 0 else st[g - 1]["tail"])
                else:
                    stages[kk](d)
    carry_ref[...] = st[-1]["tail"]


def _bias_placement():
    pq = np.zeros((LANES, 3 * LANES), np.float32)
    pk = np.zeros((3 * LANES, LANES), np.float32)
    cq = np.zeros((LANES, LANES), np.float32)
    ck = np.zeros((1, LANES), np.float32)
    for hd in range(N_HEADS):
        base = BIAS_W * hd
        for p in range(3):
            pq[base + p, p * LANES + hd] = 1.0
            pk[p * LANES + hd, base + 3 + p] = -1.0
            cq[base + 3 + p, :] = 1.0
            ck[0, base + p] = 1.0
    return (jnp.asarray(pq, BF16), jnp.asarray(pk, BF16), jnp.asarray(cq), jnp.asarray(ck))


def _qkv(x, nw, shift, scale, wqvT, wk, wf, bf, qnwT, knw, tm, b0, B):
    S = x.shape[1]
    gsum = jnp.asarray(np.kron(np.eye(N_HEADS), np.ones((HEAD_DIM, HEAD_DIM))), BF16)
    pq, pk, cq, ck = _bias_placement()
    const = lambda *shape: pl.BlockSpec(shape, lambda b, i: (0,) * len(shape),
                                        pipeline_mode=pl.Buffered(1))
    return pl.pallas_call(
        _qkv_kernel,
        out_shape=(jax.ShapeDtypeStruct((B, N_PAIRS, AUG, S), BF16),
                   jax.ShapeDtypeStruct((B, S, N_PAIRS * AUG), BF16),
                   jax.ShapeDtypeStruct((B, N_PAIRS, 2 * VROWS, S), BF16)),
        grid=(B, S // tm),
        in_specs=[pl.BlockSpec((1, tm, D_MODEL), lambda b, i: (b + b0, i, 0)),
                  const(1, D_MODEL),
                  pl.BlockSpec((1, 1, D_MODEL), lambda b, i: (b + b0, 0, 0)),
                  pl.BlockSpec((1, 1, D_MODEL), lambda b, i: (b + b0, 0, 0)),
                  const(2 * ATTN_WIDTH, D_MODEL),
                  const(D_MODEL, ATTN_WIDTH),
                  const(D_MODEL, LANES),
                  const(1, LANES),
                  const(ATTN_WIDTH, LANES),
                  const(1, ATTN_WIDTH),
                  const(ATTN_WIDTH, ATTN_WIDTH),
                  const(LANES, 3 * LANES),
                  const(3 * LANES, LANES),
                  const(LANES, LANES),
                  const(1, LANES)],
        out_specs=(pl.BlockSpec((1, N_PAIRS, AUG, tm), lambda b, i: (b, 0, 0, i)),
                   pl.BlockSpec((1, tm, N_PAIRS * AUG), lambda b, i: (b, i, 0)),
                   pl.BlockSpec((1, N_PAIRS, 2 * VROWS, tm), lambda b, i: (b, 0, 0, i))),
        scratch_shapes=[pltpu.VMEM((8, LANES), F32)],
        compiler_params=pltpu.CompilerParams(
            dimension_semantics=("arbitrary", "arbitrary"), vmem_limit_bytes=VMEM_LIMIT),
        name="qkv",
    )(x, nw, shift, scale, wqvT, wk, wf, bf, qnwT, knw, gsum, pq, pk, cq, ck)


def _attn_kernel(qT_ref, k_ref, vT_ref, *rest, tq, cw):
    o_ref, qq_ref, s_ref, smax_ref, m_ref, acc_ref = rest[-6:]
    i = pl.program_id(2)
    n = tq // cw
    chains = [(a, t, c) for a in range(2) for t in range(2) for c in range(n)]
    second = [ci for ci, (a, _, _) in enumerate(chains) if a == 1]
    feat = lax.broadcasted_iota(jnp.int32, (AUG, tq), 0)
    for t in range(2):
        bias0 = LANES + BIAS_W * (2 * pl.program_id(1) + t)
        keep = ((feat >= t * HEAD_DIM) & (feat < (t + 1) * HEAD_DIM)) | \
               ((feat >= bias0) & (feat < bias0 + BIAS_W))
        for a in range(2):
            qT = qT_ref[0, 0, :, a * tq:(a + 1) * tq]
            qh = jnp.where(keep, qT, jnp.zeros_like(qT))
            for c in range(n):
                qq_ref[chains.index((a, t, c))] = qh[:, c * cw:(c + 1) * cw]
    kpos = lax.broadcasted_iota(jnp.int32, (tq, cw), 0)
    qpos = lax.broadcasted_iota(jnp.int32, (tq, cw), 1)

    def scores(j, slot, which):
        k_blk = k_ref[0, pl.ds(pl.multiple_of(j * tq, tq), tq), :]
        for ci in which:
            s = _dot(k_blk, qq_ref[ci])
            s_ref[slot, ci] = s
            smax_ref[slot, ci] = jnp.broadcast_to(jnp.max(s, axis=0, keepdims=True), (8, cw))

    def absorb(j, slot, which, diagonal=()):
        start = pl.multiple_of(j * tq, tq)
        for ci in which:
            _, t, c = chains[ci]
            vj = vT_ref[0, 0, t * VROWS:(t + 1) * VROWS, pl.ds(start, tq)]
            s = s_ref[slot, ci]
            if ci in diagonal:
                s = jnp.where(kpos <= qpos + c * cw, s, NEG)
                smax = jnp.max(s, axis=0, keepdims=True)
            else:
                smax = smax_ref[slot, ci, 0:1]
            m = m_ref[ci, 0:1]
            m_new = jnp.maximum(m, smax)
            p = jnp.exp2(s - m_new).astype(BF16)
            acc_ref[ci] = jnp.exp2(m - m_new) * acc_ref[ci] + _dot(vj, p)
            m_ref[ci] = jnp.broadcast_to(m_new, (8, cw))

    every = list(range(len(chains)))
    first = [ci for ci in every if ci not in second]
    m_ref[...] = jnp.full(m_ref.shape, NEG, F32)
    acc_ref[...] = jnp.zeros(acc_ref.shape, F32)
    scores(0, 0, every)

    def two_blocks(jj, _):
        j = 2 * jj
        scores(j + 1, 1, every)
        absorb(j, 0, every)
        scores(j + 2, 0, every)
        absorb(j + 1, 1, every)
        return 0

    lax.fori_loop(0, i, two_blocks, 0)
    scores(2 * i + 1, 1, second)
    absorb(2 * i, 0, every, diagonal=first)
    absorb(2 * i + 1, 1, second, diagonal=second)

    for a in range(2):
        outs = [acc_ref[ci, :HEAD_DIM] / acc_ref[ci, HEAD_DIM:HEAD_DIM + 1]
                for ci in every if chains[ci][0] == a]
        oT = jnp.concatenate([jnp.concatenate(outs[:n], axis=1), jnp.concatenate(outs[n:], axis=1)],
                             axis=0)
        o_ref[0, a * tq:(a + 1) * tq, :] = oT.T.astype(BF16)


def _attention(qT_aug, k_aug, vT, tq, cw, after):
    B, S, _ = k_aug.shape
    n_chains = 4 * tq // cw
    return pl.pallas_call(
        functools.partial(_attn_kernel, tq=tq, cw=cw),
        scratch_shapes=[pltpu.VMEM((n_chains, AUG, cw), BF16),
                        pltpu.VMEM((2, n_chains, tq, cw), F32),
                        pltpu.VMEM((2, n_chains, 8, cw), F32),
                        pltpu.VMEM((n_chains, 8, cw), F32),
                        pltpu.VMEM((n_chains, VROWS, cw), F32)],
        out_shape=jax.ShapeDtypeStruct((B, S, ATTN_WIDTH), BF16),
        grid=(B, N_PAIRS, S // (2 * tq)),
        in_specs=[pl.BlockSpec((1, 1, AUG, 2 * tq), lambda b, j, i: (b, j, 0, i)),
                  pl.BlockSpec((1, S, AUG), lambda b, j, i: (b, 0, j)),
                  pl.BlockSpec((1, 1, 2 * VROWS, S), lambda b, j, i: (b, j, 0, 0))]
        + [pl.BlockSpec(memory_space=pl.ANY)] * len(after),
        out_specs=pl.BlockSpec((1, 2 * tq, LANES), lambda b, j, i: (b, i, j)),
        compiler_params=pltpu.CompilerParams(
            dimension_semantics=("arbitrary", "arbitrary", "arbitrary"),
            vmem_limit_bytes=VMEM_LIMIT),
        name="attn",
    )(qT_aug, k_aug, vT, *after)


def _pack_bf16_pairs(z):
    w = z.shape[1] // 2
    bits = pltpu.bitcast(z.astype(BF16).astype(F32), jnp.uint32)
    return bits[:, :w] | (bits[:, w:] >> 16)


def _unpack_bf16_pairs(p):
    return (pltpu.bitcast(p & jnp.uint32(0xFFFF0000), F32), pltpu.bitcast(p << 16, F32))


POST_ROWS = 256


def _post_kernel(x_ref, yb_ref, n1_ref, sh1_ref, sc1_ref, g1_ref, n2_ref, sh2_ref, sc2_ref,
                 wc_ref, cw_ref, woc_ref, woa_ref, wo_ref, wr_ref, br_ref, tri_ref,
                 x1_ref, h2_ref, ridx_ref, rw_ref, cnt_ref, carry_ref):
    tm = x_ref.shape[1]
    n_grp = max(tm // POST_ROWS, 1)
    rows = tm // n_grp
    lane = lax.broadcasted_iota(jnp.int32, (rows, LANES), 1)
    row8 = lax.broadcasted_iota(jnp.int32, (8, CONV_WIDTH), 0)
    big = jnp.int32(1 << 20)

    @pl.when(pl.program_id(1) == 0)
    def _():
        carry_ref[...] = jnp.zeros_like(carry_ref)

    @pl.when((pl.program_id(0) == 0) & (pl.program_id(1) == 0))
    def _():
        cnt_ref[...] = jnp.zeros_like(cnt_ref)

    st = [dict(rs=pl.ds(g * rows, rows)) for g in range(n_grp)]

    def conv_in(d):
        d["x"] = x_ref[0, d["rs"], :]
        d["hb"] = _modulated_norm(d["x"], n1_ref[...], sh1_ref[0], sc1_ref[0]).astype(BF16)
        d["x_in"] = _dot(d["hb"], wc_ref[:, :CONV_WIDTH])
        d["conv_c"] = _dot(d["hb"], wc_ref[:, 2 * CONV_WIDTH:3 * CONV_WIDTH])
        d["conv_b"] = _dot(d["hb"], wc_ref[:, CONV_WIDTH:2 * CONV_WIDTH])

    def conv(d, prev):
        u = d.pop("conv_c") * d.pop("x_in")
        d["u_tail"] = u[rows - 8:, :]

        def shifted(k):
            r = pltpu.roll(u, k, 0)
            top = jnp.where(row8 < k, pltpu.roll(prev, k, 0), r[:8])
            return jnp.concatenate([top, r[8:]], axis=0)

        cw = cw_ref[...]
        cv = cw[0:1] * shifted(2) + cw[1:2] * shifted(1) + cw[2:3] * u
        d["y_a"] = (d.pop("conv_b") * cv).astype(BF16)

    def gates(d):
        d["p_b"] = _dot(yb_ref[0, d["rs"], :], woa_ref[...])
        d["gate_c"] = _dot(d["hb"], wc_ref[:, 3 * CONV_WIDTH:3 * CONV_WIDTH + D_MODEL])
        d["gate_a"] = _dot(d.pop("hb"), wc_ref[:, 3 * CONV_WIDTH + D_MODEL:])

    def branch_a(d):
        d["p_a"] = _dot(d.pop("y_a"), woc_ref[...])

    def merge(d):
        d["merged"] = (_sigmoid(d.pop("gate_c")) * d.pop("p_a")
                       + _sigmoid(d.pop("gate_a")) * d.pop("p_b")).astype(BF16)

    def out_proj(d):
        d["o"] = _dot(d.pop("merged"), wo_ref[...])

    def residual(d):
        x1 = d.pop("x") + g1_ref[0] * d.pop("o")
        x1_ref[0, d["rs"], :] = x1
        h2 = _modulated_norm(x1, n2_ref[...], sh2_ref[0], sc2_ref[0])
        h2_ref[0, d["rs"], :] = _pack_bf16_pairs(h2)
        d["h_hi"] = h2.astype(BF16)
        d["h_lo"] = (h2 - d["h_hi"].astype(F32)).astype(BF16)

    def router(d):
        both = _dot(d.pop("h_hi"), wr_ref[...])
        d["lg"] = (both[:, :LANES] + both[:, LANES:] + _dot(d.pop("h_lo"), wr_ref[:, :LANES])) \
            + br_ref[...]

    def first_argmax(vals):
        mx = jnp.max(vals, axis=-1, keepdims=True)
        idx = jnp.min(jnp.where(vals == mx, lane, big), axis=-1, keepdims=True)
        return mx, idx

    def route(d):
        lg = d.pop("lg")
        is_g = (lane >= N_EXPERTS) & (lane < N_EXPERTS + N_GROUPS)
        g_mx, g_lane = first_argmax(jnp.where(is_g, lg, NEG))
        p_sel = 1.0 / jnp.sum(jnp.where(is_g, jnp.exp(lg - g_mx), 0.0), axis=-1, keepdims=True)
        g_idx = g_lane - N_EXPERTS
        in_g = (lane >= g_idx * EXPERTS_PER_GROUP) & (lane < (g_idx + 1) * EXPERTS_PER_GROUP)
        le = jnp.where(in_g, lg, NEG)
        v1, i1 = first_argmax(le)
        v2, i2 = first_argmax(jnp.where(lane == i1, NEG, le))
        e2 = jnp.exp(v2 - v1)
        w1 = p_sel / (1.0 + e2)
        w2 = w1 * e2
        rw_ref[0, d["rs"], :] = jnp.where(lane == 0, w1, 0.0) + jnp.where(lane == 1, w2, 0.0)
        d["i1"], d["i2"] = i1, i2
        d["onehot"] = jnp.where((lane == i1) | (lane == i2), 1.0, 0.0)

    def rank(d):
        onehot = d.pop("onehot")
        before = _dot(tri_ref[...], onehot.astype(BF16)) + cnt_ref[0:1, :]
        cnt_ref[...] = cnt_ref[...] + jnp.sum(onehot, axis=0, keepdims=True)
        i1, i2 = d.pop("i1"), d.pop("i2")
        r1 = jnp.sum(jnp.where(lane == i1, before, 0.0), axis=-1, keepdims=True)
        r2 = jnp.sum(jnp.where(lane == i2, before, 0.0), axis=-1, keepdims=True)
        rec = (jnp.where(lane == 0, i1.astype(F32), 0.0) + jnp.where(lane == 1, i2.astype(F32), 0.0)
               + jnp.where(lane == 2, r1, 0.0) + jnp.where(lane == 3, r2, 0.0))
        ridx_ref[0, :, d["rs"]] = rec.T[:8].astype(jnp.int32)

    stages = [conv_in, None, gates, branch_a, merge, out_proj, residual, router, route, rank]
    lag = 2
    for step in range(len(stages) + lag * (n_grp - 1)):
        for g, d in enumerate(st):
            k = step - lag * g
            if 0 <= k < len(stages):
                if stages[k] is None:
                    conv(d, carry_ref[...] if g == 0 else st[g - 1]["u_tail"])
                else:
                    stages[k](d)
    carry_ref[...] = st[-1]["u_tail"]


def _post(x, yb, n1, sh1, sc1, g1, n2, sh2, sc2, wc, cw, woc, woa, wo, wr, br, tm, b0):
    B, S, _ = yb.shape
    rows = tm // max(tm // POST_ROWS, 1)
    tri = jnp.asarray(np.tril(np.ones((rows, rows), np.float32), -1), BF16)
    const = lambda *shape: pl.BlockSpec(shape, lambda b, i: (0,) * len(shape),
                                        pipeline_mode=pl.Buffered(1))
    perb = pl.BlockSpec((1, 1, D_MODEL), lambda b, i: (b + b0, 0, 0))
    tok = lambda w: pl.BlockSpec((1, tm, w), lambda b, i: (b, i, 0))
    return pl.pallas_call(
        _post_kernel,
        out_shape=(jax.ShapeDtypeStruct((B, S, D_MODEL), F32),
                   jax.ShapeDtypeStruct((B, S, D_MODEL // 2), jnp.uint32),
                   jax.ShapeDtypeStruct((B, 8, S), jnp.int32),
                   jax.ShapeDtypeStruct((B, S, LANES), F32),
                   jax.ShapeDtypeStruct((8, LANES), F32)),
        grid=(B, S // tm),
        in_specs=[pl.BlockSpec((1, tm, D_MODEL), lambda b, i: (b + b0, i, 0)), tok(ATTN_WIDTH),
                  const(1, D_MODEL), perb, perb, perb,
                  const(1, D_MODEL), perb, perb,
                  const(D_MODEL, 3 * CONV_WIDTH + 2 * D_MODEL),
                  const(8, CONV_WIDTH),
                  const(CONV_WIDTH, D_MODEL), const(ATTN_WIDTH, D_MODEL),
                  const(D_MODEL, D_MODEL),
                  const(D_MODEL, 2 * LANES), const(1, LANES), const(rows, rows)],
        out_specs=(tok(D_MODEL), tok(D_MODEL // 2),
                   pl.BlockSpec((1, 8, tm), lambda b, i: (b, 0, i)), tok(LANES),
                   pl.BlockSpec((8, LANES), lambda b, i: (0, 0))),
        scratch_shapes=[pltpu.VMEM((8, CONV_WIDTH), F32)],
        compiler_params=pltpu.CompilerParams(
            dimension_semantics=("arbitrary", "arbitrary"), vmem_limit_bytes=VMEM_LIMIT),
        name="post",
    )(x, yb, n1, sh1, sc1, g1, n2, sh2, sc2, wc, cw, woc, woa, wo, wr, br, tri)


SC_CORES = 2
SC_SUBCORES = 16
SC_WORKERS = SC_CORES * SC_SUBCORES
SC_CHUNK = 64
ROW_WORDS = D_MODEL // 2


def _sc_mesh():
    return plsc.VectorSubcoreMesh(core_axis_name="c", subcore_axis_name="s",
                                  num_cores=SC_CORES, num_subcores=SC_SUBCORES)


def _dispatch_body(rows_hbm, idx1_hbm, idx2_hbm, xs_hbm, idx1_v, idx2_v, rows_v, *, n_chunks):
    wid = lax.axis_index("s") * SC_CORES + lax.axis_index("c")
    pltpu.sync_copy(idx1_hbm.at[wid], idx1_v)
    pltpu.sync_copy(idx2_hbm.at[wid], idx2_v)
    base = wid * (n_chunks * SC_CHUNK)

    @pl.loop(0, n_chunks)
    def _(j):
        pltpu.sync_copy(rows_hbm.at[pl.ds(base + j * SC_CHUNK, SC_CHUNK)], rows_v)
        pltpu.sync_copy(rows_v, xs_hbm.at[idx1_v.at[j]])
        pltpu.sync_copy(rows_v, xs_hbm.at[idx2_v.at[j]])


def _sc_scratch(n_chunks):
    return [pltpu.VMEM((n_chunks, SC_CHUNK), jnp.int32), pltpu.VMEM((n_chunks, SC_CHUNK), jnp.int32),
            pltpu.VMEM((SC_CHUNK, ROW_WORDS), jnp.uint32)]


def _dispatch(rows, idx1, idx2, n_slots):
    n_chunks = idx1.shape[1]
    return pl.kernel(
        functools.partial(_dispatch_body, n_chunks=n_chunks),
        out_type=jax.ShapeDtypeStruct((n_slots, ROW_WORDS), jnp.uint32),
        mesh=_sc_mesh(),
        scratch_types=_sc_scratch(n_chunks),
        name="dispatch",
    )(rows, idx1, idx2)


def _collect_body(ys_hbm, idx1_hbm, idx2_hbm, g1_hbm, g2_hbm, idx1_v, idx2_v, rows_v, *, n_chunks):
    wid = lax.axis_index("s") * SC_CORES + lax.axis_index("c")
    pltpu.sync_copy(idx1_hbm.at[wid], idx1_v)
    pltpu.sync_copy(idx2_hbm.at[wid], idx2_v)
    base = wid * (n_chunks * SC_CHUNK)

    @pl.loop(0, n_chunks)
    def _(j):
        dst = pl.ds(base + j * SC_CHUNK, SC_CHUNK)
        pltpu.sync_copy(ys_hbm.at[idx1_v.at[j]], rows_v)
        pltpu.sync_copy(rows_v, g1_hbm.at[dst])
        pltpu.sync_copy(ys_hbm.at[idx2_v.at[j]], rows_v)
        pltpu.sync_copy(rows_v, g2_hbm.at[dst])


def _collect(ys, idx1, idx2):
    n_chunks = idx1.shape[1]
    out = jax.ShapeDtypeStruct((SC_WORKERS * n_chunks * SC_CHUNK, ROW_WORDS), jnp.uint32)
    return pl.kernel(
        functools.partial(_collect_body, n_chunks=n_chunks),
        out_type=(out, out),
        mesh=_sc_mesh(),
        scratch_types=_sc_scratch(n_chunks),
        name="collect",
    )(ys, idx1, idx2)


def _moe_kernel(te_ref, nt_ref, xs_ref, wg_ref, wu_ref, wd_ref, *rest):
    ys_ref = rest[-1]

    @pl.when(pl.program_id(0) < nt_ref[0])
    def _():
        left, right = _unpack_bf16_pairs(xs_ref[...])
        xb = jnp.concatenate([left.astype(BF16), right.astype(BF16)], axis=1)
        g = _dot(xb, wg_ref[0].astype(BF16))
        u = _dot(xb, wu_ref[0].astype(BF16))
        a = (g * _sigmoid(g) * u).astype(BF16)
        ys_ref[...] = _pack_bf16_pairs(_dot(a, wd_ref[0].astype(BF16)))


def _moe(xs, tile_expert, n_tiles, wg, wu, wd, tm, after):
    n_slots = xs.shape[0]
    row_blk = lambda i, te, nt: (jnp.minimum(i, nt[0] - 1), 0)
    w_blk = lambda i, te, nt: (te[i], 0, 0)
    return pl.pallas_call(
        _moe_kernel,
        out_shape=jax.ShapeDtypeStruct((n_slots, ROW_WORDS), jnp.uint32),
        grid_spec=pltpu.PrefetchScalarGridSpec(
            num_scalar_prefetch=2,
            grid=(n_slots // tm,),
            in_specs=[pl.BlockSpec((tm, ROW_WORDS), row_blk),
                      pl.BlockSpec((1, D_MODEL, D_EXPERT), w_blk),
                      pl.BlockSpec((1, D_MODEL, D_EXPERT), w_blk),
                      pl.BlockSpec((1, D_EXPERT, D_MODEL), w_blk)]
            + [pl.BlockSpec(memory_space=pl.ANY)] * len(after),
            out_specs=pl.BlockSpec((tm, ROW_WORDS), row_blk)),
        compiler_params=pltpu.CompilerParams(
            dimension_semantics=("arbitrary",), vmem_limit_bytes=VMEM_LIMIT),
        name="moe",
    )(tile_expert, n_tiles, xs, wg, wu, wd, *after)


def _final_kernel(x1_ref, g1_ref, g2_ref, rw_ref, gate_ref, *rest):
    o_ref = rest[-1]
    rw = rw_ref[0]
    w1 = rw[:, 0:1]
    w2 = rw[:, 1:2]
    a_l, a_r = _unpack_bf16_pairs(g1_ref[0])
    b_l, b_r = _unpack_bf16_pairs(g2_ref[0])
    moe = jnp.concatenate([w1 * a_l + w2 * b_l, w1 * a_r + w2 * b_r], axis=1)
    o_ref[0] = x1_ref[0] + gate_ref[0] * moe


def _final(x1, g1, g2, rw, gate2, tm, b0, out_prev, after):
    Bg, S, _ = x1.shape
    tok = lambda w: pl.BlockSpec((1, tm, w), lambda b, i: (b, i, 0))
    extra = (() if out_prev is None else (out_prev,)) + tuple(after)
    return pl.pallas_call(
        _final_kernel,
        out_shape=jax.ShapeDtypeStruct((gate2.shape[0], S, D_MODEL), F32),
        grid=(Bg, S // tm),
        in_specs=[tok(D_MODEL), tok(ROW_WORDS), tok(ROW_WORDS), tok(LANES),
                  pl.BlockSpec((1, 1, D_MODEL), lambda b, i: (b + b0, 0, 0))]
        + [pl.BlockSpec(memory_space=pl.ANY)] * len(extra),
        out_specs=pl.BlockSpec((1, tm, D_MODEL), lambda b, i: (b + b0, i, 0)),
        input_output_aliases={} if out_prev is None else {5: 0},
        compiler_params=pltpu.CompilerParams(
            dimension_semantics=("arbitrary", "arbitrary"), vmem_limit_bytes=VMEM_LIMIT),
        name="final",
    )(x1, g1, g2, rw, gate2, *extra)


def _pick(n, pref):
    t = min(n, pref)
    assert n % t == 0, (n, t)
    return t


def _slots_kernel(offs_ref, r_ref, o_ref):
    r = r_ref[0]
    base = jnp.zeros_like(r)
    for e in range(N_EXPERTS):
        base = jnp.where(r == e, offs_ref[e], base)
    o_ref[0] = base + pltpu.roll(r, 6, 0)


def _slots(ridx, offs):
    B, _, S = ridx.shape
    return pl.pallas_call(
        _slots_kernel,
        out_shape=jax.ShapeDtypeStruct((B, 8, S), jnp.int32),
        grid_spec=pltpu.PrefetchScalarGridSpec(
            num_scalar_prefetch=1, grid=(B,),
            in_specs=[pl.BlockSpec((1, 8, S), lambda b, offs: (b, 0, 0))],
            out_specs=pl.BlockSpec((1, 8, S), lambda b, offs: (b, 0, 0))),
        compiler_params=pltpu.CompilerParams(dimension_semantics=("arbitrary",)),
        name="slots",
    )(offs, ridx)


def _route_plan(ridx, counts, tm_e, T):
    counts = counts.astype(jnp.int32)
    tiles = (counts + tm_e - 1) // tm_e
    tile_end = jnp.cumsum(tiles)
    offs = (tile_end - tiles) * tm_e
    slots = _slots(ridx, offs)
    n_chunks = T // (SC_WORKERS * SC_CHUNK)
    idx1 = slots[:, 0, :].reshape(SC_WORKERS, n_chunks, SC_CHUNK)
    idx2 = slots[:, 1, :].reshape(SC_WORKERS, n_chunks, SC_CHUNK)
    n_tiles_max = 2 * T // tm_e + N_EXPERTS
    tile_ids = jnp.arange(n_tiles_max, dtype=jnp.int32)
    tile_expert = jnp.sum((tile_end[None, :] <= tile_ids[:, None]).astype(jnp.int32), axis=1)
    tile_expert = jnp.minimum(tile_expert, N_EXPERTS - 1)
    return idx1, idx2, tile_expert, tile_end[-1:].astype(jnp.int32), n_tiles_max * tm_e


def _layer(x, c, w_ada, b_ada, norm1_w, w_in, b_forget, conv_w, q_norm_w, k_norm_w,
           w_out_conv, w_out_attn, w_o, norm2_w, w_rg, b_rg, w_re, b_re, w_gate, w_up, w_down):
    B, S, _ = x.shape
    n_grp = 2 if B % 2 == 0 and (B // 2 * S) % (SC_WORKERS * SC_CHUNK) == 0 else 1
    Bg = B // n_grp
    T = Bg * S
    assert T % (SC_WORKERS * SC_CHUNK) == 0, T
    mod = _ada(c, w_ada, b_ada.reshape(1, -1)).reshape(B, 6, 1, D_MODEL)
    shift1, scale1, gate1, shift2, scale2, gate2 = (mod[:, t] for t in range(6))

    cuts = np.cumsum([0, CONV_WIDTH, CONV_WIDTH, CONV_WIDTH, ATTN_WIDTH, ATTN_WIDTH, ATTN_WIDTH,
                      N_HEADS, D_MODEL, D_MODEL])
    w_conv3 = w_in[:, cuts[0]:cuts[3]]
    w_qvT = jnp.concatenate([w_in[:, cuts[3]:cuts[4]], w_in[:, cuts[5]:cuts[6]]], axis=1).T.astype(BF16)
    w_k = w_in[:, cuts[4]:cuts[5]].astype(BF16)
    w_f = jnp.pad(w_in[:, cuts[6]:cuts[7]], ((0, 0), (0, LANES - N_HEADS))).astype(BF16)
    b_f = jnp.pad(b_forget, (0, LANES - N_HEADS)).reshape(1, LANES)
    w_cgg = jnp.concatenate([w_conv3, w_in[:, cuts[7]:cuts[9]]], axis=1).astype(BF16)

    tm_qkv = _pick(S, 1024)
    qnwT = jnp.broadcast_to((jnp.tile(q_norm_w, N_HEADS) * (LOG2E * HEAD_DIM ** -0.5))[:, None],
                            (ATTN_WIDTH, LANES))
    tq = _pick(S // 2, 512)
    w_r = jnp.pad(jnp.concatenate([w_re, w_rg], axis=1),
                  ((0, 0), (0, LANES - N_EXPERTS - N_GROUPS)))
    w_r_hi = w_r.astype(BF16)
    w_r_lo = (w_r - w_r_hi.astype(F32)).astype(BF16)
    b_r = jnp.pad(jnp.concatenate([b_re, b_rg]), (0, LANES - N_EXPERTS - N_GROUPS)).reshape(1, LANES)
    cw = jnp.pad(conv_w, ((0, 8 - CONV_K), (0, 0)))
    tm_post = _pick(S, 1024)
    tm_e = 512
    w_oc, w_oa, w_ob, w_rs = (w_out_conv.astype(BF16), w_out_attn.astype(BF16), w_o.astype(BF16),
                              jnp.concatenate([w_r_hi, w_r_lo], axis=1))

    def experts(grp, after):
        x1, rw, idx1, idx2, tile_expert, n_tiles, xs = grp
        ys = _moe(xs, tile_expert, n_tiles, w_gate, w_up, w_down, tm_e, after)
        return ys, _collect(ys, idx1, idx2)

    def finish(g, grp, gathered, out, after):
        g1, g2 = gathered
        return _final(grp[0], g1.reshape(Bg, S, ROW_WORDS), g2.reshape(Bg, S, ROW_WORDS), grp[1],
                      gate2, _pick(S, 512), g * Bg, out, after)

    out, prev, prev_gathered, ys_prev = None, None, None, None
    for g in range(n_grp):
        b0 = g * Bg
        qT_aug, k_aug, vT = _qkv(x, norm1_w.reshape(1, -1), shift1, scale1, w_qvT, w_k, w_f, b_f,
                                 qnwT, jnp.tile(k_norm_w, N_HEADS).reshape(1, -1), tm_qkv, b0, Bg)
        if prev is not None:
            ys_prev, prev_gathered = experts(prev, [qT_aug])
        y_b = _attention(qT_aug, k_aug, vT, tq, _pick(tq, 256),
                         [] if ys_prev is None else [ys_prev])
        x1, h2p, ridx, rw, counts = _post(x, y_b, norm1_w.reshape(1, -1), shift1, scale1, gate1,
                                          norm2_w.reshape(1, -1), shift2, scale2,
                                          w_cgg, cw, w_oc, w_oa, w_ob, w_rs, b_r, tm_post, b0)
        idx1, idx2, tile_expert, n_tiles, n_slots = _route_plan(ridx, counts[0, :N_EXPERTS], tm_e, T)
        xs = _dispatch(h2p.reshape(T, ROW_WORDS), idx1, idx2, n_slots)
        if prev is not None:
            out = finish(g - 1, prev, prev_gathered, out, [idx1])
        prev = (x1, rw, idx1, idx2, tile_expert, n_tiles, xs)
    _, last_gathered = experts(prev, [] if out is None else [out])
    return finish(n_grp - 1, prev, last_gathered, out, [])


def kernel(x, c, w_ada, b_ada, norm1_w, w_in, b_forget, conv_w, q_norm_w, k_norm_w, w_out_conv,
           w_out_attn, w_o, norm2_w, w_router_group, b_router_group, w_router_expert,
           b_router_expert, w_gate, w_up, w_down):
    for l in range(w_ada.shape[0]):
        x = _layer(x, c, w_ada[l], b_ada[l], norm1_w[l], w_in[l], b_forget[l], conv_w[l],
                   q_norm_w[l], k_norm_w[l], w_out_conv[l], w_out_attn[l], w_o[l], norm2_w[l],
                   w_router_group[l], b_router_group[l], w_router_expert[l], b_router_expert[l],
                   w_gate[l], w_up[l], w_down[l])
    return x
```

```python
import functools

import jax
import jax.numpy as jnp
import numpy as np
from jax import lax
from jax.experimental import pallas as pl
from jax.experimental.pallas import tpu as pltpu
from jax.experimental.pallas import tpu_sc as plsc

D_MODEL = 1024
CONV_WIDTH = 512
CONV_K = 3
N_HEADS = 8
HEAD_DIM = 64
ATTN_WIDTH = N_HEADS * HEAD_DIM
N_PAIRS = N_HEADS // 2
N_GROUPS = 4
EXPERTS_PER_GROUP = 8
N_EXPERTS = N_GROUPS * EXPERTS_PER_GROUP
D_EXPERT = 256
EPS = 1e-6
LANES = 128
AUG = 2 * LANES
BIAS_W = 6
VROWS = HEAD_DIM + 16
NEG = -1e30
LOG2E = 1.4426950408889634

F32 = jnp.float32
BF16 = jnp.bfloat16
VMEM_LIMIT = 56 * 1024 * 1024


def _sigmoid(z):
    return 1.0 / (1.0 + jnp.exp(-z))


def _split3(z):
    hi = z.astype(BF16)
    r = z - hi.astype(F32)
    mid = r.astype(BF16)
    lo = (r - mid.astype(F32)).astype(BF16)
    return hi, mid, lo


def _dot(a, b):
    return jnp.dot(a, b, preferred_element_type=F32)


def _modulated_norm(x, nw, shift, scale):
    ms = jnp.mean(x * x, axis=-1, keepdims=True)
    return (x * lax.rsqrt(ms + EPS) * nw) * (1.0 + scale) + shift


def _ada_kernel(c_ref, w_ref, b_ref, o_ref):
    c = c_ref[...]
    a = c * _sigmoid(c)
    o_ref[...] = jnp.dot(a, w_ref[...], precision=lax.Precision.HIGHEST,
                         preferred_element_type=F32) + b_ref[...]


def _ada(c, w_ada, b_ada):
    B = c.shape[0]
    n = w_ada.shape[1] // D_MODEL
    return pl.pallas_call(
        _ada_kernel,
        out_shape=jax.ShapeDtypeStruct((B, n * D_MODEL), F32),
        grid=(n,),
        in_specs=[pl.BlockSpec((B, D_MODEL), lambda j: (0, 0)),
                  pl.BlockSpec((D_MODEL, D_MODEL), lambda j: (0, j)),
                  pl.BlockSpec((1, D_MODEL), lambda j: (0, j))],
        out_specs=pl.BlockSpec((B, D_MODEL), lambda j: (0, j)),
        compiler_params=pltpu.CompilerParams(dimension_semantics=("arbitrary",)),
        name="ada",
    )(c, w_ada, b_ada)


QKV_ROWS = 256
_NT = (((1,), (1,)), ((), ()))


def _lane_tile(a, width):
    return jnp.concatenate([a] * (width // a.shape[1]), axis=1)


def _qkv_kernel(x_ref, nw_ref, sh_ref, sc_ref, wqv_ref, wk_ref, wf_ref, bf_ref, qnw_ref, knw_ref,
                gsum_ref, pq_ref, pk_ref, cq_ref, ck_ref,
                qT_ref, k_ref, vT_ref, carry_ref):
    tm = x_ref.shape[1]
    n_grp = max(tm // QKV_ROWS, 1)
    rows = tm // n_grp
    row = lax.broadcasted_iota(jnp.int32, (rows, LANES), 0)

    @pl.when(pl.program_id(1) == 0)
    def _():
        carry_ref[...] = jnp.zeros_like(carry_ref)

    st = [dict(rs=pl.ds(g * rows, rows)) for g in range(n_grp)]

    def project(d):
        h = _modulated_norm(x_ref[0, d["rs"], :], nw_ref[...], sh_ref[0], sc_ref[0])
        hb = h.astype(BF16)
        d["qvT"] = lax.dot_general(wqv_ref[...], hb, _NT, preferred_element_type=F32)
        d["k"] = _dot(hb, wk_ref[...])
        d["fl"] = _dot(hb, wf_ref[...]) + bf_ref[...]

    def norms(d):
        qvT = d["qvT"]
        heads = []
        for hd in range(N_HEADS):
            z = qvT[hd * HEAD_DIM:(hd + 1) * HEAD_DIM]
            heads.append(z * lax.rsqrt(jnp.mean(z * z, axis=0, keepdims=True) + EPS))
        qnT = jnp.concatenate(heads, axis=0) * _lane_tile(qnw_ref[...], rows)
        for j in range(N_PAIRS):
            qT_ref[0, j, :LANES, d["rs"]] = qnT[j * LANES:(j + 1) * LANES].astype(BF16)
            for t in range(2):
                r0 = ATTN_WIDTH + (2 * j + t) * HEAD_DIM
                vT_ref[0, j, t * VROWS:t * VROWS + HEAD_DIM, d["rs"]] = qvT[r0:r0 + HEAD_DIM].astype(BF16)
                vT_ref[0, j, t * VROWS + HEAD_DIM:(t + 1) * VROWS, d["rs"]] = \
                    jnp.ones((VROWS - HEAD_DIM, rows), BF16)
        del d["qvT"]
        k = d.pop("k")
        ss = _dot((k * k).astype(BF16), gsum_ref[...])
        kn = k * lax.rsqrt(ss * (1.0 / HEAD_DIM) + EPS) * knw_ref[...]
        for j in range(N_PAIRS):
            k_ref[0, d["rs"], j * AUG:j * AUG + LANES] = kn[:, j * LANES:(j + 1) * LANES].astype(BF16)

    def forget(d, before):
        fl = d.pop("fl")
        cum = jnp.minimum(fl, 0.0) - jnp.log(1.0 + jnp.exp(-jnp.abs(fl)))
        s = 1
        while s < rows:
            cum = cum + jnp.where(row >= s, pltpu.roll(cum, s, 0), 0.0)
            s *= 2
        cum = cum + before[7:8, :]
        d["tail"] = cum[rows - 8:, :]
        d["parts"] = jnp.concatenate(_split3(cum * LOG2E), axis=1)

    def bias(d):
        parts = d.pop("parts")
        eqT = (lax.dot_general(pq_ref[...], parts, _NT, preferred_element_type=F32)
               + _lane_tile(cq_ref[...], rows)).astype(BF16)
        ek = (_dot(parts, pk_ref[...]) + ck_ref[...]).astype(BF16)
        for j in range(N_PAIRS):
            qT_ref[0, j, LANES:, d["rs"]] = eqT
            k_ref[0, d["rs"], j * AUG + LANES:(j + 1) * AUG] = ek

    stages = [project, norms, None, bias]
    lag = 1
    for step in range(len(stages) + lag * (n_grp - 1)):
        for g, d in enumerate(st):
            kk = step - lag * g
            if 0 <= kk < len(stages):
                if stages[kk] is None:
                    forget(d, carry_ref[...] if g == 0 else st[g - 1]["tail"])
                else:
                    stages[kk](d)
    carry_ref[...] = st[-1]["tail"]


def _bias_placement():
    pq = np.zeros((LANES, 3 * LANES), np.float32)
    pk = np.zeros((3 * LANES, LANES), np.float32)
    cq = np.zeros((LANES, LANES), np.float32)
    ck = np.zeros((1, LANES), np.float32)
    for hd in range(N_HEADS):
        base = BIAS_W * hd
        for p in range(3):
            pq[base + p, p * LANES + hd] = 1.0
            pk[p * LANES + hd, base + 3 + p] = -1.0
            cq[base + 3 + p, :] = 1.0
            ck[0, base + p] = 1.0
    return (jnp.asarray(pq, BF16), jnp.asarray(pk, BF16), jnp.asarray(cq), jnp.asarray(ck))


def _qkv(x, nw, shift, scale, wqvT, wk, wf, bf, qnwT, knw, tm, b0, B):
    S = x.shape[1]
    gsum = jnp.asarray(np.kron(np.eye(N_HEADS), np.ones((HEAD_DIM, HEAD_DIM))), BF16)
    pq, pk, cq, ck = _bias_placement()
    const = lambda *shape: pl.BlockSpec(shape, lambda b, i: (0,) * len(shape),
                                        pipeline_mode=pl.Buffered(1))
    return pl.pallas_call(
        _qkv_kernel,
        out_shape=(jax.ShapeDtypeStruct((B, N_PAIRS, AUG, S), BF16),
                   jax.ShapeDtypeStruct((B, S, N_PAIRS * AUG), BF16),
                   jax.ShapeDtypeStruct((B, N_PAIRS, 2 * VROWS, S), BF16)),
        grid=(B, S // tm),
        in_specs=[pl.BlockSpec((1, tm, D_MODEL), lambda b, i: (b + b0, i, 0)),
                  const(1, D_MODEL),
                  pl.BlockSpec((1, 1, D_MODEL), lambda b, i: (b + b0, 0, 0)),
                  pl.BlockSpec((1, 1, D_MODEL), lambda b, i: (b + b0, 0, 0)),
                  const(2 * ATTN_WIDTH, D_MODEL),
                  const(D_MODEL, ATTN_WIDTH),
                  const(D_MODEL, LANES),
                  const(1, LANES),
                  const(ATTN_WIDTH, LANES),
                  const(1, ATTN_WIDTH),
                  const(ATTN_WIDTH, ATTN_WIDTH),
                  const(LANES, 3 * LANES),
                  const(3 * LANES, LANES),
                  const(LANES, LANES),
                  const(1, LANES)],
        out_specs=(pl.BlockSpec((1, N_PAIRS, AUG, tm), lambda b, i: (b, 0, 0, i)),
                   pl.BlockSpec((1, tm, N_PAIRS * AUG), lambda b, i: (b, i, 0)),
                   pl.BlockSpec((1, N_PAIRS, 2 * VROWS, tm), lambda b, i: (b, 0, 0, i))),
        scratch_shapes=[pltpu.VMEM((8, LANES), F32)],
        compiler_params=pltpu.CompilerParams(
            dimension_semantics=("arbitrary", "arbitrary"), vmem_limit_bytes=VMEM_LIMIT),
        name="qkv",
    )(x, nw, shift, scale, wqvT, wk, wf, bf, qnwT, knw, gsum, pq, pk, cq, ck)


def _attn_kernel(qT_ref, k_ref, vT_ref, *rest, tq, cw):
    o_ref, qq_ref, s_ref, smax_ref, m_ref, acc_ref = rest[-6:]
    i = pl.program_id(2)
    n = tq // cw
    chains = [(a, t, c) for a in range(2) for t in range(2) for c in range(n)]
    second = [ci for ci, (a, _, _) in enumerate(chains) if a == 1]
    feat = lax.broadcasted_iota(jnp.int32, (AUG, tq), 0)
    for t in range(2):
        bias0 = LANES + BIAS_W * (2 * pl.program_id(1) + t)
        keep = ((feat >= t * HEAD_DIM) & (feat < (t + 1) * HEAD_DIM)) | \
               ((feat >= bias0) & (feat < bias0 + BIAS_W))
        for a in range(2):
            qT = qT_ref[0, 0, :, a * tq:(a + 1) * tq]
            qh = jnp.where(keep, qT, jnp.zeros_like(qT))
            for c in range(n):
                qq_ref[chains.index((a, t, c))] = qh[:, c * cw:(c + 1) * cw]
    kpos = lax.broadcasted_iota(jnp.int32, (tq, cw), 0)
    qpos = lax.broadcasted_iota(jnp.int32, (tq, cw), 1)

    def scores(j, slot, which):
        k_blk = k_ref[0, pl.ds(pl.multiple_of(j * tq, tq), tq), :]
        for ci in which:
            s = _dot(k_blk, qq_ref[ci])
            s_ref[slot, ci] = s
            smax_ref[slot, ci] = jnp.broadcast_to(jnp.max(s, axis=0, keepdims=True), (8, cw))

    def absorb(j, slot, which, diagonal=()):
        start = pl.multiple_of(j * tq, tq)
        for ci in which:
            _, t, c = chains[ci]
            vj = vT_ref[0, 0, t * VROWS:(t + 1) * VROWS, pl.ds(start, tq)]
            s = s_ref[slot, ci]
            if ci in diagonal:
                s = jnp.where(kpos <= qpos + c * cw, s, NEG)
                smax = jnp.max(s, axis=0, keepdims=True)
            else:
                smax = smax_ref[slot, ci, 0:1]
            m = m_ref[ci, 0:1]
            m_new = jnp.maximum(m, smax)
            p = jnp.exp2(s - m_new).astype(BF16)
            acc_ref[ci] = jnp.exp2(m - m_new) * acc_ref[ci] + _dot(vj, p)
            m_ref[ci] = jnp.broadcast_to(m_new, (8, cw))

    every = list(range(len(chains)))
    first = [ci for ci in every if ci not in second]
    m_ref[...] = jnp.full(m_ref.shape, NEG, F32)
    acc_ref[...] = jnp.zeros(acc_ref.shape, F32)
    scores(0, 0, every)

    def two_blocks(jj, _):
        j = 2 * jj
        scores(j + 1, 1, every)
        absorb(j, 0, every)
        scores(j + 2, 0, every)
        absorb(j + 1, 1, every)
        return 0

    lax.fori_loop(0, i, two_blocks, 0)
    scores(2 * i + 1, 1, second)
    absorb(2 * i, 0, every, diagonal=first)
    absorb(2 * i + 1, 1, second, diagonal=second)

    for a in range(2):
        outs = [acc_ref[ci, :HEAD_DIM] / acc_ref[ci, HEAD_DIM:HEAD_DIM + 1]
                for ci in every if chains[ci][0] == a]
        oT = jnp.concatenate([jnp.concatenate(outs[:n], axis=1), jnp.concatenate(outs[n:], axis=1)],
                             axis=0)
        o_ref[0, a * tq:(a + 1) * tq, :] = oT.T.astype(BF16)


def _attention(qT_aug, k_aug, vT, tq, cw, after):
    B, S, _ = k_aug.shape
    n_chains = 4 * tq // cw
    return pl.pallas_call(
        functools.partial(_attn_kernel, tq=tq, cw=cw),
        scratch_shapes=[pltpu.VMEM((n_chains, AUG, cw), BF16),
                        pltpu.VMEM((2, n_chains, tq, cw), F32),
                        pltpu.VMEM((2, n_chains, 8, cw), F32),
                        pltpu.VMEM((n_chains, 8, cw), F32),
                        pltpu.VMEM((n_chains, VROWS, cw), F32)],
        out_shape=jax.ShapeDtypeStruct((B, S, ATTN_WIDTH), BF16),
        grid=(B, N_PAIRS, S // (2 * tq)),
        in_specs=[pl.BlockSpec((1, 1, AUG, 2 * tq), lambda b, j, i: (b, j, 0, i)),
                  pl.BlockSpec((1, S, AUG), lambda b, j, i: (b, 0, j)),
                  pl.BlockSpec((1, 1, 2 * VROWS, S), lambda b, j, i: (b, j, 0, 0))]
        + [pl.BlockSpec(memory_space=pl.ANY)] * len(after),
        out_specs=pl.BlockSpec((1, 2 * tq, LANES), lambda b, j, i: (b, i, j)),
        compiler_params=pltpu.CompilerParams(
            dimension_semantics=("arbitrary", "arbitrary", "arbitrary"),
            vmem_limit_bytes=VMEM_LIMIT),
        name="attn",
    )(qT_aug, k_aug, vT, *after)


def _pack_bf16_pairs(z):
    w = z.shape[1] // 2
    bits = pltpu.bitcast(z.astype(BF16).astype(F32), jnp.uint32)
    return bits[:, :w] | (bits[:, w:] >> 16)


def _unpack_bf16_pairs(p):
    return (pltpu.bitcast(p & jnp.uint32(0xFFFF0000), F32), pltpu.bitcast(p << 16, F32))


POST_ROWS = 256


def _post_kernel(x_ref, yb_ref, n1_ref, sh1_ref, sc1_ref, g1_ref, n2_ref, sh2_ref, sc2_ref,
                 wc_ref, cw_ref, woc_ref, woa_ref, wo_ref, wr_ref, br_ref, tri_ref,
                 x1_ref, h2_ref, ridx_ref, rw_ref, cnt_ref, carry_ref):
    tm = x_ref.shape[1]
    n_grp = max(tm // POST_ROWS, 1)
    rows = tm // n_grp
    lane = lax.broadcasted_iota(jnp.int32, (rows, LANES), 1)
    row8 = lax.broadcasted_iota(jnp.int32, (8, CONV_WIDTH), 0)
    big = jnp.int32(1 << 20)

    @pl.when(pl.program_id(1) == 0)
    def _():
        carry_ref[...] = jnp.zeros_like(carry_ref)

    @pl.when((pl.program_id(0) == 0) & (pl.program_id(1) == 0))
    def _():
        cnt_ref[...] = jnp.zeros_like(cnt_ref)

    st = [dict(rs=pl.ds(g * rows, rows)) for g in range(n_grp)]

    def conv_in(d):
        d["x"] = x_ref[0, d["rs"], :]
        d["hb"] = _modulated_norm(d["x"], n1_ref[...], sh1_ref[0], sc1_ref[0]).astype(BF16)
        d["x_in"] = _dot(d["hb"], wc_ref[:, :CONV_WIDTH])
        d["conv_c"] = _dot(d["hb"], wc_ref[:, 2 * CONV_WIDTH:3 * CONV_WIDTH])
        d["conv_b"] = _dot(d["hb"], wc_ref[:, CONV_WIDTH:2 * CONV_WIDTH])

    def conv(d, prev):
        u = d.pop("conv_c") * d.pop("x_in")
        d["u_tail"] = u[rows - 8:, :]

        def shifted(k):
            r = pltpu.roll(u, k, 0)
            top = jnp.where(row8 < k, pltpu.roll(prev, k, 0), r[:8])
            return jnp.concatenate([top, r[8:]], axis=0)

        cw = cw_ref[...]
        cv = cw[0:1] * shifted(2) + cw[1:2] * shifted(1) + cw[2:3] * u
        d["y_a"] = (d.pop("conv_b") * cv).astype(BF16)

    def gates(d):
        d["p_b"] = _dot(yb_ref[0, d["rs"], :], woa_ref[...])
        d["gate_c"] = _dot(d["hb"], wc_ref[:, 3 * CONV_WIDTH:3 * CONV_WIDTH + D_MODEL])
        d["gate_a"] = _dot(d.pop("hb"), wc_ref[:, 3 * CONV_WIDTH + D_MODEL:])

    def branch_a(d):
        d["p_a"] = _dot(d.pop("y_a"), woc_ref[...])

    def merge(d):
        d["merged"] = (_sigmoid(d.pop("gate_c")) * d.pop("p_a")
                       + _sigmoid(d.pop("gate_a")) * d.pop("p_b")).astype(BF16)

    def out_proj(d):
        d["o"] = _dot(d.pop("merged"), wo_ref[...])

    def residual(d):
        x1 = d.pop("x") + g1_ref[0] * d.pop("o")
        x1_ref[0, d["rs"], :] = x1
        h2 = _modulated_norm(x1, n2_ref[...], sh2_ref[0], sc2_ref[0])
        h2_ref[0, d["rs"], :] = _pack_bf16_pairs(h2)
        d["h_hi"] = h2.astype(BF16)
        d["h_lo"] = (h2 - d["h_hi"].astype(F32)).astype(BF16)

    def router(d):
        both = _dot(d.pop("h_hi"), wr_ref[...])
        d["lg"] = (both[:, :LANES] + both[:, LANES:] + _dot(d.pop("h_lo"), wr_ref[:, :LANES])) \
            + br_ref[...]

    def first_argmax(vals):
        mx = jnp.max(vals, axis=-1, keepdims=True)
        idx = jnp.min(jnp.where(vals == mx, lane, big), axis=-1, keepdims=True)
        return mx, idx

    def route(d):
        lg = d.pop("lg")
        is_g = (lane >= N_EXPERTS) & (lane < N_EXPERTS + N_GROUPS)
        g_mx, g_lane = first_argmax(jnp.where(is_g, lg, NEG))
        p_sel = 1.0 / jnp.sum(jnp.where(is_g, jnp.exp(lg - g_mx), 0.0), axis=-1, keepdims=True)
        g_idx = g_lane - N_EXPERTS
        in_g = (lane >= g_idx * EXPERTS_PER_GROUP) & (lane < (g_idx + 1) * EXPERTS_PER_GROUP)
        le = jnp.where(in_g, lg, NEG)
        v1, i1 = first_argmax(le)
        v2, i2 = first_argmax(jnp.where(lane == i1, NEG, le))
        e2 = jnp.exp(v2 - v1)
        w1 = p_sel / (1.0 + e2)
        w2 = w1 * e2
        rw_ref[0, d["rs"], :] = jnp.where(lane == 0, w1, 0.0) + jnp.where(lane == 1, w2, 0.0)
        d["i1"], d["i2"] = i1, i2
        d["onehot"] = jnp.where((lane == i1) | (lane == i2), 1.0, 0.0)

    def rank(d):
        onehot = d.pop("onehot")
        before = _dot(tri_ref[...], onehot.astype(BF16)) + cnt_ref[0:1, :]
        cnt_ref[...] = cnt_ref[...] + jnp.sum(onehot, axis=0, keepdims=True)
        i1, i2 = d.pop("i1"), d.pop("i2")
        r1 = jnp.sum(jnp.where(lane == i1, before, 0.0), axis=-1, keepdims=True)
        r2 = jnp.sum(jnp.where(lane == i2, before, 0.0), axis=-1, keepdims=True)
        rec = (jnp.where(lane == 0, i1.astype(F32), 0.0) + jnp.where(lane == 1, i2.astype(F32), 0.0)
               + jnp.where(lane == 2, r1, 0.0) + jnp.where(lane == 3, r2, 0.0))
        ridx_ref[0, :, d["rs"]] = rec.T[:8].astype(jnp.int32)

    stages = [conv_in, None, gates, branch_a, merge, out_proj, residual, router, route, rank]
    lag = 2
    for step in range(len(stages) + lag * (n_grp - 1)):
        for g, d in enumerate(st):
            k = step - lag * g
            if 0 <= k < len(stages):
                if stages[k] is None:
                    conv(d, carry_ref[...] if g == 0 else st[g - 1]["u_tail"])
                else:
                    stages[k](d)
    carry_ref[...] = st[-1]["u_tail"]


def _post(x, yb, n1, sh1, sc1, g1, n2, sh2, sc2, wc, cw, woc, woa, wo, wr, br, tm, b0):
    B, S, _ = yb.shape
    rows = tm // max(tm // POST_ROWS, 1)
    tri = jnp.asarray(np.tril(np.ones((rows, rows), np.float32), -1), BF16)
    const = lambda *shape: pl.BlockSpec(shape, lambda b, i: (0,) * len(shape),
                                        pipeline_mode=pl.Buffered(1))
    perb = pl.BlockSpec((1, 1, D_MODEL), lambda b, i: (b + b0, 0, 0))
    tok = lambda w: pl.BlockSpec((1, tm, w), lambda b, i: (b, i, 0))
    return pl.pallas_call(
        _post_kernel,
        out_shape=(jax.ShapeDtypeStruct((B, S, D_MODEL), F32),
                   jax.ShapeDtypeStruct((B, S, D_MODEL // 2), jnp.uint32),
                   jax.ShapeDtypeStruct((B, 8, S), jnp.int32),
                   jax.ShapeDtypeStruct((B, S, LANES), F32),
                   jax.ShapeDtypeStruct((8, LANES), F32)),
        grid=(B, S // tm),
        in_specs=[pl.BlockSpec((1, tm, D_MODEL), lambda b, i: (b + b0, i, 0)), tok(ATTN_WIDTH),
                  const(1, D_MODEL), perb, perb, perb,
                  const(1, D_MODEL), perb, perb,
                  const(D_MODEL, 3 * CONV_WIDTH + 2 * D_MODEL),
                  const(8, CONV_WIDTH),
                  const(CONV_WIDTH, D_MODEL), const(ATTN_WIDTH, D_MODEL),
                  const(D_MODEL, D_MODEL),
                  const(D_MODEL, 2 * LANES), const(1, LANES), const(rows, rows)],
        out_specs=(tok(D_MODEL), tok(D_MODEL // 2),
                   pl.BlockSpec((1, 8, tm), lambda b, i: (b, 0, i)), tok(LANES),
                   pl.BlockSpec((8, LANES), lambda b, i: (0, 0))),
        scratch_shapes=[pltpu.VMEM((8, CONV_WIDTH), F32)],
        compiler_params=pltpu.CompilerParams(
            dimension_semantics=("arbitrary", "arbitrary"), vmem_limit_bytes=VMEM_LIMIT),
        name="post",
    )(x, yb, n1, sh1, sc1, g1, n2, sh2, sc2, wc, cw, woc, woa, wo, wr, br, tri)


SC_CORES = 2
SC_SUBCORES = 16
SC_WORKERS = SC_CORES * SC_SUBCORES
SC_CHUNK = 64
ROW_WORDS = D_MODEL // 2


def _sc_mesh():
    return plsc.VectorSubcoreMesh(core_axis_name="c", subcore_axis_name="s",
                                  num_cores=SC_CORES, num_subcores=SC_SUBCORES)


def _dispatch_body(rows_hbm, idx1_hbm, idx2_hbm, xs_hbm, idx1_v, idx2_v, rows_v, *, n_chunks):
    wid = lax.axis_index("s") * SC_CORES + lax.axis_index("c")
    pltpu.sync_copy(idx1_hbm.at[wid], idx1_v)
    pltpu.sync_copy(idx2_hbm.at[wid], idx2_v)
    base = wid * (n_chunks * SC_CHUNK)

    @pl.loop(0, n_chunks)
    def _(j):
        pltpu.sync_copy(rows_hbm.at[pl.ds(base + j * SC_CHUNK, SC_CHUNK)], rows_v)
        pltpu.sync_copy(rows_v, xs_hbm.at[idx1_v.at[j]])
        pltpu.sync_copy(rows_v, xs_hbm.at[idx2_v.at[j]])


def _sc_scratch(n_chunks):
    return [pltpu.VMEM((n_chunks, SC_CHUNK), jnp.int32), pltpu.VMEM((n_chunks, SC_CHUNK), jnp.int32),
            pltpu.VMEM((SC_CHUNK, ROW_WORDS), jnp.uint32)]


def _dispatch(rows, idx1, idx2, n_slots):
    n_chunks = idx1.shape[1]
    return pl.kernel(
        functools.partial(_dispatch_body, n_chunks=n_chunks),
        out_type=jax.ShapeDtypeStruct((n_slots, ROW_WORDS), jnp.uint32),
        mesh=_sc_mesh(),
        scratch_types=_sc_scratch(n_chunks),
        name="dispatch",
    )(rows, idx1, idx2)


def _collect_body(ys_hbm, idx1_hbm, idx2_hbm, g1_hbm, g2_hbm, idx1_v, idx2_v, rows_v, *, n_chunks):
    wid = lax.axis_index("s") * SC_CORES + lax.axis_index("c")
    pltpu.sync_copy(idx1_hbm.at[wid], idx1_v)
    pltpu.sync_copy(idx2_hbm.at[wid], idx2_v)
    base = wid * (n_chunks * SC_CHUNK)

    @pl.loop(0, n_chunks)
    def _(j):
        dst = pl.ds(base + j * SC_CHUNK, SC_CHUNK)
        pltpu.sync_copy(ys_hbm.at[idx1_v.at[j]], rows_v)
        pltpu.sync_copy(rows_v, g1_hbm.at[dst])
        pltpu.sync_copy(ys_hbm.at[idx2_v.at[j]], rows_v)
        pltpu.sync_copy(rows_v, g2_hbm.at[dst])


def _collect(ys, idx1, idx2):
    n_chunks = idx1.shape[1]
    out = jax.ShapeDtypeStruct((SC_WORKERS * n_chunks * SC_CHUNK, ROW_WORDS), jnp.uint32)
    return pl.kernel(
        functools.partial(_collect_body, n_chunks=n_chunks),
        out_type=(out, out),
        mesh=_sc_mesh(),
        scratch_types=_sc_scratch(n_chunks),
        name="collect",
    )(ys, idx1, idx2)


def _moe_kernel(te_ref, nt_ref, xs_ref, wg_ref, wu_ref, wd_ref, *rest):
    ys_ref = rest[-1]

    @pl.when(pl.program_id(0) < nt_ref[0])
    def _():
        left, right = _unpack_bf16_pairs(xs_ref[...])
        xb = jnp.concatenate([left.astype(BF16), right.astype(BF16)], axis=1)
        g = _dot(xb, wg_ref[0].astype(BF16))
        u = _dot(xb, wu_ref[0].astype(BF16))
        a = (g * _sigmoid(g) * u).astype(BF16)
        ys_ref[...] = _pack_bf16_pairs(_dot(a, wd_ref[0].astype(BF16)))


def _moe(xs, tile_expert, n_tiles, wg, wu, wd, tm, after):
    n_slots = xs.shape[0]
    row_blk = lambda i, te, nt: (jnp.minimum(i, nt[0] - 1), 0)
    w_blk = lambda i, te, nt: (te[i], 0, 0)
    return pl.pallas_call(
        _moe_kernel,
        out_shape=jax.ShapeDtypeStruct((n_slots, ROW_WORDS), jnp.uint32),
        grid_spec=pltpu.PrefetchScalarGridSpec(
            num_scalar_prefetch=2,
            grid=(n_slots // tm,),
            in_specs=[pl.BlockSpec((tm, ROW_WORDS), row_blk),
                      pl.BlockSpec((1, D_MODEL, D_EXPERT), w_blk),
                      pl.BlockSpec((1, D_MODEL, D_EXPERT), w_blk),
                      pl.BlockSpec((1, D_EXPERT, D_MODEL), w_blk)]
            + [pl.BlockSpec(memory_space=pl.ANY)] * len(after),
            out_specs=pl.BlockSpec((tm, ROW_WORDS), row_blk)),
        compiler_params=pltpu.CompilerParams(
            dimension_semantics=("arbitrary",), vmem_limit_bytes=VMEM_LIMIT),
        name="moe",
    )(tile_expert, n_tiles, xs, wg, wu, wd, *after)


def _final_kernel(x1_ref, g1_ref, g2_ref, rw_ref, gate_ref, *rest):
    o_ref = rest[-1]
    rw = rw_ref[0]
    w1 = rw[:, 0:1]
    w2 = rw[:, 1:2]
    a_l, a_r = _unpack_bf16_pairs(g1_ref[0])
    b_l, b_r = _unpack_bf16_pairs(g2_ref[0])
    moe = jnp.concatenate([w1 * a_l + w2 * b_l, w1 * a_r + w2 * b_r], axis=1)
    o_ref[0] = x1_ref[0] + gate_ref[0] * moe


def _final(x1, g1, g2, rw, gate2, tm, b0, src0, out_prev, after):
    nb, S, _ = g1.shape
    grp = lambda w: pl.BlockSpec((1, tm, w), lambda b, i: (b + src0, i, 0))
    tok = lambda w: pl.BlockSpec((1, tm, w), lambda b, i: (b, i, 0))
    extra = (() if out_prev is None else (out_prev,)) + tuple(after)
    return pl.pallas_call(
        _final_kernel,
        out_shape=jax.ShapeDtypeStruct((gate2.shape[0], S, D_MODEL), F32),
        grid=(nb, S // tm),
        in_specs=[grp(D_MODEL), tok(ROW_WORDS), tok(ROW_WORDS), grp(LANES),
                  pl.BlockSpec((1, 1, D_MODEL), lambda b, i: (b + b0 + src0, 0, 0))]
        + [pl.BlockSpec(memory_space=pl.ANY)] * len(extra),
        out_specs=pl.BlockSpec((1, tm, D_MODEL), lambda b, i: (b + b0 + src0, i, 0)),
        input_output_aliases={} if out_prev is None else {5: 0},
        compiler_params=pltpu.CompilerParams(
            dimension_semantics=("arbitrary", "arbitrary"), vmem_limit_bytes=VMEM_LIMIT),
        name="final",
    )(x1, g1, g2, rw, gate2, *extra)


def _pick(n, pref):
    t = min(n, pref)
    assert n % t == 0, (n, t)
    return t


def _slots_kernel(offs_ref, r_ref, o_ref):
    r = r_ref[0]
    base = jnp.zeros_like(r)
    for e in range(N_EXPERTS):
        base = jnp.where(r == e, offs_ref[e], base)
    o_ref[0] = base + pltpu.roll(r, 6, 0)


def _slots(ridx, offs):
    B, _, S = ridx.shape
    return pl.pallas_call(
        _slots_kernel,
        out_shape=jax.ShapeDtypeStruct((B, 8, S), jnp.int32),
        grid_spec=pltpu.PrefetchScalarGridSpec(
            num_scalar_prefetch=1, grid=(B,),
            in_specs=[pl.BlockSpec((1, 8, S), lambda b, offs: (b, 0, 0))],
            out_specs=pl.BlockSpec((1, 8, S), lambda b, offs: (b, 0, 0))),
        compiler_params=pltpu.CompilerParams(dimension_semantics=("arbitrary",)),
        name="slots",
    )(offs, ridx)


def _route_plan(ridx, counts, tm_e, T):
    counts = counts.astype(jnp.int32)
    tiles = (counts + tm_e - 1) // tm_e
    tile_end = jnp.cumsum(tiles)
    offs = (tile_end - tiles) * tm_e
    slots = _slots(ridx, offs)
    n_chunks = T // (SC_WORKERS * SC_CHUNK)
    idx1 = slots[:, 0, :].reshape(SC_WORKERS, n_chunks, SC_CHUNK)
    idx2 = slots[:, 1, :].reshape(SC_WORKERS, n_chunks, SC_CHUNK)
    n_tiles_max = 2 * T // tm_e + N_EXPERTS
    tile_ids = jnp.arange(n_tiles_max, dtype=jnp.int32)
    tile_expert = jnp.sum((tile_end[None, :] <= tile_ids[:, None]).astype(jnp.int32), axis=1)
    tile_expert = jnp.minimum(tile_expert, N_EXPERTS - 1)
    return idx1, idx2, tile_expert, tile_end[-1:].astype(jnp.int32), n_tiles_max * tm_e


def _layer(x, c, w_ada, b_ada, norm1_w, w_in, b_forget, conv_w, q_norm_w, k_norm_w,
           w_out_conv, w_out_attn, w_o, norm2_w, w_rg, b_rg, w_re, b_re, w_gate, w_up, w_down):
    B, S, _ = x.shape
    n_grp = 2 if B % 2 == 0 and (B // 2 * S) % (SC_WORKERS * SC_CHUNK) == 0 else 1
    Bg = B // n_grp
    T = Bg * S
    assert T % (SC_WORKERS * SC_CHUNK) == 0, T
    mod = _ada(c, w_ada, b_ada.reshape(1, -1)).reshape(B, 6, 1, D_MODEL)
    shift1, scale1, gate1, shift2, scale2, gate2 = (mod[:, t] for t in range(6))

    cuts = np.cumsum([0, CONV_WIDTH, CONV_WIDTH, CONV_WIDTH, ATTN_WIDTH, ATTN_WIDTH, ATTN_WIDTH,
                      N_HEADS, D_MODEL, D_MODEL])
    w_conv3 = w_in[:, cuts[0]:cuts[3]]
    w_qvT = jnp.concatenate([w_in[:, cuts[3]:cuts[4]], w_in[:, cuts[5]:cuts[6]]], axis=1).T.astype(BF16)
    w_k = w_in[:, cuts[4]:cuts[5]].astype(BF16)
    w_f = jnp.pad(w_in[:, cuts[6]:cuts[7]], ((0, 0), (0, LANES - N_HEADS))).astype(BF16)
    b_f = jnp.pad(b_forget, (0, LANES - N_HEADS)).reshape(1, LANES)
    w_cgg = jnp.concatenate([w_conv3, w_in[:, cuts[7]:cuts[9]]], axis=1).astype(BF16)

    tm_qkv = _pick(S, 1024)
    qnwT = jnp.broadcast_to((jnp.tile(q_norm_w, N_HEADS) * (LOG2E * HEAD_DIM ** -0.5))[:, None],
                            (ATTN_WIDTH, LANES))
    tq = _pick(S // 2, 512)
    w_r = jnp.pad(jnp.concatenate([w_re, w_rg], axis=1),
                  ((0, 0), (0, LANES - N_EXPERTS - N_GROUPS)))
    w_r_hi = w_r.astype(BF16)
    w_r_lo = (w_r - w_r_hi.astype(F32)).astype(BF16)
    b_r = jnp.pad(jnp.concatenate([b_re, b_rg]), (0, LANES - N_EXPERTS - N_GROUPS)).reshape(1, LANES)
    cw = jnp.pad(conv_w, ((0, 8 - CONV_K), (0, 0)))
    tm_post = _pick(S, 1024)
    tm_e = 512
    w_oc, w_oa, w_ob, w_rs = (w_out_conv.astype(BF16), w_out_attn.astype(BF16), w_o.astype(BF16),
                              jnp.concatenate([w_r_hi, w_r_lo], axis=1))

    n_half = 2 if Bg % 2 == 0 and (Bg // 2 * S) % (SC_WORKERS * SC_CHUNK) == 0 else 1
    Bh = Bg // n_half

    def experts(grp, after):
        x1, rw, idx1, idx2, tile_expert, n_tiles, xs = grp
        ys = _moe(xs, tile_expert, n_tiles, w_gate, w_up, w_down, tm_e, after)
        halves = lambda idx: idx.reshape(n_half, SC_WORKERS, idx.shape[1] // n_half, SC_CHUNK)
        return ys, [_collect(ys, i1, i2) for i1, i2 in zip(halves(idx1), halves(idx2))]

    def finish(g, grp, gathered, out, after):
        for hh, (g1, g2) in enumerate(gathered):
            out = _final(grp[0], g1.reshape(Bh, S, ROW_WORDS), g2.reshape(Bh, S, ROW_WORDS), grp[1],
                         gate2, _pick(S, 512), g * Bg, hh * Bh, out, after if hh == 0 else [])
        return out

    out, prev, prev_gathered, ys_prev = None, None, None, None
    for g in range(n_grp):
        b0 = g * Bg
        qT_aug, k_aug, vT = _qkv(x, norm1_w.reshape(1, -1), shift1, scale1, w_qvT, w_k, w_f, b_f,
                                 qnwT, jnp.tile(k_norm_w, N_HEADS).reshape(1, -1), tm_qkv, b0, Bg)
        if prev is not None:
            ys_prev, prev_gathered = experts(prev, [qT_aug])
        y_b = _attention(qT_aug, k_aug, vT, tq, _pick(tq, 256),
                         [] if ys_prev is None else [ys_prev])
        x1, h2p, ridx, rw, counts = _post(x, y_b, norm1_w.reshape(1, -1), shift1, scale1, gate1,
                                          norm2_w.reshape(1, -1), shift2, scale2,
                                          w_cgg, cw, w_oc, w_oa, w_ob, w_rs, b_r, tm_post, b0)
        idx1, idx2, tile_expert, n_tiles, n_slots = _route_plan(ridx, counts[0, :N_EXPERTS], tm_e, T)
        xs = _dispatch(h2p.reshape(T, ROW_WORDS), idx1, idx2, n_slots)
        if prev is not None:
            out = finish(g - 1, prev, prev_gathered, out, [idx1])
        prev = (x1, rw, idx1, idx2, tile_expert, n_tiles, xs)
    _, last_gathered = experts(prev, [] if out is None else [out])
    return finish(n_grp - 1, prev, last_gathered, out, [])


def kernel(x, c, w_ada, b_ada, norm1_w, w_in, b_forget, conv_w, q_norm_w, k_norm_w, w_out_conv,
           w_out_attn, w_o, norm2_w, w_router_group, b_router_group, w_router_expert,
           b_router_expert, w_gate, w_up, w_down):
    for l in range(w_ada.shape[0]):
        x = _layer(x, c, w_ada[l], b_ada[l], norm1_w[l], w_in[l], b_forget[l], conv_w[l],
                   q_norm_w[l], k_norm_w[l], w_out_conv[l], w_out_attn[l], w_o[l], norm2_w[l],
                   w_router_group[l], b_router_group[l], w_router_expert[l], b_router_expert[l],
                   w_gate[l], w_up[l], w_down[l])
    return x
```

```python
import functools

import jax
import jax.numpy as jnp
import numpy as np
from jax import lax
from jax.experimental import pallas as pl
from jax.experimental.pallas import tpu as pltpu
from jax.experimental.pallas import tpu_sc as plsc

D_MODEL = 1024
CONV_WIDTH = 512
CONV_K = 3
N_HEADS = 8
HEAD_DIM = 64
ATTN_WIDTH = N_HEADS * HEAD_DIM
N_PAIRS = N_HEADS // 2
N_GROUPS = 4
EXPERTS_PER_GROUP = 8
N_EXPERTS = N_GROUPS * EXPERTS_PER_GROUP
D_EXPERT = 256
EPS = 1e-6
LANES = 128
AUG = 2 * LANES
BIAS_W = 6
VROWS = HEAD_DIM + 16
NEG = -1e30
LOG2E = 1.4426950408889634

F32 = jnp.float32
BF16 = jnp.bfloat16
VMEM_LIMIT = 56 * 1024 * 1024


def _sigmoid(z):
    return 1.0 / (1.0 + jnp.exp(-z))


def _split3(z):
    hi = z.astype(BF16)
    r = z - hi.astype(F32)
    mid = r.astype(BF16)
    lo = (r - mid.astype(F32)).astype(BF16)
    return hi, mid, lo


def _dot(a, b):
    return jnp.dot(a, b, preferred_element_type=F32)


def _modulated_norm(x, nw, shift, scale):
    ms = jnp.mean(x * x, axis=-1, keepdims=True)
    return (x * lax.rsqrt(ms + EPS) * nw) * (1.0 + scale) + shift


def _ada_kernel(c_ref, w_ref, b_ref, o_ref):
    c = c_ref[...]
    a = c * _sigmoid(c)
    o_ref[...] = jnp.dot(a, w_ref[...], precision=lax.Precision.HIGHEST,
                         preferred_element_type=F32) + b_ref[...]


def _ada(c, w_ada, b_ada):
    B = c.shape[0]
    n = w_ada.shape[1] // D_MODEL
    return pl.pallas_call(
        _ada_kernel,
        out_shape=jax.ShapeDtypeStruct((B, n * D_MODEL), F32),
        grid=(n,),
        in_specs=[pl.BlockSpec((B, D_MODEL), lambda j: (0, 0)),
                  pl.BlockSpec((D_MODEL, D_MODEL), lambda j: (0, j)),
                  pl.BlockSpec((1, D_MODEL), lambda j: (0, j))],
        out_specs=pl.BlockSpec((B, D_MODEL), lambda j: (0, j)),
        compiler_params=pltpu.CompilerParams(dimension_semantics=("arbitrary",)),
        name="ada",
    )(c, w_ada, b_ada)


QKV_ROWS = 256
_NT = (((1,), (1,)), ((), ()))


def _lane_tile(a, width):
    return jnp.concatenate([a] * (width // a.shape[1]), axis=1)


def _qkv_kernel(x_ref, nw_ref, sh_ref, sc_ref, wqv_ref, wk_ref, wf_ref, bf_ref, qnw_ref, knw_ref,
                gsum_ref, pq_ref, pk_ref, cq_ref, ck_ref,
                qT_ref, k_ref, vT_ref, carry_ref):
    tm = x_ref.shape[1]
    n_grp = max(tm // QKV_ROWS, 1)
    rows = tm // n_grp
    row = lax.broadcasted_iota(jnp.int32, (rows, LANES), 0)

    @pl.when(pl.program_id(1) == 0)
    def _():
        carry_ref[...] = jnp.zeros_like(carry_ref)

    st = [dict(rs=pl.ds(g * rows, rows)) for g in range(n_grp)]

    def project(d):
        h = _modulated_norm(x_ref[0, d["rs"], :], nw_ref[...], sh_ref[0], sc_ref[0])
        hb = h.astype(BF16)
        d["qvT"] = lax.dot_general(wqv_ref[...], hb, _NT, preferred_element_type=F32)
        d["k"] = _dot(hb, wk_ref[...])
        d["fl"] = _dot(hb, wf_ref[...]) + bf_ref[...]

    def norms(d):
        qvT = d["qvT"]
        heads = []
        for hd in range(N_HEADS):
            z = qvT[hd * HEAD_DIM:(hd + 1) * HEAD_DIM]
            heads.append(z * lax.rsqrt(jnp.mean(z * z, axis=0, keepdims=True) + EPS))
        qnT = jnp.concatenate(heads, axis=0) * _lane_tile(qnw_ref[...], rows)
        for j in range(N_PAIRS):
            qT_ref[0, j, :LANES, d["rs"]] = qnT[j * LANES:(j + 1) * LANES].astype(BF16)
            for t in range(2):
                r0 = ATTN_WIDTH + (2 * j + t) * HEAD_DIM
                vT_ref[0, j, t * VROWS:t * VROWS + HEAD_DIM, d["rs"]] = qvT[r0:r0 + HEAD_DIM].astype(BF16)
                vT_ref[0, j, t * VROWS + HEAD_DIM:(t + 1) * VROWS, d["rs"]] = \
                    jnp.ones((VROWS - HEAD_DIM, rows), BF16)
        del d["qvT"]
        k = d.pop("k")
        ss = _dot((k * k).astype(BF16), gsum_ref[...])
        kn = k * lax.rsqrt(ss * (1.0 / HEAD_DIM) + EPS) * knw_ref[...]
        for j in range(N_PAIRS):
            k_ref[0, d["rs"], j * AUG:j * AUG + LANES] = kn[:, j * LANES:(j + 1) * LANES].astype(BF16)

    def forget(d, before):
        fl = d.pop("fl")
        cum = jnp.minimum(fl, 0.0) - jnp.log(1.0 + jnp.exp(-jnp.abs(fl)))
        s = 1
        while s < rows:
            cum = cum + jnp.where(row >= s, pltpu.roll(cum, s, 0), 0.0)
            s *= 2
        cum = cum + before[7:8, :]
        d["tail"] = cum[rows - 8:, :]
        d["parts"] = jnp.concatenate(_split3(cum * LOG2E), axis=1)

    def bias(d):
        parts = d.pop("parts")
        eqT = (lax.dot_general(pq_ref[...], parts, _NT, preferred_element_type=F32)
               + _lane_tile(cq_ref[...], rows)).astype(BF16)
        ek = (_dot(parts, pk_ref[...]) + ck_ref[...]).astype(BF16)
        for j in range(N_PAIRS):
            qT_ref[0, j, LANES:, d["rs"]] = eqT
            k_ref[0, d["rs"], j * AUG + LANES:(j + 1) * AUG] = ek

    stages = [project, norms, None, bias]
    lag = 1
    for step in range(len(stages) + lag * (n_grp - 1)):
        for g, d in enumerate(st):
            kk = step - lag * g
            if 0 <= kk < len(stages):
                if stages[kk] is None:
                    forget(d, carry_ref[...] if g == 0 else st[g - 1]["tail"])
                else:
                    stages[kk](d)
    carry_ref[...] = st[-1]["tail"]


def _bias_placement():
    pq = np.zeros((LANES, 3 * LANES), np.float32)
    pk = np.zeros((3 * LANES, LANES), np.float32)
    cq = np.zeros((LANES, LANES), np.float32)
    ck = np.zeros((1, LANES), np.float32)
    for hd in range(N_HEADS):
        base = BIAS_W * hd
        for p in range(3):
            pq[base + p, p * LANES + hd] = 1.0
            pk[p * LANES + hd, base + 3 + p] = -1.0
            cq[base + 3 + p, :] = 1.0
            ck[0, base + p] = 1.0
    return (jnp.asarray(pq, BF16), jnp.asarray(pk, BF16), jnp.asarray(cq), jnp.asarray(ck))


def _qkv(x, nw, shift, scale, wqvT, wk, wf, bf, qnwT, knw, tm, b0, B):
    S = x.shape[1]
    gsum = jnp.asarray(np.kron(np.eye(N_HEADS), np.ones((HEAD_DIM, HEAD_DIM))), BF16)
    pq, pk, cq, ck = _bias_placement()
    const = lambda *shape: pl.BlockSpec(shape, lambda b, i: (0,) * len(shape),
                                        pipeline_mode=pl.Buffered(1))
    return pl.pallas_call(
        _qkv_kernel,
        out_shape=(jax.ShapeDtypeStruct((B, N_PAIRS, AUG, S), BF16),
                   jax.ShapeDtypeStruct((B, S, N_PAIRS * AUG), BF16),
                   jax.ShapeDtypeStruct((B, N_PAIRS, 2 * VROWS, S), BF16)),
        grid=(B, S // tm),
        in_specs=[pl.BlockSpec((1, tm, D_MODEL), lambda b, i: (b + b0, i, 0)),
                  const(1, D_MODEL),
                  pl.BlockSpec((1, 1, D_MODEL), lambda b, i: (b + b0, 0, 0)),
                  pl.BlockSpec((1, 1, D_MODEL), lambda b, i: (b + b0, 0, 0)),
                  const(2 * ATTN_WIDTH, D_MODEL),
                  const(D_MODEL, ATTN_WIDTH),
                  const(D_MODEL, LANES),
                  const(1, LANES),
                  const(ATTN_WIDTH, LANES),
                  const(1, ATTN_WIDTH),
                  const(ATTN_WIDTH, ATTN_WIDTH),
                  const(LANES, 3 * LANES),
                  const(3 * LANES, LANES),
                  const(LANES, LANES),
                  const(1, LANES)],
        out_specs=(pl.BlockSpec((1, N_PAIRS, AUG, tm), lambda b, i: (b, 0, 0, i)),
                   pl.BlockSpec((1, tm, N_PAIRS * AUG), lambda b, i: (b, i, 0)),
                   pl.BlockSpec((1, N_PAIRS, 2 * VROWS, tm), lambda b, i: (b, 0, 0, i))),
        scratch_shapes=[pltpu.VMEM((8, LANES), F32)],
        compiler_params=pltpu.CompilerParams(
            dimension_semantics=("arbitrary", "arbitrary"), vmem_limit_bytes=VMEM_LIMIT),
        name="qkv",
    )(x, nw, shift, scale, wqvT, wk, wf, bf, qnwT, knw, gsum, pq, pk, cq, ck)


def _attn_kernel(qT_ref, k_ref, vT_ref, *rest, tq, cw, nt):
    o_ref, qq_ref, s_ref, smax_ref, m_ref, acc_ref = rest[-6:]
    i = pl.program_id(2)
    n = tq // cw
    chains = [(a, t, c) for a in range(nt) for t in range(2) for c in range(n)]
    feat = lax.broadcasted_iota(jnp.int32, (AUG, tq), 0)
    for t in range(2):
        bias0 = LANES + BIAS_W * (2 * pl.program_id(1) + t)
        keep = ((feat >= t * HEAD_DIM) & (feat < (t + 1) * HEAD_DIM)) | \
               ((feat >= bias0) & (feat < bias0 + BIAS_W))
        for a in range(nt):
            qT = qT_ref[0, 0, :, a * tq:(a + 1) * tq]
            qh = jnp.where(keep, qT, jnp.zeros_like(qT))
            for c in range(n):
                qq_ref[chains.index((a, t, c))] = qh[:, c * cw:(c + 1) * cw]
    kpos = lax.broadcasted_iota(jnp.int32, (tq, cw), 0)
    qpos = lax.broadcasted_iota(jnp.int32, (tq, cw), 1)

    def scores(j, slot, which):
        k_blk = k_ref[0, pl.ds(pl.multiple_of(j * tq, tq), tq), :]
        for ci in which:
            s = _dot(k_blk, qq_ref[ci])
            s_ref[slot, ci] = s
            smax_ref[slot, ci] = jnp.broadcast_to(jnp.max(s, axis=0, keepdims=True), (8, cw))

    def absorb(j, slot, which, diagonal=()):
        start = pl.multiple_of(j * tq, tq)
        for ci in which:
            _, t, c = chains[ci]
            vj = vT_ref[0, 0, t * VROWS:(t + 1) * VROWS, pl.ds(start, tq)]
            s = s_ref[slot, ci]
            if ci in diagonal:
                s = jnp.where(kpos <= qpos + c * cw, s, NEG)
                smax = jnp.max(s, axis=0, keepdims=True)
            else:
                smax = smax_ref[slot, ci, 0:1]
            m = m_ref[ci, 0:1]
            m_new = jnp.maximum(m, smax)
            p = jnp.exp2(s - m_new).astype(BF16)
            acc_ref[ci] = jnp.exp2(m - m_new) * acc_ref[ci] + _dot(vj, p)
            m_ref[ci] = jnp.broadcast_to(m_new, (8, cw))

    every = list(range(len(chains)))
    from_tile = lambda r: [ci for ci in every if chains[ci][0] >= r]
    m_ref[...] = jnp.full(m_ref.shape, NEG, F32)
    acc_ref[...] = jnp.zeros(acc_ref.shape, F32)
    scores(0, 0, every)

    def two_blocks(jj, _):
        j = 2 * jj
        scores(j + 1, 1, every)
        absorb(j, 0, every)
        scores(j + 2, 0, every)
        absorb(j + 1, 1, every)
        return 0

    lax.fori_loop(0, i * (nt // 2), two_blocks, 0)
    for r in range(nt):
        if r + 1 < nt:
            scores(nt * i + r + 1, (r + 1) % 2, from_tile(r + 1))
        absorb(nt * i + r, r % 2, from_tile(r),
               diagonal=[ci for ci in every if chains[ci][0] == r])

    for a in range(nt):
        outs = [acc_ref[ci, :HEAD_DIM] / acc_ref[ci, HEAD_DIM:HEAD_DIM + 1]
                for ci in every if chains[ci][0] == a]
        oT = jnp.concatenate([jnp.concatenate(outs[:n], axis=1), jnp.concatenate(outs[n:], axis=1)],
                             axis=0)
        o_ref[0, a * tq:(a + 1) * tq, :] = oT.T.astype(BF16)


def _attention(qT_aug, k_aug, vT, tq, cw, after):
    B, S, _ = k_aug.shape
    nt = 4 if S % (4 * tq) == 0 else 2
    n_chains = 2 * nt * tq // cw
    return pl.pallas_call(
        functools.partial(_attn_kernel, tq=tq, cw=cw, nt=nt),
        scratch_shapes=[pltpu.VMEM((n_chains, AUG, cw), BF16),
                        pltpu.VMEM((2, n_chains, tq, cw), F32),
                        pltpu.VMEM((2, n_chains, 8, cw), F32),
                        pltpu.VMEM((n_chains, 8, cw), F32),
                        pltpu.VMEM((n_chains, VROWS, cw), F32)],
        out_shape=jax.ShapeDtypeStruct((B, S, ATTN_WIDTH), BF16),
        grid=(B, N_PAIRS, S // (nt * tq)),
        in_specs=[pl.BlockSpec((1, 1, AUG, nt * tq), lambda b, j, i: (b, j, 0, i)),
                  pl.BlockSpec((1, S, AUG), lambda b, j, i: (b, 0, j)),
                  pl.BlockSpec((1, 1, 2 * VROWS, S), lambda b, j, i: (b, j, 0, 0))]
        + [pl.BlockSpec(memory_space=pl.ANY)] * len(after),
        out_specs=pl.BlockSpec((1, nt * tq, LANES), lambda b, j, i: (b, i, j)),
        compiler_params=pltpu.CompilerParams(
            dimension_semantics=("arbitrary", "arbitrary", "arbitrary"),
            vmem_limit_bytes=VMEM_LIMIT),
        name="attn",
    )(qT_aug, k_aug, vT, *after)


def _pack_bf16_pairs(z):
    w = z.shape[1] // 2
    bits = pltpu.bitcast(z.astype(BF16).astype(F32), jnp.uint32)
    return bits[:, :w] | (bits[:, w:] >> 16)


def _unpack_bf16_pairs(p):
    return (pltpu.bitcast(p & jnp.uint32(0xFFFF0000), F32), pltpu.bitcast(p << 16, F32))


POST_ROWS = 256


def _post_kernel(x_ref, yb_ref, n1_ref, sh1_ref, sc1_ref, g1_ref, n2_ref, sh2_ref, sc2_ref,
                 wc_ref, cw_ref, woc_ref, woa_ref, wo_ref, wr_ref, br_ref, tri_ref,
                 x1_ref, h2_ref, ridx_ref, rw_ref, cnt_ref, carry_ref):
    tm = x_ref.shape[1]
    n_grp = max(tm // POST_ROWS, 1)
    rows = tm // n_grp
    lane = lax.broadcasted_iota(jnp.int32, (rows, LANES), 1)
    row8 = lax.broadcasted_iota(jnp.int32, (8, CONV_WIDTH), 0)
    big = jnp.int32(1 << 20)

    @pl.when(pl.program_id(1) == 0)
    def _():
        carry_ref[...] = jnp.zeros_like(carry_ref)

    @pl.when((pl.program_id(0) == 0) & (pl.program_id(1) == 0))
    def _():
        cnt_ref[...] = jnp.zeros_like(cnt_ref)

    st = [dict(rs=pl.ds(g * rows, rows)) for g in range(n_grp)]

    def conv_in(d):
        d["x"] = x_ref[0, d["rs"], :]
        d["hb"] = _modulated_norm(d["x"], n1_ref[...], sh1_ref[0], sc1_ref[0]).astype(BF16)
        d["x_in"] = _dot(d["hb"], wc_ref[:, :CONV_WIDTH])
        d["conv_c"] = _dot(d["hb"], wc_ref[:, 2 * CONV_WIDTH:3 * CONV_WIDTH])
        d["conv_b"] = _dot(d["hb"], wc_ref[:, CONV_WIDTH:2 * CONV_WIDTH])

    def conv(d, prev):
        u = d.pop("conv_c") * d.pop("x_in")
        d["u_tail"] = u[rows - 8:, :]

        def shifted(k):
            r = pltpu.roll(u, k, 0)
            top = jnp.where(row8 < k, pltpu.roll(prev, k, 0), r[:8])
            return jnp.concatenate([top, r[8:]], axis=0)

        cw = cw_ref[...]
        cv = cw[0:1] * shifted(2) + cw[1:2] * shifted(1) + cw[2:3] * u
        d["y_a"] = (d.pop("conv_b") * cv).astype(BF16)

    def gates(d):
        d["p_b"] = _dot(yb_ref[0, d["rs"], :], woa_ref[...])
        d["gate_c"] = _dot(d["hb"], wc_ref[:, 3 * CONV_WIDTH:3 * CONV_WIDTH + D_MODEL])
        d["gate_a"] = _dot(d.pop("hb"), wc_ref[:, 3 * CONV_WIDTH + D_MODEL:])

    def branch_a(d):
        d["p_a"] = _dot(d.pop("y_a"), woc_ref[...])

    def merge(d):
        d["merged"] = (_sigmoid(d.pop("gate_c")) * d.pop("p_a")
                       + _sigmoid(d.pop("gate_a")) * d.pop("p_b")).astype(BF16)

    def out_proj(d):
        d["o"] = _dot(d.pop("merged"), wo_ref[...])

    def residual(d):
        x1 = d.pop("x") + g1_ref[0] * d.pop("o")
        x1_ref[0, d["rs"], :] = x1
        h2 = _modulated_norm(x1, n2_ref[...], sh2_ref[0], sc2_ref[0])
        h2_ref[0, d["rs"], :] = _pack_bf16_pairs(h2)
        d["h_hi"] = h2.astype(BF16)
        d["h_lo"] = (h2 - d["h_hi"].astype(F32)).astype(BF16)

    def router(d):
        both = _dot(d.pop("h_hi"), wr_ref[...])
        d["lg"] = (both[:, :LANES] + both[:, LANES:] + _dot(d.pop("h_lo"), wr_ref[:, :LANES])) \
            + br_ref[...]

    def first_argmax(vals):
        mx = jnp.max(vals, axis=-1, keepdims=True)
        idx = jnp.min(jnp.where(vals == mx, lane, big), axis=-1, keepdims=True)
        return mx, idx

    def route(d):
        lg = d.pop("lg")
        is_g = (lane >= N_EXPERTS) & (lane < N_EXPERTS + N_GROUPS)
        g_mx, g_lane = first_argmax(jnp.where(is_g, lg, NEG))
        p_sel = 1.0 / jnp.sum(jnp.where(is_g, jnp.exp(lg - g_mx), 0.0), axis=-1, keepdims=True)
        g_idx = g_lane - N_EXPERTS
        in_g = (lane >= g_idx * EXPERTS_PER_GROUP) & (lane < (g_idx + 1) * EXPERTS_PER_GROUP)
        le = jnp.where(in_g, lg, NEG)
        v1, i1 = first_argmax(le)
        v2, i2 = first_argmax(jnp.where(lane == i1, NEG, le))
        e2 = jnp.exp(v2 - v1)
        w1 = p_sel / (1.0 + e2)
        w2 = w1 * e2
        rw_ref[0, d["rs"], :] = jnp.where(lane == 0, w1, 0.0) + jnp.where(lane == 1, w2, 0.0)
        d["i1"], d["i2"] = i1, i2
        d["onehot"] = jnp.where((lane == i1) | (lane == i2), 1.0, 0.0)

    def rank(d):
        onehot = d.pop("onehot")
        before = _dot(tri_ref[...], onehot.astype(BF16)) + cnt_ref[0:1, :]
        cnt_ref[...] = cnt_ref[...] + jnp.sum(onehot, axis=0, keepdims=True)
        i1, i2 = d.pop("i1"), d.pop("i2")
        r1 = jnp.sum(jnp.where(lane == i1, before, 0.0), axis=-1, keepdims=True)
        r2 = jnp.sum(jnp.where(lane == i2, before, 0.0), axis=-1, keepdims=True)
        rec = (jnp.where(lane == 0, i1.astype(F32), 0.0) + jnp.where(lane == 1, i2.astype(F32), 0.0)
               + jnp.where(lane == 2, r1, 0.0) + jnp.where(lane == 3, r2, 0.0))
        ridx_ref[0, :, d["rs"]] = rec.T[:8].astype(jnp.int32)

    stages = [conv_in, None, gates, branch_a, merge, out_proj, residual, router, route, rank]
    lag = 2
    for step in range(len(stages) + lag * (n_grp - 1)):
        for g, d in enumerate(st):
            k = step - lag * g
            if 0 <= k < len(stages):
                if stages[k] is None:
                    conv(d, carry_ref[...] if g == 0 else st[g - 1]["u_tail"])
                else:
                    stages[k](d)
    carry_ref[...] = st[-1]["u_tail"]


def _post(x, yb, n1, sh1, sc1, g1, n2, sh2, sc2, wc, cw, woc, woa, wo, wr, br, tm, b0):
    B, S, _ = yb.shape
    rows = tm // max(tm // POST_ROWS, 1)
    tri = jnp.asarray(np.tril(np.ones((rows, rows), np.float32), -1), BF16)
    const = lambda *shape: pl.BlockSpec(shape, lambda b, i: (0,) * len(shape),
                                        pipeline_mode=pl.Buffered(1))
    perb = pl.BlockSpec((1, 1, D_MODEL), lambda b, i: (b + b0, 0, 0))
    tok = lambda w: pl.BlockSpec((1, tm, w), lambda b, i: (b, i, 0))
    return pl.pallas_call(
        _post_kernel,
        out_shape=(jax.ShapeDtypeStruct((B, S, D_MODEL), F32),
                   jax.ShapeDtypeStruct((B, S, D_MODEL // 2), jnp.uint32),
                   jax.ShapeDtypeStruct((B, 8, S), jnp.int32),
                   jax.ShapeDtypeStruct((B, S, LANES), F32),
                   jax.ShapeDtypeStruct((8, LANES), F32)),
        grid=(B, S // tm),
        in_specs=[pl.BlockSpec((1, tm, D_MODEL), lambda b, i: (b + b0, i, 0)), tok(ATTN_WIDTH),
                  const(1, D_MODEL), perb, perb, perb,
                  const(1, D_MODEL), perb, perb,
                  const(D_MODEL, 3 * CONV_WIDTH + 2 * D_MODEL),
                  const(8, CONV_WIDTH),
                  const(CONV_WIDTH, D_MODEL), const(ATTN_WIDTH, D_MODEL),
                  const(D_MODEL, D_MODEL),
                  const(D_MODEL, 2 * LANES), const(1, LANES), const(rows, rows)],
        out_specs=(tok(D_MODEL), tok(D_MODEL // 2),
                   pl.BlockSpec((1, 8, tm), lambda b, i: (b, 0, i)), tok(LANES),
                   pl.BlockSpec((8, LANES), lambda b, i: (0, 0))),
        scratch_shapes=[pltpu.VMEM((8, CONV_WIDTH), F32)],
        compiler_params=pltpu.CompilerParams(
            dimension_semantics=("arbitrary", "arbitrary"), vmem_limit_bytes=VMEM_LIMIT),
        name="post",
    )(x, yb, n1, sh1, sc1, g1, n2, sh2, sc2, wc, cw, woc, woa, wo, wr, br, tri)


SC_CORES = 2
SC_SUBCORES = 16
SC_WORKERS = SC_CORES * SC_SUBCORES
SC_CHUNK = 64
ROW_WORDS = D_MODEL // 2


def _sc_mesh():
    return plsc.VectorSubcoreMesh(core_axis_name="c", subcore_axis_name="s",
                                  num_cores=SC_CORES, num_subcores=SC_SUBCORES)


def _dispatch_body(rows_hbm, idx1_hbm, idx2_hbm, xs_hbm, idx1_v, idx2_v, rows_v, *, n_chunks):
    wid = lax.axis_index("s") * SC_CORES + lax.axis_index("c")
    pltpu.sync_copy(idx1_hbm.at[wid], idx1_v)
    pltpu.sync_copy(idx2_hbm.at[wid], idx2_v)
    base = wid * (n_chunks * SC_CHUNK)

    @pl.loop(0, n_chunks)
    def _(j):
        pltpu.sync_copy(rows_hbm.at[pl.ds(base + j * SC_CHUNK, SC_CHUNK)], rows_v)
        pltpu.sync_copy(rows_v, xs_hbm.at[idx1_v.at[j]])
        pltpu.sync_copy(rows_v, xs_hbm.at[idx2_v.at[j]])


def _sc_scratch(n_chunks):
    return [pltpu.VMEM((n_chunks, SC_CHUNK), jnp.int32), pltpu.VMEM((n_chunks, SC_CHUNK), jnp.int32),
            pltpu.VMEM((SC_CHUNK, ROW_WORDS), jnp.uint32)]


def _dispatch(rows, idx1, idx2, n_slots):
    n_chunks = idx1.shape[1]
    return pl.kernel(
        functools.partial(_dispatch_body, n_chunks=n_chunks),
        out_type=jax.ShapeDtypeStruct((n_slots, ROW_WORDS), jnp.uint32),
        mesh=_sc_mesh(),
        scratch_types=_sc_scratch(n_chunks),
        name="dispatch",
    )(rows, idx1, idx2)


def _collect_body(ys_hbm, idx1_hbm, idx2_hbm, g1_hbm, g2_hbm, idx1_v, idx2_v, rows_v, *, n_chunks):
    wid = lax.axis_index("s") * SC_CORES + lax.axis_index("c")
    pltpu.sync_copy(idx1_hbm.at[wid], idx1_v)
    pltpu.sync_copy(idx2_hbm.at[wid], idx2_v)
    base = wid * (n_chunks * SC_CHUNK)

    @pl.loop(0, n_chunks)
    def _(j):
        dst = pl.ds(base + j * SC_CHUNK, SC_CHUNK)
        pltpu.sync_copy(ys_hbm.at[idx1_v.at[j]], rows_v)
        pltpu.sync_copy(rows_v, g1_hbm.at[dst])
        pltpu.sync_copy(ys_hbm.at[idx2_v.at[j]], rows_v)
        pltpu.sync_copy(rows_v, g2_hbm.at[dst])


def _collect(ys, idx1, idx2):
    n_chunks = idx1.shape[1]
    out = jax.ShapeDtypeStruct((SC_WORKERS * n_chunks * SC_CHUNK, ROW_WORDS), jnp.uint32)
    return pl.kernel(
        functools.partial(_collect_body, n_chunks=n_chunks),
        out_type=(out, out),
        mesh=_sc_mesh(),
        scratch_types=_sc_scratch(n_chunks),
        name="collect",
    )(ys, idx1, idx2)


def _moe_kernel(te_ref, nt_ref, xs_ref, wg_ref, wu_ref, wd_ref, *rest):
    ys_ref = rest[-1]

    @pl.when(pl.program_id(0) < nt_ref[0])
    def _():
        left, right = _unpack_bf16_pairs(xs_ref[...])
        xb = jnp.concatenate([left.astype(BF16), right.astype(BF16)], axis=1)
        g = _dot(xb, wg_ref[0].astype(BF16))
        u = _dot(xb, wu_ref[0].astype(BF16))
        a = (g * _sigmoid(g) * u).astype(BF16)
        ys_ref[...] = _pack_bf16_pairs(_dot(a, wd_ref[0].astype(BF16)))


def _moe(xs, tile_expert, n_tiles, wg, wu, wd, tm, after):
    n_slots = xs.shape[0]
    row_blk = lambda i, te, nt: (jnp.minimum(i, nt[0] - 1), 0)
    w_blk = lambda i, te, nt: (te[i], 0, 0)
    return pl.pallas_call(
        _moe_kernel,
        out_shape=jax.ShapeDtypeStruct((n_slots, ROW_WORDS), jnp.uint32),
        grid_spec=pltpu.PrefetchScalarGridSpec(
            num_scalar_prefetch=2,
            grid=(n_slots // tm,),
            in_specs=[pl.BlockSpec((tm, ROW_WORDS), row_blk),
                      pl.BlockSpec((1, D_MODEL, D_EXPERT), w_blk),
                      pl.BlockSpec((1, D_MODEL, D_EXPERT), w_blk),
                      pl.BlockSpec((1, D_EXPERT, D_MODEL), w_blk)]
            + [pl.BlockSpec(memory_space=pl.ANY)] * len(after),
            out_specs=pl.BlockSpec((tm, ROW_WORDS), row_blk)),
        compiler_params=pltpu.CompilerParams(
            dimension_semantics=("arbitrary",), vmem_limit_bytes=VMEM_LIMIT),
        name="moe",
    )(tile_expert, n_tiles, xs, wg, wu, wd, *after)


def _final_kernel(x1_ref, g1_ref, g2_ref, rw_ref, gate_ref, *rest):
    o_ref = rest[-1]
    rw = rw_ref[0]
    w1 = rw[:, 0:1]
    w2 = rw[:, 1:2]
    a_l, a_r = _unpack_bf16_pairs(g1_ref[0])
    b_l, b_r = _unpack_bf16_pairs(g2_ref[0])
    moe = jnp.concatenate([w1 * a_l + w2 * b_l, w1 * a_r + w2 * b_r], axis=1)
    o_ref[0] = x1_ref[0] + gate_ref[0] * moe


def _final(x1, g1, g2, rw, gate2, tm, b0, src0, out_prev, after):
    nb, S, _ = g1.shape
    grp = lambda w: pl.BlockSpec((1, tm, w), lambda b, i: (b + src0, i, 0))
    tok = lambda w: pl.BlockSpec((1, tm, w), lambda b, i: (b, i, 0))
    extra = (() if out_prev is None else (out_prev,)) + tuple(after)
    return pl.pallas_call(
        _final_kernel,
        out_shape=jax.ShapeDtypeStruct((gate2.shape[0], S, D_MODEL), F32),
        grid=(nb, S // tm),
        in_specs=[grp(D_MODEL), tok(ROW_WORDS), tok(ROW_WORDS), grp(LANES),
                  pl.BlockSpec((1, 1, D_MODEL), lambda b, i: (b + b0 + src0, 0, 0))]
        + [pl.BlockSpec(memory_space=pl.ANY)] * len(extra),
        out_specs=pl.BlockSpec((1, tm, D_MODEL), lambda b, i: (b + b0 + src0, i, 0)),
        input_output_aliases={} if out_prev is None else {5: 0},
        compiler_params=pltpu.CompilerParams(
            dimension_semantics=("arbitrary", "arbitrary"), vmem_limit_bytes=VMEM_LIMIT),
        name="final",
    )(x1, g1, g2, rw, gate2, *extra)


def _pick(n, pref):
    t = min(n, pref)
    assert n % t == 0, (n, t)
    return t


def _slots_kernel(offs_ref, r_ref, o_ref):
    r = r_ref[0]
    base = jnp.zeros_like(r)
    for e in range(N_EXPERTS):
        base = jnp.where(r == e, offs_ref[e], base)
    o_ref[0] = base + pltpu.roll(r, 6, 0)


def _slots(ridx, offs):
    B, _, S = ridx.shape
    return pl.pallas_call(
        _slots_kernel,
        out_shape=jax.ShapeDtypeStruct((B, 8, S), jnp.int32),
        grid_spec=pltpu.PrefetchScalarGridSpec(
            num_scalar_prefetch=1, grid=(B,),
            in_specs=[pl.BlockSpec((1, 8, S), lambda b, offs: (b, 0, 0))],
            out_specs=pl.BlockSpec((1, 8, S), lambda b, offs: (b, 0, 0))),
        compiler_params=pltpu.CompilerParams(dimension_semantics=("arbitrary",)),
        name="slots",
    )(offs, ridx)


def _route_plan(ridx, counts, tm_e, T):
    counts = counts.astype(jnp.int32)
    tiles = (counts + tm_e - 1) // tm_e
    tile_end = jnp.cumsum(tiles)
    offs = (tile_end - tiles) * tm_e
    slots = _slots(ridx, offs)
    n_chunks = T // (SC_WORKERS * SC_CHUNK)
    idx1 = slots[:, 0, :].reshape(SC_WORKERS, n_chunks, SC_CHUNK)
    idx2 = slots[:, 1, :].reshape(SC_WORKERS, n_chunks, SC_CHUNK)
    n_tiles_max = 2 * T // tm_e + N_EXPERTS
    tile_ids = jnp.arange(n_tiles_max, dtype=jnp.int32)
    tile_expert = jnp.sum((tile_end[None, :] <= tile_ids[:, None]).astype(jnp.int32), axis=1)
    tile_expert = jnp.minimum(tile_expert, N_EXPERTS - 1)
    return idx1, idx2, tile_expert, tile_end[-1:].astype(jnp.int32), n_tiles_max * tm_e


def _layer(x, c, w_ada, b_ada, norm1_w, w_in, b_forget, conv_w, q_norm_w, k_norm_w,
           w_out_conv, w_out_attn, w_o, norm2_w, w_rg, b_rg, w_re, b_re, w_gate, w_up, w_down):
    B, S, _ = x.shape
    n_grp = 2 if B % 2 == 0 and (B // 2 * S) % (SC_WORKERS * SC_CHUNK) == 0 else 1
    Bg = B // n_grp
    T = Bg * S
    assert T % (SC_WORKERS * SC_CHUNK) == 0, T
    mod = _ada(c, w_ada, b_ada.reshape(1, -1)).reshape(B, 6, 1, D_MODEL)
    shift1, scale1, gate1, shift2, scale2, gate2 = (mod[:, t] for t in range(6))

    cuts = np.cumsum([0, CONV_WIDTH, CONV_WIDTH, CONV_WIDTH, ATTN_WIDTH, ATTN_WIDTH, ATTN_WIDTH,
                      N_HEADS, D_MODEL, D_MODEL])
    w_conv3 = w_in[:, cuts[0]:cuts[3]]
    w_qvT = jnp.concatenate([w_in[:, cuts[3]:cuts[4]], w_in[:, cuts[5]:cuts[6]]], axis=1).T.astype(BF16)
    w_k = w_in[:, cuts[4]:cuts[5]].astype(BF16)
    w_f = jnp.pad(w_in[:, cuts[6]:cuts[7]], ((0, 0), (0, LANES - N_HEADS))).astype(BF16)
    b_f = jnp.pad(b_forget, (0, LANES - N_HEADS)).reshape(1, LANES)
    w_cgg = jnp.concatenate([w_conv3, w_in[:, cuts[7]:cuts[9]]], axis=1).astype(BF16)

    tm_qkv = _pick(S, 1024)
    qnwT = jnp.broadcast_to((jnp.tile(q_norm_w, N_HEADS) * (LOG2E * HEAD_DIM ** -0.5))[:, None],
                            (ATTN_WIDTH, LANES))
    tq = _pick(S // 2, 512)
    w_r = jnp.pad(jnp.concatenate([w_re, w_rg], axis=1),
                  ((0, 0), (0, LANES - N_EXPERTS - N_GROUPS)))
    w_r_hi = w_r.astype(BF16)
    w_r_lo = (w_r - w_r_hi.astype(F32)).astype(BF16)
    b_r = jnp.pad(jnp.concatenate([b_re, b_rg]), (0, LANES - N_EXPERTS - N_GROUPS)).reshape(1, LANES)
    cw = jnp.pad(conv_w, ((0, 8 - CONV_K), (0, 0)))
    tm_post = _pick(S, 1024)
    tm_e = 512
    w_oc, w_oa, w_ob, w_rs = (w_out_conv.astype(BF16), w_out_attn.astype(BF16), w_o.astype(BF16),
                              jnp.concatenate([w_r_hi, w_r_lo], axis=1))

    n_half = 2 if Bg % 2 == 0 and (Bg // 2 * S) % (SC_WORKERS * SC_CHUNK) == 0 else 1
    Bh = Bg // n_half

    def experts(grp, after):
        x1, rw, idx1, idx2, tile_expert, n_tiles, xs = grp
        ys = _moe(xs, tile_expert, n_tiles, w_gate, w_up, w_down, tm_e, after)
        halves = lambda idx: idx.reshape(n_half, SC_WORKERS, idx.shape[1] // n_half, SC_CHUNK)
        return ys, [_collect(ys, i1, i2) for i1, i2 in zip(halves(idx1), halves(idx2))]

    def finish(g, grp, gathered, out, after):
        for hh, (g1, g2) in enumerate(gathered):
            out = _final(grp[0], g1.reshape(Bh, S, ROW_WORDS), g2.reshape(Bh, S, ROW_WORDS), grp[1],
                         gate2, _pick(S, 512), g * Bg, hh * Bh, out, after if hh == 0 else [])
        return out

    out, prev, prev_gathered, ys_prev = None, None, None, None
    for g in range(n_grp):
        b0 = g * Bg
        qT_aug, k_aug, vT = _qkv(x, norm1_w.reshape(1, -1), shift1, scale1, w_qvT, w_k, w_f, b_f,
                                 qnwT, jnp.tile(k_norm_w, N_HEADS).reshape(1, -1), tm_qkv, b0, Bg)
        if prev is not None:
            ys_prev, prev_gathered = experts(prev, [qT_aug])
        y_b = _attention(qT_aug, k_aug, vT, tq, _pick(tq, 256),
                         [] if ys_prev is None else [ys_prev])
        x1, h2p, ridx, rw, counts = _post(x, y_b, norm1_w.reshape(1, -1), shift1, scale1, gate1,
                                          norm2_w.reshape(1, -1), shift2, scale2,
                                          w_cgg, cw, w_oc, w_oa, w_ob, w_rs, b_r, tm_post, b0)
        idx1, idx2, tile_expert, n_tiles, n_slots = _route_plan(ridx, counts[0, :N_EXPERTS], tm_e, T)
        xs = _dispatch(h2p.reshape(T, ROW_WORDS), idx1, idx2, n_slots)
        if prev is not None:
            out = finish(g - 1, prev, prev_gathered, out, [idx1])
        prev = (x1, rw, idx1, idx2, tile_expert, n_tiles, xs)
    _, last_gathered = experts(prev, [] if out is None else [out])
    return finish(n_grp - 1, prev, last_gathered, out, [])


def kernel(x, c, w_ada, b_ada, norm1_w, w_in, b_forget, conv_w, q_norm_w, k_norm_w, w_out_conv,
           w_out_attn, w_o, norm2_w, w_router_group, b_router_group, w_router_expert,
           b_router_expert, w_gate, w_up, w_down):
    for l in range(w_ada.shape[0]):
        x = _layer(x, c, w_ada[l], b_ada[l], norm1_w[l], w_in[l], b_forget[l], conv_w[l],
                   q_norm_w[l], k_norm_w[l], w_out_conv[l], w_out_attn[l], w_o[l], norm2_w[l],
                   w_router_group[l], b_router_group[l], w_router_expert[l], b_router_expert[l],
                   w_gate[l], w_up[l], w_down[l])
    return x
```

```python
import functools

import jax
import jax.numpy as jnp
import numpy as np
from jax import lax
from jax.experimental import pallas as pl
from jax.experimental.pallas import tpu as pltpu
from jax.experimental.pallas import tpu_sc as plsc

D_MODEL = 1024
CONV_WIDTH = 512
CONV_K = 3
N_HEADS = 8
HEAD_DIM = 64
ATTN_WIDTH = N_HEADS * HEAD_DIM
N_PAIRS = N_HEADS // 2
N_GROUPS = 4
EXPERTS_PER_GROUP = 8
N_EXPERTS = N_GROUPS * EXPERTS_PER_GROUP
D_EXPERT = 256
EPS = 1e-6
LANES = 128
AUG = 2 * LANES
BIAS_W = 6
VROWS = HEAD_DIM + 16
NEG = -1e30
LOG2E = 1.4426950408889634

F32 = jnp.float32
BF16 = jnp.bfloat16
VMEM_LIMIT = 56 * 1024 * 1024


def _sigmoid(z):
    return 1.0 / (1.0 + jnp.exp(-z))


def _split3(z):
    hi = z.astype(BF16)
    r = z - hi.astype(F32)
    mid = r.astype(BF16)
    lo = (r - mid.astype(F32)).astype(BF16)
    return hi, mid, lo


def _dot(a, b):
    return jnp.dot(a, b, preferred_element_type=F32)


def _modulated_norm(x, nw, shift, scale):
    ms = jnp.mean(x * x, axis=-1, keepdims=True)
    return (x * lax.rsqrt(ms + EPS) * nw) * (1.0 + scale) + shift


def _ada_kernel(c_ref, w_ref, b_ref, o_ref):
    c = c_ref[...]
    a = c * _sigmoid(c)
    o_ref[...] = jnp.dot(a, w_ref[...], precision=lax.Precision.HIGHEST,
                         preferred_element_type=F32) + b_ref[...]


def _ada(c, w_ada, b_ada):
    B = c.shape[0]
    n = w_ada.shape[1] // D_MODEL
    return pl.pallas_call(
        _ada_kernel,
        out_shape=jax.ShapeDtypeStruct((B, n * D_MODEL), F32),
        grid=(n,),
        in_specs=[pl.BlockSpec((B, D_MODEL), lambda j: (0, 0)),
                  pl.BlockSpec((D_MODEL, D_MODEL), lambda j: (0, j)),
                  pl.BlockSpec((1, D_MODEL), lambda j: (0, j))],
        out_specs=pl.BlockSpec((B, D_MODEL), lambda j: (0, j)),
        compiler_params=pltpu.CompilerParams(dimension_semantics=("arbitrary",)),
        name="ada",
    )(c, w_ada, b_ada)


QKV_ROWS = 256
_NT = (((1,), (1,)), ((), ()))


def _lane_tile(a, width):
    return jnp.concatenate([a] * (width // a.shape[1]), axis=1)


def _qkv_kernel(x_ref, nw_ref, sh_ref, sc_ref, wqv_ref, wk_ref, wf_ref, bf_ref, qnw_ref, knw_ref,
                gsum_ref, pq_ref, pk_ref, cq_ref, ck_ref,
                qT_ref, k_ref, vT_ref, carry_ref):
    tm = x_ref.shape[1]
    n_grp = max(tm // QKV_ROWS, 1)
    rows = tm // n_grp
    row = lax.broadcasted_iota(jnp.int32, (rows, LANES), 0)

    @pl.when(pl.program_id(1) == 0)
    def _():
        carry_ref[...] = jnp.zeros_like(carry_ref)

    st = [dict(rs=pl.ds(g * rows, rows)) for g in range(n_grp)]

    def project(d):
        h = _modulated_norm(x_ref[0, d["rs"], :], nw_ref[...], sh_ref[0], sc_ref[0])
        hb = h.astype(BF16)
        d["qvT"] = lax.dot_general(wqv_ref[...], hb, _NT, preferred_element_type=F32)
        d["k"] = _dot(hb, wk_ref[...])
        d["fl"] = _dot(hb, wf_ref[...]) + bf_ref[...]

    def norms(d):
        qvT = d["qvT"]
        heads = []
        for hd in range(N_HEADS):
            z = qvT[hd * HEAD_DIM:(hd + 1) * HEAD_DIM]
            heads.append(z * lax.rsqrt(jnp.mean(z * z, axis=0, keepdims=True) + EPS))
        qnT = jnp.concatenate(heads, axis=0) * _lane_tile(qnw_ref[...], rows)
        for j in range(N_PAIRS):
            qT_ref[0, j, :LANES, d["rs"]] = qnT[j * LANES:(j + 1) * LANES].astype(BF16)
            for t in range(2):
                r0 = ATTN_WIDTH + (2 * j + t) * HEAD_DIM
                vT_ref[0, j, t * VROWS:t * VROWS + HEAD_DIM, d["rs"]] = qvT[r0:r0 + HEAD_DIM].astype(BF16)
                vT_ref[0, j, t * VROWS + HEAD_DIM:(t + 1) * VROWS, d["rs"]] = \
                    jnp.ones((VROWS - HEAD_DIM, rows), BF16)
        del d["qvT"]
        k = d.pop("k")
        ss = _dot((k * k).astype(BF16), gsum_ref[...])
        kn = k * lax.rsqrt(ss * (1.0 / HEAD_DIM) + EPS) * knw_ref[...]
        for j in range(N_PAIRS):
            k_ref[0, d["rs"], j * AUG:j * AUG + LANES] = kn[:, j * LANES:(j + 1) * LANES].astype(BF16)

    def forget(d, before):
        fl = d.pop("fl")
        cum = jnp.minimum(fl, 0.0) - jnp.log(1.0 + jnp.exp(-jnp.abs(fl)))
        s = 1
        while s < rows:
            cum = cum + jnp.where(row >= s, pltpu.roll(cum, s, 0), 0.0)
            s *= 2
        cum = cum + before[7:8, :]
        d["tail"] = cum[rows - 8:, :]
        d["parts"] = jnp.concatenate(_split3(cum * LOG2E), axis=1)

    def bias(d):
        parts = d.pop("parts")
        eqT = (lax.dot_general(pq_ref[...], parts, _NT, preferred_element_type=F32)
               + _lane_tile(cq_ref[...], rows)).astype(BF16)
        ek = (_dot(parts, pk_ref[...]) + ck_ref[...]).astype(BF16)
        for j in range(N_PAIRS):
            qT_ref[0, j, LANES:, d["rs"]] = eqT
            k_ref[0, d["rs"], j * AUG + LANES:(j + 1) * AUG] = ek

    stages = [project, norms, None, bias]
    lag = 1
    for step in range(len(stages) + lag * (n_grp - 1)):
        for g, d in enumerate(st):
            kk = step - lag * g
            if 0 <= kk < len(stages):
                if stages[kk] is None:
                    forget(d, carry_ref[...] if g == 0 else st[g - 1]["tail"])
                else:
                    stages[kk](d)
    carry_ref[...] = st[-1]["tail"]


def _bias_placement():
    pq = np.zeros((LANES, 3 * LANES), np.float32)
    pk = np.zeros((3 * LANES, LANES), np.float32)
    cq = np.zeros((LANES, LANES), np.float32)
    ck = np.zeros((1, LANES), np.float32)
    for hd in range(N_HEADS):
        base = BIAS_W * hd
        for p in range(3):
            pq[base + p, p * LANES + hd] = 1.0
            pk[p * LANES + hd, base + 3 + p] = -1.0
            cq[base + 3 + p, :] = 1.0
            ck[0, base + p] = 1.0
    return (jnp.asarray(pq, BF16), jnp.asarray(pk, BF16), jnp.asarray(cq), jnp.asarray(ck))


def _qkv(x, nw, shift, scale, wqvT, wk, wf, bf, qnwT, knw, tm, b0, B):
    S = x.shape[1]
    gsum = jnp.asarray(np.kron(np.eye(N_HEADS), np.ones((HEAD_DIM, HEAD_DIM))), BF16)
    pq, pk, cq, ck = _bias_placement()
    const = lambda *shape: pl.BlockSpec(shape, lambda b, i: (0,) * len(shape),
                                        pipeline_mode=pl.Buffered(1))
    return pl.pallas_call(
        _qkv_kernel,
        out_shape=(jax.ShapeDtypeStruct((B, N_PAIRS, AUG, S), BF16),
                   jax.ShapeDtypeStruct((B, S, N_PAIRS * AUG), BF16),
                   jax.ShapeDtypeStruct((B, N_PAIRS, 2 * VROWS, S), BF16)),
        grid=(B, S // tm),
        in_specs=[pl.BlockSpec((1, tm, D_MODEL), lambda b, i: (b + b0, i, 0)),
                  const(1, D_MODEL),
                  pl.BlockSpec((1, 1, D_MODEL), lambda b, i: (b + b0, 0, 0)),
                  pl.BlockSpec((1, 1, D_MODEL), lambda b, i: (b + b0, 0, 0)),
                  const(2 * ATTN_WIDTH, D_MODEL),
                  const(D_MODEL, ATTN_WIDTH),
                  const(D_MODEL, LANES),
                  const(1, LANES),
                  const(ATTN_WIDTH, LANES),
                  const(1, ATTN_WIDTH),
                  const(ATTN_WIDTH, ATTN_WIDTH),
                  const(LANES, 3 * LANES),
                  const(3 * LANES, LANES),
                  const(LANES, LANES),
                  const(1, LANES)],
        out_specs=(pl.BlockSpec((1, N_PAIRS, AUG, tm), lambda b, i: (b, 0, 0, i)),
                   pl.BlockSpec((1, tm, N_PAIRS * AUG), lambda b, i: (b, i, 0)),
                   pl.BlockSpec((1, N_PAIRS, 2 * VROWS, tm), lambda b, i: (b, 0, 0, i))),
        scratch_shapes=[pltpu.VMEM((8, LANES), F32)],
        compiler_params=pltpu.CompilerParams(
            dimension_semantics=("arbitrary", "arbitrary"), vmem_limit_bytes=VMEM_LIMIT),
        name="qkv",
    )(x, nw, shift, scale, wqvT, wk, wf, bf, qnwT, knw, gsum, pq, pk, cq, ck)


def _attn_kernel(qT_ref, k_ref, vT_ref, *rest, tq, cw, nt):
    o_ref, qq_ref, s_ref, smax_ref, m_ref, acc_ref = rest[-6:]
    i = pl.program_id(2)
    n = tq // cw
    chains = [(a, t, c) for a in range(nt) for t in range(2) for c in range(n)]
    feat = lax.broadcasted_iota(jnp.int32, (AUG, tq), 0)
    for t in range(2):
        bias0 = LANES + BIAS_W * (2 * pl.program_id(1) + t)
        keep = ((feat >= t * HEAD_DIM) & (feat < (t + 1) * HEAD_DIM)) | \
               ((feat >= bias0) & (feat < bias0 + BIAS_W))
        for a in range(nt):
            qT = qT_ref[0, 0, :, a * tq:(a + 1) * tq]
            qh = jnp.where(keep, qT, jnp.zeros_like(qT))
            for c in range(n):
                qq_ref[chains.index((a, t, c))] = qh[:, c * cw:(c + 1) * cw]
    kpos = lax.broadcasted_iota(jnp.int32, (tq, cw), 0)
    qpos = lax.broadcasted_iota(jnp.int32, (tq, cw), 1)

    def scores(j, slot, which):
        k_blk = k_ref[0, pl.ds(pl.multiple_of(j * tq, tq), tq), :]
        for ci in which:
            s = _dot(k_blk, qq_ref[ci])
            s_ref[slot, ci] = s
            smax_ref[slot, ci] = jnp.broadcast_to(jnp.max(s, axis=0, keepdims=True), (8, cw))

    def absorb(j, slot, which, diagonal=()):
        start = pl.multiple_of(j * tq, tq)
        for ci in which:
            _, t, c = chains[ci]
            vj = vT_ref[0, 0, t * VROWS:(t + 1) * VROWS, pl.ds(start, tq)]
            s = s_ref[slot, ci]
            if ci in diagonal:
                s = jnp.where(kpos <= qpos + c * cw, s, NEG)
                smax = jnp.max(s, axis=0, keepdims=True)
            else:
                smax = smax_ref[slot, ci, 0:1]
            m = m_ref[ci, 0:1]
            m_new = jnp.maximum(m, smax)
            p = jnp.exp2(s - m_new).astype(BF16)
            acc_ref[ci] = jnp.exp2(m - m_new) * acc_ref[ci] + _dot(vj, p)
            m_ref[ci] = jnp.broadcast_to(m_new, (8, cw))

    every = list(range(len(chains)))
    from_tile = lambda r: [ci for ci in every if chains[ci][0] >= r]
    m_ref[...] = jnp.full(m_ref.shape, NEG, F32)
    acc_ref[...] = jnp.zeros(acc_ref.shape, F32)
    scores(0, 0, every)

    def two_blocks(jj, _):
        j = 2 * jj
        scores(j + 1, 1, every)
        absorb(j, 0, every)
        scores(j + 2, 0, every)
        absorb(j + 1, 1, every)
        return 0

    lax.fori_loop(0, i * (nt // 2), two_blocks, 0)
    for r in range(nt):
        if r + 1 < nt:
            scores(nt * i + r + 1, (r + 1) % 2, from_tile(r + 1))
        absorb(nt * i + r, r % 2, from_tile(r),
               diagonal=[ci for ci in every if chains[ci][0] == r])

    for a in range(nt):
        outs = [acc_ref[ci, :HEAD_DIM] / acc_ref[ci, HEAD_DIM:HEAD_DIM + 1]
                for ci in every if chains[ci][0] == a]
        oT = jnp.concatenate([jnp.concatenate(outs[:n], axis=1), jnp.concatenate(outs[n:], axis=1)],
                             axis=0)
        o_ref[0, a * tq:(a + 1) * tq, :] = oT.T.astype(BF16)


def _attention(qT_aug, k_aug, vT, tq, cw, after):
    B, S, _ = k_aug.shape
    nt = 4 if S % (4 * tq) == 0 else 2
    n_chains = 2 * nt * tq // cw
    return pl.pallas_call(
        functools.partial(_attn_kernel, tq=tq, cw=cw, nt=nt),
        scratch_shapes=[pltpu.VMEM((n_chains, AUG, cw), BF16),
                        pltpu.VMEM((2, n_chains, tq, cw), F32),
                        pltpu.VMEM((2, n_chains, 8, cw), F32),
                        pltpu.VMEM((n_chains, 8, cw), F32),
                        pltpu.VMEM((n_chains, VROWS, cw), F32)],
        out_shape=jax.ShapeDtypeStruct((B, S, ATTN_WIDTH), BF16),
        grid=(B, N_PAIRS, S // (nt * tq)),
        in_specs=[pl.BlockSpec((1, 1, AUG, nt * tq), lambda b, j, i: (b, j, 0, i)),
                  pl.BlockSpec((1, S, AUG), lambda b, j, i: (b, 0, j)),
                  pl.BlockSpec((1, 1, 2 * VROWS, S), lambda b, j, i: (b, j, 0, 0))]
        + [pl.BlockSpec(memory_space=pl.ANY)] * len(after),
        out_specs=pl.BlockSpec((1, nt * tq, LANES), lambda b, j, i: (b, i, j)),
        compiler_params=pltpu.CompilerParams(
            dimension_semantics=("arbitrary", "arbitrary", "arbitrary"),
            vmem_limit_bytes=VMEM_LIMIT),
        name="attn",
    )(qT_aug, k_aug, vT, *after)


def _pack_bf16_pairs(z):
    w = z.shape[1] // 2
    bits = pltpu.bitcast(z.astype(BF16).astype(F32), jnp.uint32)
    return bits[:, :w] | (bits[:, w:] >> 16)


def _unpack_bf16_pairs(p):
    return (pltpu.bitcast(p & jnp.uint32(0xFFFF0000), F32), pltpu.bitcast(p << 16, F32))


POST_ROWS = 256


def _post_kernel(x_ref, yb_ref, n1_ref, sh1_ref, sc1_ref, g1_ref, n2_ref, sh2_ref, sc2_ref,
                 wc_ref, cw_ref, woc_ref, woa_ref, wo_ref, wr_ref, br_ref, tri_ref,
                 x1_ref, h2_ref, ridx_ref, rw_ref, cnt_ref, carry_ref):
    tm = x_ref.shape[1]
    n_grp = max(tm // POST_ROWS, 1)
    rows = tm // n_grp
    lane = lax.broadcasted_iota(jnp.int32, (rows, LANES), 1)
    row8 = lax.broadcasted_iota(jnp.int32, (8, CONV_WIDTH), 0)
    big = jnp.int32(1 << 20)

    @pl.when(pl.program_id(1) == 0)
    def _():
        carry_ref[...] = jnp.zeros_like(carry_ref)

    @pl.when((pl.program_id(0) == 0) & (pl.program_id(1) == 0))
    def _():
        cnt_ref[...] = jnp.zeros_like(cnt_ref)

    st = [dict(rs=pl.ds(g * rows, rows)) for g in range(n_grp)]

    def conv_in(d):
        d["x"] = x_ref[0, d["rs"], :]
        d["hb"] = _modulated_norm(d["x"], n1_ref[...], sh1_ref[0], sc1_ref[0]).astype(BF16)
        d["x_in"] = _dot(d["hb"], wc_ref[:, :CONV_WIDTH])
        d["conv_c"] = _dot(d["hb"], wc_ref[:, 2 * CONV_WIDTH:3 * CONV_WIDTH])
        d["conv_b"] = _dot(d["hb"], wc_ref[:, CONV_WIDTH:2 * CONV_WIDTH])

    def conv(d, prev):
        u = d.pop("conv_c") * d.pop("x_in")
        d["u_tail"] = u[rows - 8:, :]

        def shifted(k):
            r = pltpu.roll(u, k, 0)
            top = jnp.where(row8 < k, pltpu.roll(prev, k, 0), r[:8])
            return jnp.concatenate([top, r[8:]], axis=0)

        cw = cw_ref[...]
        cv = cw[0:1] * shifted(2) + cw[1:2] * shifted(1) + cw[2:3] * u
        d["y_a"] = (d.pop("conv_b") * cv).astype(BF16)

    def gates(d):
        d["p_b"] = _dot(yb_ref[0, d["rs"], :], woa_ref[...])
        d["gate_c"] = _dot(d["hb"], wc_ref[:, 3 * CONV_WIDTH:3 * CONV_WIDTH + D_MODEL])
        d["gate_a"] = _dot(d.pop("hb"), wc_ref[:, 3 * CONV_WIDTH + D_MODEL:])

    def branch_a(d):
        d["p_a"] = _dot(d.pop("y_a"), woc_ref[...])

    def merge(d):
        d["merged"] = (_sigmoid(d.pop("gate_c")) * d.pop("p_a")
                       + _sigmoid(d.pop("gate_a")) * d.pop("p_b")).astype(BF16)

    def out_proj(d):
        d["o"] = _dot(d.pop("merged"), wo_ref[...])

    def residual(d):
        x1 = d.pop("x") + g1_ref[0] * d.pop("o")
        x1_ref[0, d["rs"], :] = x1
        h2 = _modulated_norm(x1, n2_ref[...], sh2_ref[0], sc2_ref[0])
        h2_ref[0, d["rs"], :] = _pack_bf16_pairs(h2)
        d["h_hi"] = h2.astype(BF16)
        d["h_lo"] = (h2 - d["h_hi"].astype(F32)).astype(BF16)

    def router(d):
        both = _dot(d.pop("h_hi"), wr_ref[...])
        d["lg"] = (both[:, :LANES] + both[:, LANES:] + _dot(d.pop("h_lo"), wr_ref[:, :LANES])) \
            + br_ref[...]

    def first_argmax(vals):
        mx = jnp.max(vals, axis=-1, keepdims=True)
        idx = jnp.min(jnp.where(vals == mx, lane, big), axis=-1, keepdims=True)
        return mx, idx

    def route(d):
        lg = d.pop("lg")
        is_g = (lane >= N_EXPERTS) & (lane < N_EXPERTS + N_GROUPS)
        g_mx, g_lane = first_argmax(jnp.where(is_g, lg, NEG))
        p_sel = 1.0 / jnp.sum(jnp.where(is_g, jnp.exp(lg - g_mx), 0.0), axis=-1, keepdims=True)
        g_idx = g_lane - N_EXPERTS
        in_g = (lane >= g_idx * EXPERTS_PER_GROUP) & (lane < (g_idx + 1) * EXPERTS_PER_GROUP)
        le = jnp.where(in_g, lg, NEG)
        v1, i1 = first_argmax(le)
        v2, i2 = first_argmax(jnp.where(lane == i1, NEG, le))
        e2 = jnp.exp(v2 - v1)
        w1 = p_sel / (1.0 + e2)
        w2 = w1 * e2
        rw_ref[0, d["rs"], :] = jnp.where(lane == 0, w1, 0.0) + jnp.where(lane == 1, w2, 0.0)
        d["i1"], d["i2"] = i1, i2
        d["onehot"] = jnp.where((lane == i1) | (lane == i2), 1.0, 0.0)

    def rank(d):
        onehot = d.pop("onehot")
        before = _dot(tri_ref[...], onehot.astype(BF16)) + cnt_ref[0:1, :]
        cnt_ref[...] = cnt_ref[...] + jnp.sum(onehot, axis=0, keepdims=True)
        i1, i2 = d.pop("i1"), d.pop("i2")
        r1 = jnp.sum(jnp.where(lane == i1, before, 0.0), axis=-1, keepdims=True)
        r2 = jnp.sum(jnp.where(lane == i2, before, 0.0), axis=-1, keepdims=True)
        rec = (jnp.where(lane == 0, i1.astype(F32), 0.0) + jnp.where(lane == 1, i2.astype(F32), 0.0)
               + jnp.where(lane == 2, r1, 0.0) + jnp.where(lane == 3, r2, 0.0))
        ridx_ref[0, :, d["rs"]] = rec.T[:8].astype(jnp.int32)

    stages = [conv_in, None, gates, branch_a, merge, out_proj, residual, router, route, rank]
    lag = 2
    for step in range(len(stages) + lag * (n_grp - 1)):
        for g, d in enumerate(st):
            k = step - lag * g
            if 0 <= k < len(stages):
                if stages[k] is None:
                    conv(d, carry_ref[...] if g == 0 else st[g - 1]["u_tail"])
                else:
                    stages[k](d)
    carry_ref[...] = st[-1]["u_tail"]


def _post(x, yb, n1, sh1, sc1, g1, n2, sh2, sc2, wc, cw, woc, woa, wo, wr, br, tm, b0):
    B, S, _ = yb.shape
    rows = tm // max(tm // POST_ROWS, 1)
    tri = jnp.asarray(np.tril(np.ones((rows, rows), np.float32), -1), BF16)
    const = lambda *shape: pl.BlockSpec(shape, lambda b, i: (0,) * len(shape),
                                        pipeline_mode=pl.Buffered(1))
    perb = pl.BlockSpec((1, 1, D_MODEL), lambda b, i: (b + b0, 0, 0))
    tok = lambda w: pl.BlockSpec((1, tm, w), lambda b, i: (b, i, 0))
    return pl.pallas_call(
        _post_kernel,
        out_shape=(jax.ShapeDtypeStruct((B, S, D_MODEL), F32),
                   jax.ShapeDtypeStruct((B, S, D_MODEL // 2), jnp.uint32),
                   jax.ShapeDtypeStruct((B, 8, S), jnp.int32),
                   jax.ShapeDtypeStruct((B, S, LANES), F32),
                   jax.ShapeDtypeStruct((8, LANES), F32)),
        grid=(B, S // tm),
        in_specs=[pl.BlockSpec((1, tm, D_MODEL), lambda b, i: (b + b0, i, 0)), tok(ATTN_WIDTH),
                  const(1, D_MODEL), perb, perb, perb,
                  const(1, D_MODEL), perb, perb,
                  const(D_MODEL, 3 * CONV_WIDTH + 2 * D_MODEL),
                  const(8, CONV_WIDTH),
                  const(CONV_WIDTH, D_MODEL), const(ATTN_WIDTH, D_MODEL),
                  const(D_MODEL, D_MODEL),
                  const(D_MODEL, 2 * LANES), const(1, LANES), const(rows, rows)],
        out_specs=(tok(D_MODEL), tok(D_MODEL // 2),
                   pl.BlockSpec((1, 8, tm), lambda b, i: (b, 0, i)), tok(LANES),
                   pl.BlockSpec((8, LANES), lambda b, i: (0, 0))),
        scratch_shapes=[pltpu.VMEM((8, CONV_WIDTH), F32)],
        compiler_params=pltpu.CompilerParams(
            dimension_semantics=("arbitrary", "arbitrary"), vmem_limit_bytes=VMEM_LIMIT),
        name="post",
    )(x, yb, n1, sh1, sc1, g1, n2, sh2, sc2, wc, cw, woc, woa, wo, wr, br, tri)


SC_CORES = 2
SC_SUBCORES = 16
SC_WORKERS = SC_CORES * SC_SUBCORES
SC_CHUNK = 64
ROW_WORDS = D_MODEL // 2


def _sc_mesh():
    return plsc.VectorSubcoreMesh(core_axis_name="c", subcore_axis_name="s",
                                  num_cores=SC_CORES, num_subcores=SC_SUBCORES)


def _dispatch_body(rows_hbm, idx1_hbm, idx2_hbm, xs_hbm, idx1_v, idx2_v, rows_v, *, n_chunks):
    wid = lax.axis_index("s") * SC_CORES + lax.axis_index("c")
    pltpu.sync_copy(idx1_hbm.at[wid], idx1_v)
    pltpu.sync_copy(idx2_hbm.at[wid], idx2_v)
    base = wid * (n_chunks * SC_CHUNK)

    @pl.loop(0, n_chunks)
    def _(j):
        pltpu.sync_copy(rows_hbm.at[pl.ds(base + j * SC_CHUNK, SC_CHUNK)], rows_v)
        pltpu.sync_copy(rows_v, xs_hbm.at[idx1_v.at[j]])
        pltpu.sync_copy(rows_v, xs_hbm.at[idx2_v.at[j]])


def _sc_scratch(n_chunks):
    return [pltpu.VMEM((n_chunks, SC_CHUNK), jnp.int32), pltpu.VMEM((n_chunks, SC_CHUNK), jnp.int32),
            pltpu.VMEM((SC_CHUNK, ROW_WORDS), jnp.uint32)]


def _dispatch(rows, idx1, idx2, n_slots):
    n_chunks = idx1.shape[1]
    return pl.kernel(
        functools.partial(_dispatch_body, n_chunks=n_chunks),
        out_type=jax.ShapeDtypeStruct((n_slots, ROW_WORDS), jnp.uint32),
        mesh=_sc_mesh(),
        scratch_types=_sc_scratch(n_chunks),
        name="dispatch",
    )(rows, idx1, idx2)


def _collect_body(ys_hbm, idx1_hbm, idx2_hbm, g1_hbm, g2_hbm, idx1_v, idx2_v, rows_v, *, n_chunks):
    wid = lax.axis_index("s") * SC_CORES + lax.axis_index("c")
    pltpu.sync_copy(idx1_hbm.at[wid], idx1_v)
    pltpu.sync_copy(idx2_hbm.at[wid], idx2_v)
    base = wid * (n_chunks * SC_CHUNK)

    @pl.loop(0, n_chunks)
    def _(j):
        dst = pl.ds(base + j * SC_CHUNK, SC_CHUNK)
        pltpu.sync_copy(ys_hbm.at[idx1_v.at[j]], rows_v)
        pltpu.sync_copy(rows_v, g1_hbm.at[dst])
        pltpu.sync_copy(ys_hbm.at[idx2_v.at[j]], rows_v)
        pltpu.sync_copy(rows_v, g2_hbm.at[dst])


def _collect(ys, idx1, idx2):
    n_chunks = idx1.shape[1]
    out = jax.ShapeDtypeStruct((SC_WORKERS * n_chunks * SC_CHUNK, ROW_WORDS), jnp.uint32)
    return pl.kernel(
        functools.partial(_collect_body, n_chunks=n_chunks),
        out_type=(out, out),
        mesh=_sc_mesh(),
        scratch_types=_sc_scratch(n_chunks),
        name="collect",
    )(ys, idx1, idx2)


def _moe_kernel(te_ref, nt_ref, xs_ref, wg_ref, wu_ref, wd_ref, *rest):
    ys_ref, wgb_ref, wub_ref, wdb_ref = rest[-4:]
    i = pl.program_id(0)
    active = i < nt_ref[0]

    @pl.when(active & ((i == 0) | (te_ref[i] != te_ref[jnp.maximum(i - 1, 0)])))
    def _():
        wgb_ref[...] = wg_ref[0].astype(BF16)
        wub_ref[...] = wu_ref[0].astype(BF16)
        wdb_ref[...] = wd_ref[0].astype(BF16)

    @pl.when(active)
    def _():
        left, right = _unpack_bf16_pairs(xs_ref[...])
        xb = jnp.concatenate([left.astype(BF16), right.astype(BF16)], axis=1)
        g = _dot(xb, wgb_ref[...])
        u = _dot(xb, wub_ref[...])
        a = (g * _sigmoid(g) * u).astype(BF16)
        ys_ref[...] = _pack_bf16_pairs(_dot(a, wdb_ref[...]))


def _moe(xs, tile_expert, n_tiles, wg, wu, wd, tm, after):
    n_slots = xs.shape[0]
    row_blk = lambda i, te, nt: (jnp.minimum(i, nt[0] - 1), 0)
    w_blk = lambda i, te, nt: (te[i], 0, 0)
    return pl.pallas_call(
        _moe_kernel,
        out_shape=jax.ShapeDtypeStruct((n_slots, ROW_WORDS), jnp.uint32),
        grid_spec=pltpu.PrefetchScalarGridSpec(
            num_scalar_prefetch=2,
            grid=(n_slots // tm,),
            in_specs=[pl.BlockSpec((tm, ROW_WORDS), row_blk),
                      pl.BlockSpec((1, D_MODEL, D_EXPERT), w_blk),
                      pl.BlockSpec((1, D_MODEL, D_EXPERT), w_blk),
                      pl.BlockSpec((1, D_EXPERT, D_MODEL), w_blk)]
            + [pl.BlockSpec(memory_space=pl.ANY)] * len(after),
            out_specs=pl.BlockSpec((tm, ROW_WORDS), row_blk),
            scratch_shapes=[pltpu.VMEM((D_MODEL, D_EXPERT), BF16), pltpu.VMEM((D_MODEL, D_EXPERT), BF16),
                            pltpu.VMEM((D_EXPERT, D_MODEL), BF16)]),
        compiler_params=pltpu.CompilerParams(
            dimension_semantics=("arbitrary",), vmem_limit_bytes=VMEM_LIMIT),
        name="moe",
    )(tile_expert, n_tiles, xs, wg, wu, wd, *after)


def _final_kernel(x1_ref, g1_ref, g2_ref, rw_ref, gate_ref, *rest):
    o_ref = rest[-1]
    rw = rw_ref[0]
    w1 = rw[:, 0:1]
    w2 = rw[:, 1:2]
    a_l, a_r = _unpack_bf16_pairs(g1_ref[0])
    b_l, b_r = _unpack_bf16_pairs(g2_ref[0])
    moe = jnp.concatenate([w1 * a_l + w2 * b_l, w1 * a_r + w2 * b_r], axis=1)
    o_ref[0] = x1_ref[0] + gate_ref[0] * moe


def _final(x1, g1, g2, rw, gate2, tm, b0, src0, out_prev, after):
    nb, S, _ = g1.shape
    grp = lambda w: pl.BlockSpec((1, tm, w), lambda b, i: (b + src0, i, 0))
    tok = lambda w: pl.BlockSpec((1, tm, w), lambda b, i: (b, i, 0))
    extra = (() if out_prev is None else (out_prev,)) + tuple(after)
    return pl.pallas_call(
        _final_kernel,
        out_shape=jax.ShapeDtypeStruct((gate2.shape[0], S, D_MODEL), F32),
        grid=(nb, S // tm),
        in_specs=[grp(D_MODEL), tok(ROW_WORDS), tok(ROW_WORDS), grp(LANES),
                  pl.BlockSpec((1, 1, D_MODEL), lambda b, i: (b + b0 + src0, 0, 0))]
        + [pl.BlockSpec(memory_space=pl.ANY)] * len(extra),
        out_specs=pl.BlockSpec((1, tm, D_MODEL), lambda b, i: (b + b0 + src0, i, 0)),
        input_output_aliases={} if out_prev is None else {5: 0},
        compiler_params=pltpu.CompilerParams(
            dimension_semantics=("arbitrary", "arbitrary"), vmem_limit_bytes=VMEM_LIMIT),
        name="final",
    )(x1, g1, g2, rw, gate2, *extra)


def _pick(n, pref):
    t = min(n, pref)
    assert n % t == 0, (n, t)
    return t


def _slots_kernel(offs_ref, r_ref, o_ref):
    r = r_ref[0]
    base = jnp.zeros_like(r)
    for e in range(N_EXPERTS):
        base = jnp.where(r == e, offs_ref[e], base)
    o_ref[0] = base + pltpu.roll(r, 6, 0)


def _slots(ridx, offs):
    B, _, S = ridx.shape
    return pl.pallas_call(
        _slots_kernel,
        out_shape=jax.ShapeDtypeStruct((B, 8, S), jnp.int32),
        grid_spec=pltpu.PrefetchScalarGridSpec(
            num_scalar_prefetch=1, grid=(B,),
            in_specs=[pl.BlockSpec((1, 8, S), lambda b, offs: (b, 0, 0))],
            out_specs=pl.BlockSpec((1, 8, S), lambda b, offs: (b, 0, 0))),
        compiler_params=pltpu.CompilerParams(dimension_semantics=("arbitrary",)),
        name="slots",
    )(offs, ridx)


def _route_plan(ridx, counts, tm_e, T):
    counts = counts.astype(jnp.int32)
    tiles = (counts + tm_e - 1) // tm_e
    tile_end = jnp.cumsum(tiles)
    offs = (tile_end - tiles) * tm_e
    slots = _slots(ridx, offs)
    n_chunks = T // (SC_WORKERS * SC_CHUNK)
    idx1 = slots[:, 0, :].reshape(SC_WORKERS, n_chunks, SC_CHUNK)
    idx2 = slots[:, 1, :].reshape(SC_WORKERS, n_chunks, SC_CHUNK)
    n_tiles_max = 2 * T // tm_e + N_EXPERTS
    tile_ids = jnp.arange(n_tiles_max, dtype=jnp.int32)
    tile_expert = jnp.sum((tile_end[None, :] <= tile_ids[:, None]).astype(jnp.int32), axis=1)
    tile_expert = jnp.minimum(tile_expert, N_EXPERTS - 1)
    return idx1, idx2, tile_expert, tile_end[-1:].astype(jnp.int32), n_tiles_max * tm_e


def _layer(x, c, w_ada, b_ada, norm1_w, w_in, b_forget, conv_w, q_norm_w, k_norm_w,
           w_out_conv, w_out_attn, w_o, norm2_w, w_rg, b_rg, w_re, b_re, w_gate, w_up, w_down):
    B, S, _ = x.shape
    n_grp = 2 if B % 2 == 0 and (B // 2 * S) % (SC_WORKERS * SC_CHUNK) == 0 else 1
    Bg = B // n_grp
    T = Bg * S
    assert T % (SC_WORKERS * SC_CHUNK) == 0, T
    mod = _ada(c, w_ada, b_ada.reshape(1, -1)).reshape(B, 6, 1, D_MODEL)
    shift1, scale1, gate1, shift2, scale2, gate2 = (mod[:, t] for t in range(6))

    cuts = np.cumsum([0, CONV_WIDTH, CONV_WIDTH, CONV_WIDTH, ATTN_WIDTH, ATTN_WIDTH, ATTN_WIDTH,
                      N_HEADS, D_MODEL, D_MODEL])
    w_conv3 = w_in[:, cuts[0]:cuts[3]]
    w_qvT = jnp.concatenate([w_in[:, cuts[3]:cuts[4]], w_in[:, cuts[5]:cuts[6]]], axis=1).T.astype(BF16)
    w_k = w_in[:, cuts[4]:cuts[5]].astype(BF16)
    w_f = jnp.pad(w_in[:, cuts[6]:cuts[7]], ((0, 0), (0, LANES - N_HEADS))).astype(BF16)
    b_f = jnp.pad(b_forget, (0, LANES - N_HEADS)).reshape(1, LANES)
    w_cgg = jnp.concatenate([w_conv3, w_in[:, cuts[7]:cuts[9]]], axis=1).astype(BF16)

    tm_qkv = _pick(S, 1024)
    qnwT = jnp.broadcast_to((jnp.tile(q_norm_w, N_HEADS) * (LOG2E * HEAD_DIM ** -0.5))[:, None],
                            (ATTN_WIDTH, LANES))
    tq = _pick(S // 2, 512)
    w_r = jnp.pad(jnp.concatenate([w_re, w_rg], axis=1),
                  ((0, 0), (0, LANES - N_EXPERTS - N_GROUPS)))
    w_r_hi = w_r.astype(BF16)
    w_r_lo = (w_r - w_r_hi.astype(F32)).astype(BF16)
    b_r = jnp.pad(jnp.concatenate([b_re, b_rg]), (0, LANES - N_EXPERTS - N_GROUPS)).reshape(1, LANES)
    cw = jnp.pad(conv_w, ((0, 8 - CONV_K), (0, 0)))
    tm_post = _pick(S, 1024)
    tm_e = 512
    w_oc, w_oa, w_ob, w_rs = (w_out_conv.astype(BF16), w_out_attn.astype(BF16), w_o.astype(BF16),
                              jnp.concatenate([w_r_hi, w_r_lo], axis=1))

    n_half = 2 if Bg % 2 == 0 and (Bg // 2 * S) % (SC_WORKERS * SC_CHUNK) == 0 else 1
    Bh = Bg // n_half

    def experts(grp, after):
        x1, rw, idx1, idx2, tile_expert, n_tiles, xs = grp
        ys = _moe(xs, tile_expert, n_tiles, w_gate, w_up, w_down, tm_e, after)
        halves = lambda idx: idx.reshape(n_half, SC_WORKERS, idx.shape[1] // n_half, SC_CHUNK)
        return ys, [_collect(ys, i1, i2) for i1, i2 in zip(halves(idx1), halves(idx2))]

    def finish(g, grp, gathered, out, after):
        for hh, (g1, g2) in enumerate(gathered):
            out = _final(grp[0], g1.reshape(Bh, S, ROW_WORDS), g2.reshape(Bh, S, ROW_WORDS), grp[1],
                         gate2, _pick(S, 512), g * Bg, hh * Bh, out, after if hh == 0 else [])
        return out

    out, prev, prev_gathered, ys_prev = None, None, None, None
    for g in range(n_grp):
        b0 = g * Bg
        qT_aug, k_aug, vT = _qkv(x, norm1_w.reshape(1, -1), shift1, scale1, w_qvT, w_k, w_f, b_f,
                                 qnwT, jnp.tile(k_norm_w, N_HEADS).reshape(1, -1), tm_qkv, b0, Bg)
        if prev is not None:
            ys_prev, prev_gathered = experts(prev, [qT_aug])
        y_b = _attention(qT_aug, k_aug, vT, tq, _pick(tq, 256),
                         [] if ys_prev is None else [ys_prev])
        x1, h2p, ridx, rw, counts = _post(x, y_b, norm1_w.reshape(1, -1), shift1, scale1, gate1,
                                          norm2_w.reshape(1, -1), shift2, scale2,
                                          w_cgg, cw, w_oc, w_oa, w_ob, w_rs, b_r, tm_post, b0)
        idx1, idx2, tile_expert, n_tiles, n_slots = _route_plan(ridx, counts[0, :N_EXPERTS], tm_e, T)
        xs = _dispatch(h2p.reshape(T, ROW_WORDS), idx1, idx2, n_slots)
        if prev is not None:
            out = finish(g - 1, prev, prev_gathered, out, [idx1])
        prev = (x1, rw, idx1, idx2, tile_expert, n_tiles, xs)
    _, last_gathered = experts(prev, [] if out is None else [out])
    return finish(n_grp - 1, prev, last_gathered, out, [])


def kernel(x, c, w_ada, b_ada, norm1_w, w_in, b_forget, conv_w, q_norm_w, k_norm_w, w_out_conv,
           w_out_attn, w_o, norm2_w, w_router_group, b_router_group, w_router_expert,
           b_router_expert, w_gate, w_up, w_down):
    for l in range(w_ada.shape[0]):
        x = _layer(x, c, w_ada[l], b_ada[l], norm1_w[l], w_in[l], b_forget[l], conv_w[l],
                   q_norm_w[l], k_norm_w[l], w_out_conv[l], w_out_attn[l], w_o[l], norm2_w[l],
                   w_router_group[l], b_router_group[l], w_router_expert[l], b_router_expert[l],
                   w_gate[l], w_up[l], w_down[l])
    return x
```

```python
import functools

import jax
import jax.numpy as jnp
import numpy as np
from jax import lax
from jax.experimental import pallas as pl
from jax.experimental.pallas import tpu as pltpu
from jax.experimental.pallas import tpu_sc as plsc

D_MODEL = 1024
CONV_WIDTH = 512
CONV_K = 3
N_HEADS = 8
HEAD_DIM = 64
ATTN_WIDTH = N_HEADS * HEAD_DIM
N_PAIRS = N_HEADS // 2
N_GROUPS = 4
EXPERTS_PER_GROUP = 8
N_EXPERTS = N_GROUPS * EXPERTS_PER_GROUP
D_EXPERT = 256
EPS = 1e-6
LANES = 128
AUG = 2 * LANES
BIAS_W = 6
VROWS = HEAD_DIM + 16
NEG = -1e30
LOG2E = 1.4426950408889634

F32 = jnp.float32
BF16 = jnp.bfloat16
VMEM_LIMIT = 56 * 1024 * 1024


def _sigmoid(z):
    return 1.0 / (1.0 + jnp.exp(-z))


def _split3(z):
    hi = z.astype(BF16)
    r = z - hi.astype(F32)
    mid = r.astype(BF16)
    lo = (r - mid.astype(F32)).astype(BF16)
    return hi, mid, lo


def _dot(a, b):
    return jnp.dot(a, b, preferred_element_type=F32)


def _modulated_norm(x, nw, shift, scale):
    ms = jnp.mean(x * x, axis=-1, keepdims=True)
    return (x * lax.rsqrt(ms + EPS) * nw) * (1.0 + scale) + shift


def _ada_kernel(c_ref, w_ref, b_ref, o_ref):
    c = c_ref[...]
    a = c * _sigmoid(c)
    o_ref[...] = jnp.dot(a, w_ref[...], precision=lax.Precision.HIGHEST,
                         preferred_element_type=F32) + b_ref[...]


def _ada(c, w_ada, b_ada):
    B = c.shape[0]
    n = w_ada.shape[1] // D_MODEL
    return pl.pallas_call(
        _ada_kernel,
        out_shape=jax.ShapeDtypeStruct((B, n * D_MODEL), F32),
        grid=(n,),
        in_specs=[pl.BlockSpec((B, D_MODEL), lambda j: (0, 0)),
                  pl.BlockSpec((D_MODEL, D_MODEL), lambda j: (0, j)),
                  pl.BlockSpec((1, D_MODEL), lambda j: (0, j))],
        out_specs=pl.BlockSpec((B, D_MODEL), lambda j: (0, j)),
        compiler_params=pltpu.CompilerParams(dimension_semantics=("arbitrary",)),
        name="ada",
    )(c, w_ada, b_ada)


QKV_ROWS = 256
_NT = (((1,), (1,)), ((), ()))


def _lane_tile(a, width):
    return jnp.concatenate([a] * (width // a.shape[1]), axis=1)


def _qkv_kernel(x_ref, nw_ref, sh_ref, sc_ref, wqv_ref, wk_ref, wf_ref, bf_ref, qnw_ref, knw_ref,
                gsum_ref, pq_ref, pk_ref, cq_ref, ck_ref,
                qT_ref, k_ref, vT_ref, carry_ref):
    tm = x_ref.shape[1]
    n_grp = max(tm // QKV_ROWS, 1)
    rows = tm // n_grp
    row = lax.broadcasted_iota(jnp.int32, (rows, LANES), 0)

    @pl.when(pl.program_id(1) == 0)
    def _():
        carry_ref[...] = jnp.zeros_like(carry_ref)

    st = [dict(rs=pl.ds(g * rows, rows)) for g in range(n_grp)]

    def project(d):
        h = _modulated_norm(x_ref[0, d["rs"], :], nw_ref[...], sh_ref[0], sc_ref[0])
        hb = h.astype(BF16)
        d["qvT"] = lax.dot_general(wqv_ref[...], hb, _NT, preferred_element_type=F32)
        d["k"] = _dot(hb, wk_ref[...])
        d["fl"] = _dot(hb, wf_ref[...]) + bf_ref[...]

    def norms(d):
        qvT = d["qvT"]
        heads = []
        for hd in range(N_HEADS):
            z = qvT[hd * HEAD_DIM:(hd + 1) * HEAD_DIM]
            heads.append(z * lax.rsqrt(jnp.mean(z * z, axis=0, keepdims=True) + EPS))
        qnT = jnp.concatenate(heads, axis=0) * _lane_tile(qnw_ref[...], rows)
        for j in range(N_PAIRS):
            qT_ref[0, j, :LANES, d["rs"]] = qnT[j * LANES:(j + 1) * LANES].astype(BF16)
            for t in range(2):
                r0 = ATTN_WIDTH + (2 * j + t) * HEAD_DIM
                vT_ref[0, j, t * VROWS:t * VROWS + HEAD_DIM, d["rs"]] = qvT[r0:r0 + HEAD_DIM].astype(BF16)
                vT_ref[0, j, t * VROWS + HEAD_DIM:(t + 1) * VROWS, d["rs"]] = \
                    jnp.ones((VROWS - HEAD_DIM, rows), BF16)
        del d["qvT"]
        k = d.pop("k")
        ss = _dot((k * k).astype(BF16), gsum_ref[...])
        kn = k * lax.rsqrt(ss * (1.0 / HEAD_DIM) + EPS) * knw_ref[...]
        for j in range(N_PAIRS):
            k_ref[0, d["rs"], j * AUG:j * AUG + LANES] = kn[:, j * LANES:(j + 1) * LANES].astype(BF16)

    def forget(d, before):
        fl = d.pop("fl")
        cum = jnp.minimum(fl, 0.0) - jnp.log(1.0 + jnp.exp(-jnp.abs(fl)))
        s = 1
        while s < rows:
            cum = cum + jnp.where(row >= s, pltpu.roll(cum, s, 0), 0.0)
            s *= 2
        cum = cum + before[7:8, :]
        d["tail"] = cum[rows - 8:, :]
        d["parts"] = jnp.concatenate(_split3(cum * LOG2E), axis=1)

    def bias(d):
        parts = d.pop("parts")
        eqT = (lax.dot_general(pq_ref[...], parts, _NT, preferred_element_type=F32)
               + _lane_tile(cq_ref[...], rows)).astype(BF16)
        ek = (_dot(parts, pk_ref[...]) + ck_ref[...]).astype(BF16)
        for j in range(N_PAIRS):
            qT_ref[0, j, LANES:, d["rs"]] = eqT
            k_ref[0, d["rs"], j * AUG + LANES:(j + 1) * AUG] = ek

    stages = [project, norms, None, bias]
    lag = 1
    for step in range(len(stages) + lag * (n_grp - 1)):
        for g, d in enumerate(st):
            kk = step - lag * g
            if 0 <= kk < len(stages):
                if stages[kk] is None:
                    forget(d, carry_ref[...] if g == 0 else st[g - 1]["tail"])
                else:
                    stages[kk](d)
    carry_ref[...] = st[-1]["tail"]


def _bias_placement():
    pq = np.zeros((LANES, 3 * LANES), np.float32)
    pk = np.zeros((3 * LANES, LANES), np.float32)
    cq = np.zeros((LANES, LANES), np.float32)
    ck = np.zeros((1, LANES), np.float32)
    for hd in range(N_HEADS):
        base = BIAS_W * hd
        for p in range(3):
            pq[base + p, p * LANES + hd] = 1.0
            pk[p * LANES + hd, base + 3 + p] = -1.0
            cq[base + 3 + p, :] = 1.0
            ck[0, base + p] = 1.0
    return (jnp.asarray(pq, BF16), jnp.asarray(pk, BF16), jnp.asarray(cq), jnp.asarray(ck))


def _qkv(x, nw, shift, scale, wqvT, wk, wf, bf, qnwT, knw, tm, b0, B):
    S = x.shape[1]
    gsum = jnp.asarray(np.kron(np.eye(N_HEADS), np.ones((HEAD_DIM, HEAD_DIM))), BF16)
    pq, pk, cq, ck = _bias_placement()
    const = lambda *shape: pl.BlockSpec(shape, lambda b, i: (0,) * len(shape),
                                        pipeline_mode=pl.Buffered(1))
    return pl.pallas_call(
        _qkv_kernel,
        out_shape=(jax.ShapeDtypeStruct((B, N_PAIRS, AUG, S), BF16),
                   jax.ShapeDtypeStruct((B, S, N_PAIRS * AUG), BF16),
                   jax.ShapeDtypeStruct((B, N_PAIRS, 2 * VROWS, S), BF16)),
        grid=(B, S // tm),
        in_specs=[pl.BlockSpec((1, tm, D_MODEL), lambda b, i: (b + b0, i, 0)),
                  const(1, D_MODEL),
                  pl.BlockSpec((1, 1, D_MODEL), lambda b, i: (b + b0, 0, 0)),
                  pl.BlockSpec((1, 1, D_MODEL), lambda b, i: (b + b0, 0, 0)),
                  const(2 * ATTN_WIDTH, D_MODEL),
                  const(D_MODEL, ATTN_WIDTH),
                  const(D_MODEL, LANES),
                  const(1, LANES),
                  const(ATTN_WIDTH, LANES),
                  const(1, ATTN_WIDTH),
                  const(ATTN_WIDTH, ATTN_WIDTH),
                  const(LANES, 3 * LANES),
                  const(3 * LANES, LANES),
                  const(LANES, LANES),
                  const(1, LANES)],
        out_specs=(pl.BlockSpec((1, N_PAIRS, AUG, tm), lambda b, i: (b, 0, 0, i)),
                   pl.BlockSpec((1, tm, N_PAIRS * AUG), lambda b, i: (b, i, 0)),
                   pl.BlockSpec((1, N_PAIRS, 2 * VROWS, tm), lambda b, i: (b, 0, 0, i))),
        scratch_shapes=[pltpu.VMEM((8, LANES), F32)],
        compiler_params=pltpu.CompilerParams(
            dimension_semantics=("arbitrary", "arbitrary"), vmem_limit_bytes=VMEM_LIMIT),
        name="qkv",
    )(x, nw, shift, scale, wqvT, wk, wf, bf, qnwT, knw, gsum, pq, pk, cq, ck)


def _attn_kernel(qT_ref, k_ref, vT_ref, *rest, tq, cw, nt):
    o_ref, qq_ref, s_ref, smax_ref, m_ref, acc_ref = rest[-6:]
    i = pl.program_id(2)
    n = tq // cw
    chains = [(a, t, c) for a in range(nt) for t in range(2) for c in range(n)]
    feat = lax.broadcasted_iota(jnp.int32, (AUG, tq), 0)
    for t in range(2):
        bias0 = LANES + BIAS_W * (2 * pl.program_id(1) + t)
        keep = ((feat >= t * HEAD_DIM) & (feat < (t + 1) * HEAD_DIM)) | \
               ((feat >= bias0) & (feat < bias0 + BIAS_W))
        for a in range(nt):
            qT = qT_ref[0, 0, :, a * tq:(a + 1) * tq]
            qh = jnp.where(keep, qT, jnp.zeros_like(qT))
            for c in range(n):
                qq_ref[chains.index((a, t, c))] = qh[:, c * cw:(c + 1) * cw]
    kpos = lax.broadcasted_iota(jnp.int32, (tq, cw), 0)
    qpos = lax.broadcasted_iota(jnp.int32, (tq, cw), 1)

    def scores(j, slot, which):
        k_blk = k_ref[0, pl.ds(pl.multiple_of(j * tq, tq), tq), :]
        for ci in which:
            s = _dot(k_blk, qq_ref[ci])
            s_ref[slot, ci] = s
            smax_ref[slot, ci] = jnp.broadcast_to(jnp.max(s, axis=0, keepdims=True), (8, cw))

    def absorb(j, slot, which, diagonal=()):
        start = pl.multiple_of(j * tq, tq)
        for ci in which:
            _, t, c = chains[ci]
            vj = vT_ref[0, 0, t * VROWS:(t + 1) * VROWS, pl.ds(start, tq)]
            s = s_ref[slot, ci]
            if ci in diagonal:
                s = jnp.where(kpos <= qpos + c * cw, s, NEG)
                smax = jnp.max(s, axis=0, keepdims=True)
            else:
                smax = smax_ref[slot, ci, 0:1]
            m = m_ref[ci, 0:1]
            m_new = jnp.maximum(m, smax)
            p = jnp.exp2(s - m_new).astype(BF16)
            acc_ref[ci] = jnp.exp2(m - m_new) * acc_ref[ci] + _dot(vj, p)
            m_ref[ci] = jnp.broadcast_to(m_new, (8, cw))

    every = list(range(len(chains)))
    from_tile = lambda r: [ci for ci in every if chains[ci][0] >= r]
    m_ref[...] = jnp.full(m_ref.shape, NEG, F32)
    acc_ref[...] = jnp.zeros(acc_ref.shape, F32)
    scores(0, 0, every)

    def two_blocks(jj, _):
        j = 2 * jj
        scores(j + 1, 1, every)
        absorb(j, 0, every)
        scores(j + 2, 0, every)
        absorb(j + 1, 1, every)
        return 0

    lax.fori_loop(0, i * (nt // 2), two_blocks, 0)
    for r in range(nt):
        if r + 1 < nt:
            scores(nt * i + r + 1, (r + 1) % 2, from_tile(r + 1))
        absorb(nt * i + r, r % 2, from_tile(r),
               diagonal=[ci for ci in every if chains[ci][0] == r])

    for a in range(nt):
        outs = [acc_ref[ci, :HEAD_DIM] / acc_ref[ci, HEAD_DIM:HEAD_DIM + 1]
                for ci in every if chains[ci][0] == a]
        oT = jnp.concatenate([jnp.concatenate(outs[:n], axis=1), jnp.concatenate(outs[n:], axis=1)],
                             axis=0)
        o_ref[0, a * tq:(a + 1) * tq, :] = oT.T.astype(BF16)


def _attention(qT_aug, k_aug, vT, tq, cw, after):
    B, S, _ = k_aug.shape
    nt = 4 if S % (4 * tq) == 0 else 2
    n_chains = 2 * nt * tq // cw
    return pl.pallas_call(
        functools.partial(_attn_kernel, tq=tq, cw=cw, nt=nt),
        scratch_shapes=[pltpu.VMEM((n_chains, AUG, cw), BF16),
                        pltpu.VMEM((2, n_chains, tq, cw), F32),
                        pltpu.VMEM((2, n_chains, 8, cw), F32),
                        pltpu.VMEM((n_chains, 8, cw), F32),
                        pltpu.VMEM((n_chains, VROWS, cw), F32)],
        out_shape=jax.ShapeDtypeStruct((B, S, ATTN_WIDTH), BF16),
        grid=(B, N_PAIRS, S // (nt * tq)),
        in_specs=[pl.BlockSpec((1, 1, AUG, nt * tq), lambda b, j, i: (b, j, 0, i)),
                  pl.BlockSpec((1, S, AUG), lambda b, j, i: (b, 0, j)),
                  pl.BlockSpec((1, 1, 2 * VROWS, S), lambda b, j, i: (b, j, 0, 0))]
        + [pl.BlockSpec(memory_space=pl.ANY)] * len(after),
        out_specs=pl.BlockSpec((1, nt * tq, LANES), lambda b, j, i: (b, i, j)),
        compiler_params=pltpu.CompilerParams(
            dimension_semantics=("arbitrary", "arbitrary", "arbitrary"),
            vmem_limit_bytes=VMEM_LIMIT),
        name="attn",
    )(qT_aug, k_aug, vT, *after)


def _pack_bf16_pairs(z):
    w = z.shape[1] // 2
    bits = pltpu.bitcast(z.astype(BF16).astype(F32), jnp.uint32)
    return bits[:, :w] | (bits[:, w:] >> 16)


def _unpack_bf16_pairs(p):
    return (pltpu.bitcast(p & jnp.uint32(0xFFFF0000), F32), pltpu.bitcast(p << 16, F32))


POST_ROWS = 256


def _post_kernel(x_ref, yb_ref, n1_ref, sh1_ref, sc1_ref, g1_ref, n2_ref, sh2_ref, sc2_ref,
                 wc_ref, cw_ref, woc_ref, woa_ref, wo_ref, wr_ref, br_ref, tri_ref,
                 x1_ref, h2_ref, ridx_ref, rw_ref, cnt_ref, carry_ref):
    tm = x_ref.shape[1]
    n_grp = max(tm // POST_ROWS, 1)
    rows = tm // n_grp
    lane = lax.broadcasted_iota(jnp.int32, (rows, LANES), 1)
    row8 = lax.broadcasted_iota(jnp.int32, (8, CONV_WIDTH), 0)
    big = jnp.int32(1 << 20)

    @pl.when(pl.program_id(1) == 0)
    def _():
        carry_ref[...] = jnp.zeros_like(carry_ref)

    @pl.when((pl.program_id(0) == 0) & (pl.program_id(1) == 0))
    def _():
        cnt_ref[...] = jnp.zeros_like(cnt_ref)

    st = [dict(rs=pl.ds(g * rows, rows)) for g in range(n_grp)]

    def conv_in(d):
        d["x"] = x_ref[0, d["rs"], :]
        d["hb"] = _modulated_norm(d["x"], n1_ref[...], sh1_ref[0], sc1_ref[0]).astype(BF16)
        d["x_in"] = _dot(d["hb"], wc_ref[:, :CONV_WIDTH])
        d["conv_c"] = _dot(d["hb"], wc_ref[:, 2 * CONV_WIDTH:3 * CONV_WIDTH])
        d["conv_b"] = _dot(d["hb"], wc_ref[:, CONV_WIDTH:2 * CONV_WIDTH])

    def conv(d, prev):
        u = d.pop("conv_c") * d.pop("x_in")
        d["u_tail"] = u[rows - 8:, :]

        def shifted(k):
            r = pltpu.roll(u, k, 0)
            top = jnp.where(row8 < k, pltpu.roll(prev, k, 0), r[:8])
            return jnp.concatenate([top, r[8:]], axis=0)

        cw = cw_ref[...]
        cv = cw[0:1] * shifted(2) + cw[1:2] * shifted(1) + cw[2:3] * u
        d["y_a"] = (d.pop("conv_b") * cv).astype(BF16)

    def gates(d):
        d["p_b"] = _dot(yb_ref[0, d["rs"], :], woa_ref[...])
        d["gate_c"] = _dot(d["hb"], wc_ref[:, 3 * CONV_WIDTH:3 * CONV_WIDTH + D_MODEL])
        d["gate_a"] = _dot(d.pop("hb"), wc_ref[:, 3 * CONV_WIDTH + D_MODEL:])

    def branch_a(d):
        d["p_a"] = _dot(d.pop("y_a"), woc_ref[...])

    def merge(d):
        d["merged"] = (_sigmoid(d.pop("gate_c")) * d.pop("p_a")
                       + _sigmoid(d.pop("gate_a")) * d.pop("p_b")).astype(BF16)

    def out_proj(d):
        d["o"] = _dot(d.pop("merged"), wo_ref[...])

    def residual(d):
        x1 = d.pop("x") + g1_ref[0] * d.pop("o")
        x1_ref[0, d["rs"], :] = x1
        h2 = _modulated_norm(x1, n2_ref[...], sh2_ref[0], sc2_ref[0])
        h2_ref[0, d["rs"], :] = _pack_bf16_pairs(h2)
        d["h_hi"] = h2.astype(BF16)
        d["h_lo"] = (h2 - d["h_hi"].astype(F32)).astype(BF16)

    def router(d):
        both = _dot(d.pop("h_hi"), wr_ref[...])
        d["lg"] = (both[:, :LANES] + both[:, LANES:] + _dot(d.pop("h_lo"), wr_ref[:, :LANES])) \
            + br_ref[...]

    def first_argmax(vals):
        mx = jnp.max(vals, axis=-1, keepdims=True)
        idx = jnp.min(jnp.where(vals == mx, lane, big), axis=-1, keepdims=True)
        return mx, idx

    def route(d):
        lg = d.pop("lg")
        is_g = (lane >= N_EXPERTS) & (lane < N_EXPERTS + N_GROUPS)
        g_mx, g_lane = first_argmax(jnp.where(is_g, lg, NEG))
        p_sel = 1.0 / jnp.sum(jnp.where(is_g, jnp.exp(lg - g_mx), 0.0), axis=-1, keepdims=True)
        g_idx = g_lane - N_EXPERTS
        in_g = (lane >= g_idx * EXPERTS_PER_GROUP) & (lane < (g_idx + 1) * EXPERTS_PER_GROUP)
        le = jnp.where(in_g, lg, NEG)
        v1, i1 = first_argmax(le)
        v2, i2 = first_argmax(jnp.where(lane == i1, NEG, le))
        e2 = jnp.exp(v2 - v1)
        w1 = p_sel / (1.0 + e2)
        w2 = w1 * e2
        rw_ref[0, d["rs"], :] = jnp.where(lane == 0, w1, 0.0) + jnp.where(lane == 1, w2, 0.0)
        d["i1"], d["i2"] = i1, i2
        d["onehot"] = jnp.where((lane == i1) | (lane == i2), 1.0, 0.0)

    def rank(d):
        onehot = d.pop("onehot")
        before = _dot(tri_ref[...], onehot.astype(BF16)) + cnt_ref[0:1, :]
        cnt_ref[...] = cnt_ref[...] + jnp.sum(onehot, axis=0, keepdims=True)
        i1, i2 = d.pop("i1"), d.pop("i2")
        r1 = jnp.sum(jnp.where(lane == i1, before, 0.0), axis=-1, keepdims=True)
        r2 = jnp.sum(jnp.where(lane == i2, before, 0.0), axis=-1, keepdims=True)
        rec = (jnp.where(lane == 0, i1.astype(F32), 0.0) + jnp.where(lane == 1, i2.astype(F32), 0.0)
               + jnp.where(lane == 2, r1, 0.0) + jnp.where(lane == 3, r2, 0.0))
        ridx_ref[0, :, d["rs"]] = rec.T[:8].astype(jnp.int32)

    stages = [conv_in, None, gates, branch_a, merge, out_proj, residual, router, route, rank]
    lag = 2
    for step in range(len(stages) + lag * (n_grp - 1)):
        for g, d in enumerate(st):
            k = step - lag * g
            if 0 <= k < len(stages):
                if stages[k] is None:
                    conv(d, carry_ref[...] if g == 0 else st[g - 1]["u_tail"])
                else:
                    stages[k](d)
    carry_ref[...] = st[-1]["u_tail"]


def _post(x, yb, n1, sh1, sc1, g1, n2, sh2, sc2, wc, cw, woc, woa, wo, wr, br, tm, b0):
    B, S, _ = yb.shape
    rows = tm // max(tm // POST_ROWS, 1)
    tri = jnp.asarray(np.tril(np.ones((rows, rows), np.float32), -1), BF16)
    const = lambda *shape: pl.BlockSpec(shape, lambda b, i: (0,) * len(shape),
                                        pipeline_mode=pl.Buffered(1))
    perb = pl.BlockSpec((1, 1, D_MODEL), lambda b, i: (b + b0, 0, 0))
    tok = lambda w: pl.BlockSpec((1, tm, w), lambda b, i: (b, i, 0))
    return pl.pallas_call(
        _post_kernel,
        out_shape=(jax.ShapeDtypeStruct((B, S, D_MODEL), F32),
                   jax.ShapeDtypeStruct((B, S, D_MODEL // 2), jnp.uint32),
                   jax.ShapeDtypeStruct((B, 8, S), jnp.int32),
                   jax.ShapeDtypeStruct((B, S, LANES), F32),
                   jax.ShapeDtypeStruct((8, LANES), F32)),
        grid=(B, S // tm),
        in_specs=[pl.BlockSpec((1, tm, D_MODEL), lambda b, i: (b + b0, i, 0)), tok(ATTN_WIDTH),
                  const(1, D_MODEL), perb, perb, perb,
                  const(1, D_MODEL), perb, perb,
                  const(D_MODEL, 3 * CONV_WIDTH + 2 * D_MODEL),
                  const(8, CONV_WIDTH),
                  const(CONV_WIDTH, D_MODEL), const(ATTN_WIDTH, D_MODEL),
                  const(D_MODEL, D_MODEL),
                  const(D_MODEL, 2 * LANES), const(1, LANES), const(rows, rows)],
        out_specs=(tok(D_MODEL), tok(D_MODEL // 2),
                   pl.BlockSpec((1, 8, tm), lambda b, i: (b, 0, i)), tok(LANES),
                   pl.BlockSpec((8, LANES), lambda b, i: (0, 0))),
        scratch_shapes=[pltpu.VMEM((8, CONV_WIDTH), F32)],
        compiler_params=pltpu.CompilerParams(
            dimension_semantics=("arbitrary", "arbitrary"), vmem_limit_bytes=VMEM_LIMIT),
        name="post",
    )(x, yb, n1, sh1, sc1, g1, n2, sh2, sc2, wc, cw, woc, woa, wo, wr, br, tri)


SC_CORES = 2
SC_SUBCORES = 16
SC_WORKERS = SC_CORES * SC_SUBCORES
SC_CHUNK = 64
ROW_WORDS = D_MODEL // 2


def _sc_mesh():
    return plsc.VectorSubcoreMesh(core_axis_name="c", subcore_axis_name="s",
                                  num_cores=SC_CORES, num_subcores=SC_SUBCORES)


def _dispatch_body(rows_hbm, idx1_hbm, idx2_hbm, xs_hbm, idx1_v, idx2_v, rows_v, *, n_chunks):
    wid = lax.axis_index("s") * SC_CORES + lax.axis_index("c")
    pltpu.sync_copy(idx1_hbm.at[wid], idx1_v)
    pltpu.sync_copy(idx2_hbm.at[wid], idx2_v)
    base = wid * (n_chunks * SC_CHUNK)

    @pl.loop(0, n_chunks)
    def _(j):
        pltpu.sync_copy(rows_hbm.at[pl.ds(base + j * SC_CHUNK, SC_CHUNK)], rows_v)
        pltpu.sync_copy(rows_v, xs_hbm.at[idx1_v.at[j]])
        pltpu.sync_copy(rows_v, xs_hbm.at[idx2_v.at[j]])


def _sc_scratch(n_chunks):
    return [pltpu.VMEM((n_chunks, SC_CHUNK), jnp.int32), pltpu.VMEM((n_chunks, SC_CHUNK), jnp.int32),
            pltpu.VMEM((SC_CHUNK, ROW_WORDS), jnp.uint32)]


def _dispatch(rows, idx1, idx2, n_slots):
    n_chunks = idx1.shape[1]
    return pl.kernel(
        functools.partial(_dispatch_body, n_chunks=n_chunks),
        out_type=jax.ShapeDtypeStruct((n_slots, ROW_WORDS), jnp.uint32),
        mesh=_sc_mesh(),
        scratch_types=_sc_scratch(n_chunks),
        name="dispatch",
    )(rows, idx1, idx2)


def _collect_body(ys_hbm, idx1_hbm, idx2_hbm, g1_hbm, g2_hbm, idx1_v, idx2_v, rows_v, *, n_chunks):
    wid = lax.axis_index("s") * SC_CORES + lax.axis_index("c")
    pltpu.sync_copy(idx1_hbm.at[wid], idx1_v)
    pltpu.sync_copy(idx2_hbm.at[wid], idx2_v)
    base = wid * (n_chunks * SC_CHUNK)

    @pl.loop(0, n_chunks)
    def _(j):
        dst = pl.ds(base + j * SC_CHUNK, SC_CHUNK)
        pltpu.sync_copy(ys_hbm.at[idx1_v.at[j]], rows_v)
        pltpu.sync_copy(rows_v, g1_hbm.at[dst])
        pltpu.sync_copy(ys_hbm.at[idx2_v.at[j]], rows_v)
        pltpu.sync_copy(rows_v, g2_hbm.at[dst])


def _collect(ys, idx1, idx2):
    n_chunks = idx1.shape[1]
    out = jax.ShapeDtypeStruct((SC_WORKERS * n_chunks * SC_CHUNK, ROW_WORDS), jnp.uint32)
    return pl.kernel(
        functools.partial(_collect_body, n_chunks=n_chunks),
        out_type=(out, out),
        mesh=_sc_mesh(),
        scratch_types=_sc_scratch(n_chunks),
        name="collect",
    )(ys, idx1, idx2)


def _moe_kernel(te_ref, nt_ref, xs_ref, wg_ref, wu_ref, wd_ref, *rest):
    ys_ref = rest[-1]

    @pl.when(pl.program_id(0) < nt_ref[0])
    def _():
        left, right = _unpack_bf16_pairs(xs_ref[...])
        xb = jnp.concatenate([left.astype(BF16), right.astype(BF16)], axis=1)
        g = _dot(xb, wg_ref[0].astype(BF16))
        u = _dot(xb, wu_ref[0].astype(BF16))
        a = (g * _sigmoid(g) * u).astype(BF16)
        ys_ref[...] = _pack_bf16_pairs(_dot(a, wd_ref[0].astype(BF16)))


def _moe(xs, tile_expert, n_tiles, wg, wu, wd, tm, after):
    n_slots = xs.shape[0]
    row_blk = lambda i, te, nt: (jnp.minimum(i, nt[0] - 1), 0)
    w_blk = lambda i, te, nt: (te[i], 0, 0)
    return pl.pallas_call(
        _moe_kernel,
        out_shape=jax.ShapeDtypeStruct((n_slots, ROW_WORDS), jnp.uint32),
        grid_spec=pltpu.PrefetchScalarGridSpec(
            num_scalar_prefetch=2,
            grid=(n_slots // tm,),
            in_specs=[pl.BlockSpec((tm, ROW_WORDS), row_blk),
                      pl.BlockSpec((1, D_MODEL, D_EXPERT), w_blk),
                      pl.BlockSpec((1, D_MODEL, D_EXPERT), w_blk),
                      pl.BlockSpec((1, D_EXPERT, D_MODEL), w_blk)]
            + [pl.BlockSpec(memory_space=pl.ANY)] * len(after),
            out_specs=pl.BlockSpec((tm, ROW_WORDS), row_blk)),
        compiler_params=pltpu.CompilerParams(
            dimension_semantics=("arbitrary",), vmem_limit_bytes=VMEM_LIMIT),
        name="moe",
    )(tile_expert, n_tiles, xs, wg, wu, wd, *after)


def _final_kernel(x1_ref, g1_ref, g2_ref, rw_ref, gate_ref, *rest):
    o_ref = rest[-1]
    rw = rw_ref[0]
    w1 = rw[:, 0:1]
    w2 = rw[:, 1:2]
    a_l, a_r = _unpack_bf16_pairs(g1_ref[0])
    b_l, b_r = _unpack_bf16_pairs(g2_ref[0])
    moe = jnp.concatenate([w1 * a_l + w2 * b_l, w1 * a_r + w2 * b_r], axis=1)
    o_ref[0] = x1_ref[0] + gate_ref[0] * moe


def _final(x1, g1, g2, rw, gate2, tm, b0, src0, out_prev, after):
    nb, S, _ = g1.shape
    grp = lambda w: pl.BlockSpec((1, tm, w), lambda b, i: (b + src0, i, 0))
    tok = lambda w: pl.BlockSpec((1, tm, w), lambda b, i: (b, i, 0))
    extra = (() if out_prev is None else (out_prev,)) + tuple(after)
    return pl.pallas_call(
        _final_kernel,
        out_shape=jax.ShapeDtypeStruct((gate2.shape[0], S, D_MODEL), F32),
        grid=(nb, S // tm),
        in_specs=[grp(D_MODEL), tok(ROW_WORDS), tok(ROW_WORDS), grp(LANES),
                  pl.BlockSpec((1, 1, D_MODEL), lambda b, i: (b + b0 + src0, 0, 0))]
        + [pl.BlockSpec(memory_space=pl.ANY)] * len(extra),
        out_specs=pl.BlockSpec((1, tm, D_MODEL), lambda b, i: (b + b0 + src0, i, 0)),
        input_output_aliases={} if out_prev is None else {5: 0},
        compiler_params=pltpu.CompilerParams(
            dimension_semantics=("arbitrary", "arbitrary"), vmem_limit_bytes=VMEM_LIMIT),
        name="final",
    )(x1, g1, g2, rw, gate2, *extra)


def _pick(n, pref):
    t = min(n, pref)
    assert n % t == 0, (n, t)
    return t


def _slots_kernel(offs_ref, r_ref, o_ref):
    r = r_ref[0]
    base = jnp.zeros_like(r)
    for e in range(N_EXPERTS):
        base = jnp.where(r == e, offs_ref[e], base)
    o_ref[0] = base + pltpu.roll(r, 6, 0)


def _slots(ridx, offs):
    B, _, S = ridx.shape
    return pl.pallas_call(
        _slots_kernel,
        out_shape=jax.ShapeDtypeStruct((B, 8, S), jnp.int32),
        grid_spec=pltpu.PrefetchScalarGridSpec(
            num_scalar_prefetch=1, grid=(B,),
            in_specs=[pl.BlockSpec((1, 8, S), lambda b, offs: (b, 0, 0))],
            out_specs=pl.BlockSpec((1, 8, S), lambda b, offs: (b, 0, 0))),
        compiler_params=pltpu.CompilerParams(dimension_semantics=("arbitrary",)),
        name="slots",
    )(offs, ridx)


def _route_plan(ridx, counts, tm_e, T):
    counts = counts.astype(jnp.int32)
    tiles = (counts + tm_e - 1) // tm_e
    tile_end = jnp.cumsum(tiles)
    offs = (tile_end - tiles) * tm_e
    slots = _slots(ridx, offs)
    n_chunks = T // (SC_WORKERS * SC_CHUNK)
    idx1 = slots[:, 0, :].reshape(SC_WORKERS, n_chunks, SC_CHUNK)
    idx2 = slots[:, 1, :].reshape(SC_WORKERS, n_chunks, SC_CHUNK)
    n_tiles_max = 2 * T // tm_e + N_EXPERTS
    tile_ids = jnp.arange(n_tiles_max, dtype=jnp.int32)
    tile_expert = jnp.sum((tile_end[None, :] <= tile_ids[:, None]).astype(jnp.int32), axis=1)
    tile_expert = jnp.minimum(tile_expert, N_EXPERTS - 1)
    return idx1, idx2, tile_expert, tile_end[-1:].astype(jnp.int32), n_tiles_max * tm_e


def _layer(x, c, w_ada, b_ada, norm1_w, w_in, b_forget, conv_w, q_norm_w, k_norm_w,
           w_out_conv, w_out_attn, w_o, norm2_w, w_rg, b_rg, w_re, b_re, w_gate, w_up, w_down):
    B, S, _ = x.shape
    n_grp = 2 if B % 2 == 0 and (B // 2 * S) % (SC_WORKERS * SC_CHUNK) == 0 else 1
    Bg = B // n_grp
    T = Bg * S
    assert T % (SC_WORKERS * SC_CHUNK) == 0, T
    mod = _ada(c, w_ada, b_ada.reshape(1, -1)).reshape(B, 6, 1, D_MODEL)
    shift1, scale1, gate1, shift2, scale2, gate2 = (mod[:, t] for t in range(6))

    cuts = np.cumsum([0, CONV_WIDTH, CONV_WIDTH, CONV_WIDTH, ATTN_WIDTH, ATTN_WIDTH, ATTN_WIDTH,
                      N_HEADS, D_MODEL, D_MODEL])
    w_conv3 = w_in[:, cuts[0]:cuts[3]]
    w_qvT = jnp.concatenate([w_in[:, cuts[3]:cuts[4]], w_in[:, cuts[5]:cuts[6]]], axis=1).T.astype(BF16)
    w_k = w_in[:, cuts[4]:cuts[5]].astype(BF16)
    w_f = jnp.pad(w_in[:, cuts[6]:cuts[7]], ((0, 0), (0, LANES - N_HEADS))).astype(BF16)
    b_f = jnp.pad(b_forget, (0, LANES - N_HEADS)).reshape(1, LANES)
    w_cgg = jnp.concatenate([w_conv3, w_in[:, cuts[7]:cuts[9]]], axis=1).astype(BF16)

    tm_qkv = _pick(S, 1024)
    qnwT = jnp.broadcast_to((jnp.tile(q_norm_w, N_HEADS) * (LOG2E * HEAD_DIM ** -0.5))[:, None],
                            (ATTN_WIDTH, LANES))
    tq = _pick(S // 2, 512)
    w_r = jnp.pad(jnp.concatenate([w_re, w_rg], axis=1),
                  ((0, 0), (0, LANES - N_EXPERTS - N_GROUPS)))
    w_r_hi = w_r.astype(BF16)
    w_r_lo = (w_r - w_r_hi.astype(F32)).astype(BF16)
    b_r = jnp.pad(jnp.concatenate([b_re, b_rg]), (0, LANES - N_EXPERTS - N_GROUPS)).reshape(1, LANES)
    cw = jnp.pad(conv_w, ((0, 8 - CONV_K), (0, 0)))
    tm_post = _pick(S, 1024)
    tm_e = 512
    w_oc, w_oa, w_ob, w_rs = (w_out_conv.astype(BF16), w_out_attn.astype(BF16), w_o.astype(BF16),
                              jnp.concatenate([w_r_hi, w_r_lo], axis=1))

    n_half = 2 if Bg % 2 == 0 and (Bg // 2 * S) % (SC_WORKERS * SC_CHUNK) == 0 else 1
    Bh = Bg // n_half

    def experts(grp, after):
        x1, rw, idx1, idx2, tile_expert, n_tiles, xs = grp
        ys = _moe(xs, tile_expert, n_tiles, w_gate, w_up, w_down, tm_e, after)
        halves = lambda idx: idx.reshape(n_half, SC_WORKERS, idx.shape[1] // n_half, SC_CHUNK)
        return ys, [_collect(ys, i1, i2) for i1, i2 in zip(halves(idx1), halves(idx2))]

    def finish(g, grp, gathered, out, after):
        for hh, (g1, g2) in enumerate(gathered):
            out = _final(grp[0], g1.reshape(Bh, S, ROW_WORDS), g2.reshape(Bh, S, ROW_WORDS), grp[1],
                         gate2, _pick(S, 512), g * Bg, hh * Bh, out, after if hh == 0 else [])
        return out

    out, prev, prev_gathered, ys_prev = None, None, None, None
    for g in range(n_grp):
        b0 = g * Bg
        qT_aug, k_aug, vT = _qkv(x, norm1_w.reshape(1, -1), shift1, scale1, w_qvT, w_k, w_f, b_f,
                                 qnwT, jnp.tile(k_norm_w, N_HEADS).reshape(1, -1), tm_qkv, b0, Bg)
        if prev is not None:
            ys_prev, prev_gathered = experts(prev, [qT_aug])
        y_b = _attention(qT_aug, k_aug, vT, tq, _pick(tq, 256),
                         [] if ys_prev is None else [ys_prev])
        x1, h2p, ridx, rw, counts = _post(x, y_b, norm1_w.reshape(1, -1), shift1, scale1, gate1,
                                          norm2_w.reshape(1, -1), shift2, scale2,
                                          w_cgg, cw, w_oc, w_oa, w_ob, w_rs, b_r, tm_post, b0)
        idx1, idx2, tile_expert, n_tiles, n_slots = _route_plan(ridx, counts[0, :N_EXPERTS], tm_e, T)
        xs = _dispatch(h2p.reshape(T, ROW_WORDS), idx1, idx2, n_slots)
        if prev is not None:
            out = finish(g - 1, prev, prev_gathered, out, [idx1])
        prev = (x1, rw, idx1, idx2, tile_expert, n_tiles, xs)
    _, last_gathered = experts(prev, [] if out is None else [out])
    return finish(n_grp - 1, prev, last_gathered, out, [])


def kernel(x, c, w_ada, b_ada, norm1_w, w_in, b_forget, conv_w, q_norm_w, k_norm_w, w_out_conv,
           w_out_attn, w_o, norm2_w, w_router_group, b_router_group, w_router_expert,
           b_router_expert, w_gate, w_up, w_down):
    for l in range(w_ada.shape[0]):
        x = _layer(x, c, w_ada[l], b_ada[l], norm1_w[l], w_in[l], b_forget[l], conv_w[l],
                   q_norm_w[l], k_norm_w[l], w_out_conv[l], w_out_attn[l], w_o[l], norm2_w[l],
                   w_router_group[l], b_router_group[l], w_router_expert[l], b_router_expert[l],
                   w_gate[l], w_up[l], w_down[l])
    return x
```

```python
import functools

import jax
import jax.numpy as jnp
import numpy as np
from jax import lax
from jax.experimental import pallas as pl
from jax.experimental.pallas import tpu as pltpu
from jax.experimental.pallas import tpu_sc as plsc

D_MODEL = 1024
CONV_WIDTH = 512
CONV_K = 3
N_HEADS = 8
HEAD_DIM = 64
ATTN_WIDTH = N_HEADS * HEAD_DIM
N_PAIRS = N_HEADS // 2
N_GROUPS = 4
EXPERTS_PER_GROUP = 8
N_EXPERTS = N_GROUPS * EXPERTS_PER_GROUP
D_EXPERT = 256
EPS = 1e-6
LANES = 128
AUG = 2 * LANES
BIAS_W = 6
VROWS = HEAD_DIM + 16
NEG = -1e30
LOG2E = 1.4426950408889634

F32 = jnp.float32
BF16 = jnp.bfloat16
VMEM_LIMIT = 56 * 1024 * 1024


def _sigmoid(z):
    return 1.0 / (1.0 + jnp.exp(-z))


def _split3(z):
    hi = z.astype(BF16)
    r = z - hi.astype(F32)
    mid = r.astype(BF16)
    lo = (r - mid.astype(F32)).astype(BF16)
    return hi, mid, lo


def _dot(a, b):
    return jnp.dot(a, b, preferred_element_type=F32)


def _modulated_norm(x, nw, shift, scale):
    ms = jnp.mean(x * x, axis=-1, keepdims=True)
    return (x * lax.rsqrt(ms + EPS) * nw) * (1.0 + scale) + shift


def _ada_kernel(c_ref, w_ref, b_ref, o_ref):
    c = c_ref[...]
    a = c * _sigmoid(c)
    o_ref[...] = jnp.dot(a, w_ref[...], precision=lax.Precision.HIGHEST,
                         preferred_element_type=F32) + b_ref[...]


def _ada(c, w_ada, b_ada):
    B = c.shape[0]
    n = w_ada.shape[1] // D_MODEL
    return pl.pallas_call(
        _ada_kernel,
        out_shape=jax.ShapeDtypeStruct((B, n * D_MODEL), F32),
        grid=(n,),
        in_specs=[pl.BlockSpec((B, D_MODEL), lambda j: (0, 0)),
                  pl.BlockSpec((D_MODEL, D_MODEL), lambda j: (0, j)),
                  pl.BlockSpec((1, D_MODEL), lambda j: (0, j))],
        out_specs=pl.BlockSpec((B, D_MODEL), lambda j: (0, j)),
        compiler_params=pltpu.CompilerParams(dimension_semantics=("arbitrary",)),
        name="ada",
    )(c, w_ada, b_ada)


QKV_ROWS = 256
_NT = (((1,), (1,)), ((), ()))


def _lane_tile(a, width):
    return jnp.concatenate([a] * (width // a.shape[1]), axis=1)


def _qkv_kernel(x_ref, nw_ref, sh_ref, sc_ref, wqv_ref, wk_ref, wf_ref, bf_ref, qnw_ref, knw_ref,
                gsum_ref, pq_ref, pk_ref, cq_ref, ck_ref,
                qT_ref, k_ref, vT_ref, carry_ref):
    tm = x_ref.shape[1]
    n_grp = max(tm // QKV_ROWS, 1)
    rows = tm // n_grp
    row = lax.broadcasted_iota(jnp.int32, (rows, LANES), 0)

    @pl.when(pl.program_id(1) == 0)
    def _():
        carry_ref[...] = jnp.zeros_like(carry_ref)

    st = [dict(rs=pl.ds(g * rows, rows)) for g in range(n_grp)]

    def project(d):
        h = _modulated_norm(x_ref[0, d["rs"], :], nw_ref[...], sh_ref[0], sc_ref[0])
        hb = h.astype(BF16)
        d["qvT"] = lax.dot_general(wqv_ref[...], hb, _NT, preferred_element_type=F32)
        d["k"] = _dot(hb, wk_ref[...])
        d["fl"] = _dot(hb, wf_ref[...]) + bf_ref[...]

    def norms(d):
        qvT = d["qvT"]
        heads = []
        for hd in range(N_HEADS):
            z = qvT[hd * HEAD_DIM:(hd + 1) * HEAD_DIM]
            heads.append(z * lax.rsqrt(jnp.mean(z * z, axis=0, keepdims=True) + EPS))
        qnT = jnp.concatenate(heads, axis=0) * _lane_tile(qnw_ref[...], rows)
        for j in range(N_PAIRS):
            qT_ref[0, j, :LANES, d["rs"]] = qnT[j * LANES:(j + 1) * LANES].astype(BF16)
            for t in range(2):
                r0 = ATTN_WIDTH + (2 * j + t) * HEAD_DIM
                vT_ref[0, j, t * VROWS:t * VROWS + HEAD_DIM, d["rs"]] = qvT[r0:r0 + HEAD_DIM].astype(BF16)
                vT_ref[0, j, t * VROWS + HEAD_DIM:(t + 1) * VROWS, d["rs"]] = \
                    jnp.ones((VROWS - HEAD_DIM, rows), BF16)
        del d["qvT"]
        k = d.pop("k")
        ss = _dot((k * k).astype(BF16), gsum_ref[...])
        kn = k * lax.rsqrt(ss * (1.0 / HEAD_DIM) + EPS) * knw_ref[...]
        for j in range(N_PAIRS):
            k_ref[0, d["rs"], j * AUG:j * AUG + LANES] = kn[:, j * LANES:(j + 1) * LANES].astype(BF16)

    def forget(d, before):
        fl = d.pop("fl")
        cum = jnp.minimum(fl, 0.0) - jnp.log(1.0 + jnp.exp(-jnp.abs(fl)))
        s = 1
        while s < rows:
            cum = cum + jnp.where(row >= s, pltpu.roll(cum, s, 0), 0.0)
            s *= 2
        cum = cum + before[7:8, :]
        d["tail"] = cum[rows - 8:, :]
        d["parts"] = jnp.concatenate(_split3(cum * LOG2E), axis=1)

    def bias(d):
        parts = d.pop("parts")
        eqT = (lax.dot_general(pq_ref[...], parts, _NT, preferred_element_type=F32)
               + _lane_tile(cq_ref[...], rows)).astype(BF16)
        ek = (_dot(parts, pk_ref[...]) + ck_ref[...]).astype(BF16)
        for j in range(N_PAIRS):
            qT_ref[0, j, LANES:, d["rs"]] = eqT
            k_ref[0, d["rs"], j * AUG + LANES:(j + 1) * AUG] = ek

    stages = [project, norms, None, bias]
    lag = 1
    for step in range(len(stages) + lag * (n_grp - 1)):
        for g, d in enumerate(st):
            kk = step - lag * g
            if 0 <= kk < len(stages):
                if stages[kk] is None:
                    forget(d, carry_ref[...] if g == 0 else st[g - 1]["tail"])
                else:
                    stages[kk](d)
    carry_ref[...] = st[-1]["tail"]


def _bias_placement():
    pq = np.zeros((LANES, 3 * LANES), np.float32)
    pk = np.zeros((3 * LANES, LANES), np.float32)
    cq = np.zeros((LANES, LANES), np.float32)
    ck = np.zeros((1, LANES), np.float32)
    for hd in range(N_HEADS):
        base = BIAS_W * hd
        for p in range(3):
            pq[base + p, p * LANES + hd] = 1.0
            pk[p * LANES + hd, base + 3 + p] = -1.0
            cq[base + 3 + p, :] = 1.0
            ck[0, base + p] = 1.0
    return (jnp.asarray(pq, BF16), jnp.asarray(pk, BF16), jnp.asarray(cq), jnp.asarray(ck))


def _qkv(x, nw, shift, scale, wqvT, wk, wf, bf, qnwT, knw, tm, b0, B):
    S = x.shape[1]
    gsum = jnp.asarray(np.kron(np.eye(N_HEADS), np.ones((HEAD_DIM, HEAD_DIM))), BF16)
    pq, pk, cq, ck = _bias_placement()
    const = lambda *shape: pl.BlockSpec(shape, lambda b, i: (0,) * len(shape),
                                        pipeline_mode=pl.Buffered(1))
    return pl.pallas_call(
        _qkv_kernel,
        out_shape=(jax.ShapeDtypeStruct((B, N_PAIRS, AUG, S), BF16),
                   jax.ShapeDtypeStruct((B, S, N_PAIRS * AUG), BF16),
                   jax.ShapeDtypeStruct((B, N_PAIRS, 2 * VROWS, S), BF16)),
        grid=(B, S // tm),
        in_specs=[pl.BlockSpec((1, tm, D_MODEL), lambda b, i: (b + b0, i, 0)),
                  const(1, D_MODEL),
                  pl.BlockSpec((1, 1, D_MODEL), lambda b, i: (b + b0, 0, 0)),
                  pl.BlockSpec((1, 1, D_MODEL), lambda b, i: (b + b0, 0, 0)),
                  const(2 * ATTN_WIDTH, D_MODEL),
                  const(D_MODEL, ATTN_WIDTH),
                  const(D_MODEL, LANES),
                  const(1, LANES),
                  const(ATTN_WIDTH, LANES),
                  const(1, ATTN_WIDTH),
                  const(ATTN_WIDTH, ATTN_WIDTH),
                  const(LANES, 3 * LANES),
                  const(3 * LANES, LANES),
                  const(LANES, LANES),
                  const(1, LANES)],
        out_specs=(pl.BlockSpec((1, N_PAIRS, AUG, tm), lambda b, i: (b, 0, 0, i)),
                   pl.BlockSpec((1, tm, N_PAIRS * AUG), lambda b, i: (b, i, 0)),
                   pl.BlockSpec((1, N_PAIRS, 2 * VROWS, tm), lambda b, i: (b, 0, 0, i))),
        scratch_shapes=[pltpu.VMEM((8, LANES), F32)],
        compiler_params=pltpu.CompilerParams(
            dimension_semantics=("arbitrary", "arbitrary"), vmem_limit_bytes=VMEM_LIMIT),
        name="qkv",
    )(x, nw, shift, scale, wqvT, wk, wf, bf, qnwT, knw, gsum, pq, pk, cq, ck)


def _attn_kernel(qT_ref, k_ref, vT_ref, *rest, tq, cw, nt):
    o_ref, qq_ref, s_ref, smax_ref, m_ref, acc_ref = rest[-6:]
    i = pl.program_id(2)
    n = tq // cw
    chains = [(a, t, c) for a in range(nt) for t in range(2) for c in range(n)]
    feat = lax.broadcasted_iota(jnp.int32, (AUG, tq), 0)
    for t in range(2):
        bias0 = LANES + BIAS_W * (2 * pl.program_id(1) + t)
        keep = ((feat >= t * HEAD_DIM) & (feat < (t + 1) * HEAD_DIM)) | \
               ((feat >= bias0) & (feat < bias0 + BIAS_W))
        for a in range(nt):
            qT = qT_ref[0, 0, :, a * tq:(a + 1) * tq]
            qh = jnp.where(keep, qT, jnp.zeros_like(qT))
            for c in range(n):
                qq_ref[chains.index((a, t, c))] = qh[:, c * cw:(c + 1) * cw]

    def scores(j, slot, which):
        k_blk = k_ref[0, pl.ds(pl.multiple_of(j * tq, tq), tq), :]
        for ci in which:
            s = _dot(k_blk, qq_ref[ci])
            s_ref[slot, ci] = s
            smax_ref[slot, ci] = jnp.broadcast_to(jnp.max(s, axis=0, keepdims=True), (8, cw))

    def absorb(j, slot, which, diagonal=()):
        start = pl.multiple_of(j * tq, tq)
        for ci in which:
            _, t, c = chains[ci]
            if ci in diagonal:
                kv = (c + 1) * cw
                causal = lax.broadcasted_iota(jnp.int32, (kv, cw), 0) <= \
                    lax.broadcasted_iota(jnp.int32, (kv, cw), 1) + c * cw
                s = jnp.where(causal, s_ref[slot, ci, :kv, :], NEG)
                smax = jnp.max(s, axis=0, keepdims=True)
            else:
                kv = tq
                s = s_ref[slot, ci]
                smax = smax_ref[slot, ci, 0:1]
            vj = vT_ref[0, 0, t * VROWS:(t + 1) * VROWS, pl.ds(start, kv)]
            m = m_ref[ci, 0:1]
            m_new = jnp.maximum(m, smax)
            p = jnp.exp2(s - m_new).astype(BF16)
            acc_ref[ci] = jnp.exp2(m - m_new) * acc_ref[ci] + _dot(vj, p)
            m_ref[ci] = jnp.broadcast_to(m_new, (8, cw))

    every = list(range(len(chains)))
    from_tile = lambda r: [ci for ci in every if chains[ci][0] >= r]
    m_ref[...] = jnp.full(m_ref.shape, NEG, F32)
    acc_ref[...] = jnp.zeros(acc_ref.shape, F32)
    scores(0, 0, every)

    def two_blocks(jj, _):
        j = 2 * jj
        scores(j + 1, 1, every)
        absorb(j, 0, every)
        scores(j + 2, 0, every)
        absorb(j + 1, 1, every)
        return 0

    lax.fori_loop(0, i * (nt // 2), two_blocks, 0)
    for r in range(nt):
        if r + 1 < nt:
            scores(nt * i + r + 1, (r + 1) % 2, from_tile(r + 1))
        absorb(nt * i + r, r % 2, from_tile(r),
               diagonal=[ci for ci in every if chains[ci][0] == r])

    for a in range(nt):
        outs = [acc_ref[ci, :HEAD_DIM] / acc_ref[ci, HEAD_DIM:HEAD_DIM + 1]
                for ci in every if chains[ci][0] == a]
        oT = jnp.concatenate([jnp.concatenate(outs[:n], axis=1), jnp.concatenate(outs[n:], axis=1)],
                             axis=0)
        o_ref[0, a * tq:(a + 1) * tq, :] = oT.T.astype(BF16)


def _attention(qT_aug, k_aug, vT, tq, cw, after):
    B, S, _ = k_aug.shape
    nt = 4 if S % (4 * tq) == 0 else 2
    n_chains = 2 * nt * tq // cw
    return pl.pallas_call(
        functools.partial(_attn_kernel, tq=tq, cw=cw, nt=nt),
        scratch_shapes=[pltpu.VMEM((n_chains, AUG, cw), BF16),
                        pltpu.VMEM((2, n_chains, tq, cw), F32),
                        pltpu.VMEM((2, n_chains, 8, cw), F32),
                        pltpu.VMEM((n_chains, 8, cw), F32),
                        pltpu.VMEM((n_chains, VROWS, cw), F32)],
        out_shape=jax.ShapeDtypeStruct((B, S, ATTN_WIDTH), BF16),
        grid=(B, N_PAIRS, S // (nt * tq)),
        in_specs=[pl.BlockSpec((1, 1, AUG, nt * tq), lambda b, j, i: (b, j, 0, i)),
                  pl.BlockSpec((1, S, AUG), lambda b, j, i: (b, 0, j)),
                  pl.BlockSpec((1, 1, 2 * VROWS, S), lambda b, j, i: (b, j, 0, 0))]
        + [pl.BlockSpec(memory_space=pl.ANY)] * len(after),
        out_specs=pl.BlockSpec((1, nt * tq, LANES), lambda b, j, i: (b, i, j)),
        compiler_params=pltpu.CompilerParams(
            dimension_semantics=("arbitrary", "arbitrary", "arbitrary"),
            vmem_limit_bytes=VMEM_LIMIT),
        name="attn",
    )(qT_aug, k_aug, vT, *after)


def _pack_bf16_pairs(z):
    w = z.shape[1] // 2
    bits = pltpu.bitcast(z.astype(BF16).astype(F32), jnp.uint32)
    return bits[:, :w] | (bits[:, w:] >> 16)


def _unpack_bf16_pairs(p):
    return (pltpu.bitcast(p & jnp.uint32(0xFFFF0000), F32), pltpu.bitcast(p << 16, F32))


POST_ROWS = 256


def _post_kernel(x_ref, yb_ref, n1_ref, sh1_ref, sc1_ref, g1_ref, n2_ref, sh2_ref, sc2_ref,
                 wc_ref, cw_ref, woc_ref, woa_ref, wo_ref, wr_ref, br_ref, tri_ref,
                 x1_ref, h2_ref, ridx_ref, rw_ref, cnt_ref, carry_ref):
    tm = x_ref.shape[1]
    n_grp = max(tm // POST_ROWS, 1)
    rows = tm // n_grp
    lane = lax.broadcasted_iota(jnp.int32, (rows, LANES), 1)
    row8 = lax.broadcasted_iota(jnp.int32, (8, CONV_WIDTH), 0)
    big = jnp.int32(1 << 20)

    @pl.when(pl.program_id(1) == 0)
    def _():
        carry_ref[...] = jnp.zeros_like(carry_ref)

    @pl.when((pl.program_id(0) == 0) & (pl.program_id(1) == 0))
    def _():
        cnt_ref[...] = jnp.zeros_like(cnt_ref)

    st = [dict(rs=pl.ds(g * rows, rows)) for g in range(n_grp)]

    def conv_in(d):
        d["x"] = x_ref[0, d["rs"], :]
        d["hb"] = _modulated_norm(d["x"], n1_ref[...], sh1_ref[0], sc1_ref[0]).astype(BF16)
        d["x_in"] = _dot(d["hb"], wc_ref[:, :CONV_WIDTH])
        d["conv_c"] = _dot(d["hb"], wc_ref[:, 2 * CONV_WIDTH:3 * CONV_WIDTH])
        d["conv_b"] = _dot(d["hb"], wc_ref[:, CONV_WIDTH:2 * CONV_WIDTH])

    def conv(d, prev):
        u = d.pop("conv_c") * d.pop("x_in")
        d["u_tail"] = u[rows - 8:, :]

        def shifted(k):
            r = pltpu.roll(u, k, 0)
            top = jnp.where(row8 < k, pltpu.roll(prev, k, 0), r[:8])
            return jnp.concatenate([top, r[8:]], axis=0)

        cw = cw_ref[...]
        cv = cw[0:1] * shifted(2) + cw[1:2] * shifted(1) + cw[2:3] * u
        d["y_a"] = (d.pop("conv_b") * cv).astype(BF16)

    def gates(d):
        d["p_b"] = _dot(yb_ref[0, d["rs"], :], woa_ref[...])
        d["gate_c"] = _dot(d["hb"], wc_ref[:, 3 * CONV_WIDTH:3 * CONV_WIDTH + D_MODEL])
        d["gate_a"] = _dot(d.pop("hb"), wc_ref[:, 3 * CONV_WIDTH + D_MODEL:])

    def branch_a(d):
        d["p_a"] = _dot(d.pop("y_a"), woc_ref[...])

    def merge(d):
        d["merged"] = (_sigmoid(d.pop("gate_c")) * d.pop("p_a")
                       + _sigmoid(d.pop("gate_a")) * d.pop("p_b")).astype(BF16)

    def out_proj(d):
        d["o"] = _dot(d.pop("merged"), wo_ref[...])

    def residual(d):
        x1 = d.pop("x") + g1_ref[0] * d.pop("o")
        x1_ref[0, d["rs"], :] = x1
        h2 = _modulated_norm(x1, n2_ref[...], sh2_ref[0], sc2_ref[0])
        h2_ref[0, d["rs"], :] = _pack_bf16_pairs(h2)
        d["h_hi"] = h2.astype(BF16)
        d["h_lo"] = (h2 - d["h_hi"].astype(F32)).astype(BF16)

    def router(d):
        both = _dot(d.pop("h_hi"), wr_ref[...])
        d["lg"] = (both[:, :LANES] + both[:, LANES:] + _dot(d.pop("h_lo"), wr_ref[:, :LANES])) \
            + br_ref[...]

    def first_argmax(vals):
        mx = jnp.max(vals, axis=-1, keepdims=True)
        idx = jnp.min(jnp.where(vals == mx, lane, big), axis=-1, keepdims=True)
        return mx, idx

    def route(d):
        lg = d.pop("lg")
        is_g = (lane >= N_EXPERTS) & (lane < N_EXPERTS + N_GROUPS)
        g_mx, g_lane = first_argmax(jnp.where(is_g, lg, NEG))
        p_sel = 1.0 / jnp.sum(jnp.where(is_g, jnp.exp(lg - g_mx), 0.0), axis=-1, keepdims=True)
        g_idx = g_lane - N_EXPERTS
        in_g = (lane >= g_idx * EXPERTS_PER_GROUP) & (lane < (g_idx + 1) * EXPERTS_PER_GROUP)
        le = jnp.where(in_g, lg, NEG)
        v1, i1 = first_argmax(le)
        v2, i2 = first_argmax(jnp.where(lane == i1, NEG, le))
        e2 = jnp.exp(v2 - v1)
        w1 = p_sel / (1.0 + e2)
        w2 = w1 * e2
        rw_ref[0, d["rs"], :] = jnp.where(lane == 0, w1, 0.0) + jnp.where(lane == 1, w2, 0.0)
        d["i1"], d["i2"] = i1, i2
        d["onehot"] = jnp.where((lane == i1) | (lane == i2), 1.0, 0.0)

    def rank(d):
        onehot = d.pop("onehot")
        before = _dot(tri_ref[...], onehot.astype(BF16)) + cnt_ref[0:1, :]
        cnt_ref[...] = cnt_ref[...] + jnp.sum(onehot, axis=0, keepdims=True)
        i1, i2 = d.pop("i1"), d.pop("i2")
        r1 = jnp.sum(jnp.where(lane == i1, before, 0.0), axis=-1, keepdims=True)
        r2 = jnp.sum(jnp.where(lane == i2, before, 0.0), axis=-1, keepdims=True)
        rec = (jnp.where(lane == 0, i1.astype(F32), 0.0) + jnp.where(lane == 1, i2.astype(F32), 0.0)
               + jnp.where(lane == 2, r1, 0.0) + jnp.where(lane == 3, r2, 0.0))
        ridx_ref[0, :, d["rs"]] = rec.T[:8].astype(jnp.int32)

    stages = [conv_in, None, gates, branch_a, merge, out_proj, residual, router, route, rank]
    lag = 2
    for step in range(len(stages) + lag * (n_grp - 1)):
        for g, d in enumerate(st):
            k = step - lag * g
            if 0 <= k < len(stages):
                if stages[k] is None:
                    conv(d, carry_ref[...] if g == 0 else st[g - 1]["u_tail"])
                else:
                    stages[k](d)
    carry_ref[...] = st[-1]["u_tail"]


def _post(x, yb, n1, sh1, sc1, g1, n2, sh2, sc2, wc, cw, woc, woa, wo, wr, br, tm, b0):
    B, S, _ = yb.shape
    rows = tm // max(tm // POST_ROWS, 1)
    tri = jnp.asarray(np.tril(np.ones((rows, rows), np.float32), -1), BF16)
    const = lambda *shape: pl.BlockSpec(shape, lambda b, i: (0,) * len(shape),
                                        pipeline_mode=pl.Buffered(1))
    perb = pl.BlockSpec((1, 1, D_MODEL), lambda b, i: (b + b0, 0, 0))
    tok = lambda w: pl.BlockSpec((1, tm, w), lambda b, i: (b, i, 0))
    return pl.pallas_call(
        _post_kernel,
        out_shape=(jax.ShapeDtypeStruct((B, S, D_MODEL), F32),
                   jax.ShapeDtypeStruct((B, S, D_MODEL // 2), jnp.uint32),
                   jax.ShapeDtypeStruct((B, 8, S), jnp.int32),
                   jax.ShapeDtypeStruct((B, S, LANES), F32),
                   jax.ShapeDtypeStruct((8, LANES), F32)),
        grid=(B, S // tm),
        in_specs=[pl.BlockSpec((1, tm, D_MODEL), lambda b, i: (b + b0, i, 0)), tok(ATTN_WIDTH),
                  const(1, D_MODEL), perb, perb, perb,
                  const(1, D_MODEL), perb, perb,
                  const(D_MODEL, 3 * CONV_WIDTH + 2 * D_MODEL),
                  const(8, CONV_WIDTH),
                  const(CONV_WIDTH, D_MODEL), const(ATTN_WIDTH, D_MODEL),
                  const(D_MODEL, D_MODEL),
                  const(D_MODEL, 2 * LANES), const(1, LANES), const(rows, rows)],
        out_specs=(tok(D_MODEL), tok(D_MODEL // 2),
                   pl.BlockSpec((1, 8, tm), lambda b, i: (b, 0, i)), tok(LANES),
                   pl.BlockSpec((8, LANES), lambda b, i: (0, 0))),
        scratch_shapes=[pltpu.VMEM((8, CONV_WIDTH), F32)],
        compiler_params=pltpu.CompilerParams(
            dimension_semantics=("arbitrary", "arbitrary"), vmem_limit_bytes=VMEM_LIMIT),
        name="post",
    )(x, yb, n1, sh1, sc1, g1, n2, sh2, sc2, wc, cw, woc, woa, wo, wr, br, tri)


SC_CORES = 2
SC_SUBCORES = 16
SC_WORKERS = SC_CORES * SC_SUBCORES
SC_CHUNK = 64
ROW_WORDS = D_MODEL // 2


def _sc_mesh():
    return plsc.VectorSubcoreMesh(core_axis_name="c", subcore_axis_name="s",
                                  num_cores=SC_CORES, num_subcores=SC_SUBCORES)


def _dispatch_body(rows_hbm, idx1_hbm, idx2_hbm, xs_hbm, idx1_v, idx2_v, rows_v, *, n_chunks):
    wid = lax.axis_index("s") * SC_CORES + lax.axis_index("c")
    pltpu.sync_copy(idx1_hbm.at[wid], idx1_v)
    pltpu.sync_copy(idx2_hbm.at[wid], idx2_v)
    base = wid * (n_chunks * SC_CHUNK)

    @pl.loop(0, n_chunks)
    def _(j):
        pltpu.sync_copy(rows_hbm.at[pl.ds(base + j * SC_CHUNK, SC_CHUNK)], rows_v)
        pltpu.sync_copy(rows_v, xs_hbm.at[idx1_v.at[j]])
        pltpu.sync_copy(rows_v, xs_hbm.at[idx2_v.at[j]])


def _sc_scratch(n_chunks):
    return [pltpu.VMEM((n_chunks, SC_CHUNK), jnp.int32), pltpu.VMEM((n_chunks, SC_CHUNK), jnp.int32),
            pltpu.VMEM((SC_CHUNK, ROW_WORDS), jnp.uint32)]


def _dispatch(rows, idx1, idx2, n_slots):
    n_chunks = idx1.shape[1]
    return pl.kernel(
        functools.partial(_dispatch_body, n_chunks=n_chunks),
        out_type=jax.ShapeDtypeStruct((n_slots, ROW_WORDS), jnp.uint32),
        mesh=_sc_mesh(),
        scratch_types=_sc_scratch(n_chunks),
        name="dispatch",
    )(rows, idx1, idx2)


def _collect_body(ys_hbm, idx1_hbm, idx2_hbm, g1_hbm, g2_hbm, idx1_v, idx2_v, rows_v, *, n_chunks):
    wid = lax.axis_index("s") * SC_CORES + lax.axis_index("c")
    pltpu.sync_copy(idx1_hbm.at[wid], idx1_v)
    pltpu.sync_copy(idx2_hbm.at[wid], idx2_v)
    base = wid * (n_chunks * SC_CHUNK)

    @pl.loop(0, n_chunks)
    def _(j):
        dst = pl.ds(base + j * SC_CHUNK, SC_CHUNK)
        pltpu.sync_copy(ys_hbm.at[idx1_v.at[j]], rows_v)
        pltpu.sync_copy(rows_v, g1_hbm.at[dst])
        pltpu.sync_copy(ys_hbm.at[idx2_v.at[j]], rows_v)
        pltpu.sync_copy(rows_v, g2_hbm.at[dst])


def _collect(ys, idx1, idx2):
    n_chunks = idx1.shape[1]
    out = jax.ShapeDtypeStruct((SC_WORKERS * n_chunks * SC_CHUNK, ROW_WORDS), jnp.uint32)
    return pl.kernel(
        functools.partial(_collect_body, n_chunks=n_chunks),
        out_type=(out, out),
        mesh=_sc_mesh(),
        scratch_types=_sc_scratch(n_chunks),
        name="collect",
    )(ys, idx1, idx2)


def _moe_kernel(te_ref, nt_ref, xs_ref, wg_ref, wu_ref, wd_ref, *rest):
    ys_ref = rest[-1]

    @pl.when(pl.program_id(0) < nt_ref[0])
    def _():
        left, right = _unpack_bf16_pairs(xs_ref[...])
        xb = jnp.concatenate([left.astype(BF16), right.astype(BF16)], axis=1)
        g = _dot(xb, wg_ref[0].astype(BF16))
        u = _dot(xb, wu_ref[0].astype(BF16))
        a = (g * _sigmoid(g) * u).astype(BF16)
        ys_ref[...] = _pack_bf16_pairs(_dot(a, wd_ref[0].astype(BF16)))


def _moe(xs, tile_expert, n_tiles, wg, wu, wd, tm, after):
    n_slots = xs.shape[0]
    row_blk = lambda i, te, nt: (jnp.minimum(i, nt[0] - 1), 0)
    w_blk = lambda i, te, nt: (te[i], 0, 0)
    return pl.pallas_call(
        _moe_kernel,
        out_shape=jax.ShapeDtypeStruct((n_slots, ROW_WORDS), jnp.uint32),
        grid_spec=pltpu.PrefetchScalarGridSpec(
            num_scalar_prefetch=2,
            grid=(n_slots // tm,),
            in_specs=[pl.BlockSpec((tm, ROW_WORDS), row_blk),
                      pl.BlockSpec((1, D_MODEL, D_EXPERT), w_blk),
                      pl.BlockSpec((1, D_MODEL, D_EXPERT), w_blk),
                      pl.BlockSpec((1, D_EXPERT, D_MODEL), w_blk)]
            + [pl.BlockSpec(memory_space=pl.ANY)] * len(after),
            out_specs=pl.BlockSpec((tm, ROW_WORDS), row_blk)),
        compiler_params=pltpu.CompilerParams(
            dimension_semantics=("arbitrary",), vmem_limit_bytes=VMEM_LIMIT),
        name="moe",
    )(tile_expert, n_tiles, xs, wg, wu, wd, *after)


def _final_kernel(x1_ref, g1_ref, g2_ref, rw_ref, gate_ref, *rest):
    o_ref = rest[-1]
    rw = rw_ref[0]
    w1 = rw[:, 0:1]
    w2 = rw[:, 1:2]
    a_l, a_r = _unpack_bf16_pairs(g1_ref[0])
    b_l, b_r = _unpack_bf16_pairs(g2_ref[0])
    moe = jnp.concatenate([w1 * a_l + w2 * b_l, w1 * a_r + w2 * b_r], axis=1)
    o_ref[0] = x1_ref[0] + gate_ref[0] * moe


def _final(x1, g1, g2, rw, gate2, tm, b0, src0, out_prev, after):
    nb, S, _ = g1.shape
    grp = lambda w: pl.BlockSpec((1, tm, w), lambda b, i: (b + src0, i, 0))
    tok = lambda w: pl.BlockSpec((1, tm, w), lambda b, i: (b, i, 0))
    extra = (() if out_prev is None else (out_prev,)) + tuple(after)
    return pl.pallas_call(
        _final_kernel,
        out_shape=jax.ShapeDtypeStruct((gate2.shape[0], S, D_MODEL), F32),
        grid=(nb, S // tm),
        in_specs=[grp(D_MODEL), tok(ROW_WORDS), tok(ROW_WORDS), grp(LANES),
                  pl.BlockSpec((1, 1, D_MODEL), lambda b, i: (b + b0 + src0, 0, 0))]
        + [pl.BlockSpec(memory_space=pl.ANY)] * len(extra),
        out_specs=pl.BlockSpec((1, tm, D_MODEL), lambda b, i: (b + b0 + src0, i, 0)),
        input_output_aliases={} if out_prev is None else {5: 0},
        compiler_params=pltpu.CompilerParams(
            dimension_semantics=("arbitrary", "arbitrary"), vmem_limit_bytes=VMEM_LIMIT),
        name="final",
    )(x1, g1, g2, rw, gate2, *extra)


def _pick(n, pref):
    t = min(n, pref)
    assert n % t == 0, (n, t)
    return t


def _slots_kernel(offs_ref, r_ref, o_ref):
    r = r_ref[0]
    base = jnp.zeros_like(r)
    for e in range(N_EXPERTS):
        base = jnp.where(r == e, offs_ref[e], base)
    o_ref[0] = base + pltpu.roll(r, 6, 0)


def _slots(ridx, offs):
    B, _, S = ridx.shape
    return pl.pallas_call(
        _slots_kernel,
        out_shape=jax.ShapeDtypeStruct((B, 8, S), jnp.int32),
        grid_spec=pltpu.PrefetchScalarGridSpec(
            num_scalar_prefetch=1, grid=(B,),
            in_specs=[pl.BlockSpec((1, 8, S), lambda b, offs: (b, 0, 0))],
            out_specs=pl.BlockSpec((1, 8, S), lambda b, offs: (b, 0, 0))),
        compiler_params=pltpu.CompilerParams(dimension_semantics=("arbitrary",)),
        name="slots",
    )(offs, ridx)


def _route_plan(ridx, counts, tm_e, T):
    counts = counts.astype(jnp.int32)
    tiles = (counts + tm_e - 1) // tm_e
    tile_end = jnp.cumsum(tiles)
    offs = (tile_end - tiles) * tm_e
    slots = _slots(ridx, offs)
    n_chunks = T // (SC_WORKERS * SC_CHUNK)
    idx1 = slots[:, 0, :].reshape(SC_WORKERS, n_chunks, SC_CHUNK)
    idx2 = slots[:, 1, :].reshape(SC_WORKERS, n_chunks, SC_CHUNK)
    n_tiles_max = 2 * T // tm_e + N_EXPERTS
    tile_ids = jnp.arange(n_tiles_max, dtype=jnp.int32)
    tile_expert = jnp.sum((tile_end[None, :] <= tile_ids[:, None]).astype(jnp.int32), axis=1)
    tile_expert = jnp.minimum(tile_expert, N_EXPERTS - 1)
    return idx1, idx2, tile_expert, tile_end[-1:].astype(jnp.int32), n_tiles_max * tm_e


def _layer(x, c, w_ada, b_ada, norm1_w, w_in, b_forget, conv_w, q_norm_w, k_norm_w,
           w_out_conv, w_out_attn, w_o, norm2_w, w_rg, b_rg, w_re, b_re, w_gate, w_up, w_down):
    B, S, _ = x.shape
    n_grp = 2 if B % 2 == 0 and (B // 2 * S) % (SC_WORKERS * SC_CHUNK) == 0 else 1
    Bg = B // n_grp
    T = Bg * S
    assert T % (SC_WORKERS * SC_CHUNK) == 0, T
    mod = _ada(c, w_ada, b_ada.reshape(1, -1)).reshape(B, 6, 1, D_MODEL)
    shift1, scale1, gate1, shift2, scale2, gate2 = (mod[:, t] for t in range(6))

    cuts = np.cumsum([0, CONV_WIDTH, CONV_WIDTH, CONV_WIDTH, ATTN_WIDTH, ATTN_WIDTH, ATTN_WIDTH,
                      N_HEADS, D_MODEL, D_MODEL])
    w_conv3 = w_in[:, cuts[0]:cuts[3]]
    w_qvT = jnp.concatenate([w_in[:, cuts[3]:cuts[4]], w_in[:, cuts[5]:cuts[6]]], axis=1).T.astype(BF16)
    w_k = w_in[:, cuts[4]:cuts[5]].astype(BF16)
    w_f = jnp.pad(w_in[:, cuts[6]:cuts[7]], ((0, 0), (0, LANES - N_HEADS))).astype(BF16)
    b_f = jnp.pad(b_forget, (0, LANES - N_HEADS)).reshape(1, LANES)
    w_cgg = jnp.concatenate([w_conv3, w_in[:, cuts[7]:cuts[9]]], axis=1).astype(BF16)

    tm_qkv = _pick(S, 1024)
    qnwT = jnp.broadcast_to((jnp.tile(q_norm_w, N_HEADS) * (LOG2E * HEAD_DIM ** -0.5))[:, None],
                            (ATTN_WIDTH, LANES))
    tq = _pick(S // 2, 512)
    w_r = jnp.pad(jnp.concatenate([w_re, w_rg], axis=1),
                  ((0, 0), (0, LANES - N_EXPERTS - N_GROUPS)))
    w_r_hi = w_r.astype(BF16)
    w_r_lo = (w_r - w_r_hi.astype(F32)).astype(BF16)
    b_r = jnp.pad(jnp.concatenate([b_re, b_rg]), (0, LANES - N_EXPERTS - N_GROUPS)).reshape(1, LANES)
    cw = jnp.pad(conv_w, ((0, 8 - CONV_K), (0, 0)))
    tm_post = _pick(S, 1024)
    tm_e = 512
    w_oc, w_oa, w_ob, w_rs = (w_out_conv.astype(BF16), w_out_attn.astype(BF16), w_o.astype(BF16),
                              jnp.concatenate([w_r_hi, w_r_lo], axis=1))

    n_half = 2 if Bg % 2 == 0 and (Bg // 2 * S) % (SC_WORKERS * SC_CHUNK) == 0 else 1
    Bh = Bg // n_half

    def experts(grp, after):
        x1, rw, idx1, idx2, tile_expert, n_tiles, xs = grp
        ys = _moe(xs, tile_expert, n_tiles, w_gate, w_up, w_down, tm_e, after)
        halves = lambda idx: idx.reshape(n_half, SC_WORKERS, idx.shape[1] // n_half, SC_CHUNK)
        return ys, [_collect(ys, i1, i2) for i1, i2 in zip(halves(idx1), halves(idx2))]

    def finish(g, grp, gathered, out, after):
        for hh, (g1, g2) in enumerate(gathered):
            out = _final(grp[0], g1.reshape(Bh, S, ROW_WORDS), g2.reshape(Bh, S, ROW_WORDS), grp[1],
                         gate2, _pick(S, 512), g * Bg, hh * Bh, out, after if hh == 0 else [])
        return out

    out, prev, prev_gathered, ys_prev = None, None, None, None
    for g in range(n_grp):
        b0 = g * Bg
        qT_aug, k_aug, vT = _qkv(x, norm1_w.reshape(1, -1), shift1, scale1, w_qvT, w_k, w_f, b_f,
                                 qnwT, jnp.tile(k_norm_w, N_HEADS).reshape(1, -1), tm_qkv, b0, Bg)
        if prev is not None:
            ys_prev, prev_gathered = experts(prev, [qT_aug])
        y_b = _attention(qT_aug, k_aug, vT, tq, _pick(tq, 256),
                         [] if ys_prev is None else [ys_prev])
        x1, h2p, ridx, rw, counts = _post(x, y_b, norm1_w.reshape(1, -1), shift1, scale1, gate1,
                                          norm2_w.reshape(1, -1), shift2, scale2,
                                          w_cgg, cw, w_oc, w_oa, w_ob, w_rs, b_r, tm_post, b0)
        idx1, idx2, tile_expert, n_tiles, n_slots = _route_plan(ridx, counts[0, :N_EXPERTS], tm_e, T)
        xs = _dispatch(h2p.reshape(T, ROW_WORDS), idx1, idx2, n_slots)
        if prev is not None:
            out = finish(g - 1, prev, prev_gathered, out, [idx1])
        prev = (x1, rw, idx1, idx2, tile_expert, n_tiles, xs)
    _, last_gathered = experts(prev, [] if out is None else [out])
    return finish(n_grp - 1, prev, last_gathered, out, [])


def kernel(x, c, w_ada, b_ada, norm1_w, w_in, b_forget, conv_w, q_norm_w, k_norm_w, w_out_conv,
           w_out_attn, w_o, norm2_w, w_router_group, b_router_group, w_router_expert,
           b_router_expert, w_gate, w_up, w_down):
    for l in range(w_ada.shape[0]):
        x = _layer(x, c, w_ada[l], b_ada[l], norm1_w[l], w_in[l], b_forget[l], conv_w[l],
                   q_norm_w[l], k_norm_w[l], w_out_conv[l], w_out_attn[l], w_o[l], norm2_w[l],
                   w_router_group[l], b_router_group[l], w_router_expert[l], b_router_expert[l],
                   w_gate[l], w_up[l], w_down[l])
    return x
```

```python
import functools

import jax
import jax.numpy as jnp
import numpy as np
from jax import lax
from jax.experimental import pallas as pl
from jax.experimental.pallas import tpu as pltpu
from jax.experimental.pallas import tpu_sc as plsc

D_MODEL = 1024
CONV_WIDTH = 512
CONV_K = 3
N_HEADS = 8
HEAD_DIM = 64
ATTN_WIDTH = N_HEADS * HEAD_DIM
N_PAIRS = N_HEADS // 2
N_GROUPS = 4
EXPERTS_PER_GROUP = 8
N_EXPERTS = N_GROUPS * EXPERTS_PER_GROUP
D_EXPERT = 256
EPS = 1e-6
LANES = 128
AUG = 2 * LANES
BIAS_W = 6
VROWS = HEAD_DIM + 16
NEG = -1e30
LOG2E = 1.4426950408889634

F32 = jnp.float32
BF16 = jnp.bfloat16
VMEM_LIMIT = 56 * 1024 * 1024


def _sigmoid(z):
    return 1.0 / (1.0 + jnp.exp(-z))


def _split3(z):
    hi = z.astype(BF16)
    r = z - hi.astype(F32)
    mid = r.astype(BF16)
    lo = (r - mid.astype(F32)).astype(BF16)
    return hi, mid, lo


def _dot(a, b):
    return jnp.dot(a, b, preferred_element_type=F32)


def _modulated_norm(x, nw, shift, scale):
    ms = jnp.mean(x * x, axis=-1, keepdims=True)
    return (x * lax.rsqrt(ms + EPS) * nw) * (1.0 + scale) + shift


def _ada_kernel(c_ref, w_ref, b_ref, o_ref):
    c = c_ref[...]
    a = c * _sigmoid(c)
    o_ref[...] = jnp.dot(a, w_ref[...], precision=lax.Precision.HIGHEST,
                         preferred_element_type=F32) + b_ref[...]


def _ada(c, w_ada, b_ada):
    B = c.shape[0]
    n = w_ada.shape[1] // D_MODEL
    return pl.pallas_call(
        _ada_kernel,
        out_shape=jax.ShapeDtypeStruct((B, n * D_MODEL), F32),
        grid=(n,),
        in_specs=[pl.BlockSpec((B, D_MODEL), lambda j: (0, 0)),
                  pl.BlockSpec((D_MODEL, D_MODEL), lambda j: (0, j)),
                  pl.BlockSpec((1, D_MODEL), lambda j: (0, j))],
        out_specs=pl.BlockSpec((B, D_MODEL), lambda j: (0, j)),
        compiler_params=pltpu.CompilerParams(dimension_semantics=("arbitrary",)),
        name="ada",
    )(c, w_ada, b_ada)


QKV_ROWS = 256
_NT = (((1,), (1,)), ((), ()))


def _lane_tile(a, width):
    return jnp.concatenate([a] * (width // a.shape[1]), axis=1)


def _qkv_kernel(x_ref, nw_ref, sh_ref, sc_ref, wqv_ref, wk_ref, wf_ref, bf_ref, qnw_ref, knw_ref,
                gsum_ref, pq_ref, pk_ref, cq_ref, ck_ref,
                qT_ref, k_ref, vT_ref, carry_ref):
    tm = x_ref.shape[1]
    n_grp = max(tm // QKV_ROWS, 1)
    rows = tm // n_grp
    row = lax.broadcasted_iota(jnp.int32, (rows, LANES), 0)

    @pl.when(pl.program_id(1) == 0)
    def _():
        carry_ref[...] = jnp.zeros_like(carry_ref)

    st = [dict(rs=pl.ds(g * rows, rows)) for g in range(n_grp)]

    def project(d):
        h = _modulated_norm(x_ref[0, d["rs"], :], nw_ref[...], sh_ref[0], sc_ref[0])
        hb = h.astype(BF16)
        d["qvT"] = lax.dot_general(wqv_ref[...], hb, _NT, preferred_element_type=F32)
        d["k"] = _dot(hb, wk_ref[...])
        d["fl"] = _dot(hb, wf_ref[...]) + bf_ref[...]

    def norms(d):
        qvT = d["qvT"]
        heads = []
        for hd in range(N_HEADS):
            z = qvT[hd * HEAD_DIM:(hd + 1) * HEAD_DIM]
            heads.append(z * lax.rsqrt(jnp.mean(z * z, axis=0, keepdims=True) + EPS))
        qnT = jnp.concatenate(heads, axis=0) * _lane_tile(qnw_ref[...], rows)
        for j in range(N_PAIRS):
            qT_ref[0, j, :LANES, d["rs"]] = qnT[j * LANES:(j + 1) * LANES].astype(BF16)
            for t in range(2):
                r0 = ATTN_WIDTH + (2 * j + t) * HEAD_DIM
                vT_ref[0, j, t * VROWS:t * VROWS + HEAD_DIM, d["rs"]] = qvT[r0:r0 + HEAD_DIM].astype(BF16)
                vT_ref[0, j, t * VROWS + HEAD_DIM:(t + 1) * VROWS, d["rs"]] = \
                    jnp.ones((VROWS - HEAD_DIM, rows), BF16)
        del d["qvT"]
        k = d.pop("k")
        ss = _dot((k * k).astype(BF16), gsum_ref[...])
        kn = k * lax.rsqrt(ss * (1.0 / HEAD_DIM) + EPS) * knw_ref[...]
        for j in range(N_PAIRS):
            k_ref[0, d["rs"], j * AUG:j * AUG + LANES] = kn[:, j * LANES:(j + 1) * LANES].astype(BF16)

    def forget(d, before):
        fl = d.pop("fl")
        cum = jnp.minimum(fl, 0.0) - jnp.log(1.0 + jnp.exp(-jnp.abs(fl)))
        s = 1
        while s < rows:
            cum = cum + jnp.where(row >= s, pltpu.roll(cum, s, 0), 0.0)
            s *= 2
        cum = cum + before[7:8, :]
        d["tail"] = cum[rows - 8:, :]
        d["parts"] = jnp.concatenate(_split3(cum * LOG2E), axis=1)

    def bias(d):
        parts = d.pop("parts")
        eqT = (lax.dot_general(pq_ref[...], parts, _NT, preferred_element_type=F32)
               + _lane_tile(cq_ref[...], rows)).astype(BF16)
        ek = (_dot(parts, pk_ref[...]) + ck_ref[...]).astype(BF16)
        for j in range(N_PAIRS):
            qT_ref[0, j, LANES:, d["rs"]] = eqT
            k_ref[0, d["rs"], j * AUG + LANES:(j + 1) * AUG] = ek

    stages = [project, norms, None, bias]
    lag = 1
    for step in range(len(stages) + lag * (n_grp - 1)):
        for g, d in enumerate(st):
            kk = step - lag * g
            if 0 <= kk < len(stages):
                if stages[kk] is None:
                    forget(d, carry_ref[...] if g == 0 else st[g - 1]["tail"])
                else:
                    stages[kk](d)
    carry_ref[...] = st[-1]["tail"]


def _bias_placement():
    pq = np.zeros((LANES, 3 * LANES), np.float32)
    pk = np.zeros((3 * LANES, LANES), np.float32)
    cq = np.zeros((LANES, LANES), np.float32)
    ck = np.zeros((1, LANES), np.float32)
    for hd in range(N_HEADS):
        base = BIAS_W * hd
        for p in range(3):
            pq[base + p, p * LANES + hd] = 1.0
            pk[p * LANES + hd, base + 3 + p] = -1.0
            cq[base + 3 + p, :] = 1.0
            ck[0, base + p] = 1.0
    return (jnp.asarray(pq, BF16), jnp.asarray(pk, BF16), jnp.asarray(cq), jnp.asarray(ck))


def _qkv(x, nw, shift, scale, wqvT, wk, wf, bf, qnwT, knw, tm, b0, B):
    S = x.shape[1]
    gsum = jnp.asarray(np.kron(np.eye(N_HEADS), np.ones((HEAD_DIM, HEAD_DIM))), BF16)
    pq, pk, cq, ck = _bias_placement()
    const = lambda *shape: pl.BlockSpec(shape, lambda b, i: (0,) * len(shape),
                                        pipeline_mode=pl.Buffered(1))
    return pl.pallas_call(
        _qkv_kernel,
        out_shape=(jax.ShapeDtypeStruct((B, N_PAIRS, AUG, S), BF16),
                   jax.ShapeDtypeStruct((B, S, N_PAIRS * AUG), BF16),
                   jax.ShapeDtypeStruct((B, N_PAIRS, 2 * VROWS, S), BF16)),
        grid=(B, S // tm),
        in_specs=[pl.BlockSpec((1, tm, D_MODEL), lambda b, i: (b + b0, i, 0)),
                  const(1, D_MODEL),
                  pl.BlockSpec((1, 1, D_MODEL), lambda b, i: (b + b0, 0, 0)),
                  pl.BlockSpec((1, 1, D_MODEL), lambda b, i: (b + b0, 0, 0)),
                  const(2 * ATTN_WIDTH, D_MODEL),
                  const(D_MODEL, ATTN_WIDTH),
                  const(D_MODEL, LANES),
                  const(1, LANES),
                  const(ATTN_WIDTH, LANES),
                  const(1, ATTN_WIDTH),
                  const(ATTN_WIDTH, ATTN_WIDTH),
                  const(LANES, 3 * LANES),
                  const(3 * LANES, LANES),
                  const(LANES, LANES),
                  const(1, LANES)],
        out_specs=(pl.BlockSpec((1, N_PAIRS, AUG, tm), lambda b, i: (b, 0, 0, i)),
                   pl.BlockSpec((1, tm, N_PAIRS * AUG), lambda b, i: (b, i, 0)),
                   pl.BlockSpec((1, N_PAIRS, 2 * VROWS, tm), lambda b, i: (b, 0, 0, i))),
        scratch_shapes=[pltpu.VMEM((8, LANES), F32)],
        compiler_params=pltpu.CompilerParams(
            dimension_semantics=("arbitrary", "arbitrary"), vmem_limit_bytes=VMEM_LIMIT),
        name="qkv",
    )(x, nw, shift, scale, wqvT, wk, wf, bf, qnwT, knw, gsum, pq, pk, cq, ck)


def _attn_kernel(qT_ref, k_ref, vT_ref, *rest, tq, cw, nt):
    o_ref, qq_ref, s_ref, smax_ref, m_ref, acc_ref = rest[-6:]
    i = pl.program_id(2)
    n = tq // cw
    chains = [(a, t, c) for a in range(nt) for t in range(2) for c in range(n)]
    feat = lax.broadcasted_iota(jnp.int32, (AUG, tq), 0)
    for t in range(2):
        bias0 = LANES + BIAS_W * (2 * pl.program_id(1) + t)
        keep = ((feat >= t * HEAD_DIM) & (feat < (t + 1) * HEAD_DIM)) | \
               ((feat >= bias0) & (feat < bias0 + BIAS_W))
        for a in range(nt):
            qT = qT_ref[0, 0, :, a * tq:(a + 1) * tq]
            qh = jnp.where(keep, qT, jnp.zeros_like(qT))
            for c in range(n):
                qq_ref[chains.index((a, t, c))] = qh[:, c * cw:(c + 1) * cw]
    kpos = lax.broadcasted_iota(jnp.int32, (tq, cw), 0)
    qpos = lax.broadcasted_iota(jnp.int32, (tq, cw), 1)

    def scores(j, slot, which):
        k_blk = k_ref[0, pl.ds(pl.multiple_of(j * tq, tq), tq), :]
        for ci in which:
            s = _dot(k_blk, qq_ref[ci])
            s_ref[slot, ci] = s
            smax_ref[slot, ci] = jnp.broadcast_to(jnp.max(s, axis=0, keepdims=True), (8, cw))

    def absorb(j, slot, which, diagonal=()):
        start = pl.multiple_of(j * tq, tq)
        for ci in which:
            _, t, c = chains[ci]
            vj = vT_ref[0, 0, t * VROWS:(t + 1) * VROWS, pl.ds(start, tq)]
            s = s_ref[slot, ci]
            if ci in diagonal:
                s = jnp.where(kpos <= qpos + c * cw, s, NEG)
                smax = jnp.max(s, axis=0, keepdims=True)
            else:
                smax = smax_ref[slot, ci, 0:1]
            m = m_ref[ci, 0:1]
            m_new = jnp.maximum(m, smax)
            p = jnp.exp2(s - m_new).astype(BF16)
            acc_ref[ci] = jnp.exp2(m - m_new) * acc_ref[ci] + _dot(vj, p)
            m_ref[ci] = jnp.broadcast_to(m_new, (8, cw))

    every = list(range(len(chains)))
    from_tile = lambda r: [ci for ci in every if chains[ci][0] >= r]
    m_ref[...] = jnp.full(m_ref.shape, NEG, F32)
    acc_ref[...] = jnp.zeros(acc_ref.shape, F32)
    scores(0, 0, every)

    def two_blocks(jj, _):
        j = 2 * jj
        scores(j + 1, 1, every)
        absorb(j, 0, every)
        scores(j + 2, 0, every)
        absorb(j + 1, 1, every)
        return 0

    lax.fori_loop(0, i * (nt // 2), two_blocks, 0)
    for r in range(nt):
        if r + 1 < nt:
            scores(nt * i + r + 1, (r + 1) % 2, from_tile(r + 1))
        absorb(nt * i + r, r % 2, from_tile(r),
               diagonal=[ci for ci in every if chains[ci][0] == r])

    for a in range(nt):
        outs = [acc_ref[ci, :HEAD_DIM] / acc_ref[ci, HEAD_DIM:HEAD_DIM + 1]
                for ci in every if chains[ci][0] == a]
        oT = jnp.concatenate([jnp.concatenate(outs[:n], axis=1), jnp.concatenate(outs[n:], axis=1)],
                             axis=0)
        o_ref[0, a * tq:(a + 1) * tq, :] = oT.T.astype(BF16)


def _attention(qT_aug, k_aug, vT, tq, cw, after):
    B, S, _ = k_aug.shape
    nt = 4 if S % (4 * tq) == 0 else 2
    n_chains = 2 * nt * tq // cw
    return pl.pallas_call(
        functools.partial(_attn_kernel, tq=tq, cw=cw, nt=nt),
        scratch_shapes=[pltpu.VMEM((n_chains, AUG, cw), BF16),
                        pltpu.VMEM((2, n_chains, tq, cw), F32),
                        pltpu.VMEM((2, n_chains, 8, cw), F32),
                        pltpu.VMEM((n_chains, 8, cw), F32),
                        pltpu.VMEM((n_chains, VROWS, cw), F32)],
        out_shape=jax.ShapeDtypeStruct((B, S, ATTN_WIDTH), BF16),
        grid=(B, N_PAIRS, S // (nt * tq)),
        in_specs=[pl.BlockSpec((1, 1, AUG, nt * tq), lambda b, j, i: (b, j, 0, i)),
                  pl.BlockSpec((1, S, AUG), lambda b, j, i: (b, 0, j)),
                  pl.BlockSpec((1, 1, 2 * VROWS, S), lambda b, j, i: (b, j, 0, 0))]
        + [pl.BlockSpec(memory_space=pl.ANY)] * len(after),
        out_specs=pl.BlockSpec((1, nt * tq, LANES), lambda b, j, i: (b, i, j)),
        compiler_params=pltpu.CompilerParams(
            dimension_semantics=("arbitrary", "arbitrary", "arbitrary"),
            vmem_limit_bytes=VMEM_LIMIT),
        name="attn",
    )(qT_aug, k_aug, vT, *after)


def _pack_bf16_pairs(z):
    w = z.shape[1] // 2
    bits = pltpu.bitcast(z.astype(BF16).astype(F32), jnp.uint32)
    return bits[:, :w] | (bits[:, w:] >> 16)


def _unpack_bf16_pairs(p):
    return (pltpu.bitcast(p & jnp.uint32(0xFFFF0000), F32), pltpu.bitcast(p << 16, F32))


POST_ROWS = 256


def _post_kernel(x_ref, yb_ref, n1_ref, sh1_ref, sc1_ref, g1_ref, n2_ref, sh2_ref, sc2_ref,
                 wc_ref, cw_ref, woc_ref, woa_ref, wo_ref, wr_ref, br_ref, tri_ref,
                 x1_ref, h2_ref, ridx_ref, rw_ref, cnt_ref, carry_ref):
    tm = x_ref.shape[1]
    n_grp = max(tm // POST_ROWS, 1)
    rows = tm // n_grp
    lane = lax.broadcasted_iota(jnp.int32, (rows, LANES), 1)
    row8 = lax.broadcasted_iota(jnp.int32, (8, CONV_WIDTH), 0)
    big = jnp.int32(1 << 20)

    @pl.when(pl.program_id(1) == 0)
    def _():
        carry_ref[...] = jnp.zeros_like(carry_ref)

    @pl.when((pl.program_id(0) == 0) & (pl.program_id(1) == 0))
    def _():
        cnt_ref[...] = jnp.zeros_like(cnt_ref)

    st = [dict(rs=pl.ds(g * rows, rows)) for g in range(n_grp)]

    def conv_in(d):
        d["x"] = x_ref[0, d["rs"], :]
        d["hb"] = _modulated_norm(d["x"], n1_ref[...], sh1_ref[0], sc1_ref[0]).astype(BF16)
        d["x_in"] = _dot(d["hb"], wc_ref[:, :CONV_WIDTH])
        d["conv_c"] = _dot(d["hb"], wc_ref[:, 2 * CONV_WIDTH:3 * CONV_WIDTH])
        d["conv_b"] = _dot(d["hb"], wc_ref[:, CONV_WIDTH:2 * CONV_WIDTH])

    def conv(d, prev):
        u = d.pop("conv_c") * d.pop("x_in")
        d["u_tail"] = u[rows - 8:, :]

        def shifted(k):
            r = pltpu.roll(u, k, 0)
            top = jnp.where(row8 < k, pltpu.roll(prev, k, 0), r[:8])
            return jnp.concatenate([top, r[8:]], axis=0)

        cw = cw_ref[...]
        cv = cw[0:1] * shifted(2) + cw[1:2] * shifted(1) + cw[2:3] * u
        d["y_a"] = (d.pop("conv_b") * cv).astype(BF16)

    def gates(d):
        d["p_b"] = _dot(yb_ref[0, d["rs"], :], woa_ref[...])
        d["gate_c"] = _dot(d["hb"], wc_ref[:, 3 * CONV_WIDTH:3 * CONV_WIDTH + D_MODEL])
        d["gate_a"] = _dot(d.pop("hb"), wc_ref[:, 3 * CONV_WIDTH + D_MODEL:])

    def branch_a(d):
        d["p_a"] = _dot(d.pop("y_a"), woc_ref[...])

    def merge(d):
        d["merged"] = (_sigmoid(d.pop("gate_c")) * d.pop("p_a")
                       + _sigmoid(d.pop("gate_a")) * d.pop("p_b")).astype(BF16)

    def out_proj(d):
        d["o"] = _dot(d.pop("merged"), wo_ref[...])

    def residual(d):
        x1 = d.pop("x") + g1_ref[0] * d.pop("o")
        x1_ref[0, d["rs"], :] = x1
        h2 = _modulated_norm(x1, n2_ref[...], sh2_ref[0], sc2_ref[0])
        h2_ref[0, d["rs"], :] = _pack_bf16_pairs(h2)
        d["h_hi"] = h2.astype(BF16)
        d["h_lo"] = (h2 - d["h_hi"].astype(F32)).astype(BF16)

    def router(d):
        both = _dot(d.pop("h_hi"), wr_ref[...])
        d["lg"] = (both[:, :LANES] + both[:, LANES:] + _dot(d.pop("h_lo"), wr_ref[:, :LANES])) \
            + br_ref[...]

    def first_argmax(vals):
        mx = jnp.max(vals, axis=-1, keepdims=True)
        idx = jnp.min(jnp.where(vals == mx, lane, big), axis=-1, keepdims=True)
        return mx, idx

    def route(d):
        lg = d.pop("lg")
        is_g = (lane >= N_EXPERTS) & (lane < N_EXPERTS + N_GROUPS)
        g_mx, g_lane = first_argmax(jnp.where(is_g, lg, NEG))
        p_sel = 1.0 / jnp.sum(jnp.where(is_g, jnp.exp(lg - g_mx), 0.0), axis=-1, keepdims=True)
        g_idx = g_lane - N_EXPERTS
        in_g = (lane >= g_idx * EXPERTS_PER_GROUP) & (lane < (g_idx + 1) * EXPERTS_PER_GROUP)
        le = jnp.where(in_g, lg, NEG)
        v1, i1 = first_argmax(le)
        v2, i2 = first_argmax(jnp.where(lane == i1, NEG, le))
        e2 = jnp.exp(v2 - v1)
        w1 = p_sel / (1.0 + e2)
        w2 = w1 * e2
        rw_ref[0, d["rs"], :] = jnp.where(lane == 0, w1, 0.0) + jnp.where(lane == 1, w2, 0.0)
        d["i1"], d["i2"] = i1, i2
        d["onehot"] = jnp.where((lane == i1) | (lane == i2), 1.0, 0.0)

    def rank(d):
        onehot = d.pop("onehot")
        before = _dot(tri_ref[...], onehot.astype(BF16)) + cnt_ref[0:1, :]
        cnt_ref[...] = cnt_ref[...] + jnp.sum(onehot, axis=0, keepdims=True)
        i1, i2 = d.pop("i1"), d.pop("i2")
        r1 = jnp.sum(jnp.where(lane == i1, before, 0.0), axis=-1, keepdims=True)
        r2 = jnp.sum(jnp.where(lane == i2, before, 0.0), axis=-1, keepdims=True)
        rec = (jnp.where(lane == 0, i1.astype(F32), 0.0) + jnp.where(lane == 1, i2.astype(F32), 0.0)
               + jnp.where(lane == 2, r1, 0.0) + jnp.where(lane == 3, r2, 0.0))
        ridx_ref[0, :, d["rs"]] = rec.T[:8].astype(jnp.int32)

    stages = [conv_in, None, gates, branch_a, merge, out_proj, residual, router, route, rank]
    lag = 2
    for step in range(len(stages) + lag * (n_grp - 1)):
        for g, d in enumerate(st):
            k = step - lag * g
            if 0 <= k < len(stages):
                if stages[k] is None:
                    conv(d, carry_ref[...] if g == 0 else st[g - 1]["u_tail"])
                else:
                    stages[k](d)
    carry_ref[...] = st[-1]["u_tail"]


def _post(x, yb, n1, sh1, sc1, g1, n2, sh2, sc2, wc, cw, woc, woa, wo, wr, br, tm, b0):
    B, S, _ = yb.shape
    rows = tm // max(tm // POST_ROWS, 1)
    tri = jnp.asarray(np.tril(np.ones((rows, rows), np.float32), -1), BF16)
    const = lambda *shape: pl.BlockSpec(shape, lambda b, i: (0,) * len(shape),
                                        pipeline_mode=pl.Buffered(1))
    perb = pl.BlockSpec((1, 1, D_MODEL), lambda b, i: (b + b0, 0, 0))
    tok = lambda w: pl.BlockSpec((1, tm, w), lambda b, i: (b, i, 0))
    return pl.pallas_call(
        _post_kernel,
        out_shape=(jax.ShapeDtypeStruct((B, S, D_MODEL), F32),
                   jax.ShapeDtypeStruct((B, S, D_MODEL // 2), jnp.uint32),
                   jax.ShapeDtypeStruct((B, 8, S), jnp.int32),
                   jax.ShapeDtypeStruct((B, S, LANES), F32),
                   jax.ShapeDtypeStruct((8, LANES), F32)),
        grid=(B, S // tm),
        in_specs=[pl.BlockSpec((1, tm, D_MODEL), lambda b, i: (b + b0, i, 0)), tok(ATTN_WIDTH),
                  const(1, D_MODEL), perb, perb, perb,
                  const(1, D_MODEL), perb, perb,
                  const(D_MODEL, 3 * CONV_WIDTH + 2 * D_MODEL),
                  const(8, CONV_WIDTH),
                  const(CONV_WIDTH, D_MODEL), const(ATTN_WIDTH, D_MODEL),
                  const(D_MODEL, D_MODEL),
                  const(D_MODEL, 2 * LANES), const(1, LANES), const(rows, rows)],
        out_specs=(tok(D_MODEL), tok(D_MODEL // 2),
                   pl.BlockSpec((1, 8, tm), lambda b, i: (b, 0, i)), tok(LANES),
                   pl.BlockSpec((8, LANES), lambda b, i: (0, 0))),
        scratch_shapes=[pltpu.VMEM((8, CONV_WIDTH), F32)],
        compiler_params=pltpu.CompilerParams(
            dimension_semantics=("arbitrary", "arbitrary"), vmem_limit_bytes=VMEM_LIMIT),
        name="post",
    )(x, yb, n1, sh1, sc1, g1, n2, sh2, sc2, wc, cw, woc, woa, wo, wr, br, tri)


SC_CORES = 2
SC_SUBCORES = 16
SC_WORKERS = SC_CORES * SC_SUBCORES
SC_CHUNK = 64
ROW_WORDS = D_MODEL // 2


def _sc_mesh():
    return plsc.VectorSubcoreMesh(core_axis_name="c", subcore_axis_name="s",
                                  num_cores=SC_CORES, num_subcores=SC_SUBCORES)


def _dispatch_body(rows_hbm, idx1_hbm, idx2_hbm, xs_hbm, idx1_v, idx2_v, rows_v, *, n_chunks):
    wid = lax.axis_index("s") * SC_CORES + lax.axis_index("c")
    pltpu.sync_copy(idx1_hbm.at[wid], idx1_v)
    pltpu.sync_copy(idx2_hbm.at[wid], idx2_v)
    base = wid * (n_chunks * SC_CHUNK)

    @pl.loop(0, n_chunks)
    def _(j):
        pltpu.sync_copy(rows_hbm.at[pl.ds(base + j * SC_CHUNK, SC_CHUNK)], rows_v)
        pltpu.sync_copy(rows_v, xs_hbm.at[idx1_v.at[j]])
        pltpu.sync_copy(rows_v, xs_hbm.at[idx2_v.at[j]])


def _sc_scratch(n_chunks):
    return [pltpu.VMEM((n_chunks, SC_CHUNK), jnp.int32), pltpu.VMEM((n_chunks, SC_CHUNK), jnp.int32),
            pltpu.VMEM((SC_CHUNK, ROW_WORDS), jnp.uint32)]


def _dispatch(rows, idx1, idx2, n_slots):
    n_chunks = idx1.shape[1]
    return pl.kernel(
        functools.partial(_dispatch_body, n_chunks=n_chunks),
        out_type=jax.ShapeDtypeStruct((n_slots, ROW_WORDS), jnp.uint32),
        mesh=_sc_mesh(),
        scratch_types=_sc_scratch(n_chunks),
        name="dispatch",
    )(rows, idx1, idx2)


def _collect_body(ys_hbm, idx1_hbm, idx2_hbm, g1_hbm, g2_hbm, idx1_v, idx2_v, rows_v, *, n_chunks):
    wid = lax.axis_index("s") * SC_CORES + lax.axis_index("c")
    pltpu.sync_copy(idx1_hbm.at[wid], idx1_v)
    pltpu.sync_copy(idx2_hbm.at[wid], idx2_v)
    base = wid * (n_chunks * SC_CHUNK)

    @pl.loop(0, n_chunks)
    def _(j):
        dst = pl.ds(base + j * SC_CHUNK, SC_CHUNK)
        pltpu.sync_copy(ys_hbm.at[idx1_v.at[j]], rows_v)
        pltpu.sync_copy(rows_v, g1_hbm.at[dst])
        pltpu.sync_copy(ys_hbm.at[idx2_v.at[j]], rows_v)
        pltpu.sync_copy(rows_v, g2_hbm.at[dst])


def _collect(ys, idx1, idx2):
    n_chunks = idx1.shape[1]
    out = jax.ShapeDtypeStruct((SC_WORKERS * n_chunks * SC_CHUNK, ROW_WORDS), jnp.uint32)
    return pl.kernel(
        functools.partial(_collect_body, n_chunks=n_chunks),
        out_type=(out, out),
        mesh=_sc_mesh(),
        scratch_types=_sc_scratch(n_chunks),
        name="collect",
    )(ys, idx1, idx2)


def _moe_kernel(te_ref, nt_ref, xs_ref, wg_ref, wu_ref, wd_ref, *rest):
    ys_ref = rest[-1]

    @pl.when(pl.program_id(0) < nt_ref[0])
    def _():
        left, right = _unpack_bf16_pairs(xs_ref[...])
        xb = jnp.concatenate([left.astype(BF16), right.astype(BF16)], axis=1)
        g = _dot(xb, wg_ref[0].astype(BF16))
        u = _dot(xb, wu_ref[0].astype(BF16))
        a = (g * _sigmoid(g) * u).astype(BF16)
        ys_ref[...] = _pack_bf16_pairs(_dot(a, wd_ref[0].astype(BF16)))


def _moe(xs, tile_expert, n_tiles, wg, wu, wd, tm, after):
    n_slots = xs.shape[0]
    row_blk = lambda i, te, nt: (jnp.minimum(i, nt[0] - 1), 0)
    w_blk = lambda i, te, nt: (te[i], 0, 0)
    return pl.pallas_call(
        _moe_kernel,
        out_shape=jax.ShapeDtypeStruct((n_slots, ROW_WORDS), jnp.uint32),
        grid_spec=pltpu.PrefetchScalarGridSpec(
            num_scalar_prefetch=2,
            grid=(n_slots // tm,),
            in_specs=[pl.BlockSpec((tm, ROW_WORDS), row_blk),
                      pl.BlockSpec((1, D_MODEL, D_EXPERT), w_blk),
                      pl.BlockSpec((1, D_MODEL, D_EXPERT), w_blk),
                      pl.BlockSpec((1, D_EXPERT, D_MODEL), w_blk)]
            + [pl.BlockSpec(memory_space=pl.ANY)] * len(after),
            out_specs=pl.BlockSpec((tm, ROW_WORDS), row_blk)),
        compiler_params=pltpu.CompilerParams(
            dimension_semantics=("arbitrary",), vmem_limit_bytes=VMEM_LIMIT),
        name="moe",
    )(tile_expert, n_tiles, xs, wg, wu, wd, *after)


def _final_kernel(x1_ref, g1_ref, g2_ref, rw_ref, gate_ref, *rest):
    o_ref = rest[-1]
    rw = rw_ref[0]
    w1 = rw[:, 0:1]
    w2 = rw[:, 1:2]
    a_l, a_r = _unpack_bf16_pairs(g1_ref[0])
    b_l, b_r = _unpack_bf16_pairs(g2_ref[0])
    moe = jnp.concatenate([w1 * a_l + w2 * b_l, w1 * a_r + w2 * b_r], axis=1)
    o_ref[0] = x1_ref[0] + gate_ref[0] * moe


def _final(x1, g1, g2, rw, gate2, tm, b0, src0, out_prev, after):
    nb, S, _ = g1.shape
    grp = lambda w: pl.BlockSpec((1, tm, w), lambda b, i: (b + src0, i, 0))
    tok = lambda w: pl.BlockSpec((1, tm, w), lambda b, i: (b, i, 0))
    extra = (() if out_prev is None else (out_prev,)) + tuple(after)
    return pl.pallas_call(
        _final_kernel,
        out_shape=jax.ShapeDtypeStruct((gate2.shape[0], S, D_MODEL), F32),
        grid=(nb, S // tm),
        in_specs=[grp(D_MODEL), tok(ROW_WORDS), tok(ROW_WORDS), grp(LANES),
                  pl.BlockSpec((1, 1, D_MODEL), lambda b, i: (b + b0 + src0, 0, 0))]
        + [pl.BlockSpec(memory_space=pl.ANY)] * len(extra),
        out_specs=pl.BlockSpec((1, tm, D_MODEL), lambda b, i: (b + b0 + src0, i, 0)),
        input_output_aliases={} if out_prev is None else {5: 0},
        compiler_params=pltpu.CompilerParams(
            dimension_semantics=("arbitrary", "arbitrary"), vmem_limit_bytes=VMEM_LIMIT),
        name="final",
    )(x1, g1, g2, rw, gate2, *extra)


def _pick(n, pref):
    t = min(n, pref)
    assert n % t == 0, (n, t)
    return t


def _slots_kernel(offs_ref, r_ref, o_ref):
    r = r_ref[0]
    base = jnp.zeros_like(r)
    for e in range(N_EXPERTS):
        base = jnp.where(r == e, offs_ref[e], base)
    o_ref[0] = base + pltpu.roll(r, 6, 0)


def _slots(ridx, offs):
    B, _, S = ridx.shape
    return pl.pallas_call(
        _slots_kernel,
        out_shape=jax.ShapeDtypeStruct((B, 8, S), jnp.int32),
        grid_spec=pltpu.PrefetchScalarGridSpec(
            num_scalar_prefetch=1, grid=(B,),
            in_specs=[pl.BlockSpec((1, 8, S), lambda b, offs: (b, 0, 0))],
            out_specs=pl.BlockSpec((1, 8, S), lambda b, offs: (b, 0, 0))),
        compiler_params=pltpu.CompilerParams(dimension_semantics=("arbitrary",)),
        name="slots",
    )(offs, ridx)


def _route_plan(ridx, counts, tm_e, T):
    counts = counts.astype(jnp.int32)
    tiles = (counts + tm_e - 1) // tm_e
    tile_end = jnp.cumsum(tiles)
    offs = (tile_end - tiles) * tm_e
    slots = _slots(ridx, offs)
    n_chunks = T // (SC_WORKERS * SC_CHUNK)
    idx1 = slots[:, 0, :].reshape(SC_WORKERS, n_chunks, SC_CHUNK)
    idx2 = slots[:, 1, :].reshape(SC_WORKERS, n_chunks, SC_CHUNK)
    n_tiles_max = 2 * T // tm_e + N_EXPERTS
    tile_ids = jnp.arange(n_tiles_max, dtype=jnp.int32)
    tile_expert = jnp.sum((tile_end[None, :] <= tile_ids[:, None]).astype(jnp.int32), axis=1)
    tile_expert = jnp.minimum(tile_expert, N_EXPERTS - 1)
    return idx1, idx2, tile_expert, tile_end[-1:].astype(jnp.int32), n_tiles_max * tm_e


def _layer(x, c, w_ada, b_ada, norm1_w, w_in, b_forget, conv_w, q_norm_w, k_norm_w,
           w_out_conv, w_out_attn, w_o, norm2_w, w_rg, b_rg, w_re, b_re, w_gate, w_up, w_down):
    B, S, _ = x.shape
    n_grp = 2 if B % 2 == 0 and (B // 2 * S) % (SC_WORKERS * SC_CHUNK) == 0 else 1
    Bg = B // n_grp
    T = Bg * S
    assert T % (SC_WORKERS * SC_CHUNK) == 0, T
    mod = _ada(c, w_ada, b_ada.reshape(1, -1)).reshape(B, 6, 1, D_MODEL)
    shift1, scale1, gate1, shift2, scale2, gate2 = (mod[:, t] for t in range(6))

    cuts = np.cumsum([0, CONV_WIDTH, CONV_WIDTH, CONV_WIDTH, ATTN_WIDTH, ATTN_WIDTH, ATTN_WIDTH,
                      N_HEADS, D_MODEL, D_MODEL])
    w_conv3 = w_in[:, cuts[0]:cuts[3]]
    w_qvT = jnp.concatenate([w_in[:, cuts[3]:cuts[4]], w_in[:, cuts[5]:cuts[6]]], axis=1).T.astype(BF16)
    w_k = w_in[:, cuts[4]:cuts[5]].astype(BF16)
    w_f = jnp.pad(w_in[:, cuts[6]:cuts[7]], ((0, 0), (0, LANES - N_HEADS))).astype(BF16)
    b_f = jnp.pad(b_forget, (0, LANES - N_HEADS)).reshape(1, LANES)
    w_cgg = jnp.concatenate([w_conv3, w_in[:, cuts[7]:cuts[9]]], axis=1).astype(BF16)

    tm_qkv = _pick(S, 1024)
    qnwT = jnp.broadcast_to((jnp.tile(q_norm_w, N_HEADS) * (LOG2E * HEAD_DIM ** -0.5))[:, None],
                            (ATTN_WIDTH, LANES))
    tq = _pick(S // 2, 512)
    w_r = jnp.pad(jnp.concatenate([w_re, w_rg], axis=1),
                  ((0, 0), (0, LANES - N_EXPERTS - N_GROUPS)))
    w_r_hi = w_r.astype(BF16)
    w_r_lo = (w_r - w_r_hi.astype(F32)).astype(BF16)
    b_r = jnp.pad(jnp.concatenate([b_re, b_rg]), (0, LANES - N_EXPERTS - N_GROUPS)).reshape(1, LANES)
    cw = jnp.pad(conv_w, ((0, 8 - CONV_K), (0, 0)))
    tm_post = _pick(S, 1024)
    tm_e = 512
    w_oc, w_oa, w_ob, w_rs = (w_out_conv.astype(BF16), w_out_attn.astype(BF16), w_o.astype(BF16),
                              jnp.concatenate([w_r_hi, w_r_lo], axis=1))

    n_half = next(h for h in (4, 2, 1) if Bg % h == 0 and (Bg // h * S) % (SC_WORKERS * SC_CHUNK) == 0)
    Bh = Bg // n_half

    def experts(grp, after):
        x1, rw, idx1, idx2, tile_expert, n_tiles, xs = grp
        ys = _moe(xs, tile_expert, n_tiles, w_gate, w_up, w_down, tm_e, after)
        halves = lambda idx: idx.reshape(n_half, SC_WORKERS, idx.shape[1] // n_half, SC_CHUNK)
        return ys, [_collect(ys, i1, i2) for i1, i2 in zip(halves(idx1), halves(idx2))]

    def finish(g, grp, gathered, out, after):
        for hh, (g1, g2) in enumerate(gathered):
            out = _final(grp[0], g1.reshape(Bh, S, ROW_WORDS), g2.reshape(Bh, S, ROW_WORDS), grp[1],
                         gate2, _pick(S, 512), g * Bg, hh * Bh, out, after if hh == 0 else [])
        return out

    out, prev, prev_gathered, ys_prev = None, None, None, None
    for g in range(n_grp):
        b0 = g * Bg
        qT_aug, k_aug, vT = _qkv(x, norm1_w.reshape(1, -1), shift1, scale1, w_qvT, w_k, w_f, b_f,
                                 qnwT, jnp.tile(k_norm_w, N_HEADS).reshape(1, -1), tm_qkv, b0, Bg)
        if prev is not None:
            ys_prev, prev_gathered = experts(prev, [qT_aug])
        y_b = _attention(qT_aug, k_aug, vT, tq, _pick(tq, 256),
                         [] if ys_prev is None else [ys_prev])
        x1, h2p, ridx, rw, counts = _post(x, y_b, norm1_w.reshape(1, -1), shift1, scale1, gate1,
                                          norm2_w.reshape(1, -1), shift2, scale2,
                                          w_cgg, cw, w_oc, w_oa, w_ob, w_rs, b_r, tm_post, b0)
        idx1, idx2, tile_expert, n_tiles, n_slots = _route_plan(ridx, counts[0, :N_EXPERTS], tm_e, T)
        xs = _dispatch(h2p.reshape(T, ROW_WORDS), idx1, idx2, n_slots)
        if prev is not None:
            out = finish(g - 1, prev, prev_gathered, out, [idx1])
        prev = (x1, rw, idx1, idx2, tile_expert, n_tiles, xs)
    _, last_gathered = experts(prev, [] if out is None else [out])
    return finish(n_grp - 1, prev, last_gathered, out, [])


def kernel(x, c, w_ada, b_ada, norm1_w, w_in, b_forget, conv_w, q_norm_w, k_norm_w, w_out_conv,
           w_out_attn, w_o, norm2_w, w_router_group, b_router_group, w_router_expert,
           b_router_expert, w_gate, w_up, w_down):
    for l in range(w_ada.shape[0]):
        x = _layer(x, c, w_ada[l], b_ada[l], norm1_w[l], w_in[l], b_forget[l], conv_w[l],
                   q_norm_w[l], k_norm_w[l], w_out_conv[l], w_out_attn[l], w_o[l], norm2_w[l],
                   w_router_group[l], b_router_group[l], w_router_expert[l], b_router_expert[l],
                   w_gate[l], w_up[l], w_down[l])
    return x
```

```python
import functools

import jax
import jax.numpy as jnp
import numpy as np
from jax import lax
from jax.experimental import pallas as pl
from jax.experimental.pallas import tpu as pltpu
from jax.experimental.pallas import tpu_sc as plsc

D_MODEL = 1024
CONV_WIDTH = 512
CONV_K = 3
N_HEADS = 8
HEAD_DIM = 64
ATTN_WIDTH = N_HEADS * HEAD_DIM
N_PAIRS = N_HEADS // 2
N_GROUPS = 4
EXPERTS_PER_GROUP = 8
N_EXPERTS = N_GROUPS * EXPERTS_PER_GROUP
D_EXPERT = 256
EPS = 1e-6
LANES = 128
AUG = 2 * LANES
BIAS_W = 6
VROWS = HEAD_DIM + 16
NEG = -1e30
LOG2E = 1.4426950408889634

F32 = jnp.float32
BF16 = jnp.bfloat16
VMEM_LIMIT = 56 * 1024 * 1024


def _sigmoid(z):
    return 1.0 / (1.0 + jnp.exp(-z))


def _split3(z):
    hi = z.astype(BF16)
    r = z - hi.astype(F32)
    mid = r.astype(BF16)
    lo = (r - mid.astype(F32)).astype(BF16)
    return hi, mid, lo


def _dot(a, b):
    return jnp.dot(a, b, preferred_element_type=F32)


def _modulated_norm(x, nw, shift, scale):
    ms = jnp.mean(x * x, axis=-1, keepdims=True)
    return (x * lax.rsqrt(ms + EPS) * nw) * (1.0 + scale) + shift


def _ada_kernel(c_ref, w_ref, b_ref, o_ref):
    c = c_ref[...]
    a = c * _sigmoid(c)
    o_ref[...] = jnp.dot(a, w_ref[...], precision=lax.Precision.HIGHEST,
                         preferred_element_type=F32) + b_ref[...]


def _ada(c, w_ada, b_ada):
    B = c.shape[0]
    n = w_ada.shape[1] // D_MODEL
    return pl.pallas_call(
        _ada_kernel,
        out_shape=jax.ShapeDtypeStruct((B, n * D_MODEL), F32),
        grid=(n,),
        in_specs=[pl.BlockSpec((B, D_MODEL), lambda j: (0, 0)),
                  pl.BlockSpec((D_MODEL, D_MODEL), lambda j: (0, j)),
                  pl.BlockSpec((1, D_MODEL), lambda j: (0, j))],
        out_specs=pl.BlockSpec((B, D_MODEL), lambda j: (0, j)),
        compiler_params=pltpu.CompilerParams(dimension_semantics=("arbitrary",)),
        name="ada",
    )(c, w_ada, b_ada)


QKV_ROWS = 256
_NT = (((1,), (1,)), ((), ()))


def _lane_tile(a, width):
    return jnp.concatenate([a] * (width // a.shape[1]), axis=1)


def _qkv_kernel(x_ref, nw_ref, sh_ref, sc_ref, wqv_ref, wk_ref, wf_ref, bf_ref, qnw_ref, knw_ref,
                gsum_ref, pq_ref, pk_ref, cq_ref, ck_ref,
                qT_ref, k_ref, vT_ref, carry_ref):
    tm = x_ref.shape[1]
    n_grp = max(tm // QKV_ROWS, 1)
    rows = tm // n_grp
    row = lax.broadcasted_iota(jnp.int32, (rows, LANES), 0)

    @pl.when(pl.program_id(1) == 0)
    def _():
        carry_ref[...] = jnp.zeros_like(carry_ref)

    st = [dict(rs=pl.ds(g * rows, rows)) for g in range(n_grp)]

    def project(d):
        h = _modulated_norm(x_ref[0, d["rs"], :], nw_ref[...], sh_ref[0], sc_ref[0])
        hb = h.astype(BF16)
        d["qvT"] = lax.dot_general(wqv_ref[...], hb, _NT, preferred_element_type=F32)
        d["k"] = _dot(hb, wk_ref[...])
        d["fl"] = _dot(hb, wf_ref[...]) + bf_ref[...]

    def norms(d):
        qvT = d["qvT"]
        heads = []
        for hd in range(N_HEADS):
            z = qvT[hd * HEAD_DIM:(hd + 1) * HEAD_DIM]
            heads.append(z * lax.rsqrt(jnp.mean(z * z, axis=0, keepdims=True) + EPS))
        qnT = jnp.concatenate(heads, axis=0) * _lane_tile(qnw_ref[...], rows)
        for j in range(N_PAIRS):
            qT_ref[0, j, :LANES, d["rs"]] = qnT[j * LANES:(j + 1) * LANES].astype(BF16)
            for t in range(2):
                r0 = ATTN_WIDTH + (2 * j + t) * HEAD_DIM
                vT_ref[0, j, t * VROWS:t * VROWS + HEAD_DIM, d["rs"]] = qvT[r0:r0 + HEAD_DIM].astype(BF16)
                vT_ref[0, j, t * VROWS + HEAD_DIM:(t + 1) * VROWS, d["rs"]] = \
                    jnp.ones((VROWS - HEAD_DIM, rows), BF16)
        del d["qvT"]
        k = d.pop("k")
        ss = _dot((k * k).astype(BF16), gsum_ref[...])
        kn = k * lax.rsqrt(ss * (1.0 / HEAD_DIM) + EPS) * knw_ref[...]
        for j in range(N_PAIRS):
            k_ref[0, d["rs"], j * AUG:j * AUG + LANES] = kn[:, j * LANES:(j + 1) * LANES].astype(BF16)

    def forget(d, before):
        fl = d.pop("fl")
        cum = jnp.minimum(fl, 0.0) - jnp.log(1.0 + jnp.exp(-jnp.abs(fl)))
        s = 1
        while s < rows:
            cum = cum + jnp.where(row >= s, pltpu.roll(cum, s, 0), 0.0)
            s *= 2
        cum = cum + before[7:8, :]
        d["tail"] = cum[rows - 8:, :]
        d["parts"] = jnp.concatenate(_split3(cum * LOG2E), axis=1)

    def bias(d):
        parts = d.pop("parts")
        eqT = (lax.dot_general(pq_ref[...], parts, _NT, preferred_element_type=F32)
               + _lane_tile(cq_ref[...], rows)).astype(BF16)
        ek = (_dot(parts, pk_ref[...]) + ck_ref[...]).astype(BF16)
        for j in range(N_PAIRS):
            qT_ref[0, j, LANES:, d["rs"]] = eqT
            k_ref[0, d["rs"], j * AUG + LANES:(j + 1) * AUG] = ek

    stages = [project, norms, None, bias]
    lag = 1
    for step in range(len(stages) + lag * (n_grp - 1)):
        for g, d in enumerate(st):
            kk = step - lag * g
            if 0 <= kk < len(stages):
                if stages[kk] is None:
                    forget(d, carry_ref[...] if g == 0 else st[g - 1]["tail"])
                else:
                    stages[kk](d)
    carry_ref[...] = st[-1]["tail"]


def _bias_placement():
    pq = np.zeros((LANES, 3 * LANES), np.float32)
    pk = np.zeros((3 * LANES, LANES), np.float32)
    cq = np.zeros((LANES, LANES), np.float32)
    ck = np.zeros((1, LANES), np.float32)
    for hd in range(N_HEADS):
        base = BIAS_W * hd
        for p in range(3):
            pq[base + p, p * LANES + hd] = 1.0
            pk[p * LANES + hd, base + 3 + p] = -1.0
            cq[base + 3 + p, :] = 1.0
            ck[0, base + p] = 1.0
    return (jnp.asarray(pq, BF16), jnp.asarray(pk, BF16), jnp.asarray(cq), jnp.asarray(ck))


def _qkv(x, nw, shift, scale, wqvT, wk, wf, bf, qnwT, knw, tm, b0, B):
    S = x.shape[1]
    gsum = jnp.asarray(np.kron(np.eye(N_HEADS), np.ones((HEAD_DIM, HEAD_DIM))), BF16)
    pq, pk, cq, ck = _bias_placement()
    const = lambda *shape: pl.BlockSpec(shape, lambda b, i: (0,) * len(shape),
                                        pipeline_mode=pl.Buffered(1))
    return pl.pallas_call(
        _qkv_kernel,
        out_shape=(jax.ShapeDtypeStruct((B, N_PAIRS, AUG, S), BF16),
                   jax.ShapeDtypeStruct((B, S, N_PAIRS * AUG), BF16),
                   jax.ShapeDtypeStruct((B, N_PAIRS, 2 * VROWS, S), BF16)),
        grid=(B, S // tm),
        in_specs=[pl.BlockSpec((1, tm, D_MODEL), lambda b, i: (b + b0, i, 0)),
                  const(1, D_MODEL),
                  pl.BlockSpec((1, 1, D_MODEL), lambda b, i: (b + b0, 0, 0)),
                  pl.BlockSpec((1, 1, D_MODEL), lambda b, i: (b + b0, 0, 0)),
                  const(2 * ATTN_WIDTH, D_MODEL),
                  const(D_MODEL, ATTN_WIDTH),
                  const(D_MODEL, LANES),
                  const(1, LANES),
                  const(ATTN_WIDTH, LANES),
                  const(1, ATTN_WIDTH),
                  const(ATTN_WIDTH, ATTN_WIDTH),
                  const(LANES, 3 * LANES),
                  const(3 * LANES, LANES),
                  const(LANES, LANES),
                  const(1, LANES)],
        out_specs=(pl.BlockSpec((1, N_PAIRS, AUG, tm), lambda b, i: (b, 0, 0, i)),
                   pl.BlockSpec((1, tm, N_PAIRS * AUG), lambda b, i: (b, i, 0)),
                   pl.BlockSpec((1, N_PAIRS, 2 * VROWS, tm), lambda b, i: (b, 0, 0, i))),
        scratch_shapes=[pltpu.VMEM((8, LANES), F32)],
        compiler_params=pltpu.CompilerParams(
            dimension_semantics=("arbitrary", "arbitrary"), vmem_limit_bytes=VMEM_LIMIT),
        name="qkv",
    )(x, nw, shift, scale, wqvT, wk, wf, bf, qnwT, knw, gsum, pq, pk, cq, ck)


def _attn_kernel(qT_ref, k_ref, vT_ref, *rest, tq, cw, nt):
    o_ref, qq_ref, s_ref, smax_ref, m_ref, acc_ref = rest[-6:]
    i = pl.program_id(2)
    n = tq // cw
    chains = [(a, t, c) for a in range(nt) for t in range(2) for c in range(n)]
    feat = lax.broadcasted_iota(jnp.int32, (AUG, tq), 0)
    for t in range(2):
        bias0 = LANES + BIAS_W * (2 * pl.program_id(1) + t)
        keep = ((feat >= t * HEAD_DIM) & (feat < (t + 1) * HEAD_DIM)) | \
               ((feat >= bias0) & (feat < bias0 + BIAS_W))
        for a in range(nt):
            qT = qT_ref[0, 0, :, a * tq:(a + 1) * tq]
            qh = jnp.where(keep, qT, jnp.zeros_like(qT))
            for c in range(n):
                qq_ref[chains.index((a, t, c))] = qh[:, c * cw:(c + 1) * cw]
    kpos = lax.broadcasted_iota(jnp.int32, (tq, cw), 0)
    qpos = lax.broadcasted_iota(jnp.int32, (tq, cw), 1)

    def scores(j, slot, which):
        k_blk = k_ref[0, pl.ds(pl.multiple_of(j * tq, tq), tq), :]
        for ci in which:
            s = _dot(k_blk, qq_ref[ci])
            s_ref[slot, ci] = s
            smax_ref[slot, ci] = jnp.broadcast_to(jnp.max(s, axis=0, keepdims=True), (8, cw))

    def absorb(j, slot, which, diagonal=()):
        start = pl.multiple_of(j * tq, tq)
        for ci in which:
            _, t, c = chains[ci]
            vj = vT_ref[0, 0, t * VROWS:(t + 1) * VROWS, pl.ds(start, tq)]
            s = s_ref[slot, ci]
            if ci in diagonal:
                s = jnp.where(kpos <= qpos + c * cw, s, NEG)
                smax = jnp.max(s, axis=0, keepdims=True)
            else:
                smax = smax_ref[slot, ci, 0:1]
            m = m_ref[ci, 0:1]
            m_new = jnp.maximum(m, smax)
            p = jnp.exp2(s - m_new).astype(BF16)
            acc_ref[ci] = jnp.exp2(m - m_new) * acc_ref[ci] + _dot(vj, p)
            m_ref[ci] = jnp.broadcast_to(m_new, (8, cw))

    every = list(range(len(chains)))
    from_tile = lambda r: [ci for ci in every if chains[ci][0] >= r]
    m_ref[...] = jnp.full(m_ref.shape, NEG, F32)
    acc_ref[...] = jnp.zeros(acc_ref.shape, F32)
    scores(0, 0, every)

    def two_blocks(jj, _):
        j = 2 * jj
        scores(j + 1, 1, every)
        absorb(j, 0, every)
        scores(j + 2, 0, every)
        absorb(j + 1, 1, every)
        return 0

    lax.fori_loop(0, i * (nt // 2), two_blocks, 0)
    for r in range(nt):
        if r + 1 < nt:
            scores(nt * i + r + 1, (r + 1) % 2, from_tile(r + 1))
        absorb(nt * i + r, r % 2, from_tile(r),
               diagonal=[ci for ci in every if chains[ci][0] == r])

    for a in range(nt):
        outs = [acc_ref[ci, :HEAD_DIM] / acc_ref[ci, HEAD_DIM:HEAD_DIM + 1]
                for ci in every if chains[ci][0] == a]
        oT = jnp.concatenate([jnp.concatenate(outs[:n], axis=1), jnp.concatenate(outs[n:], axis=1)],
                             axis=0)
        o_ref[0, a * tq:(a + 1) * tq, :] = oT.T.astype(BF16)


def _attention(qT_aug, k_aug, vT, tq, cw, after):
    B, S, _ = k_aug.shape
    nt = 4 if S % (4 * tq) == 0 else 2
    n_chains = 2 * nt * tq // cw
    return pl.pallas_call(
        functools.partial(_attn_kernel, tq=tq, cw=cw, nt=nt),
        scratch_shapes=[pltpu.VMEM((n_chains, AUG, cw), BF16),
                        pltpu.VMEM((2, n_chains, tq, cw), F32),
                        pltpu.VMEM((2, n_chains, 8, cw), F32),
                        pltpu.VMEM((n_chains, 8, cw), F32),
                        pltpu.VMEM((n_chains, VROWS, cw), F32)],
        out_shape=jax.ShapeDtypeStruct((B, S, ATTN_WIDTH), BF16),
        grid=(B, N_PAIRS, S // (nt * tq)),
        in_specs=[pl.BlockSpec((1, 1, AUG, nt * tq), lambda b, j, i: (b, j, 0, i)),
                  pl.BlockSpec((1, S, AUG), lambda b, j, i: (b, 0, j)),
                  pl.BlockSpec((1, 1, 2 * VROWS, S), lambda b, j, i: (b, j, 0, 0))]
        + [pl.BlockSpec(memory_space=pl.ANY)] * len(after),
        out_specs=pl.BlockSpec((1, nt * tq, LANES), lambda b, j, i: (b, i, j)),
        compiler_params=pltpu.CompilerParams(
            dimension_semantics=("arbitrary", "arbitrary", "arbitrary"),
            vmem_limit_bytes=VMEM_LIMIT),
        name="attn",
    )(qT_aug, k_aug, vT, *after)


def _pack_bf16_pairs(z):
    w = z.shape[1] // 2
    bits = pltpu.bitcast(z.astype(BF16).astype(F32), jnp.uint32)
    return bits[:, :w] | (bits[:, w:] >> 16)


def _unpack_bf16_pairs(p):
    return (pltpu.bitcast(p & jnp.uint32(0xFFFF0000), F32), pltpu.bitcast(p << 16, F32))


POST_ROWS = 256


def _post_kernel(x_ref, yb_ref, n1_ref, sh1_ref, sc1_ref, g1_ref, n2_ref, sh2_ref, sc2_ref,
                 wc_ref, cw_ref, woc_ref, woa_ref, wo_ref, wr_ref, br_ref, tri_ref,
                 x1_ref, h2_ref, ridx_ref, rw_ref, cnt_ref, carry_ref):
    tm = x_ref.shape[1]
    n_grp = max(tm // POST_ROWS, 1)
    rows = tm // n_grp
    lane = lax.broadcasted_iota(jnp.int32, (rows, LANES), 1)
    row8 = lax.broadcasted_iota(jnp.int32, (8, CONV_WIDTH), 0)
    big = jnp.int32(1 << 20)

    @pl.when(pl.program_id(1) == 0)
    def _():
        carry_ref[...] = jnp.zeros_like(carry_ref)

    @pl.when((pl.program_id(0) == 0) & (pl.program_id(1) == 0))
    def _():
        cnt_ref[...] = jnp.zeros_like(cnt_ref)

    st = [dict(rs=pl.ds(g * rows, rows)) for g in range(n_grp)]

    def conv_in(d):
        d["x"] = x_ref[0, d["rs"], :]
        d["hb"] = _modulated_norm(d["x"], n1_ref[...], sh1_ref[0], sc1_ref[0]).astype(BF16)
        d["x_in"] = _dot(d["hb"], wc_ref[:, :CONV_WIDTH])
        d["conv_c"] = _dot(d["hb"], wc_ref[:, 2 * CONV_WIDTH:3 * CONV_WIDTH])
        d["conv_b"] = _dot(d["hb"], wc_ref[:, CONV_WIDTH:2 * CONV_WIDTH])

    def conv(d, prev):
        u = d.pop("conv_c") * d.pop("x_in")
        d["u_tail"] = u[rows - 8:, :]

        def shifted(k):
            r = pltpu.roll(u, k, 0)
            top = jnp.where(row8 < k, pltpu.roll(prev, k, 0), r[:8])
            return jnp.concatenate([top, r[8:]], axis=0)

        cw = cw_ref[...]
        cv = cw[0:1] * shifted(2) + cw[1:2] * shifted(1) + cw[2:3] * u
        d["y_a"] = (d.pop("conv_b") * cv).astype(BF16)

    def gates(d):
        d["p_b"] = _dot(yb_ref[0, d["rs"], :], woa_ref[...])
        d["gate_c"] = _dot(d["hb"], wc_ref[:, 3 * CONV_WIDTH:3 * CONV_WIDTH + D_MODEL])
        d["gate_a"] = _dot(d.pop("hb"), wc_ref[:, 3 * CONV_WIDTH + D_MODEL:])

    def branch_a(d):
        d["p_a"] = _dot(d.pop("y_a"), woc_ref[...])

    def merge(d):
        d["merged"] = (_sigmoid(d.pop("gate_c")) * d.pop("p_a")
                       + _sigmoid(d.pop("gate_a")) * d.pop("p_b")).astype(BF16)

    def out_proj(d):
        d["o"] = _dot(d.pop("merged"), wo_ref[...])

    def residual(d):
        x1 = d.pop("x") + g1_ref[0] * d.pop("o")
        x1_ref[0, d["rs"], :] = x1
        h2 = _modulated_norm(x1, n2_ref[...], sh2_ref[0], sc2_ref[0])
        h2_ref[0, d["rs"], :] = _pack_bf16_pairs(h2)
        d["h_hi"] = h2.astype(BF16)
        d["h_lo"] = (h2 - d["h_hi"].astype(F32)).astype(BF16)

    def router(d):
        both = _dot(d.pop("h_hi"), wr_ref[...])
        d["lg"] = (both[:, :LANES] + both[:, LANES:] + _dot(d.pop("h_lo"), wr_ref[:, :LANES])) \
            + br_ref[...]

    def first_argmax(vals):
        mx = jnp.max(vals, axis=-1, keepdims=True)
        idx = jnp.min(jnp.where(vals == mx, lane, big), axis=-1, keepdims=True)
        return mx, idx

    def route(d):
        lg = d.pop("lg")
        is_g = (lane >= N_EXPERTS) & (lane < N_EXPERTS + N_GROUPS)
        g_mx, g_lane = first_argmax(jnp.where(is_g, lg, NEG))
        p_sel = 1.0 / jnp.sum(jnp.where(is_g, jnp.exp(lg - g_mx), 0.0), axis=-1, keepdims=True)
        g_idx = g_lane - N_EXPERTS
        in_g = (lane >= g_idx * EXPERTS_PER_GROUP) & (lane < (g_idx + 1) * EXPERTS_PER_GROUP)
        le = jnp.where(in_g, lg, NEG)
        v1, i1 = first_argmax(le)
        v2, i2 = first_argmax(jnp.where(lane == i1, NEG, le))
        e2 = jnp.exp(v2 - v1)
        w1 = p_sel / (1.0 + e2)
        w2 = w1 * e2
        rw_ref[0, d["rs"], :] = jnp.where(lane == 0, w1, 0.0) + jnp.where(lane == 1, w2, 0.0)
        d["i1"], d["i2"] = i1, i2
        d["onehot"] = jnp.where((lane == i1) | (lane == i2), 1.0, 0.0)

    def rank(d):
        onehot = d.pop("onehot")
        before = _dot(tri_ref[...], onehot.astype(BF16)) + cnt_ref[0:1, :]
        cnt_ref[...] = cnt_ref[...] + jnp.sum(onehot, axis=0, keepdims=True)
        i1, i2 = d.pop("i1"), d.pop("i2")
        r1 = jnp.sum(jnp.where(lane == i1, before, 0.0), axis=-1, keepdims=True)
        r2 = jnp.sum(jnp.where(lane == i2, before, 0.0), axis=-1, keepdims=True)
        rec = (jnp.where(lane == 0, i1.astype(F32), 0.0) + jnp.where(lane == 1, i2.astype(F32), 0.0)
               + jnp.where(lane == 2, r1, 0.0) + jnp.where(lane == 3, r2, 0.0))
        ridx_ref[0, :, d["rs"]] = rec.T[:8].astype(jnp.int32)

    stages = [conv_in, None, gates, branch_a, merge, out_proj, residual, router, route, rank]
    lag = 2
    for step in range(len(stages) + lag * (n_grp - 1)):
        for g, d in enumerate(st):
            k = step - lag * g
            if 0 <= k < len(stages):
                if stages[k] is None:
                    conv(d, carry_ref[...] if g == 0 else st[g - 1]["u_tail"])
                else:
                    stages[k](d)
    carry_ref[...] = st[-1]["u_tail"]


def _post(x, yb, n1, sh1, sc1, g1, n2, sh2, sc2, wc, cw, woc, woa, wo, wr, br, tm, b0):
    B, S, _ = yb.shape
    rows = tm // max(tm // POST_ROWS, 1)
    tri = jnp.asarray(np.tril(np.ones((rows, rows), np.float32), -1), BF16)
    const = lambda *shape: pl.BlockSpec(shape, lambda b, i: (0,) * len(shape),
                                        pipeline_mode=pl.Buffered(1))
    perb = pl.BlockSpec((1, 1, D_MODEL), lambda b, i: (b + b0, 0, 0))
    tok = lambda w: pl.BlockSpec((1, tm, w), lambda b, i: (b, i, 0))
    return pl.pallas_call(
        _post_kernel,
        out_shape=(jax.ShapeDtypeStruct((B, S, D_MODEL), F32),
                   jax.ShapeDtypeStruct((B, S, D_MODEL // 2), jnp.uint32),
                   jax.ShapeDtypeStruct((B, 8, S), jnp.int32),
                   jax.ShapeDtypeStruct((B, S, LANES), F32),
                   jax.ShapeDtypeStruct((8, LANES), F32)),
        grid=(B, S // tm),
        in_specs=[pl.BlockSpec((1, tm, D_MODEL), lambda b, i: (b + b0, i, 0)), tok(ATTN_WIDTH),
                  const(1, D_MODEL), perb, perb, perb,
                  const(1, D_MODEL), perb, perb,
                  const(D_MODEL, 3 * CONV_WIDTH + 2 * D_MODEL),
                  const(8, CONV_WIDTH),
                  const(CONV_WIDTH, D_MODEL), const(ATTN_WIDTH, D_MODEL),
                  const(D_MODEL, D_MODEL),
                  const(D_MODEL, 2 * LANES), const(1, LANES), const(rows, rows)],
        out_specs=(tok(D_MODEL), tok(D_MODEL // 2),
                   pl.BlockSpec((1, 8, tm), lambda b, i: (b, 0, i)), tok(LANES),
                   pl.BlockSpec((8, LANES), lambda b, i: (0, 0))),
        scratch_shapes=[pltpu.VMEM((8, CONV_WIDTH), F32)],
        compiler_params=pltpu.CompilerParams(
            dimension_semantics=("arbitrary", "arbitrary"), vmem_limit_bytes=VMEM_LIMIT),
        name="post",
    )(x, yb, n1, sh1, sc1, g1, n2, sh2, sc2, wc, cw, woc, woa, wo, wr, br, tri)


SC_CORES = 2
SC_SUBCORES = 16
SC_WORKERS = SC_CORES * SC_SUBCORES
SC_CHUNK = 64
ROW_WORDS = D_MODEL // 2


def _sc_mesh():
    return plsc.VectorSubcoreMesh(core_axis_name="c", subcore_axis_name="s",
                                  num_cores=SC_CORES, num_subcores=SC_SUBCORES)


def _dispatch_body(rows_hbm, idx1_hbm, idx2_hbm, xs_hbm, idx1_v, idx2_v, rows_v, *, n_chunks):
    wid = lax.axis_index("s") * SC_CORES + lax.axis_index("c")
    pltpu.sync_copy(idx1_hbm.at[wid], idx1_v)
    pltpu.sync_copy(idx2_hbm.at[wid], idx2_v)
    base = wid * (n_chunks * SC_CHUNK)

    @pl.loop(0, n_chunks)
    def _(j):
        pltpu.sync_copy(rows_hbm.at[pl.ds(base + j * SC_CHUNK, SC_CHUNK)], rows_v)
        pltpu.sync_copy(rows_v, xs_hbm.at[idx1_v.at[j]])
        pltpu.sync_copy(rows_v, xs_hbm.at[idx2_v.at[j]])


def _sc_scratch(n_chunks):
    return [pltpu.VMEM((n_chunks, SC_CHUNK), jnp.int32), pltpu.VMEM((n_chunks, SC_CHUNK), jnp.int32),
            pltpu.VMEM((SC_CHUNK, ROW_WORDS), jnp.uint32)]


def _dispatch(rows, idx1, idx2, n_slots):
    n_chunks = idx1.shape[1]
    return pl.kernel(
        functools.partial(_dispatch_body, n_chunks=n_chunks),
        out_type=jax.ShapeDtypeStruct((n_slots, ROW_WORDS), jnp.uint32),
        mesh=_sc_mesh(),
        scratch_types=_sc_scratch(n_chunks),
        name="dispatch",
    )(rows, idx1, idx2)


def _collect_body(ys_hbm, idx1_hbm, idx2_hbm, g1_hbm, g2_hbm, idx1_v, idx2_v, rows_v, *, n_chunks):
    wid = lax.axis_index("s") * SC_CORES + lax.axis_index("c")
    pltpu.sync_copy(idx1_hbm.at[wid], idx1_v)
    pltpu.sync_copy(idx2_hbm.at[wid], idx2_v)
    base = wid * (n_chunks * SC_CHUNK)

    @pl.loop(0, n_chunks)
    def _(j):
        dst = pl.ds(base + j * SC_CHUNK, SC_CHUNK)
        pltpu.sync_copy(ys_hbm.at[idx1_v.at[j]], rows_v)
        pltpu.sync_copy(rows_v, g1_hbm.at[dst])
        pltpu.sync_copy(ys_hbm.at[idx2_v.at[j]], rows_v)
        pltpu.sync_copy(rows_v, g2_hbm.at[dst])


def _collect(ys, idx1, idx2):
    n_chunks = idx1.shape[1]
    out = jax.ShapeDtypeStruct((SC_WORKERS * n_chunks * SC_CHUNK, ROW_WORDS), jnp.uint32)
    return pl.kernel(
        functools.partial(_collect_body, n_chunks=n_chunks),
        out_type=(out, out),
        mesh=_sc_mesh(),
        scratch_types=_sc_scratch(n_chunks),
        name="collect",
    )(ys, idx1, idx2)


def _moe_kernel(te_ref, nt_ref, xs_ref, wg_ref, wu_ref, wd_ref, *rest):
    ys_ref = rest[-1]

    @pl.when(pl.program_id(0) < nt_ref[0])
    def _():
        left, right = _unpack_bf16_pairs(xs_ref[...])
        xb = jnp.concatenate([left.astype(BF16), right.astype(BF16)], axis=1)
        g = _dot(xb, wg_ref[0].astype(BF16))
        u = _dot(xb, wu_ref[0].astype(BF16))
        a = (g * _sigmoid(g) * u).astype(BF16)
        ys_ref[...] = _pack_bf16_pairs(_dot(a, wd_ref[0].astype(BF16)))


def _moe(xs, tile_expert, n_tiles, wg, wu, wd, tm, after):
    n_slots = xs.shape[0]
    row_blk = lambda i, te, nt: (jnp.minimum(i, nt[0] - 1), 0)
    w_blk = lambda i, te, nt: (te[i], 0, 0)
    return pl.pallas_call(
        _moe_kernel,
        out_shape=jax.ShapeDtypeStruct((n_slots, ROW_WORDS), jnp.uint32),
        grid_spec=pltpu.PrefetchScalarGridSpec(
            num_scalar_prefetch=2,
            grid=(n_slots // tm,),
            in_specs=[pl.BlockSpec((tm, ROW_WORDS), row_blk),
                      pl.BlockSpec((1, D_MODEL, D_EXPERT), w_blk),
                      pl.BlockSpec((1, D_MODEL, D_EXPERT), w_blk),
                      pl.BlockSpec((1, D_EXPERT, D_MODEL), w_blk)]
            + [pl.BlockSpec(memory_space=pl.ANY)] * len(after),
            out_specs=pl.BlockSpec((tm, ROW_WORDS), row_blk)),
        compiler_params=pltpu.CompilerParams(
            dimension_semantics=("arbitrary",), vmem_limit_bytes=VMEM_LIMIT),
        name="moe",
    )(tile_expert, n_tiles, xs, wg, wu, wd, *after)


def _final_kernel(x1_ref, g1_ref, g2_ref, rw_ref, gate_ref, *rest):
    o_ref = rest[-1]
    rw = rw_ref[0]
    w1 = rw[:, 0:1]
    w2 = rw[:, 1:2]
    a_l, a_r = _unpack_bf16_pairs(g1_ref[0])
    b_l, b_r = _unpack_bf16_pairs(g2_ref[0])
    moe = jnp.concatenate([w1 * a_l + w2 * b_l, w1 * a_r + w2 * b_r], axis=1)
    o_ref[0] = x1_ref[0] + gate_ref[0] * moe


def _final(x1, g1, g2, rw, gate2, tm, b0, src0, out_prev, after):
    nb, S, _ = g1.shape
    grp = lambda w: pl.BlockSpec((1, tm, w), lambda b, i: (b + src0, i, 0))
    tok = lambda w: pl.BlockSpec((1, tm, w), lambda b, i: (b, i, 0))
    extra = (() if out_prev is None else (out_prev,)) + tuple(after)
    return pl.pallas_call(
        _final_kernel,
        out_shape=jax.ShapeDtypeStruct((gate2.shape[0], S, D_MODEL), F32),
        grid=(nb, S // tm),
        in_specs=[grp(D_MODEL), tok(ROW_WORDS), tok(ROW_WORDS), grp(LANES),
                  pl.BlockSpec((1, 1, D_MODEL), lambda b, i: (b + b0 + src0, 0, 0))]
        + [pl.BlockSpec(memory_space=pl.ANY)] * len(extra),
        out_specs=pl.BlockSpec((1, tm, D_MODEL), lambda b, i: (b + b0 + src0, i, 0)),
        input_output_aliases={} if out_prev is None else {5: 0},
        compiler_params=pltpu.CompilerParams(
            dimension_semantics=("arbitrary", "arbitrary"), vmem_limit_bytes=VMEM_LIMIT),
        name="final",
    )(x1, g1, g2, rw, gate2, *extra)


def _pick(n, pref):
    t = min(n, pref)
    assert n % t == 0, (n, t)
    return t


def _slots_kernel(offs_ref, r_ref, o_ref):
    r = r_ref[0]
    base = jnp.zeros_like(r)
    for e in range(N_EXPERTS):
        base = jnp.where(r == e, offs_ref[e], base)
    o_ref[0] = base + pltpu.roll(r, 6, 0)


def _slots(ridx, offs):
    B, _, S = ridx.shape
    return pl.pallas_call(
        _slots_kernel,
        out_shape=jax.ShapeDtypeStruct((B, 8, S), jnp.int32),
        grid_spec=pltpu.PrefetchScalarGridSpec(
            num_scalar_prefetch=1, grid=(B,),
            in_specs=[pl.BlockSpec((1, 8, S), lambda b, offs: (b, 0, 0))],
            out_specs=pl.BlockSpec((1, 8, S), lambda b, offs: (b, 0, 0))),
        compiler_params=pltpu.CompilerParams(dimension_semantics=("arbitrary",)),
        name="slots",
    )(offs, ridx)


def _route_plan(ridx, counts, tm_e, T):
    counts = counts.astype(jnp.int32)
    tiles = (counts + tm_e - 1) // tm_e
    tile_end = jnp.cumsum(tiles)
    offs = (tile_end - tiles) * tm_e
    slots = _slots(ridx, offs)
    n_chunks = T // (SC_WORKERS * SC_CHUNK)
    idx1 = slots[:, 0, :].reshape(SC_WORKERS, n_chunks, SC_CHUNK)
    idx2 = slots[:, 1, :].reshape(SC_WORKERS, n_chunks, SC_CHUNK)
    n_tiles_max = 2 * T // tm_e + N_EXPERTS
    tile_ids = jnp.arange(n_tiles_max, dtype=jnp.int32)
    tile_expert = jnp.sum((tile_end[None, :] <= tile_ids[:, None]).astype(jnp.int32), axis=1)
    tile_expert = jnp.minimum(tile_expert, N_EXPERTS - 1)
    return idx1, idx2, tile_expert, tile_end[-1:].astype(jnp.int32), n_tiles_max * tm_e


def _layer(x, c, w_ada, b_ada, norm1_w, w_in, b_forget, conv_w, q_norm_w, k_norm_w,
           w_out_conv, w_out_attn, w_o, norm2_w, w_rg, b_rg, w_re, b_re, w_gate, w_up, w_down):
    B, S, _ = x.shape
    n_grp = 2 if B % 2 == 0 and (B // 2 * S) % (SC_WORKERS * SC_CHUNK) == 0 else 1
    Bg = B // n_grp
    T = Bg * S
    assert T % (SC_WORKERS * SC_CHUNK) == 0, T
    mod = _ada(c, w_ada, b_ada.reshape(1, -1)).reshape(B, 6, 1, D_MODEL)
    shift1, scale1, gate1, shift2, scale2, gate2 = (mod[:, t] for t in range(6))

    cuts = np.cumsum([0, CONV_WIDTH, CONV_WIDTH, CONV_WIDTH, ATTN_WIDTH, ATTN_WIDTH, ATTN_WIDTH,
                      N_HEADS, D_MODEL, D_MODEL])
    w_conv3 = w_in[:, cuts[0]:cuts[3]]
    w_qvT = jnp.concatenate([w_in[:, cuts[3]:cuts[4]], w_in[:, cuts[5]:cuts[6]]], axis=1).T.astype(BF16)
    w_k = w_in[:, cuts[4]:cuts[5]].astype(BF16)
    w_f = jnp.pad(w_in[:, cuts[6]:cuts[7]], ((0, 0), (0, LANES - N_HEADS))).astype(BF16)
    b_f = jnp.pad(b_forget, (0, LANES - N_HEADS)).reshape(1, LANES)
    w_cgg = jnp.concatenate([w_conv3, w_in[:, cuts[7]:cuts[9]]], axis=1).astype(BF16)

    tm_qkv = _pick(S, 1024)
    qnwT = jnp.broadcast_to((jnp.tile(q_norm_w, N_HEADS) * (LOG2E * HEAD_DIM ** -0.5))[:, None],
                            (ATTN_WIDTH, LANES))
    tq = _pick(S // 2, 512)
    w_r = jnp.pad(jnp.concatenate([w_re, w_rg], axis=1),
                  ((0, 0), (0, LANES - N_EXPERTS - N_GROUPS)))
    w_r_hi = w_r.astype(BF16)
    w_r_lo = (w_r - w_r_hi.astype(F32)).astype(BF16)
    b_r = jnp.pad(jnp.concatenate([b_re, b_rg]), (0, LANES - N_EXPERTS - N_GROUPS)).reshape(1, LANES)
    cw = jnp.pad(conv_w, ((0, 8 - CONV_K), (0, 0)))
    tm_post = _pick(S, 1024)
    tm_e = 512
    w_oc, w_oa, w_ob, w_rs = (w_out_conv.astype(BF16), w_out_attn.astype(BF16), w_o.astype(BF16),
                              jnp.concatenate([w_r_hi, w_r_lo], axis=1))

    n_half = 2 if Bg % 2 == 0 and (Bg // 2 * S) % (SC_WORKERS * SC_CHUNK) == 0 else 1
    Bh = Bg // n_half

    def experts(grp, after):
        x1, rw, idx1, idx2, tile_expert, n_tiles, xs = grp
        ys = _moe(xs, tile_expert, n_tiles, w_gate, w_up, w_down, tm_e, after)
        halves = lambda idx: idx.reshape(n_half, SC_WORKERS, idx.shape[1] // n_half, SC_CHUNK)
        return ys, [_collect(ys, i1, i2) for i1, i2 in zip(halves(idx1), halves(idx2))]

    def finish(g, grp, gathered, out, after):
        for hh, (g1, g2) in enumerate(gathered):
            out = _final(grp[0], g1.reshape(Bh, S, ROW_WORDS), g2.reshape(Bh, S, ROW_WORDS), grp[1],
                         gate2, _pick(S, 1024), g * Bg, hh * Bh, out, after if hh == 0 else [])
        return out

    out, prev, prev_gathered, ys_prev = None, None, None, None
    for g in range(n_grp):
        b0 = g * Bg
        qT_aug, k_aug, vT = _qkv(x, norm1_w.reshape(1, -1), shift1, scale1, w_qvT, w_k, w_f, b_f,
                                 qnwT, jnp.tile(k_norm_w, N_HEADS).reshape(1, -1), tm_qkv, b0, Bg)
        if prev is not None:
            ys_prev, prev_gathered = experts(prev, [qT_aug])
        y_b = _attention(qT_aug, k_aug, vT, tq, _pick(tq, 256),
                         [] if ys_prev is None else [ys_prev])
        x1, h2p, ridx, rw, counts = _post(x, y_b, norm1_w.reshape(1, -1), shift1, scale1, gate1,
                                          norm2_w.reshape(1, -1), shift2, scale2,
                                          w_cgg, cw, w_oc, w_oa, w_ob, w_rs, b_r, tm_post, b0)
        idx1, idx2, tile_expert, n_tiles, n_slots = _route_plan(ridx, counts[0, :N_EXPERTS], tm_e, T)
        xs = _dispatch(h2p.reshape(T, ROW_WORDS), idx1, idx2, n_slots)
        if prev is not None:
            out = finish(g - 1, prev, prev_gathered, out, [idx1])
        prev = (x1, rw, idx1, idx2, tile_expert, n_tiles, xs)
    _, last_gathered = experts(prev, [] if out is None else [out])
    return finish(n_grp - 1, prev, last_gathered, out, [])


def kernel(x, c, w_ada, b_ada, norm1_w, w_in, b_forget, conv_w, q_norm_w, k_norm_w, w_out_conv,
           w_out_attn, w_o, norm2_w, w_router_group, b_router_group, w_router_expert,
           b_router_expert, w_gate, w_up, w_down):
    for l in range(w_ada.shape[0]):
        x = _layer(x, c, w_ada[l], b_ada[l], norm1_w[l], w_in[l], b_forget[l], conv_w[l],
                   q_norm_w[l], k_norm_w[l], w_out_conv[l], w_out_attn[l], w_o[l], norm2_w[l],
                   w_router_group[l], b_router_group[l], w_router_expert[l], b_router_expert[l],
                   w_gate[l], w_up[l], w_down[l])
    return x
```

```python
import functools

import jax
import jax.numpy as jnp
import numpy as np
from jax import lax
from jax.experimental import pallas as pl
from jax.experimental.pallas import tpu as pltpu
from jax.experimental.pallas import tpu_sc as plsc

D_MODEL = 1024
CONV_WIDTH = 512
CONV_K = 3
N_HEADS = 8
HEAD_DIM = 64
ATTN_WIDTH = N_HEADS * HEAD_DIM
N_PAIRS = N_HEADS // 2
N_GROUPS = 4
EXPERTS_PER_GROUP = 8
N_EXPERTS = N_GROUPS * EXPERTS_PER_GROUP
D_EXPERT = 256
EPS = 1e-6
LANES = 128
AUG = 2 * LANES
BIAS_W = 6
VROWS = HEAD_DIM + 16
NEG = -1e30
LOG2E = 1.4426950408889634

F32 = jnp.float32
BF16 = jnp.bfloat16
VMEM_LIMIT = 56 * 1024 * 1024


def _sigmoid(z):
    return 1.0 / (1.0 + jnp.exp(-z))


def _split3(z):
    hi = z.astype(BF16)
    r = z - hi.astype(F32)
    mid = r.astype(BF16)
    lo = (r - mid.astype(F32)).astype(BF16)
    return hi, mid, lo


def _dot(a, b):
    return jnp.dot(a, b, preferred_element_type=F32)


def _modulated_norm(x, nw, shift, scale):
    ms = jnp.mean(x * x, axis=-1, keepdims=True)
    return (x * lax.rsqrt(ms + EPS) * nw) * (1.0 + scale) + shift


def _ada_kernel(c_ref, w_ref, b_ref, o_ref):
    c = c_ref[...]
    a = c * _sigmoid(c)
    a_hi = a.astype(BF16)
    a_lo = (a - a_hi.astype(F32)).astype(BF16)
    w = w_ref[...]
    w_hi = w.astype(BF16)
    w_lo = (w - w_hi.astype(F32)).astype(BF16)
    o_ref[...] = (_dot(a_hi, w_hi) + _dot(a_lo, w_hi) + _dot(a_hi, w_lo)) + b_ref[...]


def _ada(c, w_ada, b_ada):
    B = c.shape[0]
    n = w_ada.shape[1] // D_MODEL
    return pl.pallas_call(
        _ada_kernel,
        out_shape=jax.ShapeDtypeStruct((B, n * D_MODEL), F32),
        grid=(n,),
        in_specs=[pl.BlockSpec((B, D_MODEL), lambda j: (0, 0)),
                  pl.BlockSpec((D_MODEL, D_MODEL), lambda j: (0, j)),
                  pl.BlockSpec((1, D_MODEL), lambda j: (0, j))],
        out_specs=pl.BlockSpec((B, D_MODEL), lambda j: (0, j)),
        compiler_params=pltpu.CompilerParams(dimension_semantics=("arbitrary",)),
        name="ada",
    )(c, w_ada, b_ada)


QKV_ROWS = 256
_NT = (((1,), (1,)), ((), ()))


def _lane_tile(a, width):
    return jnp.concatenate([a] * (width // a.shape[1]), axis=1)


def _qkv_kernel(x_ref, nw_ref, sh_ref, sc_ref, wqv_ref, wk_ref, wf_ref, bf_ref, qnw_ref, knw_ref,
                gsum_ref, pq_ref, pk_ref, cq_ref, ck_ref,
                qT_ref, k_ref, vT_ref, carry_ref):
    tm = x_ref.shape[1]
    n_grp = max(tm // QKV_ROWS, 1)
    rows = tm // n_grp
    row = lax.broadcasted_iota(jnp.int32, (rows, LANES), 0)

    @pl.when(pl.program_id(1) == 0)
    def _():
        carry_ref[...] = jnp.zeros_like(carry_ref)

    st = [dict(rs=pl.ds(g * rows, rows)) for g in range(n_grp)]

    def project(d):
        h = _modulated_norm(x_ref[0, d["rs"], :], nw_ref[...], sh_ref[0], sc_ref[0])
        hb = h.astype(BF16)
        d["qvT"] = lax.dot_general(wqv_ref[...], hb, _NT, preferred_element_type=F32)
        d["k"] = _dot(hb, wk_ref[...])
        d["fl"] = _dot(hb, wf_ref[...]) + bf_ref[...]

    def norms(d):
        qvT = d["qvT"]
        heads = []
        for hd in range(N_HEADS):
            z = qvT[hd * HEAD_DIM:(hd + 1) * HEAD_DIM]
            heads.append(z * lax.rsqrt(jnp.mean(z * z, axis=0, keepdims=True) + EPS))
        qnT = jnp.concatenate(heads, axis=0) * _lane_tile(qnw_ref[...], rows)
        for j in range(N_PAIRS):
            qT_ref[0, j, :LANES, d["rs"]] = qnT[j * LANES:(j + 1) * LANES].astype(BF16)
            for t in range(2):
                r0 = ATTN_WIDTH + (2 * j + t) * HEAD_DIM
                vT_ref[0, j, t * VROWS:t * VROWS + HEAD_DIM, d["rs"]] = qvT[r0:r0 + HEAD_DIM].astype(BF16)
                vT_ref[0, j, t * VROWS + HEAD_DIM:(t + 1) * VROWS, d["rs"]] = \
                    jnp.ones((VROWS - HEAD_DIM, rows), BF16)
        del d["qvT"]
        k = d.pop("k")
        ss = _dot((k * k).astype(BF16), gsum_ref[...])
        kn = k * lax.rsqrt(ss * (1.0 / HEAD_DIM) + EPS) * knw_ref[...]
        for j in range(N_PAIRS):
            k_ref[0, d["rs"], j * AUG:j * AUG + LANES] = kn[:, j * LANES:(j + 1) * LANES].astype(BF16)

    def forget(d, before):
        fl = d.pop("fl")
        cum = jnp.minimum(fl, 0.0) - jnp.log(1.0 + jnp.exp(-jnp.abs(fl)))
        s = 1
        while s < rows:
            cum = cum + jnp.where(row >= s, pltpu.roll(cum, s, 0), 0.0)
            s *= 2
        cum = cum + before[7:8, :]
        d["tail"] = cum[rows - 8:, :]
        d["parts"] = jnp.concatenate(_split3(cum * LOG2E), axis=1)

    def bias(d):
        parts = d.pop("parts")
        eqT = (lax.dot_general(pq_ref[...], parts, _NT, preferred_element_type=F32)
               + _lane_tile(cq_ref[...], rows)).astype(BF16)
        ek = (_dot(parts, pk_ref[...]) + ck_ref[...]).astype(BF16)
        for j in range(N_PAIRS):
            qT_ref[0, j, LANES:, d["rs"]] = eqT
            k_ref[0, d["rs"], j * AUG + LANES:(j + 1) * AUG] = ek

    stages = [project, norms, None, bias]
    lag = 1
    for step in range(len(stages) + lag * (n_grp - 1)):
        for g, d in enumerate(st):
            kk = step - lag * g
            if 0 <= kk < len(stages):
                if stages[kk] is None:
                    forget(d, carry_ref[...] if g == 0 else st[g - 1]["tail"])
                else:
                    stages[kk](d)
    carry_ref[...] = st[-1]["tail"]


def _bias_placement():
    pq = np.zeros((LANES, 3 * LANES), np.float32)
    pk = np.zeros((3 * LANES, LANES), np.float32)
    cq = np.zeros((LANES, LANES), np.float32)
    ck = np.zeros((1, LANES), np.float32)
    for hd in range(N_HEADS):
        base = BIAS_W * hd
        for p in range(3):
            pq[base + p, p * LANES + hd] = 1.0
            pk[p * LANES + hd, base + 3 + p] = -1.0
            cq[base + 3 + p, :] = 1.0
            ck[0, base + p] = 1.0
    return (jnp.asarray(pq, BF16), jnp.asarray(pk, BF16), jnp.asarray(cq), jnp.asarray(ck))


def _qkv(x, nw, shift, scale, wqvT, wk, wf, bf, qnwT, knw, tm, b0, B):
    S = x.shape[1]
    gsum = jnp.asarray(np.kron(np.eye(N_HEADS), np.ones((HEAD_DIM, HEAD_DIM))), BF16)
    pq, pk, cq, ck = _bias_placement()
    const = lambda *shape: pl.BlockSpec(shape, lambda b, i: (0,) * len(shape),
                                        pipeline_mode=pl.Buffered(1))
    return pl.pallas_call(
        _qkv_kernel,
        out_shape=(jax.ShapeDtypeStruct((B, N_PAIRS, AUG, S), BF16),
                   jax.ShapeDtypeStruct((B, S, N_PAIRS * AUG), BF16),
                   jax.ShapeDtypeStruct((B, N_PAIRS, 2 * VROWS, S), BF16)),
        grid=(B, S // tm),
        in_specs=[pl.BlockSpec((1, tm, D_MODEL), lambda b, i: (b + b0, i, 0)),
                  const(1, D_MODEL),
                  pl.BlockSpec((1, 1, D_MODEL), lambda b, i: (b + b0, 0, 0)),
                  pl.BlockSpec((1, 1, D_MODEL), lambda b, i: (b + b0, 0, 0)),
                  const(2 * ATTN_WIDTH, D_MODEL),
                  const(D_MODEL, ATTN_WIDTH),
                  const(D_MODEL, LANES),
                  const(1, LANES),
                  const(ATTN_WIDTH, LANES),
                  const(1, ATTN_WIDTH),
                  const(ATTN_WIDTH, ATTN_WIDTH),
                  const(LANES, 3 * LANES),
                  const(3 * LANES, LANES),
                  const(LANES, LANES),
                  const(1, LANES)],
        out_specs=(pl.BlockSpec((1, N_PAIRS, AUG, tm), lambda b, i: (b, 0, 0, i)),
                   pl.BlockSpec((1, tm, N_PAIRS * AUG), lambda b, i: (b, i, 0)),
                   pl.BlockSpec((1, N_PAIRS, 2 * VROWS, tm), lambda b, i: (b, 0, 0, i))),
        scratch_shapes=[pltpu.VMEM((8, LANES), F32)],
        compiler_params=pltpu.CompilerParams(
            dimension_semantics=("arbitrary", "arbitrary"), vmem_limit_bytes=VMEM_LIMIT),
        name="qkv",
    )(x, nw, shift, scale, wqvT, wk, wf, bf, qnwT, knw, gsum, pq, pk, cq, ck)


def _attn_kernel(qT_ref, k_ref, vT_ref, *rest, tq, cw, nt):
    o_ref, qq_ref, s_ref, smax_ref, m_ref, acc_ref = rest[-6:]
    i = pl.program_id(2)
    n = tq // cw
    chains = [(a, t, c) for a in range(nt) for t in range(2) for c in range(n)]
    feat = lax.broadcasted_iota(jnp.int32, (AUG, tq), 0)
    for t in range(2):
        bias0 = LANES + BIAS_W * (2 * pl.program_id(1) + t)
        keep = ((feat >= t * HEAD_DIM) & (feat < (t + 1) * HEAD_DIM)) | \
               ((feat >= bias0) & (feat < bias0 + BIAS_W))
        for a in range(nt):
            qT = qT_ref[0, 0, :, a * tq:(a + 1) * tq]
            qh = jnp.where(keep, qT, jnp.zeros_like(qT))
            for c in range(n):
                qq_ref[chains.index((a, t, c))] = qh[:, c * cw:(c + 1) * cw]
    kpos = lax.broadcasted_iota(jnp.int32, (tq, cw), 0)
    qpos = lax.broadcasted_iota(jnp.int32, (tq, cw), 1)

    def scores(j, slot, which):
        k_blk = k_ref[0, pl.ds(pl.multiple_of(j * tq, tq), tq), :]
        for ci in which:
            s = _dot(k_blk, qq_ref[ci])
            s_ref[slot, ci] = s
            smax_ref[slot, ci] = jnp.broadcast_to(jnp.max(s, axis=0, keepdims=True), (8, cw))

    def absorb(j, slot, which, diagonal=()):
        start = pl.multiple_of(j * tq, tq)
        for ci in which:
            _, t, c = chains[ci]
            vj = vT_ref[0, 0, t * VROWS:(t + 1) * VROWS, pl.ds(start, tq)]
            s = s_ref[slot, ci]
            if ci in diagonal:
                s = jnp.where(kpos <= qpos + c * cw, s, NEG)
                smax = jnp.max(s, axis=0, keepdims=True)
            else:
                smax = smax_ref[slot, ci, 0:1]
            m = m_ref[ci, 0:1]
            m_new = jnp.maximum(m, smax)
            p = jnp.exp2(s - m_new).astype(BF16)
            acc_ref[ci] = jnp.exp2(m - m_new) * acc_ref[ci] + _dot(vj, p)
            m_ref[ci] = jnp.broadcast_to(m_new, (8, cw))

    every = list(range(len(chains)))
    from_tile = lambda r: [ci for ci in every if chains[ci][0] >= r]
    m_ref[...] = jnp.full(m_ref.shape, NEG, F32)
    acc_ref[...] = jnp.zeros(acc_ref.shape, F32)
    scores(0, 0, every)

    def two_blocks(jj, _):
        j = 2 * jj
        scores(j + 1, 1, every)
        absorb(j, 0, every)
        scores(j + 2, 0, every)
        absorb(j + 1, 1, every)
        return 0

    lax.fori_loop(0, i * (nt // 2), two_blocks, 0)
    for r in range(nt):
        if r + 1 < nt:
            scores(nt * i + r + 1, (r + 1) % 2, from_tile(r + 1))
        absorb(nt * i + r, r % 2, from_tile(r),
               diagonal=[ci for ci in every if chains[ci][0] == r])

    for a in range(nt):
        outs = [acc_ref[ci, :HEAD_DIM] / acc_ref[ci, HEAD_DIM:HEAD_DIM + 1]
                for ci in every if chains[ci][0] == a]
        oT = jnp.concatenate([jnp.concatenate(outs[:n], axis=1), jnp.concatenate(outs[n:], axis=1)],
                             axis=0)
        o_ref[0, a * tq:(a + 1) * tq, :] = oT.T.astype(BF16)


def _attention(qT_aug, k_aug, vT, tq, cw, after):
    B, S, _ = k_aug.shape
    nt = 4 if S % (4 * tq) == 0 else 2
    n_chains = 2 * nt * tq // cw
    return pl.pallas_call(
        functools.partial(_attn_kernel, tq=tq, cw=cw, nt=nt),
        scratch_shapes=[pltpu.VMEM((n_chains, AUG, cw), BF16),
                        pltpu.VMEM((2, n_chains, tq, cw), F32),
                        pltpu.VMEM((2, n_chains, 8, cw), F32),
                        pltpu.VMEM((n_chains, 8, cw), F32),
                        pltpu.VMEM((n_chains, VROWS, cw), F32)],
        out_shape=jax.ShapeDtypeStruct((B, S, ATTN_WIDTH), BF16),
        grid=(B, N_PAIRS, S // (nt * tq)),
        in_specs=[pl.BlockSpec((1, 1, AUG, nt * tq), lambda b, j, i: (b, j, 0, i)),
                  pl.BlockSpec((1, S, AUG), lambda b, j, i: (b, 0, j)),
                  pl.BlockSpec((1, 1, 2 * VROWS, S), lambda b, j, i: (b, j, 0, 0))]
        + [pl.BlockSpec(memory_space=pl.ANY)] * len(after),
        out_specs=pl.BlockSpec((1, nt * tq, LANES), lambda b, j, i: (b, i, j)),
        compiler_params=pltpu.CompilerParams(
            dimension_semantics=("arbitrary", "arbitrary", "arbitrary"),
            vmem_limit_bytes=VMEM_LIMIT),
        name="attn",
    )(qT_aug, k_aug, vT, *after)


def _pack_bf16_pairs(z):
    w = z.shape[1] // 2
    bits = pltpu.bitcast(z.astype(BF16).astype(F32), jnp.uint32)
    return bits[:, :w] | (bits[:, w:] >> 16)


def _unpack_bf16_pairs(p):
    return (pltpu.bitcast(p & jnp.uint32(0xFFFF0000), F32), pltpu.bitcast(p << 16, F32))


POST_ROWS = 256


def _post_kernel(x_ref, yb_ref, n1_ref, sh1_ref, sc1_ref, g1_ref, n2_ref, sh2_ref, sc2_ref,
                 wc_ref, cw_ref, woc_ref, woa_ref, wo_ref, wr_ref, br_ref, tri_ref,
                 x1_ref, h2_ref, ridx_ref, rw_ref, cnt_ref, carry_ref):
    tm = x_ref.shape[1]
    n_grp = max(tm // POST_ROWS, 1)
    rows = tm // n_grp
    lane = lax.broadcasted_iota(jnp.int32, (rows, LANES), 1)
    row8 = lax.broadcasted_iota(jnp.int32, (8, CONV_WIDTH), 0)
    big = jnp.int32(1 << 20)

    @pl.when(pl.program_id(1) == 0)
    def _():
        carry_ref[...] = jnp.zeros_like(carry_ref)

    @pl.when((pl.program_id(0) == 0) & (pl.program_id(1) == 0))
    def _():
        cnt_ref[...] = jnp.zeros_like(cnt_ref)

    st = [dict(rs=pl.ds(g * rows, rows)) for g in range(n_grp)]

    def conv_in(d):
        d["x"] = x_ref[0, d["rs"], :]
        d["hb"] = _modulated_norm(d["x"], n1_ref[...], sh1_ref[0], sc1_ref[0]).astype(BF16)
        d["x_in"] = _dot(d["hb"], wc_ref[:, :CONV_WIDTH])
        d["conv_c"] = _dot(d["hb"], wc_ref[:, 2 * CONV_WIDTH:3 * CONV_WIDTH])
        d["conv_b"] = _dot(d["hb"], wc_ref[:, CONV_WIDTH:2 * CONV_WIDTH])

    def conv(d, prev):
        u = d.pop("conv_c") * d.pop("x_in")
        d["u_tail"] = u[rows - 8:, :]

        def shifted(k):
            r = pltpu.roll(u, k, 0)
            top = jnp.where(row8 < k, pltpu.roll(prev, k, 0), r[:8])
            return jnp.concatenate([top, r[8:]], axis=0)

        cw = cw_ref[...]
        cv = cw[0:1] * shifted(2) + cw[1:2] * shifted(1) + cw[2:3] * u
        d["y_a"] = (d.pop("conv_b") * cv).astype(BF16)

    def gates(d):
        d["p_b"] = _dot(yb_ref[0, d["rs"], :], woa_ref[...])
        d["gate_c"] = _dot(d["hb"], wc_ref[:, 3 * CONV_WIDTH:3 * CONV_WIDTH + D_MODEL])
        d["gate_a"] = _dot(d.pop("hb"), wc_ref[:, 3 * CONV_WIDTH + D_MODEL:])

    def branch_a(d):
        d["p_a"] = _dot(d.pop("y_a"), woc_ref[...])

    def merge(d):
        d["merged"] = (_sigmoid(d.pop("gate_c")) * d.pop("p_a")
                       + _sigmoid(d.pop("gate_a")) * d.pop("p_b")).astype(BF16)

    def out_proj(d):
        d["o"] = _dot(d.pop("merged"), wo_ref[...])

    def residual(d):
        x1 = d.pop("x") + g1_ref[0] * d.pop("o")
        x1_ref[0, d["rs"], :] = x1
        h2 = _modulated_norm(x1, n2_ref[...], sh2_ref[0], sc2_ref[0])
        h2_ref[0, d["rs"], :] = _pack_bf16_pairs(h2)
        d["h_hi"] = h2.astype(BF16)
        d["h_lo"] = (h2 - d["h_hi"].astype(F32)).astype(BF16)

    def router(d):
        both = _dot(d.pop("h_hi"), wr_ref[...])
        d["lg"] = (both[:, :LANES] + both[:, LANES:] + _dot(d.pop("h_lo"), wr_ref[:, :LANES])) \
            + br_ref[...]

    def first_argmax(vals):
        mx = jnp.max(vals, axis=-1, keepdims=True)
        idx = jnp.min(jnp.where(vals == mx, lane, big), axis=-1, keepdims=True)
        return mx, idx

    def route(d):
        lg = d.pop("lg")
        is_g = (lane >= N_EXPERTS) & (lane < N_EXPERTS + N_GROUPS)
        g_mx, g_lane = first_argmax(jnp.where(is_g, lg, NEG))
        p_sel = 1.0 / jnp.sum(jnp.where(is_g, jnp.exp(lg - g_mx), 0.0), axis=-1, keepdims=True)
        g_idx = g_lane - N_EXPERTS
        in_g = (lane >= g_idx * EXPERTS_PER_GROUP) & (lane < (g_idx + 1) * EXPERTS_PER_GROUP)
        le = jnp.where(in_g, lg, NEG)
        v1, i1 = first_argmax(le)
        v2, i2 = first_argmax(jnp.where(lane == i1, NEG, le))
        e2 = jnp.exp(v2 - v1)
        w1 = p_sel / (1.0 + e2)
        w2 = w1 * e2
        rw_ref[0, d["rs"], :] = jnp.where(lane == 0, w1, 0.0) + jnp.where(lane == 1, w2, 0.0)
        d["i1"], d["i2"] = i1, i2
        d["onehot"] = jnp.where((lane == i1) | (lane == i2), 1.0, 0.0)

    def rank(d):
        onehot = d.pop("onehot")
        before = _dot(tri_ref[...], onehot.astype(BF16)) + cnt_ref[0:1, :]
        cnt_ref[...] = cnt_ref[...] + jnp.sum(onehot, axis=0, keepdims=True)
        i1, i2 = d.pop("i1"), d.pop("i2")
        r1 = jnp.sum(jnp.where(lane == i1, before, 0.0), axis=-1, keepdims=True)
        r2 = jnp.sum(jnp.where(lane == i2, before, 0.0), axis=-1, keepdims=True)
        rec = (jnp.where(lane == 0, i1.astype(F32), 0.0) + jnp.where(lane == 1, i2.astype(F32), 0.0)
               + jnp.where(lane == 2, r1, 0.0) + jnp.where(lane == 3, r2, 0.0))
        ridx_ref[0, :, d["rs"]] = rec.T[:8].astype(jnp.int32)

    stages = [conv_in, None, gates, branch_a, merge, out_proj, residual, router, route, rank]
    lag = 2
    for step in range(len(stages) + lag * (n_grp - 1)):
        for g, d in enumerate(st):
            k = step - lag * g
            if 0 <= k < len(stages):
                if stages[k] is None:
                    conv(d, carry_ref[...] if g == 0 else st[g - 1]["u_tail"])
                else:
                    stages[k](d)
    carry_ref[...] = st[-1]["u_tail"]


def _post(x, yb, n1, sh1, sc1, g1, n2, sh2, sc2, wc, cw, woc, woa, wo, wr, br, tm, b0):
    B, S, _ = yb.shape
    rows = tm // max(tm // POST_ROWS, 1)
    tri = jnp.asarray(np.tril(np.ones((rows, rows), np.float32), -1), BF16)
    const = lambda *shape: pl.BlockSpec(shape, lambda b, i: (0,) * len(shape),
                                        pipeline_mode=pl.Buffered(1))
    perb = pl.BlockSpec((1, 1, D_MODEL), lambda b, i: (b + b0, 0, 0))
    tok = lambda w: pl.BlockSpec((1, tm, w), lambda b, i: (b, i, 0))
    return pl.pallas_call(
        _post_kernel,
        out_shape=(jax.ShapeDtypeStruct((B, S, D_MODEL), F32),
                   jax.ShapeDtypeStruct((B, S, D_MODEL // 2), jnp.uint32),
                   jax.ShapeDtypeStruct((B, 8, S), jnp.int32),
                   jax.ShapeDtypeStruct((B, S, LANES), F32),
                   jax.ShapeDtypeStruct((8, LANES), F32)),
        grid=(B, S // tm),
        in_specs=[pl.BlockSpec((1, tm, D_MODEL), lambda b, i: (b + b0, i, 0)), tok(ATTN_WIDTH),
                  const(1, D_MODEL), perb, perb, perb,
                  const(1, D_MODEL), perb, perb,
                  const(D_MODEL, 3 * CONV_WIDTH + 2 * D_MODEL),
                  const(8, CONV_WIDTH),
                  const(CONV_WIDTH, D_MODEL), const(ATTN_WIDTH, D_MODEL),
                  const(D_MODEL, D_MODEL),
                  const(D_MODEL, 2 * LANES), const(1, LANES), const(rows, rows)],
        out_specs=(tok(D_MODEL), tok(D_MODEL // 2),
                   pl.BlockSpec((1, 8, tm), lambda b, i: (b, 0, i)), tok(LANES),
                   pl.BlockSpec((8, LANES), lambda b, i: (0, 0))),
        scratch_shapes=[pltpu.VMEM((8, CONV_WIDTH), F32)],
        compiler_params=pltpu.CompilerParams(
            dimension_semantics=("arbitrary", "arbitrary"), vmem_limit_bytes=VMEM_LIMIT),
        name="post",
    )(x, yb, n1, sh1, sc1, g1, n2, sh2, sc2, wc, cw, woc, woa, wo, wr, br, tri)


SC_CORES = 2
SC_SUBCORES = 16
SC_WORKERS = SC_CORES * SC_SUBCORES
SC_CHUNK = 128
ROW_WORDS = D_MODEL // 2


def _sc_mesh():
    return plsc.VectorSubcoreMesh(core_axis_name="c", subcore_axis_name="s",
                                  num_cores=SC_CORES, num_subcores=SC_SUBCORES)


def _dispatch_body(rows_hbm, idx1_hbm, idx2_hbm, xs_hbm, idx1_v, idx2_v, rows_v, *, n_chunks):
    wid = lax.axis_index("s") * SC_CORES + lax.axis_index("c")
    pltpu.sync_copy(idx1_hbm.at[wid], idx1_v)
    pltpu.sync_copy(idx2_hbm.at[wid], idx2_v)
    base = wid * (n_chunks * SC_CHUNK)

    @pl.loop(0, n_chunks)
    def _(j):
        pltpu.sync_copy(rows_hbm.at[pl.ds(base + j * SC_CHUNK, SC_CHUNK)], rows_v)
        pltpu.sync_copy(rows_v, xs_hbm.at[idx1_v.at[j]])
        pltpu.sync_copy(rows_v, xs_hbm.at[idx2_v.at[j]])


def _sc_scratch(n_chunks):
    return [pltpu.VMEM((n_chunks, SC_CHUNK), jnp.int32), pltpu.VMEM((n_chunks, SC_CHUNK), jnp.int32),
            pltpu.VMEM((SC_CHUNK, ROW_WORDS), jnp.uint32)]


def _dispatch(rows, idx1, idx2, n_slots):
    n_chunks = idx1.shape[1]
    return pl.kernel(
        functools.partial(_dispatch_body, n_chunks=n_chunks),
        out_type=jax.ShapeDtypeStruct((n_slots, ROW_WORDS), jnp.uint32),
        mesh=_sc_mesh(),
        scratch_types=_sc_scratch(n_chunks),
        name="dispatch",
    )(rows, idx1, idx2)


def _collect_body(ys_hbm, idx1_hbm, idx2_hbm, g1_hbm, g2_hbm, idx1_v, idx2_v, rows_v, *, n_chunks):
    wid = lax.axis_index("s") * SC_CORES + lax.axis_index("c")
    pltpu.sync_copy(idx1_hbm.at[wid], idx1_v)
    pltpu.sync_copy(idx2_hbm.at[wid], idx2_v)
    base = wid * (n_chunks * SC_CHUNK)

    @pl.loop(0, n_chunks)
    def _(j):
        dst = pl.ds(base + j * SC_CHUNK, SC_CHUNK)
        pltpu.sync_copy(ys_hbm.at[idx1_v.at[j]], rows_v)
        pltpu.sync_copy(rows_v, g1_hbm.at[dst])
        pltpu.sync_copy(ys_hbm.at[idx2_v.at[j]], rows_v)
        pltpu.sync_copy(rows_v, g2_hbm.at[dst])


def _collect(ys, idx1, idx2):
    n_chunks = idx1.shape[1]
    out = jax.ShapeDtypeStruct((SC_WORKERS * n_chunks * SC_CHUNK, ROW_WORDS), jnp.uint32)
    return pl.kernel(
        functools.partial(_collect_body, n_chunks=n_chunks),
        out_type=(out, out),
        mesh=_sc_mesh(),
        scratch_types=_sc_scratch(n_chunks),
        name="collect",
    )(ys, idx1, idx2)


def _moe_kernel(te_ref, nt_ref, xs_ref, wg_ref, wu_ref, wd_ref, *rest):
    ys_ref = rest[-1]

    @pl.when(pl.program_id(0) < nt_ref[0])
    def _():
        left, right = _unpack_bf16_pairs(xs_ref[...])
        xb = jnp.concatenate([left.astype(BF16), right.astype(BF16)], axis=1)
        g = _dot(xb, wg_ref[0].astype(BF16))
        u = _dot(xb, wu_ref[0].astype(BF16))
        a = (g * _sigmoid(g) * u).astype(BF16)
        ys_ref[...] = _pack_bf16_pairs(_dot(a, wd_ref[0].astype(BF16)))


def _moe(xs, tile_expert, n_tiles, wg, wu, wd, tm, after):
    n_slots = xs.shape[0]
    row_blk = lambda i, te, nt: (jnp.minimum(i, nt[0] - 1), 0)
    w_blk = lambda i, te, nt: (te[i], 0, 0)
    return pl.pallas_call(
        _moe_kernel,
        out_shape=jax.ShapeDtypeStruct((n_slots, ROW_WORDS), jnp.uint32),
        grid_spec=pltpu.PrefetchScalarGridSpec(
            num_scalar_prefetch=2,
            grid=(n_slots // tm,),
            in_specs=[pl.BlockSpec((tm, ROW_WORDS), row_blk),
                      pl.BlockSpec((1, D_MODEL, D_EXPERT), w_blk),
                      pl.BlockSpec((1, D_MODEL, D_EXPERT), w_blk),
                      pl.BlockSpec((1, D_EXPERT, D_MODEL), w_blk)]
            + [pl.BlockSpec(memory_space=pl.ANY)] * len(after),
            out_specs=pl.BlockSpec((tm, ROW_WORDS), row_blk)),
        compiler_params=pltpu.CompilerParams(
            dimension_semantics=("arbitrary",), vmem_limit_bytes=VMEM_LIMIT),
        name="moe",
    )(tile_expert, n_tiles, xs, wg, wu, wd, *after)


def _final_kernel(x1_ref, g1_ref, g2_ref, rw_ref, gate_ref, *rest):
    o_ref = rest[-1]
    rw = rw_ref[0]
    w1 = rw[:, 0:1]
    w2 = rw[:, 1:2]
    a_l, a_r = _unpack_bf16_pairs(g1_ref[0])
    b_l, b_r = _unpack_bf16_pairs(g2_ref[0])
    moe = jnp.concatenate([w1 * a_l + w2 * b_l, w1 * a_r + w2 * b_r], axis=1)
    o_ref[0] = x1_ref[0] + gate_ref[0] * moe


def _final(x1, g1, g2, rw, gate2, tm, b0, src0, out_prev, after):
    nb, S, _ = g1.shape
    grp = lambda w: pl.BlockSpec((1, tm, w), lambda b, i: (b + src0, i, 0))
    tok = lambda w: pl.BlockSpec((1, tm, w), lambda b, i: (b, i, 0))
    extra = (() if out_prev is None else (out_prev,)) + tuple(after)
    return pl.pallas_call(
        _final_kernel,
        out_shape=jax.ShapeDtypeStruct((gate2.shape[0], S, D_MODEL), F32),
        grid=(nb, S // tm),
        in_specs=[grp(D_MODEL), tok(ROW_WORDS), tok(ROW_WORDS), grp(LANES),
                  pl.BlockSpec((1, 1, D_MODEL), lambda b, i: (b + b0 + src0, 0, 0))]
        + [pl.BlockSpec(memory_space=pl.ANY)] * len(extra),
        out_specs=pl.BlockSpec((1, tm, D_MODEL), lambda b, i: (b + b0 + src0, i, 0)),
        input_output_aliases={} if out_prev is None else {5: 0},
        compiler_params=pltpu.CompilerParams(
            dimension_semantics=("arbitrary", "arbitrary"), vmem_limit_bytes=VMEM_LIMIT),
        name="final",
    )(x1, g1, g2, rw, gate2, *extra)


def _pick(n, pref):
    t = min(n, pref)
    assert n % t == 0, (n, t)
    return t


def _slots_kernel(offs_ref, r_ref, o_ref):
    r = r_ref[0]
    base = jnp.zeros_like(r)
    for e in range(N_EXPERTS):
        base = jnp.where(r == e, offs_ref[e], base)
    o_ref[0] = base + pltpu.roll(r, 6, 0)


def _slots(ridx, offs):
    B, _, S = ridx.shape
    return pl.pallas_call(
        _slots_kernel,
        out_shape=jax.ShapeDtypeStruct((B, 8, S), jnp.int32),
        grid_spec=pltpu.PrefetchScalarGridSpec(
            num_scalar_prefetch=1, grid=(B,),
            in_specs=[pl.BlockSpec((1, 8, S), lambda b, offs: (b, 0, 0))],
            out_specs=pl.BlockSpec((1, 8, S), lambda b, offs: (b, 0, 0))),
        compiler_params=pltpu.CompilerParams(dimension_semantics=("arbitrary",)),
        name="slots",
    )(offs, ridx)


def _route_plan(ridx, counts, tm_e, T):
    counts = counts.astype(jnp.int32)
    tiles = (counts + tm_e - 1) // tm_e
    tile_end = jnp.cumsum(tiles)
    offs = (tile_end - tiles) * tm_e
    slots = _slots(ridx, offs)
    n_chunks = T // (SC_WORKERS * SC_CHUNK)
    idx1 = slots[:, 0, :].reshape(SC_WORKERS, n_chunks, SC_CHUNK)
    idx2 = slots[:, 1, :].reshape(SC_WORKERS, n_chunks, SC_CHUNK)
    n_tiles_max = 2 * T // tm_e + N_EXPERTS
    tile_ids = jnp.arange(n_tiles_max, dtype=jnp.int32)
    tile_expert = jnp.sum((tile_end[None, :] <= tile_ids[:, None]).astype(jnp.int32), axis=1)
    tile_expert = jnp.minimum(tile_expert, N_EXPERTS - 1)
    return idx1, idx2, tile_expert, tile_end[-1:].astype(jnp.int32), n_tiles_max * tm_e


def _layer(x, c, w_ada, b_ada, norm1_w, w_in, b_forget, conv_w, q_norm_w, k_norm_w,
           w_out_conv, w_out_attn, w_o, norm2_w, w_rg, b_rg, w_re, b_re, w_gate, w_up, w_down):
    B, S, _ = x.shape
    n_grp = 2 if B % 2 == 0 and (B // 2 * S) % (SC_WORKERS * SC_CHUNK) == 0 else 1
    Bg = B // n_grp
    T = Bg * S
    assert T % (SC_WORKERS * SC_CHUNK) == 0, T
    mod = _ada(c, w_ada, b_ada.reshape(1, -1)).reshape(B, 6, 1, D_MODEL)
    shift1, scale1, gate1, shift2, scale2, gate2 = (mod[:, t] for t in range(6))

    cuts = np.cumsum([0, CONV_WIDTH, CONV_WIDTH, CONV_WIDTH, ATTN_WIDTH, ATTN_WIDTH, ATTN_WIDTH,
                      N_HEADS, D_MODEL, D_MODEL])
    w_conv3 = w_in[:, cuts[0]:cuts[3]]
    w_qvT = jnp.concatenate([w_in[:, cuts[3]:cuts[4]], w_in[:, cuts[5]:cuts[6]]], axis=1).T.astype(BF16)
    w_k = w_in[:, cuts[4]:cuts[5]].astype(BF16)
    w_f = jnp.pad(w_in[:, cuts[6]:cuts[7]], ((0, 0), (0, LANES - N_HEADS))).astype(BF16)
    b_f = jnp.pad(b_forget, (0, LANES - N_HEADS)).reshape(1, LANES)
    w_cgg = jnp.concatenate([w_conv3, w_in[:, cuts[7]:cuts[9]]], axis=1).astype(BF16)

    tm_qkv = _pick(S, 1024)
    qnwT = jnp.broadcast_to((jnp.tile(q_norm_w, N_HEADS) * (LOG2E * HEAD_DIM ** -0.5))[:, None],
                            (ATTN_WIDTH, LANES))
    tq = _pick(S // 2, 512)
    w_r = jnp.pad(jnp.concatenate([w_re, w_rg], axis=1),
                  ((0, 0), (0, LANES - N_EXPERTS - N_GROUPS)))
    w_r_hi = w_r.astype(BF16)
    w_r_lo = (w_r - w_r_hi.astype(F32)).astype(BF16)
    b_r = jnp.pad(jnp.concatenate([b_re, b_rg]), (0, LANES - N_EXPERTS - N_GROUPS)).reshape(1, LANES)
    cw = jnp.pad(conv_w, ((0, 8 - CONV_K), (0, 0)))
    tm_post = _pick(S, 1024)
    tm_e = 512
    w_oc, w_oa, w_ob, w_rs = (w_out_conv.astype(BF16), w_out_attn.astype(BF16), w_o.astype(BF16),
                              jnp.concatenate([w_r_hi, w_r_lo], axis=1))

    n_half = 2 if Bg % 2 == 0 and (Bg // 2 * S) % (SC_WORKERS * SC_CHUNK) == 0 else 1
    Bh = Bg // n_half

    def experts(grp, after):
        x1, rw, idx1, idx2, tile_expert, n_tiles, xs = grp
        ys = _moe(xs, tile_expert, n_tiles, w_gate, w_up, w_down, tm_e, after)
        halves = lambda idx: idx.reshape(n_half, SC_WORKERS, idx.shape[1] // n_half, SC_CHUNK)
        return ys, [_collect(ys, i1, i2) for i1, i2 in zip(halves(idx1), halves(idx2))]

    def finish(g, grp, gathered, out, after):
        for hh, (g1, g2) in enumerate(gathered):
            out = _final(grp[0], g1.reshape(Bh, S, ROW_WORDS), g2.reshape(Bh, S, ROW_WORDS), grp[1],
                         gate2, _pick(S, 512), g * Bg, hh * Bh, out, after if hh == 0 else [])
        return out

    out, prev, prev_gathered, ys_prev = None, None, None, None
    for g in range(n_grp):
        b0 = g * Bg
        qT_aug, k_aug, vT = _qkv(x, norm1_w.reshape(1, -1), shift1, scale1, w_qvT, w_k, w_f, b_f,
                                 qnwT, jnp.tile(k_norm_w, N_HEADS).reshape(1, -1), tm_qkv, b0, Bg)
        if prev is not None:
            ys_prev, prev_gathered = experts(prev, [qT_aug])
        y_b = _attention(qT_aug, k_aug, vT, tq, _pick(tq, 256),
                         [] if ys_prev is None else [ys_prev])
        x1, h2p, ridx, rw, counts = _post(x, y_b, norm1_w.reshape(1, -1), shift1, scale1, gate1,
                                          norm2_w.reshape(1, -1), shift2, scale2,
                                          w_cgg, cw, w_oc, w_oa, w_ob, w_rs, b_r, tm_post, b0)
        idx1, idx2, tile_expert, n_tiles, n_slots = _route_plan(ridx, counts[0, :N_EXPERTS], tm_e, T)
        xs = _dispatch(h2p.reshape(T, ROW_WORDS), idx1, idx2, n_slots)
        if prev is not None:
            out = finish(g - 1, prev, prev_gathered, out, [idx1])
        prev = (x1, rw, idx1, idx2, tile_expert, n_tiles, xs)
    _, last_gathered = experts(prev, [] if out is None else [out])
    return finish(n_grp - 1, prev, last_gathered, out, [])


def kernel(x, c, w_ada, b_ada, norm1_w, w_in, b_forget, conv_w, q_norm_w, k_norm_w, w_out_conv,
           w_out_attn, w_o, norm2_w, w_router_group, b_router_group, w_router_expert,
           b_router_expert, w_gate, w_up, w_down):
    for l in range(w_ada.shape[0]):
        x = _layer(x, c, w_ada[l], b_ada[l], norm1_w[l], w_in[l], b_forget[l], conv_w[l],
                   q_norm_w[l], k_norm_w[l], w_out_conv[l], w_out_attn[l], w_o[l], norm2_w[l],
                   w_router_group[l], b_router_group[l], w_router_expert[l], b_router_expert[l],
                   w_gate[l], w_up[l], w_down[l])
    return x
```
